```python
import math
import jax
import jax.numpy as jnp
from jax import lax
import numpy as np

D_MODEL = 1024
BATCH = 4
SEQ = 4096
DEPTH = 4
DEC_BATCH = 128
DEC_SEQ = 4
PAST_LEN = 8192
PAGE_SIZE = 128

N_MIXERS = 3
N_A = (DEPTH + 2) // N_MIXERS
N_B = (DEPTH + 1) // N_MIXERS
N_C = DEPTH // N_MIXERS

HA = 8
DKA = D_MODEL // 2 // HA
DVA = D_MODEL // HA
GATE_SOFTCAP = 15.0
HB = 4
DKB = D_MODEL // 2 // HB
DVB = D_MODEL // HB
GLA_RANK = 16
GLA_TAU = 16.0
HC = 16
HKV = 4
HDC = D_MODEL // HC
GROUP = HC // HKV
WINDOW = 128
D_FF = 2816
CONV_W = 3

CHUNK = 64
EPS = 1e-6
NEG = -1e30

A_IN = 2 * HA * DKA + 2 * HA * DVA + 2 * HA
B_IN = 2 * HB * DKB + 2 * HB * DVB + GLA_RANK
C_IN = (HC + 2 * HKV) * HDC

kernel_name = "hybrid_mlstm_gla_swa_convffn_step"


def _rmsnorm(x, g):
    xf = x.astype(jnp.float32)
    y = xf * lax.rsqrt(jnp.mean(xf * xf, axis=-1, keepdims=True) + EPS)
    return (y * g.astype(jnp.float32)).astype(x.dtype)


def _softcap(z):
    return GATE_SOFTCAP * jnp.tanh(z / GATE_SOFTCAP)


def _to_chunks(t, L):
    b, T = t.shape[:2]
    return jnp.moveaxis(t.reshape((b, T // L, L) + t.shape[2:]), 1, 0)


def _from_chunks(t):
    nc, b, L = t.shape[:3]
    return jnp.moveaxis(t, 0, 1).reshape((b, nc * L) + t.shape[3:])


def _mlstm_chunk(carry, inp):
    c, n, m = carry
    q, k, v, ig, lf = inp
    L = q.shape[1]
    f_cum = jnp.cumsum(lf, axis=1)
    a = ig - f_cum
    m_t = f_cum + jnp.maximum(m[:, None], lax.cummax(a, axis=1))
    causal = jnp.tril(jnp.ones((L, L), bool))[None, :, :, None]
    logw = a[:, None, :, :] + (f_cum - m_t)[:, :, None, :]
    w = jnp.where(causal, jnp.exp(jnp.minimum(logw, 0.0)), 0.0)
    qk = jnp.einsum('bthd,bshd->btsh', q, k) * w
    g = jnp.exp(jnp.minimum(f_cum + m[:, None] - m_t, 0.0))
    num = jnp.einsum('btsh,bshv->bthv', qk, v) + g[..., None] * jnp.einsum('bthd,bhdv->bthv', q, c)
    den = jnp.sum(qk, axis=2) + g * jnp.einsum('bthd,bhd->bth', q, n)
    h = num / jnp.maximum(jnp.abs(den), jnp.exp(-m_t))[..., None]
    m_last = m_t[:, -1]
    f_last = f_cum[:, -1]
    w_last = jnp.exp(jnp.minimum(a + (f_last - m_last)[:, None], 0.0))
    g_last = jnp.exp(jnp.minimum(f_last + m - m_last, 0.0))
    c_new = g_last[..., None, None] * c + jnp.einsum('bsh,bshd,bshv->bhdv', w_last, k, v)
    n_new = g_last[..., None] * n + jnp.einsum('bsh,bshd->bhd', w_last, k)
    return (c_new, n_new, m_last), h


def _mlstm_mixer(h, c0, n0, m0, w_in, b_i, b_f, norm_g, w_out):
    b, T, _ = h.shape
    f32 = jnp.float32
    proj = jnp.einsum('btd,de->bte', h, w_in).astype(f32)
    s1 = HA * DKA
    s2 = 2 * HA * DKA
    s3 = s2 + HA * DVA
    s4 = s3 + HA * DVA
    s5 = s4 + HA
    q, k, v, o, ig, fg = jnp.split(proj, [s1, s2, s3, s4, s5], axis=-1)
    q = q.reshape(b, T, HA, DKA)
    k = k.reshape(b, T, HA, DKA) * (DKA ** -0.5)
    v = v.reshape(b, T, HA, DVA)
    ig = _softcap(ig + b_i.astype(f32))
    lf = jax.nn.log_sigmoid(_softcap(fg + b_f.astype(f32)))
    L = math.gcd(CHUNK, T)
    xs = (_to_chunks(q, L), _to_chunks(k, L), _to_chunks(v, L), _to_chunks(ig, L), _to_chunks(lf, L))
    carry0 = (c0.astype(f32), n0.astype(f32), m0.astype(f32))
    (c1, n1, m1), hs = lax.scan(_mlstm_chunk, carry0, xs)
    hs = _from_chunks(hs)
    hs = _rmsnorm(hs, norm_g.reshape(HA, DVA)) * jax.nn.sigmoid(o).reshape(b, T, HA, DVA)
    y = jnp.einsum('bte,ed->btd', hs.reshape(b, T, HA * DVA).astype(h.dtype), w_out)
    return y, c1, n1, m1


def _gla_chunk(s, inp):
    q, k, v, lg = inp
    L = q.shape[1]
    bc = jnp.cumsum(lg, axis=1)
    causal = jnp.tril(jnp.ones((L, L), bool))[None, :, :, None, None]
    diff = bc[:, :, None] - bc[:, None, :]
    decay = jnp.where(causal, jnp.exp(jnp.minimum(diff, 0.0)), 0.0)
    att = jnp.einsum('bthd,bshd,btshd->btsh', q, k, decay)
    o = jnp.einsum('btsh,bshv->bthv', att, v) + jnp.einsum('bthd,bhdv->bthv', q * jnp.exp(bc), s)
    b_last = bc[:, -1]
    s_new = jnp.exp(b_last)[..., None] * s + jnp.einsum('bshd,bshv->bhdv', k * jnp.exp(b_last[:, None] - bc), v)
    return s_new, o


def _gla_mixer(h, s0, w_in, w_gate_up, b_gate, norm_g, w_out):
    b, T, _ = h.shape
    f32 = jnp.float32
    proj = jnp.einsum('btd,de->bte', h, w_in).astype(f32)
    s1 = HB * DKB
    s2 = 2 * HB * DKB
    s3 = s2 + HB * DVB
    s4 = s3 + HB * DVB
    q, k, v, r, z = jnp.split(proj, [s1, s2, s3, s4], axis=-1)
    q = q.reshape(b, T, HB, DKB) * (DKB ** -0.5)
    k = k.reshape(b, T, HB, DKB)
    v = v.reshape(b, T, HB, DVB)
    lg = jax.nn.log_sigmoid(jnp.einsum('btr,re->bte', z, w_gate_up.astype(f32)) + b_gate.astype(f32)) / GLA_TAU
    lg = lg.reshape(b, T, HB, DKB)
    L = math.gcd(CHUNK, T)
    xs = (_to_chunks(q, L), _to_chunks(k, L), _to_chunks(v, L), _to_chunks(lg, L))
    s1_state, o = lax.scan(_gla_chunk, s0.astype(f32), xs)
    o = _from_chunks(o)
    o = _rmsnorm(o, norm_g.reshape(HB, DVB)) * jax.nn.silu(r).reshape(b, T, HB, DVB)
    y = jnp.einsum('bte,ed->btd', o.reshape(b, T, HB * DVB).astype(h.dtype), w_out)
    return y, s1_state


def _sink_attention(q, k, v, mask, sinks):
    s = jnp.einsum('bnqkgd,bnrkd->bnkgqr', q, k) * (HDC ** -0.5)
    s = jnp.where(mask[None, :, None, None], s, NEG)
    sink = jnp.broadcast_to(sinks[None, None, :, :, None, None], s.shape[:-1] + (1,))
    p = jax.nn.softmax(jnp.concatenate([s, sink], axis=-1), axis=-1)[..., :-1]
    return jnp.einsum('bnkgqr,bnrkd->bnqkgd', p, v)


def _swa_mixer(h, k_buf, v_buf, w_in, b_in, sinks, w_out):
    b, T, _ = h.shape
    f32 = jnp.float32
    proj = (jnp.einsum('btd,de->bte', h, w_in) + b_in).astype(f32)
    q, k, v = jnp.split(proj, [HC * HDC, HC * HDC + HKV * HDC], axis=-1)
    q = q.reshape(b, T, HKV, GROUP, HDC)
    k = k.reshape(b, T, HKV, HDC)
    v = v.reshape(b, T, HKV, HDC)
    sinks = sinks.astype(f32).reshape(HKV, GROUP)
    if k_buf is None:
        nb = T // WINDOW
        pad = jnp.zeros((b, WINDOW, HKV, HDC), f32)
        kb = jnp.concatenate([pad, k], axis=1).reshape(b, nb + 1, WINDOW, HKV, HDC)
        vb = jnp.concatenate([pad, v], axis=1).reshape(b, nb + 1, WINDOW, HKV, HDC)
        kb = jnp.concatenate([kb[:, :-1], kb[:, 1:]], axis=2)
        vb = jnp.concatenate([vb[:, :-1], vb[:, 1:]], axis=2)
        qi = jnp.arange(WINDOW)
        ri = jnp.arange(2 * WINDOW)
        rel = WINDOW + qi[:, None] - ri[None, :]
        key_pos = (jnp.arange(nb)[:, None] - 1) * WINDOW + ri[None, :]
        mask = ((rel >= 0) & (rel <= WINDOW))[None] & (key_pos >= 0)[:, None, :]
        out = _sink_attention(q.reshape(b, nb, WINDOW, HKV, GROUP, HDC), kb, vb, mask, sinks)
        k_keep = k[:, -WINDOW:]
        v_keep = v[:, -WINDOW:]
    else:
        wb = k_buf.shape[1]
        k_all = jnp.concatenate([k_buf.astype(f32), k], axis=1)
        v_all = jnp.concatenate([v_buf.astype(f32), v], axis=1)
        q_pos = PAST_LEN + jnp.arange(T)
        k_pos = PAST_LEN - wb + jnp.arange(wb + T)
        rel = q_pos[:, None] - k_pos[None, :]
        mask = ((rel >= 0) & (rel <= WINDOW))[None]
        out = _sink_attention(q[:, None], k_all[:, None], v_all[:, None], mask, sinks)
        k_keep = k_all[:, -wb:]
        v_keep = v_all[:, -wb:]
    y = jnp.einsum('bte,ed->btd', out.reshape(b, T, HC * HDC).astype(h.dtype), w_out)
    return y, k_keep, v_keep


def _conv_ffn(h, conv_state, w_up, conv_w, conv_b, w_down):
    b, T, _ = h.shape
    f32 = jnp.float32
    u = jnp.einsum('btd,df->btf', h, w_up).astype(f32)
    gate, up = jnp.split(u, 2, axis=-1)
    g_ext = jnp.concatenate([conv_state.astype(f32), gate], axis=1)
    cw = conv_w.astype(f32)
    conv = conv_b.astype(f32) + sum(cw[j] * g_ext[:, j:j + T] for j in range(CONV_W))
    y = jnp.einsum('btf,fd->btd', (jax.nn.silu(conv) * up).astype(h.dtype), w_down)
    return y, g_ext[:, -(CONV_W - 1):]


def _trunk(x, st, w):
    new = {'a_c': [], 'a_n': [], 'a_m': [], 'b_s': [], 'c_k': [], 'c_v': [], 'f': []}
    for i in range(DEPTH):
        kind, j = i % N_MIXERS, i // N_MIXERS
        hn = _rmsnorm(x, w['norm_mix'][i])
        if kind == 0:
            y, c1, n1, m1 = _mlstm_mixer(hn, st['a_c'][j], st['a_n'][j], st['a_m'][j], w['a_w_in'][j],
                                         w['a_b_i'][j], w['a_b_f'][j], w['a_norm'][j], w['a_w_out'][j])
            new['a_c'].append(c1)
            new['a_n'].append(n1)
            new['a_m'].append(m1)
        elif kind == 1:
            y, s1 = _gla_mixer(hn, st['b_s'][j], w['b_w_in'][j], w['b_w_gate_up'][j], w['b_b_gate'][j],
                               w['b_norm'][j], w['b_w_out'][j])
            new['b_s'].append(s1)
        else:
            y, k1, v1 = _swa_mixer(hn, st['c_k'][j], st['c_v'][j], w['c_w_in'][j], w['c_b_in'][j],
                                   w['c_sinks'][j], w['c_w_out'][j])
            new['c_k'].append(k1)
            new['c_v'].append(v1)
        x = x + y
        y, f1 = _conv_ffn(_rmsnorm(x, w['norm_ffn'][i]), st['f'][i], w['f_w_up'][i], w['f_conv_w'][i],
                          w['f_conv_b'][i], w['f_w_down'][i])
        x = x + y
        new['f'].append(f1)
    out = {name: jnp.stack(vals).astype(x.dtype) for name, vals in new.items()}
    return _rmsnorm(x, w['norm_final']), out


def setup_inputs(seed: int = 0) -> dict:
    key = jax.random.key(seed)
    ks = iter(jax.random.split(key, 32))

    def nrm(shape, scale=1.0):
        return jax.random.normal(next(ks), shape, jnp.float32) * scale

    w_buf = min(WINDOW, PAST_LEN)
    return {
        'x_prompt': nrm((BATCH, SEQ, D_MODEL)),
        'x_sample': nrm((DEC_BATCH, DEC_SEQ, D_MODEL)),
        'state_mlstm_c': nrm((N_A, DEC_BATCH, HA, DKA, DVA), 0.5),
        'state_mlstm_n': nrm((N_A, DEC_BATCH, HA, DKA), 0.5),
        'state_mlstm_m': nrm((N_A, DEC_BATCH, HA), 1.0),
        'state_gla': nrm((N_B, DEC_BATCH, HB, DKB, DVB), 0.5),
        'cache_swa_k': nrm((N_C, DEC_BATCH, w_buf, HKV, HDC)),
        'cache_swa_v': nrm((N_C, DEC_BATCH, w_buf, HKV, HDC)),
        'state_ffn_conv': nrm((DEPTH, DEC_BATCH, CONV_W - 1, D_FF)),
        'norm_mix_g': 1.0 + nrm((DEPTH, D_MODEL), 0.1),
        'norm_ffn_g': 1.0 + nrm((DEPTH, D_MODEL), 0.1),
        'norm_final_g': 1.0 + nrm((D_MODEL,), 0.1),
        'a_w_in': nrm((N_A, D_MODEL, A_IN), D_MODEL ** -0.5),
        'a_b_i': nrm((N_A, HA), 0.1),
        'a_b_f': 3.0 + nrm((N_A, HA), 0.5),
        'a_norm_g': 1.0 + nrm((N_A, HA * DVA), 0.1),
        'a_w_out': nrm((N_A, HA * DVA, D_MODEL), (HA * DVA) ** -0.5),
        'b_w_in': nrm((N_B, D_MODEL, B_IN), D_MODEL ** -0.5),
        'b_w_gate_up': nrm((N_B, GLA_RANK, HB * DKB), GLA_RANK ** -0.5),
        'b_b_gate': nrm((N_B, HB * DKB), 0.1),
        'b_norm_g': 1.0 + nrm((N_B, HB * DVB), 0.1),
        'b_w_out': nrm((N_B, HB * DVB, D_MODEL), (HB * DVB) ** -0.5),
        'c_w_in': nrm((N_C, D_MODEL, C_IN), D_MODEL ** -0.5),
        'c_b_in': nrm((N_C, C_IN), 0.02),
        'c_sinks': nrm((N_C, HC), 1.0),
        'c_w_out': nrm((N_C, HC * HDC, D_MODEL), (HC * HDC) ** -0.5),
        'f_w_up': nrm((DEPTH, D_MODEL, 2 * D_FF), D_MODEL ** -0.5),
        'f_conv_w': nrm((DEPTH, CONV_W, D_FF), CONV_W ** -0.5),
        'f_conv_b': nrm((DEPTH, D_FF), 0.02),
        'f_w_down': nrm((DEPTH, D_FF, D_MODEL), D_FF ** -0.5),
    }


def reference(x_prompt, x_sample, state_mlstm_c, state_mlstm_n, state_mlstm_m, state_gla,
              cache_swa_k, cache_swa_v, state_ffn_conv, norm_mix_g, norm_ffn_g, norm_final_g,
              a_w_in, a_b_i, a_b_f, a_norm_g, a_w_out, b_w_in, b_w_gate_up, b_b_gate, b_norm_g, b_w_out,
              c_w_in, c_b_in, c_sinks, c_w_out, f_w_up, f_conv_w, f_conv_b, f_w_down):
    w = {'norm_mix': norm_mix_g, 'norm_ffn': norm_ffn_g, 'norm_final': norm_final_g,
         'a_w_in': a_w_in, 'a_b_i': a_b_i, 'a_b_f': a_b_f, 'a_norm': a_norm_g, 'a_w_out': a_w_out,
         'b_w_in': b_w_in, 'b_w_gate_up': b_w_gate_up, 'b_b_gate': b_b_gate, 'b_norm': b_norm_g,
         'b_w_out': b_w_out, 'c_w_in': c_w_in, 'c_b_in': c_b_in, 'c_sinks': c_sinks, 'c_w_out': c_w_out,
         'f_w_up': f_w_up, 'f_conv_w': f_conv_w, 'f_conv_b': f_conv_b, 'f_w_down': f_w_down}
    f32 = jnp.float32
    bp = x_prompt.shape[0]
    st_p = {'a_c': jnp.zeros((N_A, bp, HA, DKA, DVA), f32),
            'a_n': jnp.zeros((N_A, bp, HA, DKA), f32),
            'a_m': jnp.zeros((N_A, bp, HA), f32),
            'b_s': jnp.zeros((N_B, bp, HB, DKB, DVB), f32),
            'c_k': [None] * N_C, 'c_v': [None] * N_C,
            'f': jnp.zeros((DEPTH, bp, CONV_W - 1, D_FF), x_prompt.dtype)}
    st_s = {'a_c': state_mlstm_c, 'a_n': state_mlstm_n, 'a_m': state_mlstm_m, 'b_s': state_gla,
            'c_k': cache_swa_k, 'c_v': cache_swa_v, 'f': state_ffn_conv}
    y_prompt, np_ = _trunk(x_prompt, st_p, w)
    y_sample, ns_ = _trunk(x_sample, st_s, w)
    return (y_prompt, y_sample,
            np_['a_c'], np_['a_n'], np_['a_m'], np_['b_s'], np_['c_k'], np_['c_v'], np_['f'],
            ns_['a_c'], ns_['a_n'], ns_['a_m'], ns_['b_s'], ns_['c_k'], ns_['c_v'], ns_['f'])
```

```python
import functools
import math

import jax
import jax.numpy as jnp
from jax import lax
from jax.experimental import pallas as pl
from jax.experimental.pallas import tpu as pltpu

F32 = jnp.float32
BF16 = jnp.bfloat16

D_MODEL = 1024
DEPTH = 4
N_MIXERS = 3
HA, DKA, DVA = 8, 64, 128
GATE_SOFTCAP = 15.0
HB, DKB, DVB = 4, 128, 256
GLA_RANK = 16
GLA_TAU = 16.0
HC, HKV, HDC = 16, 4, 64
WINDOW = 128
D_FF = 2816
CONV_W = 3
EPS = 1e-6
NEG = -1e30

ROWS = 128
LANES = 128
SUBLANES = 8
SAMPLE_T_PAD = 8
GLA_SUB = 16
VMEM_LIMIT = 56 * 1024 * 1024


def _params(sem):
    return pltpu.CompilerParams(dimension_semantics=sem, vmem_limit_bytes=VMEM_LIMIT)


def _dot(a, b):
    return jnp.dot(a.astype(BF16), b.astype(BF16), preferred_element_type=F32)


def _dot_nt(a, b):
    return lax.dot_general(a.astype(BF16), b.astype(BF16), (((1,), (1,)), ((), ())),
                           preferred_element_type=F32)


def _dot_tn(a, b):
    return lax.dot_general(a.astype(BF16), b.astype(BF16), (((0,), (0,)), ((), ())),
                           preferred_element_type=F32)


def _sigmoid(x):
    return 1.0 / (1.0 + jnp.exp(-x))


def _log_sigmoid(x):
    return jnp.minimum(x, 0.0) - jnp.log(1.0 + jnp.exp(-jnp.abs(x)))


def _softcap(z):
    return GATE_SOFTCAP * jnp.tanh(z / GATE_SOFTCAP)


def _col(x, h):
    lane = lax.broadcasted_iota(jnp.int32, x.shape, 1)
    return jnp.sum(jnp.where(lane == h, x, 0.0), axis=-1, keepdims=True)


def _rep_rows(x, reps):
    g, c = x.shape
    if g == 1:
        return jnp.broadcast_to(x, (reps, c))
    return jnp.concatenate([jnp.broadcast_to(x[b:b + 1], (reps, c)) for b in range(g)], axis=0)


def _seg_scan(x, seg_len, op, tpos):
    s = 1
    while s < seg_len:
        shifted = pltpu.roll(x, s, axis=0)
        x = jnp.where(tpos >= s, op(x, shifted), x)
        s *= 2
    return x


def _expand_blockdiag(x, gb, seg_len):
    if gb == 1:
        return x
    seq = lax.broadcasted_iota(jnp.int32, x.shape, 0) // seg_len
    return jnp.concatenate([jnp.where(seq == b, x, 0.0) for b in range(gb)], axis=1)


def _norm_matmul_kernel(*refs, has_bias, has_extra):
    x_ref, g_ref, w_ref = refs[:3]
    pos = 3
    b_ref = None
    we_ref = None
    if has_bias:
        b_ref = refs[pos]
        pos += 1
    if has_extra:
        we_ref = refs[pos]
        pos += 1
    o_ref = refs[pos]
    pos += 1
    oe_ref = None
    if has_extra:
        oe_ref = refs[pos]
        pos += 1
    xn_ref = refs[pos]

    j = pl.program_id(1)

    @pl.when(j == 0)
    def _():
        x = x_ref[...]
        ms = jnp.mean(x * x, axis=-1, keepdims=True)
        xn = (x * lax.rsqrt(ms + EPS) * g_ref[...]).astype(BF16)
        xn_ref[...] = xn
        if has_extra:
            oe_ref[...] = jnp.dot(xn, we_ref[...], preferred_element_type=F32)

    acc = jnp.dot(xn_ref[...], w_ref[...], preferred_element_type=F32)
    if has_bias:
        acc = acc + b_ref[...]
    o_ref[...] = acc


def _norm_matmul(x, g, w, e_main, bias=None, w_extra=None, tm=512, tn=512):
    n, d = x.shape
    assert n % tm == 0 and e_main % tn == 0
    has_bias = bias is not None
    has_extra = w_extra is not None
    in_specs = [pl.BlockSpec((tm, d), lambda i, j: (i, 0)),
                pl.BlockSpec((1, d), lambda i, j: (0, 0)),
                pl.BlockSpec((d, tn), lambda i, j: (0, j))]
    args = [x, g.reshape(1, d), w]
    if has_bias:
        in_specs.append(pl.BlockSpec((1, tn), lambda i, j: (0, j)))
        args.append(bias.reshape(1, -1))
    out_shape = [jax.ShapeDtypeStruct((n, e_main), F32)]
    out_specs = [pl.BlockSpec((tm, tn), lambda i, j: (i, j))]
    if has_extra:
        ex = w_extra.shape[1]
        in_specs.append(pl.BlockSpec((d, ex), lambda i, j: (0, 0)))
        args.append(w_extra)
        out_shape.append(jax.ShapeDtypeStruct((n, ex), F32))
        out_specs.append(pl.BlockSpec((tm, ex), lambda i, j: (i, 0)))
    outs = pl.pallas_call(
        functools.partial(_norm_matmul_kernel, has_bias=has_bias, has_extra=has_extra),
        grid=(n // tm, e_main // tn),
        in_specs=in_specs, out_specs=out_specs, out_shape=out_shape,
        scratch_shapes=[pltpu.VMEM((tm, d), BF16)],
        compiler_params=_params(("parallel", "arbitrary")),
        name="norm_matmul",
    )(*args)
    return outs if has_extra else outs[0]


def _matmul_residual_kernel(a_ref, w_ref, r_ref, o_ref):
    o_ref[...] = r_ref[...] + jnp.dot(a_ref[...].astype(BF16), w_ref[...], preferred_element_type=F32)


def _matmul_residual(a, w, res, tm=512):
    n, k = a.shape
    d = w.shape[1]
    return pl.pallas_call(
        _matmul_residual_kernel,
        grid=(n // tm,),
        in_specs=[pl.BlockSpec((tm, k), lambda i: (i, 0)),
                  pl.BlockSpec((k, d), lambda i: (0, 0)),
                  pl.BlockSpec((tm, d), lambda i: (i, 0))],
        out_specs=pl.BlockSpec((tm, d), lambda i: (i, 0)),
        out_shape=jax.ShapeDtypeStruct((n, d), F32),
        compiler_params=_params(("parallel",)),
        name="matmul_residual",
    )(a, w, res)


def _ffn_kernel(*refs, tm, tf, seq_len, carry_mode, apply_final):
    x_ref, g_ref, wg_ref, wu_ref, cw_ref, cb_ref, wd_ref = refs[:7]
    pos = 7
    s1_ref = s2_ref = fg_ref = None
    if not carry_mode:
        s1_ref, s2_ref = refs[pos], refs[pos + 1]
        pos += 2
    if apply_final:
        fg_ref = refs[pos]
        pos += 1
    o_ref, go_ref = refs[pos], refs[pos + 1]
    xn_ref, acc_ref, gs_ref = refs[pos + 2], refs[pos + 3], refs[pos + 4]

    i = pl.program_id(0)
    j = pl.program_id(1)
    nj = pl.num_programs(1)

    @pl.when(j == 0)
    def _():
        x = x_ref[...]
        ms = jnp.mean(x * x, axis=-1, keepdims=True)
        xn_ref[...] = (x * lax.rsqrt(ms + EPS) * g_ref[...]).astype(BF16)
        acc_ref[...] = jnp.zeros_like(acc_ref)

    xn = xn_ref[...]
    gate = jnp.dot(xn, wg_ref[...], preferred_element_type=F32)
    up = jnp.dot(xn, wu_ref[...], preferred_element_type=F32)

    if carry_mode:
        starts_sequence = (i * tm) % seq_len == 0

        @pl.when(starts_sequence)
        def _():
            gs_ref[j, pl.ds(0, SUBLANES), :] = jnp.zeros((SUBLANES, tf), F32)

        @pl.when(jnp.logical_not(starts_sequence))
        def _():
            gs_ref[j, pl.ds(0, SUBLANES), :] = gs_ref[j, pl.ds(tm, SUBLANES), :]
    else:
        gs_ref[j, pl.ds(0, SUBLANES), :] = jnp.zeros((SUBLANES, tf), F32)
    gs_ref[j, pl.ds(SUBLANES, tm), :] = gate
    g1 = gs_ref[j, pl.ds(SUBLANES - 1, tm), :]
    g2 = gs_ref[j, pl.ds(SUBLANES - 2, tm), :]
    if not carry_mode:
        tpos = lax.broadcasted_iota(jnp.int32, (tm, tf), 0) % seq_len
        g1 = jnp.where(tpos >= 1, g1, s1_ref[...])
        g2 = jnp.where(tpos >= 2, g2, s2_ref[...])
    cw = cw_ref[...]
    conv = cb_ref[...] + cw[0:1] * g2 + cw[1:2] * g1 + cw[2:3] * gate
    hidden = (conv * _sigmoid(conv) * up).astype(BF16)
    acc_ref[...] += jnp.dot(hidden, wd_ref[...], preferred_element_type=F32)

    if carry_mode:
        go_ref[0] = gate[tm - SUBLANES:tm]
    else:
        go_ref[...] = gate

    @pl.when(j == nj - 1)
    def _():
        y = x_ref[...] + acc_ref[...]
        if apply_final:
            ms = jnp.mean(y * y, axis=-1, keepdims=True)
            y = y * lax.rsqrt(ms + EPS) * fg_ref[...]
        o_ref[...] = y


def _ffn(x, g, w_up, conv_w, conv_b, w_down, seq_len, conv_prev=None, final_g=None, tm=512, tf=1408):
    n, d = x.shape
    f = w_down.shape[0]
    assert n % tm == 0 and f % tf == 0
    nj = f // tf
    carry_mode = conv_prev is None
    apply_final = final_g is not None
    in_specs = [pl.BlockSpec((tm, d), lambda i, j: (i, 0)),
                pl.BlockSpec((1, d), lambda i, j: (0, 0)),
                pl.BlockSpec((d, tf), lambda i, j: (0, j)),
                pl.BlockSpec((d, tf), lambda i, j: (0, j + nj)),
                pl.BlockSpec((CONV_W, tf), lambda i, j: (0, j)),
                pl.BlockSpec((1, tf), lambda i, j: (0, j)),
                pl.BlockSpec((tf, d), lambda i, j: (j, 0))]
    args = [x, g.reshape(1, d), w_up, w_up, conv_w, conv_b.reshape(1, f), w_down]
    if carry_mode:
        assert seq_len % tm == 0
        tiles_per_seq = seq_len // tm
        go_shape = jax.ShapeDtypeStruct((n // seq_len, SUBLANES, f), F32)
        go_spec = pl.BlockSpec((1, SUBLANES, tf), lambda i, j: (i // tiles_per_seq, 0, j))
    else:
        assert tm % seq_len == 0
        in_specs += [pl.BlockSpec((tm, tf), lambda i, j: (i, j))] * 2
        args += list(conv_prev)
        go_shape = jax.ShapeDtypeStruct((n, f), F32)
        go_spec = pl.BlockSpec((tm, tf), lambda i, j: (i, j))
    if apply_final:
        in_specs.append(pl.BlockSpec((1, d), lambda i, j: (0, 0)))
        args.append(final_g.reshape(1, d))
    return pl.pallas_call(
        functools.partial(_ffn_kernel, tm=tm, tf=tf, seq_len=seq_len, carry_mode=carry_mode,
                          apply_final=apply_final),
        grid=(n // tm, nj),
        in_specs=in_specs,
        out_specs=[pl.BlockSpec((tm, d), lambda i, j: (i, 0)), go_spec],
        out_shape=[jax.ShapeDtypeStruct((n, d), F32), go_shape],
        scratch_shapes=[pltpu.VMEM((tm, d), BF16), pltpu.VMEM((tm, d), F32),
                        pltpu.VMEM((nj, tm + SUBLANES, tf), F32)],
        compiler_params=_params(("arbitrary", "arbitrary")),
        name="conv_ffn",
    )(*args)


def _mlstm_kernel(q_ref, k_ref, v_ref, o_ref, gi_ref, gf_ref, bi_ref, bf_ref, ng_ref,
                  c0_ref, n0_ref, m0_ref, y_ref, c1_ref, n1_ref, m1_ref, fs_ref, ms_ref,
                  *, gb, seg_len, t_valid):
    chunk = pl.program_id(1)

    @pl.when(chunk == 0)
    def _():
        c1_ref[...] = c0_ref[...]
        n1_ref[...] = n0_ref[...]
        m1_ref[...] = m0_ref[...]

    shape = (ROWS, LANES)
    row = lax.broadcasted_iota(jnp.int32, shape, 0)
    lane = lax.broadcasted_iota(jnp.int32, shape, 1)
    tpos = row % seg_len
    low_half = lane < DKA

    ig = _softcap(gi_ref[...] + bi_ref[...])
    lf = _log_sigmoid(_softcap(gf_ref[...] + bf_ref[...]))
    if t_valid < seg_len:
        valid = tpos < t_valid
        ig = jnp.where(valid, ig, NEG)
        lf = jnp.where(valid, lf, 0.0)
    f_cum = _seg_scan(lf, seg_len, jnp.add, tpos)
    a = ig - f_cum
    cmax = _seg_scan(a, seg_len, jnp.maximum, tpos)
    m_prev_g = m1_ref[0]
    m_prev = _rep_rows(m_prev_g, seg_len)
    m_t = f_cum + jnp.maximum(m_prev, cmax)
    g = jnp.exp(jnp.minimum(f_cum + m_prev - m_t, 0.0))
    u = f_cum - m_t
    em = jnp.exp(-m_t)

    fs_ref[...] = f_cum
    ms_ref[...] = m_t
    f_last_g = fs_ref[pl.ds(seg_len - 1, gb, stride=seg_len), :]
    m_last_g = ms_ref[pl.ds(seg_len - 1, gb, stride=seg_len), :]
    f_last = _rep_rows(f_last_g, seg_len)
    m_last = _rep_rows(m_last_g, seg_len)
    w_last = jnp.exp(jnp.minimum(a + f_last - m_last, 0.0))
    g_last_g = jnp.exp(jnp.minimum(f_last_g + m_prev_g - m_last_g, 0.0))
    g_last_rows = _rep_rows(g_last_g, LANES)

    a_t = a.T
    col_i = lax.broadcasted_iota(jnp.int32, (ROWS, ROWS), 1)
    row_i = lax.broadcasted_iota(jnp.int32, (ROWS, ROWS), 0)
    causal = (col_i <= row_i) & ((col_i // seg_len) == (row_i // seg_len))
    state_row = lax.broadcasted_iota(jnp.int32, (gb * LANES, LANES), 0) % LANES
    sel = (lax.broadcasted_iota(jnp.int32, (max(gb, SUBLANES), ROWS), 1) // seg_len
           == lax.broadcasted_iota(jnp.int32, (max(gb, SUBLANES), ROWS), 0)).astype(BF16)
    lane_g = lax.broadcasted_iota(jnp.int32, (gb, LANES), 1)

    for p in range(HA // 2):
        cols = slice(p * LANES, (p + 1) * LANES)
        qp = q_ref[:, cols]
        kp = k_ref[:, cols] * (DKA ** -0.5)
        c_p = c1_ref[:, p].reshape(gb * LANES, DVA)
        n_p = n1_ref[0, p]
        n_rows = _rep_rows(n_p, seg_len)
        v_heads = []
        for half in range(2):
            h = 2 * p + half
            hcols = slice(h * DVA, (h + 1) * DVA)
            qm = jnp.where(low_half if half == 0 else ~low_half, qp, 0.0)
            s = _dot_nt(qm, kp)
            logw = a_t[h:h + 1, :] + _col(u, h)
            w = jnp.where(causal, jnp.exp(jnp.minimum(logw, 0.0)), 0.0)
            qk = s * w
            vh = v_ref[:, hcols]
            v_heads.append(vh)
            inter = _dot(_expand_blockdiag(qm, gb, seg_len), c_p)
            gcol = _col(g, h)
            num = _dot(qk, vh) + gcol * inter
            den = (jnp.sum(qk, axis=-1, keepdims=True)
                   + gcol * jnp.sum(qm * n_rows, axis=-1, keepdims=True))
            hh = num / jnp.maximum(jnp.abs(den), _col(em, h))
            msq = jnp.mean(hh * hh, axis=-1, keepdims=True)
            hn = hh * lax.rsqrt(msq + EPS) * ng_ref[:, hcols]
            y_ref[:, hcols] = hn * _sigmoid(o_ref[:, hcols])

        ha, hb = 2 * p, 2 * p + 1
        k2 = kp * jnp.where(low_half, _col(w_last, ha), _col(w_last, hb))
        kx = _expand_blockdiag(k2, gb, seg_len)
        upd = jnp.where(state_row < DKA, _dot_tn(kx, v_heads[0]), _dot_tn(kx, v_heads[1]))
        scale = jnp.where(state_row < DKA, _col(g_last_rows, ha), _col(g_last_rows, hb))
        c1_ref[:, p] = (scale * c_p + upd).reshape(gb, LANES, DVA)
        n_upd = jnp.dot(sel, k2.astype(BF16), preferred_element_type=F32)[:gb]
        n_scale = jnp.where(lane_g < DKA, _col(g_last_g, ha), _col(g_last_g, hb))
        n1_ref[0, p] = n_scale * n_p + n_upd

    m1_ref[0] = m_last_g


def _mlstm(proj, gates, b_i, b_f, norm_g, c0, n0, m0, gb, seg_len, t_valid, n_chunks):
    n = proj.shape[0]
    n_groups = n // (ROWS * n_chunks)
    bsz = n_groups * gb
    half_pairs = HA // 2
    c0r = c0.reshape(bsz, half_pairs, LANES, DVA)
    n0r = n0.reshape(n_groups, gb, half_pairs, LANES).transpose(0, 2, 1, 3)
    m0r = jnp.pad(m0, ((0, 0), (0, LANES - HA))).reshape(n_groups, gb, LANES)
    pad = lambda b: jnp.pad(b, (0, LANES - HA)).reshape(1, LANES)
    blk = lambda width, cb: pl.BlockSpec((ROWS, width), lambda i, c: (i * n_chunks + c, cb))
    y, c1, n1, m1 = pl.pallas_call(
        functools.partial(_mlstm_kernel, gb=gb, seg_len=seg_len, t_valid=t_valid),
        grid=(n_groups, n_chunks),
        in_specs=[blk(HA * DKA, 0), blk(HA * DKA, 1), blk(HA * DVA, 1), blk(HA * DVA, 2),
                  blk(LANES, 0), blk(LANES, 1),
                  pl.BlockSpec((1, LANES), lambda i, c: (0, 0)),
                  pl.BlockSpec((1, LANES), lambda i, c: (0, 0)),
                  pl.BlockSpec((1, HA * DVA), lambda i, c: (0, 0)),
                  pl.BlockSpec((gb, half_pairs, LANES, DVA), lambda i, c: (i, 0, 0, 0)),
                  pl.BlockSpec((1, half_pairs, gb, LANES), lambda i, c: (i, 0, 0, 0)),
                  pl.BlockSpec((1, gb, LANES), lambda i, c: (i, 0, 0))],
        out_specs=[blk(HA * DVA, 0),
                   pl.BlockSpec((gb, half_pairs, LANES, DVA), lambda i, c: (i, 0, 0, 0)),
                   pl.BlockSpec((1, half_pairs, gb, LANES), lambda i, c: (i, 0, 0, 0)),
                   pl.BlockSpec((1, gb, LANES), lambda i, c: (i, 0, 0))],
        out_shape=[jax.ShapeDtypeStruct((n, HA * DVA), F32),
                   jax.ShapeDtypeStruct(c0r.shape, F32),
                   jax.ShapeDtypeStruct(n0r.shape, F32),
                   jax.ShapeDtypeStruct(m0r.shape, F32)],
        scratch_shapes=[pltpu.VMEM((ROWS, LANES), F32), pltpu.VMEM((ROWS, LANES), F32)],
        compiler_params=_params(("parallel", "arbitrary")),
        name="mlstm_mixer",
    )(proj, proj, proj, proj, gates, gates, pad(b_i), pad(b_f), norm_g.reshape(1, -1), c0r, n0r, m0r)
    c1 = c1.reshape(bsz, HA, DKA, DVA)
    n1 = n1.transpose(0, 2, 1, 3).reshape(bsz, HA, DKA)
    m1 = m1.reshape(bsz, LANES)[:, :HA]
    return y, c1, n1, m1


def _gla_kernel(q_ref, k_ref, v_ref, r_ref, z_ref, wgu_ref, bg_ref, ng_ref, s0_ref,
                y_ref, s1_ref, bc_ref, *, gb, seg_len, t_valid, heads):
    chunk = pl.program_id(2)

    @pl.when(chunk == 0)
    def _():
        s1_ref[...] = s0_ref[...]

    width = heads * DKB
    row = lax.broadcasted_iota(jnp.int32, (ROWS, width), 0)
    tpos = row % seg_len
    lg = _log_sigmoid(jnp.dot(z_ref[...].astype(BF16), wgu_ref[...], preferred_element_type=F32)
                      + bg_ref[...]) / GLA_TAU
    k_all = k_ref[...]
    if t_valid < seg_len:
        valid = tpos < t_valid
        lg = jnp.where(valid, lg, 0.0)
        k_all = jnp.where(valid, k_all, 0.0)
    bc = _seg_scan(lg, seg_len, jnp.add, tpos)

    n_sub = ROWS // GLA_SUB
    sub_row = lax.broadcasted_iota(jnp.int32, (GLA_SUB, ROWS), 0)
    sub_col = lax.broadcasted_iota(jnp.int32, (GLA_SUB, ROWS), 1)

    for h in range(heads):
        kcols = slice(h * DKB, (h + 1) * DKB)
        vcols = slice(h * DVB, (h + 1) * DVB)
        qh = q_ref[:, kcols] * (DKB ** -0.5)
        kh = k_all[:, kcols]
        vh = v_ref[:, vcols]
        bch = bc[:, kcols]
        bc_ref[...] = bch
        b_last_g = bc_ref[pl.ds(seg_len - 1, gb, stride=seg_len), :]
        b_last = _rep_rows(b_last_g, seg_len)
        s_h = s1_ref[:, h].reshape(gb * DKB, DVB)

        inter = _dot(_expand_blockdiag(qh * jnp.exp(bch), gb, seg_len), s_h)

        att_rows = []
        for blk in range(n_sub):
            r0 = blk * GLA_SUB
            q_b = qh[r0:r0 + GLA_SUB]
            k_b = kh[r0:r0 + GLA_SUB]
            bc_b = bch[r0:r0 + GLA_SUB]
            att = jnp.zeros((GLA_SUB, ROWS), F32)
            for s in range(GLA_SUB):
                decay = jnp.exp(jnp.minimum(bc_b - bc_b[s:s + 1], 0.0))
                cs = jnp.sum(q_b * k_b[s:s + 1] * decay, axis=-1, keepdims=True)
                ok = (sub_col == r0 + s) & (sub_row >= s)
                if seg_len < GLA_SUB:
                    ok = ok & ((sub_row // seg_len) == (s // seg_len))
                att = jnp.where(ok, cs, att)
            if seg_len > GLA_SUB and blk > 0:
                b_edge = bch[r0 - 1:r0]
                q_t = q_b * jnp.exp(bc_b - b_edge)
                k_t = kh * jnp.exp(jnp.minimum(b_edge - bch, 0.0))
                att = att + jnp.where(sub_col < r0, _dot_nt(q_t, k_t), 0.0)
            att_rows.append(att)
        att_full = jnp.concatenate(att_rows, axis=0)

        o = inter + _dot(att_full, vh)
        msq = jnp.mean(o * o, axis=-1, keepdims=True)
        on = o * lax.rsqrt(msq + EPS) * ng_ref[:, vcols]
        r = r_ref[:, vcols]
        y_ref[:, vcols] = on * (r * _sigmoid(r))

        k_d = kh * jnp.exp(b_last - bch)
        upd = _dot_tn(_expand_blockdiag(k_d, gb, seg_len), vh)
        e_last = jnp.exp(b_last_g)
        e_cols = jnp.concatenate(
            [jnp.broadcast_to(e_last[b:b + 1], (DKB, DKB)).T for b in range(gb)], axis=0)
        e_cols = jnp.concatenate([e_cols] * (DVB // DKB), axis=1)
        s1_ref[:, h] = (e_cols * s_h + upd).reshape(gb, DKB, DVB)


def _gla(proj, z, w_gate_up, b_gate, norm_g, s0, gb, seg_len, t_valid, n_chunks, heads):
    n = proj.shape[0]
    n_groups = n // (ROWS * n_chunks)
    n_hb = HB // heads
    wq, wv = heads * DKB, heads * DVB
    q_off, k_off = 0, (HB * DKB) // wq
    v_off, r_off = (2 * HB * DKB) // wv, (2 * HB * DKB + HB * DVB) // wv
    blk = lambda width, off: pl.BlockSpec((ROWS, width), lambda i, hb, c: (i * n_chunks + c, off + hb))
    state_spec = pl.BlockSpec((gb, heads, DKB, DVB), lambda i, hb, c: (i, hb, 0, 0))
    wgu = jnp.zeros((LANES, HB * DKB), BF16).at[:GLA_RANK].set(w_gate_up.astype(BF16))
    y, s1 = pl.pallas_call(
        functools.partial(_gla_kernel, gb=gb, seg_len=seg_len, t_valid=t_valid, heads=heads),
        grid=(n_groups, n_hb, n_chunks),
        in_specs=[blk(wq, q_off), blk(wq, k_off), blk(wv, v_off), blk(wv, r_off),
                  pl.BlockSpec((ROWS, LANES), lambda i, hb, c: (i * n_chunks + c, 0)),
                  pl.BlockSpec((LANES, wq), lambda i, hb, c: (0, hb)),
                  pl.BlockSpec((1, wq), lambda i, hb, c: (0, hb)),
                  pl.BlockSpec((1, wv), lambda i, hb, c: (0, hb)),
                  state_spec],
        out_specs=[blk(wv, 0), state_spec],
        out_shape=[jax.ShapeDtypeStruct((n, HB * DVB), F32),
                   jax.ShapeDtypeStruct(s0.shape, F32)],
        scratch_shapes=[pltpu.VMEM((ROWS, DKB), F32)],
        compiler_params=_params(("parallel", "parallel", "arbitrary")),
        name="gla_mixer",
    )(proj, proj, proj, proj, z, wgu, b_gate.reshape(1, -1), norm_g.reshape(1, -1), s0)
    return y, s1


def _swa_kernel(q_ref, kc_ref, vc_ref, kp_ref, vp_ref, sink_ref, o_ref, *, units, q_len, prev_from_grid):
    if prev_from_grid:
        prev_valid = pl.program_id(1) > 0
    rows4 = (HC // HKV) * q_len
    q_idx = lax.broadcasted_iota(jnp.int32, (rows4, WINDOW), 0) % q_len
    key_p = lax.broadcasted_iota(jnp.int32, (rows4, WINDOW), 1)
    key_c = lax.broadcasted_iota(jnp.int32, (rows4, ROWS), 1)
    lane = lax.broadcasted_iota(jnp.int32, (q_len, LANES), 1)
    low = lane < HDC
    sinks = sink_ref[...]
    scale = HDC ** -0.5
    group = HC // HKV

    def unit_body(u, carry):
        qrow = pl.multiple_of(u * q_len, q_len)
        prow = pl.multiple_of(u * WINDOW, WINDOW)
        mask_p = key_p >= q_idx
        if prev_from_grid:
            mask_p = mask_p & prev_valid
        mask_c = ((key_c // q_len) == u) & ((key_c % q_len) <= q_idx)
        out_tiles = [None] * (HC // 2)
        for kk in range(HKV):
            kv_cols = slice((kk // 2) * LANES, (kk // 2 + 1) * LANES)
            k_half = kk % 2
            k_sel = low if k_half == 0 else ~low
            k_prev = kp_ref[pl.ds(prow, WINDOW), kv_cols]
            v_prev = vp_ref[pl.ds(prow, WINDOW), kv_cols]
            k_cur = kc_ref[:, kv_cols]
            v_cur = vc_ref[:, kv_cols]
            q_parts = []
            sink_parts = []
            for gi in range(group):
                hq = kk * group + gi
                tile = q_ref[pl.ds(qrow, q_len), (hq // 2) * LANES:(hq // 2 + 1) * LANES]
                if hq % 2 != k_half:
                    tile = pltpu.roll(tile, HDC, axis=1)
                q_parts.append(jnp.where(k_sel, tile, 0.0))
                sink_parts.append(jnp.broadcast_to(_col(sinks, hq), (q_len, 1)))
            q4 = jnp.concatenate(q_parts, axis=0)
            sink = jnp.concatenate(sink_parts, axis=0)
            s_p = jnp.where(mask_p, _dot_nt(q4, k_prev) * scale, NEG)
            s_c = jnp.where(mask_c, _dot_nt(q4, k_cur) * scale, NEG)
            mx = jnp.maximum(jnp.maximum(jnp.max(s_p, axis=-1, keepdims=True),
                                         jnp.max(s_c, axis=-1, keepdims=True)), sink)
            p_p = jnp.exp(s_p - mx)
            p_c = jnp.exp(s_c - mx)
            den = (jnp.sum(p_p, axis=-1, keepdims=True) + jnp.sum(p_c, axis=-1, keepdims=True)
                   + jnp.exp(sink - mx))
            out4 = (_dot(p_p, v_prev) + _dot(p_c, v_cur)) / den
            for gi in range(group):
                hq = kk * group + gi
                part = out4[gi * q_len:(gi + 1) * q_len]
                if hq % 2 != k_half:
                    part = pltpu.roll(part, HDC, axis=1)
                prev = out_tiles[hq // 2]
                sel = low if hq % 2 == 0 else ~low
                out_tiles[hq // 2] = part if prev is None else jnp.where(sel, part, prev)
        for t in range(HC // 2):
            o_ref[pl.ds(qrow, q_len), t * LANES:(t + 1) * LANES] = out_tiles[t]
        return carry

    if units == 1:
        unit_body(0, 0)
    else:
        lax.fori_loop(0, units, unit_body, 0)


def _swa(proj, sinks, units, q_len, n_chunks, prev_k=None, prev_v=None):
    n = proj.shape[0]
    n_groups = n // (ROWS * n_chunks)
    kv_w = HKV * HDC
    k_blk, v_blk = (HC * HDC) // kv_w, (HC * HDC) // kv_w + 1
    prev_from_grid = prev_k is None
    cur = lambda blk: pl.BlockSpec((ROWS, kv_w), lambda i, c: (i * n_chunks + c, blk))
    if prev_from_grid:
        prev = lambda blk: pl.BlockSpec((ROWS, kv_w), lambda i, c: (i * n_chunks + jnp.maximum(c - 1, 0), blk))
        prev_specs = [prev(k_blk), prev(v_blk)]
        prev_args = [proj, proj]
    else:
        prev_specs = [pl.BlockSpec((units * WINDOW, kv_w), lambda i, c: (i, 0))] * 2
        prev_args = [prev_k, prev_v]
    sink_row = jnp.pad(sinks.astype(F32), (0, LANES - HC)).reshape(1, LANES)
    return pl.pallas_call(
        functools.partial(_swa_kernel, units=units, q_len=q_len, prev_from_grid=prev_from_grid),
        grid=(n_groups, n_chunks),
        in_specs=[pl.BlockSpec((ROWS, HC * HDC), lambda i, c: (i * n_chunks + c, 0)),
                  cur(k_blk), cur(v_blk)] + prev_specs +
                 [pl.BlockSpec((1, LANES), lambda i, c: (0, 0))],
        out_specs=pl.BlockSpec((ROWS, HC * HDC), lambda i, c: (i * n_chunks + c, 0)),
        out_shape=jax.ShapeDtypeStruct((n, HC * HDC), F32),
        compiler_params=_params(("parallel", "arbitrary")),
        name="swa_mixer",
    )(proj, proj, proj, *prev_args, sink_row)


def _trunk(x3, st, w, is_prompt):
    bsz, t_in, _ = x3.shape
    if is_prompt:
        t, t_valid, gb, seg_len = t_in, ROWS, 1, ROWS
        x = x3.reshape(bsz * t, D_MODEL)
    else:
        t, t_valid, gb, seg_len = SAMPLE_T_PAD, t_in, ROWS // SAMPLE_T_PAD, SAMPLE_T_PAD
        x = jnp.pad(x3, ((0, 0), (0, t - t_in), (0, 0))).reshape(bsz * t, D_MODEL)
    n_chunks = (gb * t) // ROWS
    new = {'a_c': [], 'a_n': [], 'a_m': [], 'b_s': [], 'c_k': [], 'c_v': [], 'f': []}

    for i in range(DEPTH):
        kind, j = i % N_MIXERS, i // N_MIXERS
        if kind == 0:
            proj, gates = _norm_matmul(x, w['norm_mix'][i], w['a_w_in'][j], 2 * HA * (DKA + DVA),
                                       w_extra=w['a_w_gates'][j], tn=1024)
            y, c1, n1, m1 = _mlstm(proj, gates, w['a_b_i'][j], w['a_b_f'][j], w['a_norm'][j],
                                   st['a_c'][j], st['a_n'][j], st['a_m'][j], gb, seg_len, t_valid, n_chunks)
            new['a_c'].append(c1)
            new['a_n'].append(n1)
            new['a_m'].append(m1)
            x = _matmul_residual(y, w['a_w_out'][j], x)
        elif kind == 1:
            proj, z = _norm_matmul(x, w['norm_mix'][i], w['b_w_in'][j], 2 * HB * (DKB + DVB),
                                   w_extra=w['b_w_z'][j], tn=1024)
            y, s1 = _gla(proj, z, w['b_w_gate_up'][j], w['b_b_gate'][j], w['b_norm'][j], st['b_s'][j],
                         gb, seg_len, t_valid, n_chunks, heads=HB if is_prompt else 1)
            new['b_s'].append(s1)
            x = _matmul_residual(y, w['b_w_out'][j], x)
        else:
            proj = _norm_matmul(x, w['norm_mix'][i], w['c_w_in'][j], (HC + 2 * HKV) * HDC,
                                bias=w['c_b_in'][j], tn=512)
            kv_w = HKV * HDC
            k_new = proj[:, HC * HDC:HC * HDC + kv_w].reshape(bsz, t, HKV, HDC)
            v_new = proj[:, HC * HDC + kv_w:].reshape(bsz, t, HKV, HDC)
            if is_prompt:
                y = _swa(proj, w['c_sinks'][j], 1, ROWS, n_chunks)
                new['c_k'].append(k_new[:, -WINDOW:])
                new['c_v'].append(v_new[:, -WINDOW:])
            else:
                k_buf, v_buf = st['c_k'][j], st['c_v'][j]
                y = _swa(proj, w['c_sinks'][j], gb, seg_len, n_chunks,
                         prev_k=k_buf.reshape(bsz * WINDOW, kv_w), prev_v=v_buf.reshape(bsz * WINDOW, kv_w))
                new['c_k'].append(jnp.concatenate([k_buf[:, t_valid:], k_new[:, :t_valid]], axis=1))
                new['c_v'].append(jnp.concatenate([v_buf[:, t_valid:], v_new[:, :t_valid]], axis=1))
            x = _matmul_residual(y, w['c_w_out'][j], x)

        final_g = w['norm_final'] if i == DEPTH - 1 else None
        if is_prompt:
            x, gate_tail = _ffn(x, w['norm_ffn'][i], w['f_w_up'][i], w['f_conv_w'][i], w['f_conv_b'][i],
                                w['f_w_down'][i], seq_len=t, final_g=final_g)
            new['f'].append(gate_tail[:, SUBLANES - (CONV_W - 1):])
        else:
            state = st['f'][i]
            zeros = jnp.zeros((bsz, t, D_FF), F32)
            s1 = zeros.at[:, 0].set(state[:, 1]).reshape(bsz * t, D_FF)
            s2 = zeros.at[:, 0].set(state[:, 0]).at[:, 1].set(state[:, 1]).reshape(bsz * t, D_FF)
            x, gate_all = _ffn(x, w['norm_ffn'][i], w['f_w_up'][i], w['f_conv_w'][i], w['f_conv_b'][i],
                               w['f_w_down'][i], seq_len=t, conv_prev=(s1, s2), final_g=final_g)
            g_ext = jnp.concatenate([state, gate_all.reshape(bsz, t, D_FF)[:, :t_valid]], axis=1)
            new['f'].append(g_ext[:, -(CONV_W - 1):])

    out = {name: jnp.stack(vals) for name, vals in new.items()}
    y = x.reshape(bsz, t, D_MODEL)[:, :t_in]
    return y, out


def kernel(x_prompt, x_sample, state_mlstm_c, state_mlstm_n, state_mlstm_m, state_gla, cache_swa_k, cache_swa_v, state_ffn_conv, norm_mix_g, norm_ffn_g, norm_final_g, a_w_in, a_b_i, a_b_f, a_norm_g, a_w_out, b_w_in, b_w_gate_up, b_b_gate, b_norm_g, b_w_out, c_w_in, c_b_in, c_sinks, c_w_out, f_w_up, f_conv_w, f_conv_b, f_w_down):
    n_a, n_b, n_c = a_w_in.shape[0], b_w_in.shape[0], c_w_in.shape[0]
    e_a = 2 * HA * (DKA + DVA)
    e_b = 2 * HB * (DKB + DVB)
    a_w_gates = jnp.zeros((n_a, D_MODEL, 2 * LANES), BF16)
    a_w_gates = a_w_gates.at[:, :, :HA].set(a_w_in[:, :, e_a:e_a + HA].astype(BF16))
    a_w_gates = a_w_gates.at[:, :, LANES:LANES + HA].set(a_w_in[:, :, e_a + HA:].astype(BF16))
    b_w_z = jnp.zeros((n_b, D_MODEL, LANES), BF16).at[:, :, :GLA_RANK].set(b_w_in[:, :, e_b:].astype(BF16))
    w = {'norm_mix': norm_mix_g, 'norm_ffn': norm_ffn_g, 'norm_final': norm_final_g,
         'a_w_in': a_w_in.astype(BF16), 'a_w_gates': a_w_gates, 'a_b_i': a_b_i, 'a_b_f': a_b_f,
         'a_norm': a_norm_g, 'a_w_out': a_w_out.astype(BF16),
         'b_w_in': b_w_in.astype(BF16), 'b_w_z': b_w_z, 'b_w_gate_up': b_w_gate_up, 'b_b_gate': b_b_gate,
         'b_norm': b_norm_g, 'b_w_out': b_w_out.astype(BF16),
         'c_w_in': c_w_in.astype(BF16), 'c_b_in': c_b_in, 'c_sinks': c_sinks, 'c_w_out': c_w_out.astype(BF16),
         'f_w_up': f_w_up.astype(BF16), 'f_conv_w': f_conv_w, 'f_conv_b': f_conv_b,
         'f_w_down': f_w_down.astype(BF16)}
    bp = x_prompt.shape[0]
    st_p = {'a_c': jnp.zeros((n_a, bp, HA, DKA, DVA), F32),
            'a_n': jnp.zeros((n_a, bp, HA, DKA), F32),
            'a_m': jnp.zeros((n_a, bp, HA), F32),
            'b_s': jnp.zeros((n_b, bp, HB, DKB, DVB), F32),
            'c_k': [None] * n_c, 'c_v': [None] * n_c, 'f': None}
    st_s = {'a_c': state_mlstm_c, 'a_n': state_mlstm_n, 'a_m': state_mlstm_m, 'b_s': state_gla,
            'c_k': cache_swa_k, 'c_v': cache_swa_v, 'f': state_ffn_conv}
    y_prompt, np_ = _trunk(x_prompt, st_p, w, True)
    y_sample, ns_ = _trunk(x_sample, st_s, w, False)
    return (y_prompt, y_sample,
            np_['a_c'], np_['a_n'], np_['a_m'], np_['b_s'], np_['c_k'], np_['c_v'], np_['f'],
            ns_['a_c'], ns_['a_n'], ns_['a_m'], ns_['b_s'], ns_['c_k'], ns_['c_v'], ns_['f'])
```

```python
import functools
import math

import jax
import jax.numpy as jnp
from jax import lax
from jax.experimental import pallas as pl
from jax.experimental.pallas import tpu as pltpu

F32 = jnp.float32
BF16 = jnp.bfloat16

D_MODEL = 1024
DEPTH = 4
N_MIXERS = 3
HA, DKA, DVA = 8, 64, 128
GATE_SOFTCAP = 15.0
HB, DKB, DVB = 4, 128, 256
GLA_RANK = 16
GLA_TAU = 16.0
HC, HKV, HDC = 16, 4, 64
WINDOW = 128
D_FF = 2816
CONV_W = 3
EPS = 1e-6
NEG = -1e30

ROWS = 128
LANES = 128
SUBLANES = 8
SAMPLE_T_PAD = 8
GLA_SUB = 16
FFN_CHUNK = 256
VMEM_LIMIT = 56 * 1024 * 1024


def _params(sem):
    return pltpu.CompilerParams(dimension_semantics=sem, vmem_limit_bytes=VMEM_LIMIT)


def _dot(a, b):
    return jnp.dot(a.astype(BF16), b.astype(BF16), preferred_element_type=F32)


def _dot_nt(a, b):
    return lax.dot_general(a.astype(BF16), b.astype(BF16), (((1,), (1,)), ((), ())),
                           preferred_element_type=F32)


def _dot_tn(a, b):
    return lax.dot_general(a.astype(BF16), b.astype(BF16), (((0,), (0,)), ((), ())),
                           preferred_element_type=F32)


def _sigmoid(x):
    return 1.0 / (1.0 + jnp.exp(-x))


def _log_sigmoid(x):
    return jnp.minimum(x, 0.0) - jnp.log(1.0 + jnp.exp(-jnp.abs(x)))


def _softcap(z):
    return GATE_SOFTCAP * jnp.tanh(z / GATE_SOFTCAP)


def _col(x, h):
    lane = lax.broadcasted_iota(jnp.int32, x.shape, 1)
    return jnp.sum(jnp.where(lane == h, x, 0.0), axis=-1, keepdims=True)


def _rep_rows(x, reps):
    g, c = x.shape
    if g == 1:
        return jnp.broadcast_to(x, (reps, c))
    return jnp.concatenate([jnp.broadcast_to(x[b:b + 1], (reps, c)) for b in range(g)], axis=0)


def _seg_scan(x, seg_len, op, tpos):
    s = 1
    while s < seg_len:
        shifted = pltpu.roll(x, s, axis=0)
        x = jnp.where(tpos >= s, op(x, shifted), x)
        s *= 2
    return x


def _expand_blockdiag(x, gb, seg_len):
    if gb == 1:
        return x
    seq = lax.broadcasted_iota(jnp.int32, x.shape, 0) // seg_len
    return jnp.concatenate([jnp.where(seq == b, x, 0.0) for b in range(gb)], axis=1)


def _norm_matmul_kernel(*refs, has_bias, has_extra):
    x_ref, g_ref, w_ref = refs[:3]
    pos = 3
    b_ref = None
    we_ref = None
    if has_bias:
        b_ref = refs[pos]
        pos += 1
    if has_extra:
        we_ref = refs[pos]
        pos += 1
    o_ref = refs[pos]
    pos += 1
    oe_ref = None
    if has_extra:
        oe_ref = refs[pos]
        pos += 1
    xn_ref = refs[pos]

    j = pl.program_id(1)

    @pl.when(j == 0)
    def _():
        x = x_ref[...]
        ms = jnp.mean(x * x, axis=-1, keepdims=True)
        xn = (x * lax.rsqrt(ms + EPS) * g_ref[...]).astype(BF16)
        xn_ref[...] = xn
        if has_extra:
            oe_ref[...] = jnp.dot(xn, we_ref[...], preferred_element_type=F32)

    acc = jnp.dot(xn_ref[...], w_ref[...], preferred_element_type=F32)
    if has_bias:
        acc = acc + b_ref[...]
    o_ref[...] = acc.astype(o_ref.dtype)


def _norm_matmul(x, g, w, e_main, bias=None, w_extra=None, out_dtype=F32, tm=512, tn=512):
    n, d = x.shape
    assert n % tm == 0 and e_main % tn == 0
    has_bias = bias is not None
    has_extra = w_extra is not None
    in_specs = [pl.BlockSpec((tm, d), lambda i, j: (i, 0)),
                pl.BlockSpec((1, d), lambda i, j: (0, 0)),
                pl.BlockSpec((d, tn), lambda i, j: (0, j))]
    args = [x, g.reshape(1, d), w]
    if has_bias:
        in_specs.append(pl.BlockSpec((1, tn), lambda i, j: (0, j)))
        args.append(bias.reshape(1, -1))
    out_shape = [jax.ShapeDtypeStruct((n, e_main), out_dtype)]
    out_specs = [pl.BlockSpec((tm, tn), lambda i, j: (i, j))]
    if has_extra:
        ex = w_extra.shape[1]
        in_specs.append(pl.BlockSpec((d, ex), lambda i, j: (0, 0)))
        args.append(w_extra)
        out_shape.append(jax.ShapeDtypeStruct((n, ex), F32))
        out_specs.append(pl.BlockSpec((tm, ex), lambda i, j: (i, 0)))
    outs = pl.pallas_call(
        functools.partial(_norm_matmul_kernel, has_bias=has_bias, has_extra=has_extra),
        grid=(n // tm, e_main // tn),
        in_specs=in_specs, out_specs=out_specs, out_shape=out_shape,
        scratch_shapes=[pltpu.VMEM((tm, d), BF16)],
        compiler_params=_params(("parallel", "arbitrary")),
        name="norm_matmul",
    )(*args)
    return outs if has_extra else outs[0]


def _tail_kernel(*refs, tm, seq_len, carry_mode, apply_final):
    x_ref, y_ref, wo_ref, g_ref, wu_ref, cw_ref, cb_ref, wd_ref = refs[:8]
    pos = 8
    st_ref = fg_ref = carry_ref = None
    if not carry_mode:
        st_ref = refs[pos]
        pos += 1
    if apply_final:
        fg_ref = refs[pos]
        pos += 1
    o_ref, go_ref = refs[pos], refs[pos + 1]
    if carry_mode:
        carry_ref = refs[pos + 2]

    d_ff = wd_ref.shape[0]
    x2 = x_ref[...] + jnp.dot(y_ref[...].astype(BF16), wo_ref[...], preferred_element_type=F32)
    ms = jnp.mean(x2 * x2, axis=-1, keepdims=True)
    xn = (x2 * lax.rsqrt(ms + EPS) * g_ref[...]).astype(BF16)

    if carry_mode:
        @pl.when((pl.program_id(0) * tm) % seq_len == 0)
        def _():
            carry_ref[...] = jnp.zeros_like(carry_ref)
        row = lax.broadcasted_iota(jnp.int32, (tm, FFN_CHUNK), 0)
    else:
        n_seq = tm // seq_len
        tpos = lax.broadcasted_iota(jnp.int32, (n_seq, seq_len, FFN_CHUNK), 1)

    acc = jnp.zeros((tm, x_ref.shape[1]), F32)
    for c in range(d_ff // FFN_CHUNK):
        cols = slice(c * FFN_CHUNK, (c + 1) * FFN_CHUNK)
        up_cols = slice(d_ff + c * FFN_CHUNK, d_ff + (c + 1) * FFN_CHUNK)
        gate = jnp.dot(xn, wu_ref[:, cols], preferred_element_type=F32)
        up = jnp.dot(xn, wu_ref[:, up_cols], preferred_element_type=F32)
        if carry_mode:
            prev = carry_ref[:, cols]
            p1, p2 = prev[SUBLANES - 1:SUBLANES], prev[SUBLANES - 2:SUBLANES - 1]
            g1 = jnp.where(row == 0, p1, pltpu.roll(gate, 1, axis=0))
            g2 = jnp.where(row == 0, p2, jnp.where(row == 1, p1, pltpu.roll(gate, 2, axis=0)))
            tail = gate[tm - SUBLANES:tm]
            carry_ref[:, cols] = tail
            go_ref[0, :, cols] = tail
        else:
            gate3 = gate.reshape(n_seq, seq_len, FFN_CHUNK)
            st = st_ref[:, :, cols]
            g1 = jnp.where(tpos >= 1, pltpu.roll(gate3, 1, axis=1), pltpu.roll(st, 1, axis=1))
            g2 = jnp.where(tpos >= 2, pltpu.roll(gate3, 2, axis=1), pltpu.roll(st, 2, axis=1))
            g1 = g1.reshape(tm, FFN_CHUNK)
            g2 = g2.reshape(tm, FFN_CHUNK)
            go_ref[:, cols] = gate
        cw = cw_ref[:, cols]
        conv = cb_ref[:, cols] + cw[0:1] * g2 + cw[1:2] * g1 + cw[2:3] * gate
        hidden = (conv * _sigmoid(conv) * up).astype(BF16)
        acc = acc + jnp.dot(hidden, wd_ref[cols, :], preferred_element_type=F32)

    out = x2 + acc
    if apply_final:
        ms = jnp.mean(out * out, axis=-1, keepdims=True)
        out = out * lax.rsqrt(ms + EPS) * fg_ref[...]
    o_ref[...] = out


def _layer_tail(x, y, w_out, g, w_up, conv_w, conv_b, w_down, seq_len, conv_state=None, final_g=None, tm=512):
    n, d = x.shape
    f = w_down.shape[0]
    assert n % tm == 0 and f % FFN_CHUNK == 0
    carry_mode = conv_state is None
    apply_final = final_g is not None
    resident = lambda shape: pl.BlockSpec(shape, lambda i: (0,) * len(shape), pipeline_mode=pl.Buffered(1))
    in_specs = [pl.BlockSpec((tm, d), lambda i: (i, 0)),
                pl.BlockSpec((tm, d), lambda i: (i, 0)),
                resident((d, d)), resident((1, d)), resident((d, 2 * f)),
                resident((CONV_W, f)), resident((1, f)), resident((f, d))]
    args = [x, y, w_out, g.reshape(1, d), w_up, conv_w, conv_b.reshape(1, f), w_down]
    scratch = []
    if carry_mode:
        assert seq_len % tm == 0
        go_shape = jax.ShapeDtypeStruct((n // tm, SUBLANES, f), F32)
        go_spec = pl.BlockSpec((1, SUBLANES, f), lambda i: (i, 0, 0))
        scratch.append(pltpu.VMEM((SUBLANES, f), F32))
    else:
        assert seq_len == SUBLANES and tm % seq_len == 0
        in_specs.append(pl.BlockSpec((tm // seq_len, seq_len, f), lambda i: (i, 0, 0)))
        args.append(conv_state)
        go_shape = jax.ShapeDtypeStruct((n, f), F32)
        go_spec = pl.BlockSpec((tm, f), lambda i: (i, 0))
    if apply_final:
        in_specs.append(resident((1, d)))
        args.append(final_g.reshape(1, d))
    return pl.pallas_call(
        functools.partial(_tail_kernel, tm=tm, seq_len=seq_len, carry_mode=carry_mode, apply_final=apply_final),
        grid=(n // tm,),
        in_specs=in_specs,
        out_specs=[pl.BlockSpec((tm, d), lambda i: (i, 0)), go_spec],
        out_shape=[jax.ShapeDtypeStruct((n, d), F32), go_shape],
        scratch_shapes=scratch,
        compiler_params=_params(("arbitrary",)),
        name="layer_tail",
    )(*args)


def _mlstm_kernel(q_ref, k_ref, v_ref, o_ref, gi_ref, gf_ref, bi_ref, bf_ref, ng_ref,
                  c0_ref, n0_ref, m0_ref, y_ref, c1_ref, n1_ref, m1_ref, fs_ref, ms_ref,
                  *, gb, seg_len, t_valid):
    chunk = pl.program_id(1)

    @pl.when(chunk == 0)
    def _():
        c1_ref[...] = c0_ref[...]
        n1_ref[...] = n0_ref[...]
        m1_ref[...] = m0_ref[...]

    shape = (ROWS, LANES)
    row = lax.broadcasted_iota(jnp.int32, shape, 0)
    lane = lax.broadcasted_iota(jnp.int32, shape, 1)
    tpos = row % seg_len
    low_half = lane < DKA

    ig = _softcap(gi_ref[...] + bi_ref[...])
    lf = _log_sigmoid(_softcap(gf_ref[...] + bf_ref[...]))
    if t_valid < seg_len:
        valid = tpos < t_valid
        ig = jnp.where(valid, ig, NEG)
        lf = jnp.where(valid, lf, 0.0)
    f_cum = _seg_scan(lf, seg_len, jnp.add, tpos)
    a = ig - f_cum
    cmax = _seg_scan(a, seg_len, jnp.maximum, tpos)
    m_prev_g = m1_ref[0]
    m_prev = _rep_rows(m_prev_g, seg_len)
    m_t = f_cum + jnp.maximum(m_prev, cmax)
    g = jnp.exp(jnp.minimum(f_cum + m_prev - m_t, 0.0))
    u = f_cum - m_t
    em = jnp.exp(-m_t)

    fs_ref[...] = f_cum
    ms_ref[...] = m_t
    f_last_g = fs_ref[pl.ds(seg_len - 1, gb, stride=seg_len), :]
    m_last_g = ms_ref[pl.ds(seg_len - 1, gb, stride=seg_len), :]
    f_last = _rep_rows(f_last_g, seg_len)
    m_last = _rep_rows(m_last_g, seg_len)
    w_last = jnp.exp(jnp.minimum(a + f_last - m_last, 0.0))
    g_last_g = jnp.exp(jnp.minimum(f_last_g + m_prev_g - m_last_g, 0.0))
    g_last_rows = _rep_rows(g_last_g, LANES)

    a_t = a.T
    col_i = lax.broadcasted_iota(jnp.int32, (ROWS, ROWS), 1)
    row_i = lax.broadcasted_iota(jnp.int32, (ROWS, ROWS), 0)
    causal = (col_i <= row_i) & ((col_i // seg_len) == (row_i // seg_len))
    state_row = lax.broadcasted_iota(jnp.int32, (gb * LANES, LANES), 0) % LANES
    sel = (lax.broadcasted_iota(jnp.int32, (max(gb, SUBLANES), ROWS), 1) // seg_len
           == lax.broadcasted_iota(jnp.int32, (max(gb, SUBLANES), ROWS), 0)).astype(BF16)
    lane_g = lax.broadcasted_iota(jnp.int32, (gb, LANES), 1)

    for p in range(HA // 2):
        cols = slice(p * LANES, (p + 1) * LANES)
        qp = q_ref[:, cols]
        kp = k_ref[:, cols] * (DKA ** -0.5)
        c_p = c1_ref[:, p].reshape(gb * LANES, DVA)
        n_p = n1_ref[0, p]
        n_rows = _rep_rows(n_p, seg_len)
        v_heads = []
        for half in range(2):
            h = 2 * p + half
            hcols = slice(h * DVA, (h + 1) * DVA)
            qm = jnp.where(low_half if half == 0 else ~low_half, qp, 0.0)
            s = _dot_nt(qm, kp)
            logw = a_t[h:h + 1, :] + _col(u, h)
            w = jnp.where(causal, jnp.exp(jnp.minimum(logw, 0.0)), 0.0)
            qk = s * w
            vh = v_ref[:, hcols]
            v_heads.append(vh)
            inter = _dot(_expand_blockdiag(qm, gb, seg_len), c_p)
            gcol = _col(g, h)
            num = _dot(qk, vh) + gcol * inter
            den = (jnp.sum(qk, axis=-1, keepdims=True)
                   + gcol * jnp.sum(qm * n_rows, axis=-1, keepdims=True))
            hh = num / jnp.maximum(jnp.abs(den), _col(em, h))
            msq = jnp.mean(hh * hh, axis=-1, keepdims=True)
            hn = hh * lax.rsqrt(msq + EPS) * ng_ref[:, hcols]
            y_ref[:, hcols] = (hn * _sigmoid(o_ref[:, hcols].astype(F32))).astype(y_ref.dtype)

        ha, hb = 2 * p, 2 * p + 1
        k2 = kp * jnp.where(low_half, _col(w_last, ha), _col(w_last, hb))
        kx = _expand_blockdiag(k2, gb, seg_len)
        upd = jnp.where(state_row < DKA, _dot_tn(kx, v_heads[0]), _dot_tn(kx, v_heads[1]))
        scale = jnp.where(state_row < DKA, _col(g_last_rows, ha), _col(g_last_rows, hb))
        c1_ref[:, p] = (scale * c_p + upd).reshape(gb, LANES, DVA)
        n_upd = jnp.dot(sel, k2.astype(BF16), preferred_element_type=F32)[:gb]
        n_scale = jnp.where(lane_g < DKA, _col(g_last_g, ha), _col(g_last_g, hb))
        n1_ref[0, p] = n_scale * n_p + n_upd

    m1_ref[0] = m_last_g


def _mlstm(proj, gates, b_i, b_f, norm_g, c0, n0, m0, gb, seg_len, t_valid, n_chunks):
    n = proj.shape[0]
    n_groups = n // (ROWS * n_chunks)
    bsz = n_groups * gb
    half_pairs = HA // 2
    c0r = c0.reshape(bsz, half_pairs, LANES, DVA)
    n0r = n0.reshape(n_groups, gb, half_pairs, LANES).transpose(0, 2, 1, 3)
    m0r = jnp.pad(m0, ((0, 0), (0, LANES - HA))).reshape(n_groups, gb, LANES)
    pad = lambda b: jnp.pad(b, (0, LANES - HA)).reshape(1, LANES)
    blk = lambda width, cb: pl.BlockSpec((ROWS, width), lambda i, c: (i * n_chunks + c, cb))
    y, c1, n1, m1 = pl.pallas_call(
        functools.partial(_mlstm_kernel, gb=gb, seg_len=seg_len, t_valid=t_valid),
        grid=(n_groups, n_chunks),
        in_specs=[blk(HA * DKA, 0), blk(HA * DKA, 1), blk(HA * DVA, 1), blk(HA * DVA, 2),
                  blk(LANES, 0), blk(LANES, 1),
                  pl.BlockSpec((1, LANES), lambda i, c: (0, 0)),
                  pl.BlockSpec((1, LANES), lambda i, c: (0, 0)),
                  pl.BlockSpec((1, HA * DVA), lambda i, c: (0, 0)),
                  pl.BlockSpec((gb, half_pairs, LANES, DVA), lambda i, c: (i, 0, 0, 0)),
                  pl.BlockSpec((1, half_pairs, gb, LANES), lambda i, c: (i, 0, 0, 0)),
                  pl.BlockSpec((1, gb, LANES), lambda i, c: (i, 0, 0))],
        out_specs=[blk(HA * DVA, 0),
                   pl.BlockSpec((gb, half_pairs, LANES, DVA), lambda i, c: (i, 0, 0, 0)),
                   pl.BlockSpec((1, half_pairs, gb, LANES), lambda i, c: (i, 0, 0, 0)),
                   pl.BlockSpec((1, gb, LANES), lambda i, c: (i, 0, 0))],
        out_shape=[jax.ShapeDtypeStruct((n, HA * DVA), BF16),
                   jax.ShapeDtypeStruct(c0r.shape, F32),
                   jax.ShapeDtypeStruct(n0r.shape, F32),
                   jax.ShapeDtypeStruct(m0r.shape, F32)],
        scratch_shapes=[pltpu.VMEM((ROWS, LANES), F32), pltpu.VMEM((ROWS, LANES), F32)],
        compiler_params=_params(("parallel", "arbitrary")),
        name="mlstm_mixer",
    )(proj, proj, proj, proj, gates, gates, pad(b_i), pad(b_f), norm_g.reshape(1, -1), c0r, n0r, m0r)
    c1 = c1.reshape(bsz, HA, DKA, DVA)
    n1 = n1.transpose(0, 2, 1, 3).reshape(bsz, HA, DKA)
    m1 = m1.reshape(bsz, LANES)[:, :HA]
    return y, c1, n1, m1


def _gla_kernel(q_ref, k_ref, v_ref, r_ref, z_ref, wgu_ref, bg_ref, ng_ref, s0_ref,
                y_ref, s1_ref, bc_ref, *, gb, seg_len, t_valid, heads):
    chunk = pl.program_id(2)

    @pl.when(chunk == 0)
    def _():
        s1_ref[...] = s0_ref[...]

    width = heads * DKB
    row = lax.broadcasted_iota(jnp.int32, (ROWS, width), 0)
    tpos = row % seg_len
    lg = _log_sigmoid(jnp.dot(z_ref[...].astype(BF16), wgu_ref[...], preferred_element_type=F32)
                      + bg_ref[...]) / GLA_TAU
    k_all = k_ref[...].astype(F32)
    if t_valid < seg_len:
        valid = tpos < t_valid
        lg = jnp.where(valid, lg, 0.0)
        k_all = jnp.where(valid, k_all, 0.0)
    bc = _seg_scan(lg, seg_len, jnp.add, tpos)

    n_sub = ROWS // GLA_SUB
    sub_row = lax.broadcasted_iota(jnp.int32, (GLA_SUB, ROWS), 0)
    sub_col = lax.broadcasted_iota(jnp.int32, (GLA_SUB, ROWS), 1)

    for h in range(heads):
        kcols = slice(h * DKB, (h + 1) * DKB)
        vcols = slice(h * DVB, (h + 1) * DVB)
        qh = q_ref[:, kcols].astype(F32) * (DKB ** -0.5)
        kh = k_all[:, kcols]
        vh = v_ref[:, vcols]
        bch = bc[:, kcols]
        bc_ref[...] = bch
        b_last_g = bc_ref[pl.ds(seg_len - 1, gb, stride=seg_len), :]
        b_last = _rep_rows(b_last_g, seg_len)
        s_h = s1_ref[:, h].reshape(gb * DKB, DVB)

        inter = _dot(_expand_blockdiag(qh * jnp.exp(bch), gb, seg_len), s_h)

        att_rows = []
        for blk in range(n_sub):
            r0 = blk * GLA_SUB
            q_b = qh[r0:r0 + GLA_SUB]
            k_b = kh[r0:r0 + GLA_SUB]
            bc_b = bch[r0:r0 + GLA_SUB]
            att = jnp.zeros((GLA_SUB, ROWS), F32)
            for s in range(GLA_SUB):
                decay = jnp.exp(jnp.minimum(bc_b - bc_b[s:s + 1], 0.0))
                cs = jnp.sum(q_b * k_b[s:s + 1] * decay, axis=-1, keepdims=True)
                ok = (sub_col == r0 + s) & (sub_row >= s)
                if seg_len < GLA_SUB:
                    ok = ok & ((sub_row // seg_len) == (s // seg_len))
                att = jnp.where(ok, cs, att)
            if seg_len > GLA_SUB and blk > 0:
                b_edge = bch[r0 - 1:r0]
                q_t = q_b * jnp.exp(bc_b - b_edge)
                k_t = kh * jnp.exp(jnp.minimum(b_edge - bch, 0.0))
                att = att + jnp.where(sub_col < r0, _dot_nt(q_t, k_t), 0.0)
            att_rows.append(att)
        att_full = jnp.concatenate(att_rows, axis=0)

        o = inter + _dot(att_full, vh)
        msq = jnp.mean(o * o, axis=-1, keepdims=True)
        on = o * lax.rsqrt(msq + EPS) * ng_ref[:, vcols]
        r = r_ref[:, vcols].astype(F32)
        y_ref[:, vcols] = (on * (r * _sigmoid(r))).astype(y_ref.dtype)

        k_d = kh * jnp.exp(b_last - bch)
        upd = _dot_tn(_expand_blockdiag(k_d, gb, seg_len), vh)
        e_last = jnp.exp(b_last_g)
        e_cols = jnp.concatenate(
            [jnp.broadcast_to(e_last[b:b + 1], (DKB, DKB)).T for b in range(gb)], axis=0)
        e_cols = jnp.concatenate([e_cols] * (DVB // DKB), axis=1)
        s1_ref[:, h] = (e_cols * s_h + upd).reshape(gb, DKB, DVB)


def _gla(proj, z, w_gate_up, b_gate, norm_g, s0, gb, seg_len, t_valid, n_chunks, heads):
    n = proj.shape[0]
    n_groups = n // (ROWS * n_chunks)
    n_hb = HB // heads
    wq, wv = heads * DKB, heads * DVB
    q_off, k_off = 0, (HB * DKB) // wq
    v_off, r_off = (2 * HB * DKB) // wv, (2 * HB * DKB + HB * DVB) // wv
    blk = lambda width, off: pl.BlockSpec((ROWS, width), lambda i, hb, c: (i * n_chunks + c, off + hb))
    state_spec = pl.BlockSpec((gb, heads, DKB, DVB), lambda i, hb, c: (i, hb, 0, 0))
    wgu = jnp.zeros((LANES, HB * DKB), BF16).at[:GLA_RANK].set(w_gate_up.astype(BF16))
    y, s1 = pl.pallas_call(
        functools.partial(_gla_kernel, gb=gb, seg_len=seg_len, t_valid=t_valid, heads=heads),
        grid=(n_groups, n_hb, n_chunks),
        in_specs=[blk(wq, q_off), blk(wq, k_off), blk(wv, v_off), blk(wv, r_off),
                  pl.BlockSpec((ROWS, LANES), lambda i, hb, c: (i * n_chunks + c, 0)),
                  pl.BlockSpec((LANES, wq), lambda i, hb, c: (0, hb)),
                  pl.BlockSpec((1, wq), lambda i, hb, c: (0, hb)),
                  pl.BlockSpec((1, wv), lambda i, hb, c: (0, hb)),
                  state_spec],
        out_specs=[blk(wv, 0), state_spec],
        out_shape=[jax.ShapeDtypeStruct((n, HB * DVB), BF16),
                   jax.ShapeDtypeStruct(s0.shape, F32)],
        scratch_shapes=[pltpu.VMEM((ROWS, DKB), F32)],
        compiler_params=_params(("parallel", "parallel", "arbitrary")),
        name="gla_mixer",
    )(proj, proj, proj, proj, z, wgu, b_gate.reshape(1, -1), norm_g.reshape(1, -1), s0)
    return y, s1


def _swa_kernel(q_ref, kc_ref, vc_ref, kp_ref, vp_ref, sink_ref, o_ref, *, units, q_len, prev_from_grid):
    if prev_from_grid:
        prev_valid = pl.program_id(1) > 0
    rows4 = (HC // HKV) * q_len
    q_idx = lax.broadcasted_iota(jnp.int32, (rows4, WINDOW), 0) % q_len
    key_p = lax.broadcasted_iota(jnp.int32, (rows4, WINDOW), 1)
    key_c = lax.broadcasted_iota(jnp.int32, (rows4, ROWS), 1)
    lane = lax.broadcasted_iota(jnp.int32, (q_len, LANES), 1)
    low = lane < HDC
    sinks = sink_ref[...]
    scale = HDC ** -0.5
    group = HC // HKV

    def unit_body(u, carry):
        qrow = pl.multiple_of(u * q_len, q_len)
        prow = pl.multiple_of(u * WINDOW, WINDOW)
        mask_p = key_p >= q_idx
        if prev_from_grid:
            mask_p = mask_p & prev_valid
        mask_c = ((key_c // q_len) == u) & ((key_c % q_len) <= q_idx)
        out_tiles = [None] * (HC // 2)
        for kk in range(HKV):
            kv_cols = slice((kk // 2) * LANES, (kk // 2 + 1) * LANES)
            k_half = kk % 2
            k_sel = low if k_half == 0 else ~low
            k_prev = kp_ref[pl.ds(prow, WINDOW), kv_cols]
            v_prev = vp_ref[pl.ds(prow, WINDOW), kv_cols]
            k_cur = kc_ref[:, kv_cols]
            v_cur = vc_ref[:, kv_cols]
            q_parts = []
            sink_parts = []
            for gi in range(group):
                hq = kk * group + gi
                tile = q_ref[pl.ds(qrow, q_len), (hq // 2) * LANES:(hq // 2 + 1) * LANES]
                if hq % 2 != k_half:
                    tile = pltpu.roll(tile, HDC, axis=1)
                q_parts.append(jnp.where(k_sel, tile, 0.0))
                sink_parts.append(jnp.broadcast_to(_col(sinks, hq), (q_len, 1)))
            q4 = jnp.concatenate(q_parts, axis=0)
            sink = jnp.concatenate(sink_parts, axis=0)
            s_p = jnp.where(mask_p, _dot_nt(q4, k_prev) * scale, NEG)
            s_c = jnp.where(mask_c, _dot_nt(q4, k_cur) * scale, NEG)
            mx = jnp.maximum(jnp.maximum(jnp.max(s_p, axis=-1, keepdims=True),
                                         jnp.max(s_c, axis=-1, keepdims=True)), sink)
            p_p = jnp.exp(s_p - mx)
            p_c = jnp.exp(s_c - mx)
            den = (jnp.sum(p_p, axis=-1, keepdims=True) + jnp.sum(p_c, axis=-1, keepdims=True)
                   + jnp.exp(sink - mx))
            out4 = (_dot(p_p, v_prev) + _dot(p_c, v_cur)) / den
            for gi in range(group):
                hq = kk * group + gi
                part = out4[gi * q_len:(gi + 1) * q_len]
                if hq % 2 != k_half:
                    part = pltpu.roll(part, HDC, axis=1)
                prev = out_tiles[hq // 2]
                sel = low if hq % 2 == 0 else ~low
                out_tiles[hq // 2] = part if prev is None else jnp.where(sel, part, prev)
        for t in range(HC // 2):
            o_ref[pl.ds(qrow, q_len), t * LANES:(t + 1) * LANES] = out_tiles[t]
        return carry

    if units == 1:
        unit_body(0, 0)
    else:
        lax.fori_loop(0, units, unit_body, 0)


def _swa(proj, sinks, units, q_len, n_chunks, prev_k=None, prev_v=None):
    n = proj.shape[0]
    n_groups = n // (ROWS * n_chunks)
    kv_w = HKV * HDC
    k_blk, v_blk = (HC * HDC) // kv_w, (HC * HDC) // kv_w + 1
    prev_from_grid = prev_k is None
    cur = lambda blk: pl.BlockSpec((ROWS, kv_w), lambda i, c: (i * n_chunks + c, blk))
    if prev_from_grid:
        prev = lambda blk: pl.BlockSpec((ROWS, kv_w), lambda i, c: (i * n_chunks + jnp.maximum(c - 1, 0), blk))
        prev_specs = [prev(k_blk), prev(v_blk)]
        prev_args = [proj, proj]
    else:
        prev_specs = [pl.BlockSpec((units * WINDOW, kv_w), lambda i, c: (i, 0))] * 2
        prev_args = [prev_k, prev_v]
    sink_row = jnp.pad(sinks.astype(F32), (0, LANES - HC)).reshape(1, LANES)
    return pl.pallas_call(
        functools.partial(_swa_kernel, units=units, q_len=q_len, prev_from_grid=prev_from_grid),
        grid=(n_groups, n_chunks),
        in_specs=[pl.BlockSpec((ROWS, HC * HDC), lambda i, c: (i * n_chunks + c, 0)),
                  cur(k_blk), cur(v_blk)] + prev_specs +
                 [pl.BlockSpec((1, LANES), lambda i, c: (0, 0))],
        out_specs=pl.BlockSpec((ROWS, HC * HDC), lambda i, c: (i * n_chunks + c, 0)),
        out_shape=jax.ShapeDtypeStruct((n, HC * HDC), F32),
        compiler_params=_params(("parallel", "arbitrary")),
        name="swa_mixer",
    )(proj, proj, proj, *prev_args, sink_row)


def _trunk(x3, st, w, is_prompt):
    bsz, t_in, _ = x3.shape
    if is_prompt:
        t, t_valid, gb, seg_len = t_in, ROWS, 1, ROWS
        x = x3.reshape(bsz * t, D_MODEL)
    else:
        t, t_valid, gb, seg_len = SAMPLE_T_PAD, t_in, ROWS // SAMPLE_T_PAD, SAMPLE_T_PAD
        x = jnp.pad(x3, ((0, 0), (0, t - t_in), (0, 0))).reshape(bsz * t, D_MODEL)
    n_chunks = (gb * t) // ROWS
    new = {'a_c': [], 'a_n': [], 'a_m': [], 'b_s': [], 'c_k': [], 'c_v': [], 'f': []}

    for i in range(DEPTH):
        kind, j = i % N_MIXERS, i // N_MIXERS
        if kind == 0:
            proj, gates = _norm_matmul(x, w['norm_mix'][i], w['a_w_in'][j], 2 * HA * (DKA + DVA),
                                       w_extra=w['a_w_gates'][j], out_dtype=BF16, tn=1024)
            y, c1, n1, m1 = _mlstm(proj, gates, w['a_b_i'][j], w['a_b_f'][j], w['a_norm'][j],
                                   st['a_c'][j], st['a_n'][j], st['a_m'][j], gb, seg_len, t_valid, n_chunks)
            new['a_c'].append(c1)
            new['a_n'].append(n1)
            new['a_m'].append(m1)
            w_out = w['a_w_out'][j]
        elif kind == 1:
            proj, z = _norm_matmul(x, w['norm_mix'][i], w['b_w_in'][j], 2 * HB * (DKB + DVB),
                                   w_extra=w['b_w_z'][j], out_dtype=BF16, tn=1024)
            y, s1 = _gla(proj, z, w['b_w_gate_up'][j], w['b_b_gate'][j], w['b_norm'][j], st['b_s'][j],
                         gb, seg_len, t_valid, n_chunks, heads=HB if is_prompt else 1)
            new['b_s'].append(s1)
            w_out = w['b_w_out'][j]
        else:
            proj = _norm_matmul(x, w['norm_mix'][i], w['c_w_in'][j], (HC + 2 * HKV) * HDC,
                                bias=w['c_b_in'][j], tn=512)
            kv_w = HKV * HDC
            k_new = proj[:, HC * HDC:HC * HDC + kv_w].reshape(bsz, t, HKV, HDC)
            v_new = proj[:, HC * HDC + kv_w:].reshape(bsz, t, HKV, HDC)
            if is_prompt:
                y = _swa(proj, w['c_sinks'][j], 1, ROWS, n_chunks)
                new['c_k'].append(k_new[:, -WINDOW:])
                new['c_v'].append(v_new[:, -WINDOW:])
            else:
                k_buf, v_buf = st['c_k'][j], st['c_v'][j]
                y = _swa(proj, w['c_sinks'][j], gb, seg_len, n_chunks,
                         prev_k=k_buf.reshape(bsz * WINDOW, kv_w), prev_v=v_buf.reshape(bsz * WINDOW, kv_w))
                new['c_k'].append(jnp.concatenate([k_buf[:, t_valid:], k_new[:, :t_valid]], axis=1))
                new['c_v'].append(jnp.concatenate([v_buf[:, t_valid:], v_new[:, :t_valid]], axis=1))
            w_out = w['c_w_out'][j]

        final_g = w['norm_final'] if i == DEPTH - 1 else None
        tail_args = (x, y, w_out, w['norm_ffn'][i], w['f_w_up'][i], w['f_conv_w'][i], w['f_conv_b'][i],
                     w['f_w_down'][i])
        if is_prompt:
            tm = 512
            x, gate_tail = _layer_tail(*tail_args, seq_len=t, final_g=final_g, tm=tm)
            seq_tails = gate_tail.reshape(bsz, t // tm, SUBLANES, D_FF)[:, -1]
            new['f'].append(seq_tails[:, SUBLANES - (CONV_W - 1):])
        else:
            state = st['f'][i]
            state_pad = jnp.pad(state, ((0, 0), (t - (CONV_W - 1), 0), (0, 0)))
            x, gate_all = _layer_tail(*tail_args, seq_len=t, conv_state=state_pad, final_g=final_g, tm=256)
            g_ext = jnp.concatenate([state, gate_all.reshape(bsz, t, D_FF)[:, :t_valid]], axis=1)
            new['f'].append(g_ext[:, -(CONV_W - 1):])

    out = {name: jnp.stack(vals) for name, vals in new.items()}
    y = x.reshape(bsz, t, D_MODEL)[:, :t_in]
    return y, out


def kernel(x_prompt, x_sample, state_mlstm_c, state_mlstm_n, state_mlstm_m, state_gla, cache_swa_k, cache_swa_v, state_ffn_conv, norm_mix_g, norm_ffn_g, norm_final_g, a_w_in, a_b_i, a_b_f, a_norm_g, a_w_out, b_w_in, b_w_gate_up, b_b_gate, b_norm_g, b_w_out, c_w_in, c_b_in, c_sinks, c_w_out, f_w_up, f_conv_w, f_conv_b, f_w_down):
    n_a, n_b, n_c = a_w_in.shape[0], b_w_in.shape[0], c_w_in.shape[0]
    e_a = 2 * HA * (DKA + DVA)
    e_b = 2 * HB * (DKB + DVB)
    a_w_gates = jnp.zeros((n_a, D_MODEL, 2 * LANES), BF16)
    a_w_gates = a_w_gates.at[:, :, :HA].set(a_w_in[:, :, e_a:e_a + HA].astype(BF16))
    a_w_gates = a_w_gates.at[:, :, LANES:LANES + HA].set(a_w_in[:, :, e_a + HA:].astype(BF16))
    b_w_z = jnp.zeros((n_b, D_MODEL, LANES), BF16).at[:, :, :GLA_RANK].set(b_w_in[:, :, e_b:].astype(BF16))
    w = {'norm_mix': norm_mix_g, 'norm_ffn': norm_ffn_g, 'norm_final': norm_final_g,
         'a_w_in': a_w_in.astype(BF16), 'a_w_gates': a_w_gates, 'a_b_i': a_b_i, 'a_b_f': a_b_f,
         'a_norm': a_norm_g, 'a_w_out': a_w_out.astype(BF16),
         'b_w_in': b_w_in.astype(BF16), 'b_w_z': b_w_z, 'b_w_gate_up': b_w_gate_up, 'b_b_gate': b_b_gate,
         'b_norm': b_norm_g, 'b_w_out': b_w_out.astype(BF16),
         'c_w_in': c_w_in.astype(BF16), 'c_b_in': c_b_in, 'c_sinks': c_sinks, 'c_w_out': c_w_out.astype(BF16),
         'f_w_up': f_w_up.astype(BF16), 'f_conv_w': f_conv_w, 'f_conv_b': f_conv_b,
         'f_w_down': f_w_down.astype(BF16)}
    bp = x_prompt.shape[0]
    st_p = {'a_c': jnp.zeros((n_a, bp, HA, DKA, DVA), F32),
            'a_n': jnp.zeros((n_a, bp, HA, DKA), F32),
            'a_m': jnp.zeros((n_a, bp, HA), F32),
            'b_s': jnp.zeros((n_b, bp, HB, DKB, DVB), F32),
            'c_k': [None] * n_c, 'c_v': [None] * n_c, 'f': None}
    st_s = {'a_c': state_mlstm_c, 'a_n': state_mlstm_n, 'a_m': state_mlstm_m, 'b_s': state_gla,
            'c_k': cache_swa_k, 'c_v': cache_swa_v, 'f': state_ffn_conv}
    y_prompt, np_ = _trunk(x_prompt, st_p, w, True)
    y_sample, ns_ = _trunk(x_sample, st_s, w, False)
    return (y_prompt, y_sample,
            np_['a_c'], np_['a_n'], np_['a_m'], np_['b_s'], np_['c_k'], np_['c_v'], np_['f'],
            ns_['a_c'], ns_['a_n'], ns_['a_m'], ns_['b_s'], ns_['c_k'], ns_['c_v'], ns_['f'])
```

```python
import functools
import math

import jax
import jax.numpy as jnp
from jax import lax
from jax.experimental import pallas as pl
from jax.experimental.pallas import tpu as pltpu

F32 = jnp.float32
BF16 = jnp.bfloat16

D_MODEL = 1024
DEPTH = 4
N_MIXERS = 3
HA, DKA, DVA = 8, 64, 128
GATE_SOFTCAP = 15.0
HB, DKB, DVB = 4, 128, 256
GLA_RANK = 16
GLA_TAU = 16.0
HC, HKV, HDC = 16, 4, 64
WINDOW = 128
D_FF = 2816
CONV_W = 3
EPS = 1e-6
NEG = -1e30
LOG2_E = 1.4426950408889634

ROWS = 128
LANES = 128
SUBLANES = 8
SAMPLE_T_PAD = 8
GLA_SUB = 8
FFN_CHUNK = 256
PROJ_CHUNK = 512
TAIL_TM = 256
VMEM_LIMIT = 56 * 1024 * 1024


def _params(sem):
    return pltpu.CompilerParams(dimension_semantics=sem, vmem_limit_bytes=VMEM_LIMIT)


def _dot(a, b):
    return jnp.dot(a.astype(BF16), b.astype(BF16), preferred_element_type=F32)


def _dot_nt(a, b):
    return lax.dot_general(a.astype(BF16), b.astype(BF16), (((1,), (1,)), ((), ())),
                           preferred_element_type=F32)


def _dot_tn(a, b):
    return lax.dot_general(a.astype(BF16), b.astype(BF16), (((0,), (0,)), ((), ())),
                           preferred_element_type=F32)


def _sigmoid(x):
    return 1.0 / (1.0 + jnp.exp(-x))


def _log_sigmoid(x):
    return jnp.minimum(x, 0.0) - jnp.log(1.0 + jnp.exp(-jnp.abs(x)))


def _softcap(z):
    return GATE_SOFTCAP * jnp.tanh(z / GATE_SOFTCAP)


def _col(x, h):
    lane = lax.broadcasted_iota(jnp.int32, x.shape, 1)
    return jnp.sum(jnp.where(lane == h, x, 0.0), axis=-1, keepdims=True)


def _expand_heads(x, n_heads, width, terms):
    lanes = x.shape[1]
    src = lax.broadcasted_iota(jnp.int32, (terms * lanes, n_heads * width), 0) % lanes
    dst = lax.broadcasted_iota(jnp.int32, (terms * lanes, n_heads * width), 1) // width
    pieces, rest = [], x
    for _ in range(terms):
        piece = rest.astype(BF16)
        pieces.append(piece)
        rest = rest - piece.astype(F32)
    return jnp.dot(jnp.concatenate(pieces, axis=1), (src == dst).astype(BF16), preferred_element_type=F32)


def _rep_rows(x, reps):
    g, c = x.shape
    if g == 1:
        return jnp.broadcast_to(x, (reps, c))
    return jnp.concatenate([jnp.broadcast_to(x[b:b + 1], (reps, c)) for b in range(g)], axis=0)


def _seg_scan(x, seg_len, op, tpos):
    s = 1
    while s < seg_len:
        shifted = pltpu.roll(x, s, axis=0)
        x = jnp.where(tpos >= s, op(x, shifted), x)
        s *= 2
    return x


def _seg_cumsum(x, seg_len):
    rows = x.shape[0]
    dst = lax.broadcasted_iota(jnp.int32, (rows, 3 * rows), 0)
    src = lax.broadcasted_iota(jnp.int32, (rows, 3 * rows), 1) % rows
    tri = ((src <= dst) & ((src // seg_len) == (dst // seg_len))).astype(BF16)
    pieces, rest = [], x
    for _ in range(3):
        piece = rest.astype(BF16)
        pieces.append(piece)
        rest = rest - piece.astype(F32)
    return jnp.dot(tri, jnp.concatenate(pieces, axis=0), preferred_element_type=F32)


def _expand_blockdiag(x, gb, seg_len):
    if gb == 1:
        return x
    seq = lax.broadcasted_iota(jnp.int32, x.shape, 0) // seg_len
    return jnp.concatenate([jnp.where(seq == b, x, 0.0) for b in range(gb)], axis=1)


def _norm_matmul_kernel(*refs, has_bias, has_extra):
    x_ref, g_ref, w_ref = refs[:3]
    pos = 3
    b_ref = None
    we_ref = None
    if has_bias:
        b_ref = refs[pos]
        pos += 1
    if has_extra:
        we_ref = refs[pos]
        pos += 1
    o_ref = refs[pos]
    pos += 1
    oe_ref = None
    if has_extra:
        oe_ref = refs[pos]
        pos += 1
    xn_ref = refs[pos]

    j = pl.program_id(1)

    @pl.when(j == 0)
    def _():
        x = x_ref[...]
        ms = jnp.mean(x * x, axis=-1, keepdims=True)
        xn = (x * lax.rsqrt(ms + EPS) * g_ref[...]).astype(BF16)
        xn_ref[...] = xn
        if has_extra:
            oe_ref[...] = jnp.dot(xn, we_ref[...], preferred_element_type=F32)

    acc = jnp.dot(xn_ref[...], w_ref[...], preferred_element_type=F32)
    if has_bias:
        acc = acc + b_ref[...]
    o_ref[...] = acc.astype(o_ref.dtype)


def _norm_matmul(x, g, w, e_main, bias=None, w_extra=None, out_dtype=F32, tm=512, tn=512):
    n, d = x.shape
    assert n % tm == 0 and e_main % tn == 0
    has_bias = bias is not None
    has_extra = w_extra is not None
    in_specs = [pl.BlockSpec((tm, d), lambda i, j: (i, 0)),
                pl.BlockSpec((1, d), lambda i, j: (0, 0)),
                pl.BlockSpec((d, tn), lambda i, j: (0, j))]
    args = [x, g.reshape(1, d), w]
    if has_bias:
        in_specs.append(pl.BlockSpec((1, tn), lambda i, j: (0, j)))
        args.append(bias.reshape(1, -1))
    out_shape = [jax.ShapeDtypeStruct((n, e_main), out_dtype)]
    out_specs = [pl.BlockSpec((tm, tn), lambda i, j: (i, j))]
    if has_extra:
        ex = w_extra.shape[1]
        in_specs.append(pl.BlockSpec((d, ex), lambda i, j: (0, 0)))
        args.append(w_extra)
        out_shape.append(jax.ShapeDtypeStruct((n, ex), F32))
        out_specs.append(pl.BlockSpec((tm, ex), lambda i, j: (i, 0)))
    outs = pl.pallas_call(
        functools.partial(_norm_matmul_kernel, has_bias=has_bias, has_extra=has_extra),
        grid=(n // tm, e_main // tn),
        in_specs=in_specs, out_specs=out_specs, out_shape=out_shape,
        scratch_shapes=[pltpu.VMEM((tm, d), BF16)],
        compiler_params=_params(("parallel", "arbitrary")),
        name="norm_matmul",
    )(*args)
    return outs if has_extra else outs[0]


def _tail_kernel(*refs, tm, seq_len, carry_mode, apply_final, proj_main, proj_bias, proj_extra):
    refs = list(refs)
    take = lambda cond=True: refs.pop(0) if cond else None
    x_ref, y_ref, wo_ref, g_ref, wu_ref, cw_ref, cb_ref, wd_ref = [take() for _ in range(8)]
    st_ref = take(not carry_mode)
    fg_ref = take(apply_final)
    has_proj = proj_main > 0
    gn_ref, wn_ref = take(has_proj), take(has_proj)
    bn_ref, wne_ref = take(proj_bias), take(proj_extra)
    o_ref, go_ref = take(), take()
    p_ref, pe_ref = take(has_proj), take(proj_extra)
    h_ref = take()
    carry_ref = take(carry_mode)

    d_ff = wd_ref.shape[0]
    x2 = x_ref[...] + jnp.dot(y_ref[...].astype(BF16), wo_ref[...], preferred_element_type=F32)
    ms = jnp.mean(x2 * x2, axis=-1, keepdims=True)
    xn = (x2 * lax.rsqrt(ms + EPS) * g_ref[...]).astype(BF16)

    if carry_mode:
        @pl.when((pl.program_id(0) * tm) % seq_len == 0)
        def _():
            carry_ref[...] = jnp.zeros_like(carry_ref)
        row = lax.broadcasted_iota(jnp.int32, (tm, FFN_CHUNK), 0)
    else:
        n_seq = tm // seq_len
        tpos = lax.broadcasted_iota(jnp.int32, (n_seq, seq_len, FFN_CHUNK), 1)

    for c in range(d_ff // FFN_CHUNK):
        cols = slice(c * FFN_CHUNK, (c + 1) * FFN_CHUNK)
        up_cols = slice(d_ff + c * FFN_CHUNK, d_ff + (c + 1) * FFN_CHUNK)
        gate = jnp.dot(xn, wu_ref[:, cols], preferred_element_type=F32)
        up = jnp.dot(xn, wu_ref[:, up_cols], preferred_element_type=F32)
        if carry_mode:
            prev = carry_ref[:, cols]
            p1, p2 = prev[SUBLANES - 1:SUBLANES], prev[SUBLANES - 2:SUBLANES - 1]
            g1 = jnp.where(row == 0, p1, pltpu.roll(gate, 1, axis=0))
            g2 = jnp.where(row == 0, p2, jnp.where(row == 1, p1, pltpu.roll(gate, 2, axis=0)))
            tail = gate[tm - SUBLANES:tm]
            carry_ref[:, cols] = tail
            go_ref[0, :, cols] = tail
        else:
            gate3 = gate.reshape(n_seq, seq_len, FFN_CHUNK)
            st = st_ref[:, :, cols]
            g1 = jnp.where(tpos >= 1, pltpu.roll(gate3, 1, axis=1), pltpu.roll(st, 1, axis=1))
            g2 = jnp.where(tpos >= 2, pltpu.roll(gate3, 2, axis=1), pltpu.roll(st, 2, axis=1))
            g1 = g1.reshape(tm, FFN_CHUNK)
            g2 = g2.reshape(tm, FFN_CHUNK)
            go_ref[:, cols] = gate
        cw = cw_ref[:, cols]
        conv = cb_ref[:, cols] + cw[0:1] * g2 + cw[1:2] * g1 + cw[2:3] * gate
        h_ref[:, cols] = (conv * _sigmoid(conv) * up).astype(BF16)

    out = x2 + jnp.dot(h_ref[...], wd_ref[...], preferred_element_type=F32)
    if apply_final:
        ms = jnp.mean(out * out, axis=-1, keepdims=True)
        out = out * lax.rsqrt(ms + EPS) * fg_ref[...]
    o_ref[...] = out

    if has_proj:
        ms = jnp.mean(out * out, axis=-1, keepdims=True)
        xn2 = (out * lax.rsqrt(ms + EPS) * gn_ref[...]).astype(BF16)
        for c in range(proj_main // PROJ_CHUNK):
            cols = slice(c * PROJ_CHUNK, (c + 1) * PROJ_CHUNK)
            acc = jnp.dot(xn2, wn_ref[:, cols], preferred_element_type=F32)
            if proj_bias:
                acc = acc + bn_ref[:, cols]
            p_ref[:, cols] = acc.astype(p_ref.dtype)
        if proj_extra:
            pe_ref[...] = jnp.dot(xn2, wne_ref[...], preferred_element_type=F32)


def _layer_tail(x, y, w_out, g, w_up, conv_w, conv_b, w_down, seq_len, conv_state=None, final_g=None,
                next_proj=None, tm=256):
    n, d = x.shape
    f = w_down.shape[0]
    assert n % tm == 0 and f % FFN_CHUNK == 0
    carry_mode = conv_state is None
    apply_final = final_g is not None
    resident = lambda shape: pl.BlockSpec(shape, lambda i: (0,) * len(shape), pipeline_mode=pl.Buffered(1))
    in_specs = [pl.BlockSpec((tm, d), lambda i: (i, 0)),
                pl.BlockSpec((tm, d), lambda i: (i, 0)),
                resident((d, d)), resident((1, d)), resident((d, 2 * f)),
                resident((CONV_W, f)), resident((1, f)), resident((f, d))]
    args = [x, y, w_out, g.reshape(1, d), w_up, conv_w, conv_b.reshape(1, f), w_down]
    scratch = []
    if carry_mode:
        assert seq_len % tm == 0
        go_shape = jax.ShapeDtypeStruct((n // tm, SUBLANES, f), F32)
        go_spec = pl.BlockSpec((1, SUBLANES, f), lambda i: (i, 0, 0))
        scratch.append(pltpu.VMEM((SUBLANES, f), F32))
    else:
        assert seq_len == SUBLANES and tm % seq_len == 0
        in_specs.append(pl.BlockSpec((tm // seq_len, seq_len, f), lambda i: (i, 0, 0)))
        args.append(conv_state)
        go_shape = jax.ShapeDtypeStruct((n, f), F32)
        go_spec = pl.BlockSpec((tm, f), lambda i: (i, 0))
    if apply_final:
        in_specs.append(resident((1, d)))
        args.append(final_g.reshape(1, d))
    out_specs = [pl.BlockSpec((tm, d), lambda i: (i, 0)), go_spec]
    out_shape = [jax.ShapeDtypeStruct((n, d), F32), go_shape]
    proj_main, proj_bias, proj_extra = 0, False, False
    if next_proj is not None:
        proj_main = next_proj['e_main']
        proj_bias = next_proj['bias'] is not None
        proj_extra = next_proj['w_extra'] is not None
        assert proj_main % PROJ_CHUNK == 0
        in_specs += [resident((1, d)), resident(next_proj['w'].shape)]
        args += [next_proj['g'].reshape(1, d), next_proj['w']]
        if proj_bias:
            in_specs.append(resident((1, proj_main)))
            args.append(next_proj['bias'].reshape(1, proj_main))
        if proj_extra:
            in_specs.append(resident(next_proj['w_extra'].shape))
            args.append(next_proj['w_extra'])
        out_specs.append(pl.BlockSpec((tm, proj_main), lambda i: (i, 0)))
        out_shape.append(jax.ShapeDtypeStruct((n, proj_main), next_proj['out_dtype']))
        if proj_extra:
            ex = next_proj['w_extra'].shape[1]
            out_specs.append(pl.BlockSpec((tm, ex), lambda i: (i, 0)))
            out_shape.append(jax.ShapeDtypeStruct((n, ex), F32))
    scratch = [pltpu.VMEM((tm, f), BF16)] + scratch
    return pl.pallas_call(
        functools.partial(_tail_kernel, tm=tm, seq_len=seq_len, carry_mode=carry_mode, apply_final=apply_final,
                          proj_main=proj_main, proj_bias=proj_bias, proj_extra=proj_extra),
        grid=(n // tm,),
        in_specs=in_specs,
        out_specs=out_specs,
        out_shape=out_shape,
        scratch_shapes=scratch,
        compiler_params=_params(("arbitrary",)),
        name="layer_tail",
    )(*args)


def _mlstm_kernel(q_ref, k_ref, v_ref, o_ref, gi_ref, gf_ref, bi_ref, bf_ref, ng_ref,
                  c0_ref, n0_ref, m0_ref, y_ref, c1_ref, n1_ref, m1_ref, fs_ref, ms_ref,
                  *, gb, seg_len, t_valid):
    chunk = pl.program_id(1)

    @pl.when(chunk == 0)
    def _():
        c1_ref[...] = c0_ref[...]
        n1_ref[...] = n0_ref[...]
        m1_ref[...] = m0_ref[...]

    shape = (ROWS, LANES)
    row = lax.broadcasted_iota(jnp.int32, shape, 0)
    lane = lax.broadcasted_iota(jnp.int32, shape, 1)
    tpos = row % seg_len
    low_half = lane < DKA

    ig = _softcap(gi_ref[...] + bi_ref[...])
    lf = _log_sigmoid(_softcap(gf_ref[...] + bf_ref[...]))
    if t_valid < seg_len:
        valid = tpos < t_valid
        ig = jnp.where(valid, ig, NEG)
        lf = jnp.where(valid, lf, 0.0)
    f_cum = _seg_scan(lf, seg_len, jnp.add, tpos)
    a = ig - f_cum
    cmax = _seg_scan(a, seg_len, jnp.maximum, tpos)
    m_prev_g = m1_ref[0]
    m_prev = _rep_rows(m_prev_g, seg_len)
    m_t = f_cum + jnp.maximum(m_prev, cmax)
    g = jnp.exp(jnp.minimum(f_cum + m_prev - m_t, 0.0))
    u = f_cum - m_t
    em = jnp.exp(-m_t)

    fs_ref[...] = f_cum
    ms_ref[...] = m_t
    f_last_g = fs_ref[pl.ds(seg_len - 1, gb, stride=seg_len), :]
    m_last_g = ms_ref[pl.ds(seg_len - 1, gb, stride=seg_len), :]
    f_last = _rep_rows(f_last_g, seg_len)
    m_last = _rep_rows(m_last_g, seg_len)
    w_last = jnp.exp(jnp.minimum(a + f_last - m_last, 0.0))
    g_last_g = jnp.exp(jnp.minimum(f_last_g + m_prev_g - m_last_g, 0.0))

    a_t = a.T
    u_w = _expand_heads(u, HA, DVA, 3)
    g_w = _expand_heads(g, HA, DVA, 2)
    em_w = _expand_heads(em, HA, DVA, 2)
    wl_w = _expand_heads(w_last, HA, DKA, 2)
    gl_rows = g_last_g if gb >= SUBLANES else jnp.broadcast_to(g_last_g, (SUBLANES, LANES))
    gl_w = _expand_heads(gl_rows, HA, DKA, 2)[:gb]

    col_i = lax.broadcasted_iota(jnp.int32, (ROWS, 2 * ROWS), 1) % ROWS
    row_i = lax.broadcasted_iota(jnp.int32, (ROWS, 2 * ROWS), 0)
    causal2 = (col_i <= row_i) & ((col_i // seg_len) == (row_i // seg_len))
    state_low = lax.broadcasted_iota(jnp.int32, (gb * LANES, LANES), 0) % LANES < DKA
    flat_low = lax.broadcasted_iota(jnp.int32, (LANES, gb * LANES), 1) % LANES < DKA
    sel = (lax.broadcasted_iota(jnp.int32, (max(gb, SUBLANES), ROWS), 1) // seg_len
           == lax.broadcasted_iota(jnp.int32, (max(gb, SUBLANES), ROWS), 0)).astype(BF16)
    ones_blk = (lax.broadcasted_iota(jnp.int32, (2 * ROWS, 2 * DVA), 0) // ROWS
                == lax.broadcasted_iota(jnp.int32, (2 * ROWS, 2 * DVA), 1) // DVA).astype(BF16)
    zeros_v = jnp.zeros((ROWS, DVA), v_ref.dtype)

    q_all = q_ref[...]
    k_all = k_ref[...] * (DKA ** -0.5)
    k2_all = k_all * wl_w

    for p in range(HA // 2):
        cols = slice(p * LANES, (p + 1) * LANES)
        wide = slice(2 * p * DVA, (2 * p + 2) * DVA)
        qp, kp, k2 = q_all[:, cols], k_all[:, cols], k2_all[:, cols]
        c_p = c1_ref[:, p].reshape(gb * LANES, DVA)
        n_p = n1_ref[0, p]
        v2 = v_ref[:, wide]
        v_a, v_b = v2[:, :DVA], v2[:, DVA:]

        k_sep = jnp.concatenate([jnp.where(low_half, kp, 0.0), jnp.where(low_half, 0.0, kp)], axis=0)
        s2 = _dot_nt(qp, k_sep)
        logw2 = jnp.concatenate([a_t[2 * p:2 * p + 1], a_t[2 * p + 1:2 * p + 2]], axis=1) + u_w[:, wide]
        qk2 = (s2 * jnp.where(causal2, jnp.exp(jnp.minimum(logw2, 0.0)), 0.0)).astype(BF16)
        v_bd = jnp.concatenate([jnp.concatenate([v_a, zeros_v], axis=1),
                                jnp.concatenate([zeros_v, v_b], axis=1)], axis=0)
        num2 = jnp.dot(qk2, v_bd, preferred_element_type=F32)
        dsum2 = jnp.dot(qk2, ones_blk, preferred_element_type=F32)

        qx = _expand_blockdiag(qp, gb, seg_len)
        c_sep = jnp.concatenate([jnp.where(state_low, c_p, 0.0), jnp.where(state_low, 0.0, c_p)], axis=1)
        inter2 = _dot(qx, c_sep)
        n_flat = n_p if gb == 1 else jnp.concatenate([n_p[b:b + 1] for b in range(gb)], axis=1)
        n_b = jnp.broadcast_to(n_flat, (LANES, gb * LANES))
        n_sep = jnp.concatenate([jnp.where(flat_low, n_b, 0.0), jnp.where(flat_low, 0.0, n_b)], axis=0)
        dint2 = _dot_nt(qx, n_sep)

        g2 = g_w[:, wide]
        hh2 = (num2 + g2 * inter2) / jnp.maximum(jnp.abs(dsum2 + g2 * dint2), em_w[:, wide])
        for half in range(2):
            hcols = slice((2 * p + half) * DVA, (2 * p + half + 1) * DVA)
            hh = hh2[:, half * DVA:(half + 1) * DVA]
            msq = jnp.mean(hh * hh, axis=-1, keepdims=True)
            hn = hh * lax.rsqrt(msq + EPS) * ng_ref[:, hcols]
            y_ref[:, hcols] = (hn * _sigmoid(o_ref[:, hcols].astype(F32))).astype(y_ref.dtype)

        upd2 = _dot_tn(_expand_blockdiag(k2, gb, seg_len), v2)
        upd = jnp.where(state_low, upd2[:, :DVA], upd2[:, DVA:])
        n_scale = gl_w[:, cols]
        scale = jnp.concatenate([jnp.broadcast_to(n_scale[b:b + 1], (LANES, LANES)).T for b in range(gb)],
                                axis=0)
        c1_ref[:, p] = (scale * c_p + upd).reshape(gb, LANES, DVA)
        n_upd = jnp.dot(sel, k2.astype(BF16), preferred_element_type=F32)[:gb]
        n1_ref[0, p] = n_scale * n_p + n_upd

    m1_ref[0] = m_last_g


def _mlstm(proj, gates, b_i, b_f, norm_g, c0, n0, m0, gb, seg_len, t_valid, n_chunks):
    n = proj.shape[0]
    n_groups = n // (ROWS * n_chunks)
    bsz = n_groups * gb
    half_pairs = HA // 2
    c0r = c0.reshape(bsz, half_pairs, LANES, DVA)
    n0r = n0.reshape(n_groups, gb, half_pairs, LANES).transpose(0, 2, 1, 3)
    m0r = jnp.pad(m0, ((0, 0), (0, LANES - HA))).reshape(n_groups, gb, LANES)
    pad = lambda b: jnp.pad(b, (0, LANES - HA)).reshape(1, LANES)
    blk = lambda width, cb: pl.BlockSpec((ROWS, width), lambda i, c: (i * n_chunks + c, cb))
    y, c1, n1, m1 = pl.pallas_call(
        functools.partial(_mlstm_kernel, gb=gb, seg_len=seg_len, t_valid=t_valid),
        grid=(n_groups, n_chunks),
        in_specs=[blk(HA * DKA, 0), blk(HA * DKA, 1), blk(HA * DVA, 1), blk(HA * DVA, 2),
                  blk(LANES, 0), blk(LANES, 1),
                  pl.BlockSpec((1, LANES), lambda i, c: (0, 0)),
                  pl.BlockSpec((1, LANES), lambda i, c: (0, 0)),
                  pl.BlockSpec((1, HA * DVA), lambda i, c: (0, 0)),
                  pl.BlockSpec((gb, half_pairs, LANES, DVA), lambda i, c: (i, 0, 0, 0)),
                  pl.BlockSpec((1, half_pairs, gb, LANES), lambda i, c: (i, 0, 0, 0)),
                  pl.BlockSpec((1, gb, LANES), lambda i, c: (i, 0, 0))],
        out_specs=[blk(HA * DVA, 0),
                   pl.BlockSpec((gb, half_pairs, LANES, DVA), lambda i, c: (i, 0, 0, 0)),
                   pl.BlockSpec((1, half_pairs, gb, LANES), lambda i, c: (i, 0, 0, 0)),
                   pl.BlockSpec((1, gb, LANES), lambda i, c: (i, 0, 0))],
        out_shape=[jax.ShapeDtypeStruct((n, HA * DVA), BF16),
                   jax.ShapeDtypeStruct(c0r.shape, F32),
                   jax.ShapeDtypeStruct(n0r.shape, F32),
                   jax.ShapeDtypeStruct(m0r.shape, F32)],
        scratch_shapes=[pltpu.VMEM((ROWS, LANES), F32), pltpu.VMEM((ROWS, LANES), F32)],
        compiler_params=_params(("parallel", "arbitrary")),
        name="mlstm_mixer",
    )(proj, proj, proj, proj, gates, gates, pad(b_i), pad(b_f), norm_g.reshape(1, -1), c0r, n0r, m0r)
    c1 = c1.reshape(bsz, HA, DKA, DVA)
    n1 = n1.transpose(0, 2, 1, 3).reshape(bsz, HA, DKA)
    m1 = m1.reshape(bsz, LANES)[:, :HA]
    return y, c1, n1, m1


def _gla_kernel(q_ref, k_ref, v_ref, r_ref, z_ref, wgu_ref, bg_ref, ng_ref, s0_ref,
                y_ref, s1_ref, bc_ref, b2_ref, kf_ref, *, gb, seg_len, t_valid, heads):
    chunk = pl.program_id(2)

    @pl.when(chunk == 0)
    def _():
        s1_ref[...] = s0_ref[...]

    width = heads * DKB
    row = lax.broadcasted_iota(jnp.int32, (ROWS, width), 0)
    tpos = row % seg_len
    lg = _log_sigmoid(jnp.dot(z_ref[...].astype(BF16), wgu_ref[...], preferred_element_type=F32)
                      + bg_ref[...]) / GLA_TAU
    k_all = k_ref[...].astype(F32)
    if t_valid < seg_len:
        valid = tpos < t_valid
        lg = jnp.where(valid, lg, 0.0)
        k_all = jnp.where(valid, k_all, 0.0)
    bc = _seg_cumsum(lg, seg_len)

    n_sub = ROWS // GLA_SUB
    sub_row = lax.broadcasted_iota(jnp.int32, (GLA_SUB, ROWS), 0)
    sub_col = lax.broadcasted_iota(jnp.int32, (GLA_SUB, ROWS), 1)

    for h in range(heads):
        kcols = slice(h * DKB, (h + 1) * DKB)
        vcols = slice(h * DVB, (h + 1) * DVB)
        qh = q_ref[:, kcols].astype(F32) * (DKB ** -0.5)
        kh = k_all[:, kcols]
        vh = v_ref[:, vcols]
        bch = bc[:, kcols]
        bc_ref[...] = bch
        b2_ref[...] = bch * LOG2_E
        kf_ref[...] = kh
        b_last_g = bc_ref[pl.ds(seg_len - 1, gb, stride=seg_len), :]
        b_last = _rep_rows(b_last_g, seg_len)
        s_h = s1_ref[:, h].reshape(gb * DKB, DVB)

        inter = _dot(_expand_blockdiag(qh * jnp.exp(bch), gb, seg_len), s_h)

        att_rows = []
        k_decayed = []
        for blk in range(n_sub):
            r0 = blk * GLA_SUB
            q_b = qh[r0:r0 + GLA_SUB]
            bc_b = bch[r0:r0 + GLA_SUB]
            b2_b = b2_ref[r0:r0 + GLA_SUB, :]
            att = jnp.zeros((GLA_SUB, ROWS), F32)
            for s in range(GLA_SUB):
                k_row = jnp.broadcast_to(kf_ref[r0 + s:r0 + s + 1, :], (GLA_SUB, DKB))
                b2_row = jnp.broadcast_to(b2_ref[r0 + s:r0 + s + 1, :], (GLA_SUB, DKB))
                cs = jnp.sum(q_b * k_row * jnp.exp2(b2_b - b2_row), axis=-1, keepdims=True)
                att = jnp.where(sub_col == r0 + s, cs, att)
            att = jnp.where(sub_col <= sub_row + r0, att, 0.0)
            if seg_len > GLA_SUB and blk > 0:
                b_edge = bch[r0 - 1:r0]
                if blk > 1:
                    step = jnp.exp(b_edge - bch[r0 - GLA_SUB - 1:r0 - GLA_SUB])
                    k_decayed = [kb * step for kb in k_decayed]
                k_decayed.append(kh[r0 - GLA_SUB:r0] * jnp.exp(b_edge - bch[r0 - GLA_SUB:r0]))
                k_t = jnp.concatenate(k_decayed + [jnp.zeros((ROWS - r0, DKB), F32)], axis=0)
                att = att + _dot_nt(q_b * jnp.exp(bc_b - b_edge), k_t)
            att_rows.append(att)
        att_full = jnp.concatenate(att_rows, axis=0)

        o = inter + _dot(att_full, vh)
        msq = jnp.mean(o * o, axis=-1, keepdims=True)
        on = o * lax.rsqrt(msq + EPS) * ng_ref[:, vcols]
        r = r_ref[:, vcols].astype(F32)
        y_ref[:, vcols] = (on * (r * _sigmoid(r))).astype(y_ref.dtype)

        k_d = kh * jnp.exp(b_last - bch)
        upd = _dot_tn(_expand_blockdiag(k_d, gb, seg_len), vh)
        e_last = jnp.exp(b_last_g)
        e_cols = jnp.concatenate(
            [jnp.broadcast_to(e_last[b:b + 1], (DKB, DKB)).T for b in range(gb)], axis=0)
        e_cols = jnp.concatenate([e_cols] * (DVB // DKB), axis=1)
        s1_ref[:, h] = (e_cols * s_h + upd).reshape(gb, DKB, DVB)


def _gla(proj, z, w_gate_up, b_gate, norm_g, s0, gb, seg_len, t_valid, n_chunks, heads):
    n = proj.shape[0]
    n_groups = n // (ROWS * n_chunks)
    n_hb = HB // heads
    wq, wv = heads * DKB, heads * DVB
    q_off, k_off = 0, (HB * DKB) // wq
    v_off, r_off = (2 * HB * DKB) // wv, (2 * HB * DKB + HB * DVB) // wv
    blk = lambda width, off: pl.BlockSpec((ROWS, width), lambda i, hb, c: (i * n_chunks + c, off + hb))
    state_spec = pl.BlockSpec((gb, heads, DKB, DVB), lambda i, hb, c: (i, hb, 0, 0))
    wgu = jnp.zeros((LANES, HB * DKB), BF16).at[:GLA_RANK].set(w_gate_up.astype(BF16))
    y, s1 = pl.pallas_call(
        functools.partial(_gla_kernel, gb=gb, seg_len=seg_len, t_valid=t_valid, heads=heads),
        grid=(n_groups, n_hb, n_chunks),
        in_specs=[blk(wq, q_off), blk(wq, k_off), blk(wv, v_off), blk(wv, r_off),
                  pl.BlockSpec((ROWS, LANES), lambda i, hb, c: (i * n_chunks + c, 0)),
                  pl.BlockSpec((LANES, wq), lambda i, hb, c: (0, hb)),
                  pl.BlockSpec((1, wq), lambda i, hb, c: (0, hb)),
                  pl.BlockSpec((1, wv), lambda i, hb, c: (0, hb)),
                  state_spec],
        out_specs=[blk(wv, 0), state_spec],
        out_shape=[jax.ShapeDtypeStruct((n, HB * DVB), BF16),
                   jax.ShapeDtypeStruct(s0.shape, F32)],
        scratch_shapes=[pltpu.VMEM((ROWS, DKB), F32)] * 3,
        compiler_params=_params(("parallel", "parallel", "arbitrary")),
        name="gla_mixer",
    )(proj, proj, proj, proj, z, wgu, b_gate.reshape(1, -1), norm_g.reshape(1, -1), s0)
    return y, s1


def _swa_kernel(q_ref, kc_ref, vc_ref, kp_ref, vp_ref, sink_ref, o_ref, *, units, q_len, prev_from_grid):
    group = HC // HKV
    rows4 = group * ROWS
    row = lax.broadcasted_iota(jnp.int32, (rows4, ROWS), 0) % ROWS
    key = lax.broadcasted_iota(jnp.int32, (rows4, ROWS), 1)
    q_pos = row % q_len
    mask_p = key >= q_pos
    if prev_from_grid:
        mask_p = mask_p & (pl.program_id(1) > 0)
    mask_c = ((key // q_len) == (row // q_len)) & ((key % q_len) <= q_pos)
    bias_p = jnp.where(mask_p, 0.0, NEG)
    bias_c = jnp.where(mask_c, 0.0, NEG)
    low = lax.broadcasted_iota(jnp.int32, (ROWS, LANES), 1) < HDC
    ones = jnp.ones((WINDOW, LANES), BF16)
    sinks = sink_ref[...]
    scale = HDC ** -0.5

    def per_unit(lhs, rhs_ref, cols, contract_rhs_rows):
        mm = _dot if contract_rhs_rows else _dot_nt
        if units == 1:
            return mm(lhs, rhs_ref[:, cols])
        pieces = [[None] * units for _ in range(group)]
        for u in range(units):
            rows_u = jnp.concatenate([lhs[gi * ROWS + u * q_len:gi * ROWS + (u + 1) * q_len]
                                      for gi in range(group)], axis=0)
            res = mm(rows_u, rhs_ref[u * WINDOW:(u + 1) * WINDOW, cols])
            for gi in range(group):
                pieces[gi][u] = res[gi * q_len:(gi + 1) * q_len]
        return jnp.concatenate([pieces[gi][u] for gi in range(group) for u in range(units)], axis=0)

    out_tiles = [None] * (HC // 2)
    for kk in range(HKV):
        kv_cols = slice((kk // 2) * LANES, (kk // 2 + 1) * LANES)
        k_half = kk % 2
        k_sel = low if k_half == 0 else ~low
        q_parts, sink_parts = [], []
        for gi in range(group):
            hq = kk * group + gi
            tile = q_ref[:, (hq // 2) * LANES:(hq // 2 + 1) * LANES] * scale
            if hq % 2 != k_half:
                tile = pltpu.roll(tile, HDC, axis=1)
            q_parts.append(jnp.where(k_sel, tile, 0.0))
            sink_parts.append(jnp.broadcast_to(_col(sinks, hq), (ROWS, LANES)))
        q4 = jnp.concatenate(q_parts, axis=0)
        sink = jnp.concatenate(sink_parts, axis=0)
        s_p = per_unit(q4, kp_ref, kv_cols, False) + bias_p
        s_c = _dot_nt(q4, kc_ref[:, kv_cols]) + bias_c
        row_max = jnp.max(jnp.maximum(s_p, s_c), axis=-1, keepdims=True)
        mx = jnp.maximum(jnp.broadcast_to(row_max, (rows4, LANES)), sink)
        p_p = jnp.exp(s_p - mx).astype(BF16)
        p_c = jnp.exp(s_c - mx).astype(BF16)
        den = (jnp.dot(p_p, ones, preferred_element_type=F32) + jnp.dot(p_c, ones, preferred_element_type=F32)
               + jnp.exp(sink - mx))
        out4 = (per_unit(p_p, vp_ref, kv_cols, True) + _dot(p_c, vc_ref[:, kv_cols])) / den
        for gi in range(group):
            hq = kk * group + gi
            part = out4[gi * ROWS:(gi + 1) * ROWS]
            if hq % 2 != k_half:
                part = pltpu.roll(part, HDC, axis=1)
            prev = out_tiles[hq // 2]
            out_tiles[hq // 2] = part if prev is None else jnp.where(low if hq % 2 == 0 else ~low, part, prev)
    for t in range(HC // 2):
        o_ref[:, t * LANES:(t + 1) * LANES] = out_tiles[t].astype(o_ref.dtype)


def _swa(proj, sinks, units, q_len, n_chunks, prev_k=None, prev_v=None):
    n = proj.shape[0]
    n_groups = n // (ROWS * n_chunks)
    kv_w = HKV * HDC
    k_blk, v_blk = (HC * HDC) // kv_w, (HC * HDC) // kv_w + 1
    prev_from_grid = prev_k is None
    cur = lambda blk: pl.BlockSpec((ROWS, kv_w), lambda i, c: (i * n_chunks + c, blk))
    if prev_from_grid:
        prev = lambda blk: pl.BlockSpec((ROWS, kv_w), lambda i, c: (i * n_chunks + jnp.maximum(c - 1, 0), blk))
        prev_specs = [prev(k_blk), prev(v_blk)]
        prev_args = [proj, proj]
    else:
        prev_specs = [pl.BlockSpec((units * WINDOW, kv_w), lambda i, c: (i, 0))] * 2
        prev_args = [prev_k, prev_v]
    sink_row = jnp.pad(sinks.astype(F32), (0, LANES - HC)).reshape(1, LANES)
    return pl.pallas_call(
        functools.partial(_swa_kernel, units=units, q_len=q_len, prev_from_grid=prev_from_grid),
        grid=(n_groups, n_chunks),
        in_specs=[pl.BlockSpec((ROWS, HC * HDC), lambda i, c: (i * n_chunks + c, 0)),
                  cur(k_blk), cur(v_blk)] + prev_specs +
                 [pl.BlockSpec((1, LANES), lambda i, c: (0, 0))],
        out_specs=pl.BlockSpec((ROWS, HC * HDC), lambda i, c: (i * n_chunks + c, 0)),
        out_shape=jax.ShapeDtypeStruct((n, HC * HDC), BF16),
        compiler_params=_params(("parallel", "arbitrary")),
        name="swa_mixer",
    )(proj, proj, proj, *prev_args, sink_row)


def _proj_spec(w, i):
    kind, j = i % N_MIXERS, i // N_MIXERS
    spec = {'g': w['norm_mix'][i], 'bias': None, 'w_extra': None, 'out_dtype': BF16}
    if kind == 0:
        spec.update(w=w['a_w_in'][j], e_main=2 * HA * (DKA + DVA), w_extra=w['a_w_gates'][j])
    elif kind == 1:
        spec.update(w=w['b_w_in'][j], e_main=2 * HB * (DKB + DVB), w_extra=w['b_w_z'][j])
    else:
        spec.update(w=w['c_w_in'][j], e_main=(HC + 2 * HKV) * HDC, bias=w['c_b_in'][j], out_dtype=F32)
    return spec


def _trunk(x3, st, w, is_prompt):
    bsz, t_in, _ = x3.shape
    if is_prompt:
        t, t_valid, gb, seg_len = t_in, ROWS, 1, ROWS
        x = x3.reshape(bsz * t, D_MODEL)
    else:
        t, t_valid, gb, seg_len = SAMPLE_T_PAD, t_in, ROWS // SAMPLE_T_PAD, SAMPLE_T_PAD
        x = jnp.pad(x3, ((0, 0), (0, t - t_in), (0, 0))).reshape(bsz * t, D_MODEL)
    n_chunks = (gb * t) // ROWS
    new = {'a_c': [], 'a_n': [], 'a_m': [], 'b_s': [], 'c_k': [], 'c_v': [], 'f': []}

    spec0 = _proj_spec(w, 0)
    projected = _norm_matmul(x, spec0['g'], spec0['w'], spec0['e_main'], bias=spec0['bias'],
                             w_extra=spec0['w_extra'], out_dtype=spec0['out_dtype'], tn=1024)
    for i in range(DEPTH):
        kind, j = i % N_MIXERS, i // N_MIXERS
        if kind == 0:
            proj, gates = projected
            y, c1, n1, m1 = _mlstm(proj, gates, w['a_b_i'][j], w['a_b_f'][j], w['a_norm'][j],
                                   st['a_c'][j], st['a_n'][j], st['a_m'][j], gb, seg_len, t_valid, n_chunks)
            new['a_c'].append(c1)
            new['a_n'].append(n1)
            new['a_m'].append(m1)
            w_out = w['a_w_out'][j]
        elif kind == 1:
            proj, z = projected
            y, s1 = _gla(proj, z, w['b_w_gate_up'][j], w['b_b_gate'][j], w['b_norm'][j], st['b_s'][j],
                         gb, seg_len, t_valid, n_chunks, heads=HB if is_prompt else 1)
            new['b_s'].append(s1)
            w_out = w['b_w_out'][j]
        else:
            proj = projected[0]
            kv_w = HKV * HDC
            k_new = proj[:, HC * HDC:HC * HDC + kv_w].reshape(bsz, t, HKV, HDC)
            v_new = proj[:, HC * HDC + kv_w:].reshape(bsz, t, HKV, HDC)
            if is_prompt:
                y = _swa(proj, w['c_sinks'][j], 1, ROWS, n_chunks)
                new['c_k'].append(k_new[:, -WINDOW:])
                new['c_v'].append(v_new[:, -WINDOW:])
            else:
                k_buf, v_buf = st['c_k'][j], st['c_v'][j]
                y = _swa(proj, w['c_sinks'][j], gb, seg_len, n_chunks,
                         prev_k=k_buf.reshape(bsz * WINDOW, kv_w), prev_v=v_buf.reshape(bsz * WINDOW, kv_w))
                new['c_k'].append(jnp.concatenate([k_buf[:, t_valid:], k_new[:, :t_valid]], axis=1))
                new['c_v'].append(jnp.concatenate([v_buf[:, t_valid:], v_new[:, :t_valid]], axis=1))
            w_out = w['c_w_out'][j]

        last = i == DEPTH - 1
        tail_args = (x, y, w_out, w['norm_ffn'][i], w['f_w_up'][i], w['f_conv_w'][i], w['f_conv_b'][i],
                     w['f_w_down'][i])
        tail_kw = dict(seq_len=t, final_g=w['norm_final'] if last else None,
                       next_proj=None if last else _proj_spec(w, i + 1), tm=TAIL_TM)
        if is_prompt:
            x, gate_tail, *projected = _layer_tail(*tail_args, **tail_kw)
            seq_tails = gate_tail.reshape(bsz, t // TAIL_TM, SUBLANES, D_FF)[:, -1]
            new['f'].append(seq_tails[:, SUBLANES - (CONV_W - 1):])
        else:
            state = st['f'][i]
            state_pad = jnp.pad(state, ((0, 0), (t - (CONV_W - 1), 0), (0, 0)))
            x, gate_all, *projected = _layer_tail(*tail_args, conv_state=state_pad, **tail_kw)
            g_ext = jnp.concatenate([state, gate_all.reshape(bsz, t, D_FF)[:, :t_valid]], axis=1)
            new['f'].append(g_ext[:, -(CONV_W - 1):])

    out = {name: jnp.stack(vals) for name, vals in new.items()}
    y = x.reshape(bsz, t, D_MODEL)[:, :t_in]
    return y, out


def kernel(x_prompt, x_sample, state_mlstm_c, state_mlstm_n, state_mlstm_m, state_gla, cache_swa_k, cache_swa_v, state_ffn_conv, norm_mix_g, norm_ffn_g, norm_final_g, a_w_in, a_b_i, a_b_f, a_norm_g, a_w_out, b_w_in, b_w_gate_up, b_b_gate, b_norm_g, b_w_out, c_w_in, c_b_in, c_sinks, c_w_out, f_w_up, f_conv_w, f_conv_b, f_w_down):
    n_a, n_b, n_c = a_w_in.shape[0], b_w_in.shape[0], c_w_in.shape[0]
    e_a = 2 * HA * (DKA + DVA)
    e_b = 2 * HB * (DKB + DVB)
    a_w_gates = jnp.zeros((n_a, D_MODEL, 2 * LANES), BF16)
    a_w_gates = a_w_gates.at[:, :, :HA].set(a_w_in[:, :, e_a:e_a + HA].astype(BF16))
    a_w_gates = a_w_gates.at[:, :, LANES:LANES + HA].set(a_w_in[:, :, e_a + HA:].astype(BF16))
    b_w_z = jnp.zeros((n_b, D_MODEL, LANES), BF16).at[:, :, :GLA_RANK].set(b_w_in[:, :, e_b:].astype(BF16))
    w = {'norm_mix': norm_mix_g, 'norm_ffn': norm_ffn_g, 'norm_final': norm_final_g,
         'a_w_in': a_w_in.astype(BF16), 'a_w_gates': a_w_gates, 'a_b_i': a_b_i, 'a_b_f': a_b_f,
         'a_norm': a_norm_g, 'a_w_out': a_w_out.astype(BF16),
         'b_w_in': b_w_in.astype(BF16), 'b_w_z': b_w_z, 'b_w_gate_up': b_w_gate_up, 'b_b_gate': b_b_gate,
         'b_norm': b_norm_g, 'b_w_out': b_w_out.astype(BF16),
         'c_w_in': c_w_in.astype(BF16), 'c_b_in': c_b_in, 'c_sinks': c_sinks, 'c_w_out': c_w_out.astype(BF16),
         'f_w_up': f_w_up.astype(BF16), 'f_conv_w': f_conv_w, 'f_conv_b': f_conv_b,
         'f_w_down': f_w_down.astype(BF16)}
    bp = x_prompt.shape[0]
    st_p = {'a_c': jnp.zeros((n_a, bp, HA, DKA, DVA), F32),
            'a_n': jnp.zeros((n_a, bp, HA, DKA), F32),
            'a_m': jnp.zeros((n_a, bp, HA), F32),
            'b_s': jnp.zeros((n_b, bp, HB, DKB, DVB), F32),
            'c_k': [None] * n_c, 'c_v': [None] * n_c, 'f': None}
    st_s = {'a_c': state_mlstm_c, 'a_n': state_mlstm_n, 'a_m': state_mlstm_m, 'b_s': state_gla,
            'c_k': cache_swa_k, 'c_v': cache_swa_v, 'f': state_ffn_conv}
    y_prompt, np_ = _trunk(x_prompt, st_p, w, True)
    y_sample, ns_ = _trunk(x_sample, st_s, w, False)
    return (y_prompt, y_sample,
            np_['a_c'], np_['a_n'], np_['a_m'], np_['b_s'], np_['c_k'], np_['c_v'], np_['f'],
            ns_['a_c'], ns_['a_n'], ns_['a_m'], ns_['b_s'], ns_['c_k'], ns_['c_v'], ns_['f'])
```

```python
import functools
import math

import jax
import jax.numpy as jnp
from jax import lax
from jax.experimental import pallas as pl
from jax.experimental.pallas import tpu as pltpu

F32 = jnp.float32
BF16 = jnp.bfloat16

D_MODEL = 1024
DEPTH = 4
N_MIXERS = 3
HA, DKA, DVA = 8, 64, 128
GATE_SOFTCAP = 15.0
HB, DKB, DVB = 4, 128, 256
GLA_RANK = 16
GLA_TAU = 16.0
HC, HKV, HDC = 16, 4, 64
WINDOW = 128
D_FF = 2816
CONV_W = 3
EPS = 1e-6
NEG = -1e30
LOG2_E = 1.4426950408889634

ROWS = 128
LANES = 128
SUBLANES = 8
SAMPLE_T_PAD = 8
GLA_SUB = 8
FFN_CHUNK = 256
PROJ_CHUNK = 512
TAIL_TM = 256
VMEM_LIMIT = 56 * 1024 * 1024


def _params(sem):
    return pltpu.CompilerParams(dimension_semantics=sem, vmem_limit_bytes=VMEM_LIMIT)


def _dot(a, b):
    return jnp.dot(a.astype(BF16), b.astype(BF16), preferred_element_type=F32)


def _dot_nt(a, b):
    return lax.dot_general(a.astype(BF16), b.astype(BF16), (((1,), (1,)), ((), ())),
                           preferred_element_type=F32)


def _dot_tn(a, b):
    return lax.dot_general(a.astype(BF16), b.astype(BF16), (((0,), (0,)), ((), ())),
                           preferred_element_type=F32)


def _sigmoid(x):
    return 1.0 / (1.0 + jnp.exp(-x))


def _log_sigmoid(x):
    return jnp.minimum(x, 0.0) - jnp.log(1.0 + jnp.exp(-jnp.abs(x)))


def _softcap(z):
    return GATE_SOFTCAP * jnp.tanh(z / GATE_SOFTCAP)


def _col(x, h):
    lane = lax.broadcasted_iota(jnp.int32, x.shape, 1)
    return jnp.sum(jnp.where(lane == h, x, 0.0), axis=-1, keepdims=True)


def _expand_heads(x, n_heads, width, terms):
    lanes = x.shape[1]
    src = lax.broadcasted_iota(jnp.int32, (terms * lanes, n_heads * width), 0) % lanes
    dst = lax.broadcasted_iota(jnp.int32, (terms * lanes, n_heads * width), 1) // width
    pieces, rest = [], x
    for _ in range(terms):
        piece = rest.astype(BF16)
        pieces.append(piece)
        rest = rest - piece.astype(F32)
    return jnp.dot(jnp.concatenate(pieces, axis=1), (src == dst).astype(BF16), preferred_element_type=F32)


def _rep_rows(x, reps):
    g, c = x.shape
    if g == 1:
        return jnp.broadcast_to(x, (reps, c))
    return jnp.concatenate([jnp.broadcast_to(x[b:b + 1], (reps, c)) for b in range(g)], axis=0)


def _seg_scan(x, seg_len, op, tpos):
    s = 1
    while s < seg_len:
        shifted = pltpu.roll(x, s, axis=0)
        x = jnp.where(tpos >= s, op(x, shifted), x)
        s *= 2
    return x


def _seg_cumsum(x, seg_len):
    rows = x.shape[0]
    dst = lax.broadcasted_iota(jnp.int32, (rows, 3 * rows), 0)
    src = lax.broadcasted_iota(jnp.int32, (rows, 3 * rows), 1) % rows
    tri = ((src <= dst) & ((src // seg_len) == (dst // seg_len))).astype(BF16)
    pieces, rest = [], x
    for _ in range(3):
        piece = rest.astype(BF16)
        pieces.append(piece)
        rest = rest - piece.astype(F32)
    return jnp.dot(tri, jnp.concatenate(pieces, axis=0), preferred_element_type=F32)


def _expand_blockdiag(x, gb, seg_len):
    if gb == 1:
        return x
    seq = lax.broadcasted_iota(jnp.int32, x.shape, 0) // seg_len
    return jnp.concatenate([jnp.where(seq == b, x, 0.0) for b in range(gb)], axis=1)


def _resident(arr, layer=None):
    shape = arr.shape if layer is None else arr.shape[1:]
    block = shape if layer is None else (None,) + shape
    lead = () if layer is None else (layer,)
    return pl.BlockSpec(block, lambda *_: lead + (0,) * len(shape), pipeline_mode=pl.Buffered(1))


def _project(x, rows, gn_ref, wn_ref, bn_ref, wne_ref, p_ref, pe_ref, e_main):
    ms = jnp.mean(x * x, axis=-1, keepdims=True)
    xn = (x * lax.rsqrt(ms + EPS) * gn_ref[...]).astype(BF16)
    for c in range(e_main // PROJ_CHUNK):
        cols = slice(c * PROJ_CHUNK, (c + 1) * PROJ_CHUNK)
        acc = jnp.dot(xn, wn_ref[:, cols], preferred_element_type=F32)
        if bn_ref is not None:
            acc = acc + bn_ref[:, cols]
        p_ref[rows, cols] = acc.astype(p_ref.dtype)
    if wne_ref is not None:
        pe_ref[rows, :] = jnp.dot(xn, wne_ref[...], preferred_element_type=F32)


def _proj_operands(spec, n, tm):
    g, g_layer = spec['g']
    w, w_layer = spec['w']
    in_specs = [_resident(g, g_layer), _resident(w, w_layer)]
    args = [g, w]
    if spec['bias'] is not None:
        in_specs.append(_resident(*spec['bias']))
        args.append(spec['bias'][0])
    out_specs = [pl.BlockSpec((tm, spec['e_main']), lambda i: (i, 0))]
    out_shape = [jax.ShapeDtypeStruct((n, spec['e_main']), spec['out_dtype'])]
    if spec['w_extra'] is not None:
        in_specs.append(_resident(*spec['w_extra']))
        args.append(spec['w_extra'][0])
        ex = spec['w_extra'][0].shape[-1]
        out_specs.append(pl.BlockSpec((tm, ex), lambda i: (i, 0)))
        out_shape.append(jax.ShapeDtypeStruct((n, ex), F32))
    return in_specs, args, out_specs, out_shape


def _proj_kernel(*refs, e_main, has_bias, has_extra, tm):
    refs = list(refs)
    take = lambda cond=True: refs.pop(0) if cond else None
    x_ref, gn_ref, wn_ref = take(), take(), take()
    bn_ref, wne_ref = take(has_bias), take(has_extra)
    p_ref, pe_ref = take(), take(has_extra)
    for r in range(2):
        rows = slice(r * tm // 2, (r + 1) * tm // 2)
        _project(x_ref[rows, :], rows, gn_ref, wn_ref, bn_ref, wne_ref, p_ref, pe_ref, e_main)


def _norm_matmul(x, spec, tm=512):
    n, d = x.shape
    assert n % tm == 0 and spec['e_main'] % PROJ_CHUNK == 0
    in_specs, args, out_specs, out_shape = _proj_operands(spec, n, tm)
    return pl.pallas_call(
        functools.partial(_proj_kernel, e_main=spec['e_main'], has_bias=spec['bias'] is not None,
                          has_extra=spec['w_extra'] is not None, tm=tm),
        grid=(n // tm,),
        in_specs=[pl.BlockSpec((tm, d), lambda i: (i, 0))] + in_specs,
        out_specs=out_specs, out_shape=out_shape,
        compiler_params=_params(("parallel",)),
        name="norm_matmul",
    )(x, *args)


def _tail_kernel(*refs, tm, seq_len, carry_mode, apply_final, proj_main, proj_bias, proj_extra):
    refs = list(refs)
    take = lambda cond=True: refs.pop(0) if cond else None
    x_ref, y_ref, wo_ref, g_ref, wu_ref, cw_ref, cb_ref, wd_ref = [take() for _ in range(8)]
    st_ref = take(not carry_mode)
    fg_ref = take(apply_final)
    has_proj = proj_main > 0
    gn_ref, wn_ref = take(has_proj), take(has_proj)
    bn_ref, wne_ref = take(proj_bias), take(proj_extra)
    o_ref, go_ref = take(), take()
    p_ref, pe_ref = take(has_proj), take(proj_extra)
    h_ref = take()
    carry_ref = take(carry_mode)

    d_ff = wd_ref.shape[0]
    x2 = x_ref[...] + jnp.dot(y_ref[...].astype(BF16), wo_ref[...], preferred_element_type=F32)
    ms = jnp.mean(x2 * x2, axis=-1, keepdims=True)
    xn = (x2 * lax.rsqrt(ms + EPS) * g_ref[...]).astype(BF16)

    if carry_mode:
        @pl.when((pl.program_id(0) * tm) % seq_len == 0)
        def _():
            carry_ref[...] = jnp.zeros_like(carry_ref)
        row = lax.broadcasted_iota(jnp.int32, (tm, FFN_CHUNK), 0)
    else:
        n_seq = tm // seq_len
        tpos = lax.broadcasted_iota(jnp.int32, (n_seq, seq_len, FFN_CHUNK), 1)

    for c in range(d_ff // FFN_CHUNK):
        cols = slice(c * FFN_CHUNK, (c + 1) * FFN_CHUNK)
        up_cols = slice(d_ff + c * FFN_CHUNK, d_ff + (c + 1) * FFN_CHUNK)
        gate = jnp.dot(xn, wu_ref[:, cols], preferred_element_type=F32)
        up = jnp.dot(xn, wu_ref[:, up_cols], preferred_element_type=F32)
        if carry_mode:
            prev = carry_ref[:, cols]
            p1, p2 = prev[SUBLANES - 1:SUBLANES], prev[SUBLANES - 2:SUBLANES - 1]
            g1 = jnp.where(row == 0, p1, pltpu.roll(gate, 1, axis=0))
            g2 = jnp.where(row == 0, p2, jnp.where(row == 1, p1, pltpu.roll(gate, 2, axis=0)))
            tail = gate[tm - SUBLANES:tm]
            carry_ref[:, cols] = tail
            go_ref[0, :, cols] = tail
        else:
            gate3 = gate.reshape(n_seq, seq_len, FFN_CHUNK)
            st = st_ref[:, :, cols]
            g1 = jnp.where(tpos >= 1, pltpu.roll(gate3, 1, axis=1), pltpu.roll(st, 1, axis=1))
            g2 = jnp.where(tpos >= 2, pltpu.roll(gate3, 2, axis=1), pltpu.roll(st, 2, axis=1))
            g1 = g1.reshape(tm, FFN_CHUNK)
            g2 = g2.reshape(tm, FFN_CHUNK)
            go_ref[:, cols] = gate
        cw = cw_ref[:, cols]
        conv = cb_ref[:, cols] + cw[0:1] * g2 + cw[1:2] * g1 + cw[2:3] * gate
        h_ref[:, cols] = (conv * _sigmoid(conv) * up).astype(BF16)

    out = x2 + jnp.dot(h_ref[...], wd_ref[...], preferred_element_type=F32)
    if apply_final:
        ms = jnp.mean(out * out, axis=-1, keepdims=True)
        out = out * lax.rsqrt(ms + EPS) * fg_ref[...]
    o_ref[...] = out

    if has_proj:
        _project(out, slice(None), gn_ref, wn_ref, bn_ref, wne_ref, p_ref, pe_ref, proj_main)


def _layer_tail(x, y, w_out, ffn, layer, seq_len, conv_state=None, final_g=None, next_proj=None, tm=256):
    n, d = x.shape
    f = ffn['w_down'].shape[1]
    assert n % tm == 0 and f % FFN_CHUNK == 0
    carry_mode = conv_state is None
    apply_final = final_g is not None
    in_specs = [pl.BlockSpec((tm, d), lambda i: (i, 0)),
                pl.BlockSpec((tm, d), lambda i: (i, 0)),
                _resident(*w_out)]
    args = [x, y, w_out[0]]
    for name in ('g', 'w_up', 'conv_w', 'conv_b', 'w_down'):
        in_specs.append(_resident(ffn[name], layer))
        args.append(ffn[name])
    scratch = []
    if carry_mode:
        assert seq_len % tm == 0
        go_shape = jax.ShapeDtypeStruct((n // tm, SUBLANES, f), F32)
        go_spec = pl.BlockSpec((1, SUBLANES, f), lambda i: (i, 0, 0))
        scratch.append(pltpu.VMEM((SUBLANES, f), F32))
    else:
        assert seq_len == SUBLANES and tm % seq_len == 0
        in_specs.append(pl.BlockSpec((tm // seq_len, seq_len, f), lambda i: (i, 0, 0)))
        args.append(conv_state)
        go_shape = jax.ShapeDtypeStruct((n, f), F32)
        go_spec = pl.BlockSpec((tm, f), lambda i: (i, 0))
    if apply_final:
        in_specs.append(_resident(final_g))
        args.append(final_g)
    out_specs = [pl.BlockSpec((tm, d), lambda i: (i, 0)), go_spec]
    out_shape = [jax.ShapeDtypeStruct((n, d), F32), go_shape]
    proj_main, proj_bias, proj_extra = 0, False, False
    if next_proj is not None:
        proj_main = next_proj['e_main']
        proj_bias = next_proj['bias'] is not None
        proj_extra = next_proj['w_extra'] is not None
        assert proj_main % PROJ_CHUNK == 0
        p_in_specs, p_args, p_out_specs, p_out_shape = _proj_operands(next_proj, n, tm)
        in_specs += p_in_specs
        args += p_args
        out_specs += p_out_specs
        out_shape += p_out_shape
    scratch = [pltpu.VMEM((tm, f), BF16)] + scratch
    return pl.pallas_call(
        functools.partial(_tail_kernel, tm=tm, seq_len=seq_len, carry_mode=carry_mode, apply_final=apply_final,
                          proj_main=proj_main, proj_bias=proj_bias, proj_extra=proj_extra),
        grid=(n // tm,),
        in_specs=in_specs,
        out_specs=out_specs,
        out_shape=out_shape,
        scratch_shapes=scratch,
        compiler_params=_params(("arbitrary",)),
        name="layer_tail",
    )(*args)


def _mlstm_kernel(q_ref, k_ref, v_ref, o_ref, gi_ref, gf_ref, bi_ref, bf_ref, ng_ref,
                  c0_ref, n0_ref, m0_ref, y_ref, c1_ref, n1_ref, m1_ref, fs_ref, ms_ref,
                  *, gb, seg_len, t_valid):
    chunk = pl.program_id(1)

    @pl.when(chunk == 0)
    def _():
        c1_ref[...] = c0_ref[...]
        n1_ref[...] = n0_ref[...]
        m1_ref[...] = m0_ref[...]

    shape = (ROWS, LANES)
    row = lax.broadcasted_iota(jnp.int32, shape, 0)
    lane = lax.broadcasted_iota(jnp.int32, shape, 1)
    tpos = row % seg_len
    low_half = lane < DKA

    ig = _softcap(gi_ref[...] + bi_ref[...])
    lf = _log_sigmoid(_softcap(gf_ref[...] + bf_ref[...]))
    if t_valid < seg_len:
        valid = tpos < t_valid
        ig = jnp.where(valid, ig, NEG)
        lf = jnp.where(valid, lf, 0.0)
    f_cum = _seg_scan(lf, seg_len, jnp.add, tpos)
    a = ig - f_cum
    cmax = _seg_scan(a, seg_len, jnp.maximum, tpos)
    m_prev_g = m1_ref[0]
    m_prev = _rep_rows(m_prev_g, seg_len)
    m_t = f_cum + jnp.maximum(m_prev, cmax)
    g = jnp.exp(jnp.minimum(f_cum + m_prev - m_t, 0.0))
    u = f_cum - m_t
    em = jnp.exp(-m_t)

    fs_ref[...] = f_cum
    ms_ref[...] = m_t
    f_last_g = fs_ref[pl.ds(seg_len - 1, gb, stride=seg_len), :]
    m_last_g = ms_ref[pl.ds(seg_len - 1, gb, stride=seg_len), :]
    f_last = _rep_rows(f_last_g, seg_len)
    m_last = _rep_rows(m_last_g, seg_len)
    w_last = jnp.exp(jnp.minimum(a + f_last - m_last, 0.0))
    g_last_g = jnp.exp(jnp.minimum(f_last_g + m_prev_g - m_last_g, 0.0))

    a_t = a.T
    u_w = _expand_heads(u, HA, DVA, 3)
    g_w = _expand_heads(g, HA, DVA, 2)
    em_w = _expand_heads(em, HA, DVA, 2)
    wl_w = _expand_heads(w_last, HA, DKA, 2)
    gl_rows = g_last_g if gb >= SUBLANES else jnp.broadcast_to(g_last_g, (SUBLANES, LANES))
    gl_w = _expand_heads(gl_rows, HA, DKA, 2)[:gb]

    col_i = lax.broadcasted_iota(jnp.int32, (ROWS, 2 * ROWS), 1) % ROWS
    row_i = lax.broadcasted_iota(jnp.int32, (ROWS, 2 * ROWS), 0)
    causal2 = (col_i <= row_i) & ((col_i // seg_len) == (row_i // seg_len))
    state_low = lax.broadcasted_iota(jnp.int32, (gb * LANES, LANES), 0) % LANES < DKA
    flat_low = lax.broadcasted_iota(jnp.int32, (LANES, gb * LANES), 1) % LANES < DKA
    sel = (lax.broadcasted_iota(jnp.int32, (max(gb, SUBLANES), ROWS), 1) // seg_len
           == lax.broadcasted_iota(jnp.int32, (max(gb, SUBLANES), ROWS), 0)).astype(BF16)
    ones_blk = (lax.broadcasted_iota(jnp.int32, (2 * ROWS, 2 * DVA), 0) // ROWS
                == lax.broadcasted_iota(jnp.int32, (2 * ROWS, 2 * DVA), 1) // DVA).astype(BF16)
    zeros_v = jnp.zeros((ROWS, DVA), v_ref.dtype)

    q_all = q_ref[...]
    k_all = k_ref[...] * (DKA ** -0.5)
    k2_all = k_all * wl_w

    for p in range(HA // 2):
        cols = slice(p * LANES, (p + 1) * LANES)
        wide = slice(2 * p * DVA, (2 * p + 2) * DVA)
        qp, kp, k2 = q_all[:, cols], k_all[:, cols], k2_all[:, cols]
        c_p = c1_ref[:, p].reshape(gb * LANES, DVA)
        n_p = n1_ref[0, p]
        v2 = v_ref[:, wide]
        v_a, v_b = v2[:, :DVA], v2[:, DVA:]

        k_sep = jnp.concatenate([jnp.where(low_half, kp, 0.0), jnp.where(low_half, 0.0, kp)], axis=0)
        s2 = _dot_nt(qp, k_sep)
        logw2 = jnp.concatenate([a_t[2 * p:2 * p + 1], a_t[2 * p + 1:2 * p + 2]], axis=1) + u_w[:, wide]
        qk2 = (s2 * jnp.where(causal2, jnp.exp(jnp.minimum(logw2, 0.0)), 0.0)).astype(BF16)
        v_bd = jnp.concatenate([jnp.concatenate([v_a, zeros_v], axis=1),
                                jnp.concatenate([zeros_v, v_b], axis=1)], axis=0)
        num2 = jnp.dot(qk2, v_bd, preferred_element_type=F32)
        dsum2 = jnp.dot(qk2, ones_blk, preferred_element_type=F32)

        qx = _expand_blockdiag(qp, gb, seg_len)
        c_sep = jnp.concatenate([jnp.where(state_low, c_p, 0.0), jnp.where(state_low, 0.0, c_p)], axis=1)
        inter2 = _dot(qx, c_sep)
        n_flat = n_p if gb == 1 else jnp.concatenate([n_p[b:b + 1] for b in range(gb)], axis=1)
        n_b = jnp.broadcast_to(n_flat, (LANES, gb * LANES))
        n_sep = jnp.concatenate([jnp.where(flat_low, n_b, 0.0), jnp.where(flat_low, 0.0, n_b)], axis=0)
        dint2 = _dot_nt(qx, n_sep)

        g2 = g_w[:, wide]
        hh2 = (num2 + g2 * inter2) / jnp.maximum(jnp.abs(dsum2 + g2 * dint2), em_w[:, wide])
        for half in range(2):
            hcols = slice((2 * p + half) * DVA, (2 * p + half + 1) * DVA)
            hh = hh2[:, half * DVA:(half + 1) * DVA]
            msq = jnp.mean(hh * hh, axis=-1, keepdims=True)
            hn = hh * lax.rsqrt(msq + EPS) * ng_ref[:, hcols]
            y_ref[:, hcols] = (hn * _sigmoid(o_ref[:, hcols].astype(F32))).astype(y_ref.dtype)

        upd2 = _dot_tn(_expand_blockdiag(k2, gb, seg_len), v2)
        upd = jnp.where(state_low, upd2[:, :DVA], upd2[:, DVA:])
        n_scale = gl_w[:, cols]
        scale = jnp.concatenate([jnp.broadcast_to(n_scale[b:b + 1], (LANES, LANES)).T for b in range(gb)],
                                axis=0)
        c1_ref[:, p] = (scale * c_p + upd).reshape(gb, LANES, DVA)
        n_upd = jnp.dot(sel, k2.astype(BF16), preferred_element_type=F32)[:gb]
        n1_ref[0, p] = n_scale * n_p + n_upd

    m1_ref[0] = m_last_g


def _mlstm(proj, gates, b_i, b_f, norm_g, c_all, layer, n0, m0, gb, seg_len, t_valid, n_chunks):
    n = proj.shape[0]
    n_groups = n // (ROWS * n_chunks)
    bsz = n_groups * gb
    half_pairs = HA // 2
    c0r = c_all.reshape(c_all.shape[0], bsz, half_pairs, LANES, DVA)
    n0r = n0.reshape(n_groups, gb, half_pairs, LANES).transpose(0, 2, 1, 3)
    m0r = jnp.pad(m0, ((0, 0), (0, LANES - HA))).reshape(n_groups, gb, LANES)
    pad = lambda b: jnp.pad(b, (0, LANES - HA)).reshape(1, LANES)
    blk = lambda width, cb: pl.BlockSpec((ROWS, width), lambda i, c: (i * n_chunks + c, cb))
    y, c1, n1, m1 = pl.pallas_call(
        functools.partial(_mlstm_kernel, gb=gb, seg_len=seg_len, t_valid=t_valid),
        grid=(n_groups, n_chunks),
        in_specs=[blk(HA * DKA, 0), blk(HA * DKA, 1), blk(HA * DVA, 1), blk(HA * DVA, 2),
                  blk(LANES, 0), blk(LANES, 1),
                  pl.BlockSpec((1, LANES), lambda i, c: (0, 0)),
                  pl.BlockSpec((1, LANES), lambda i, c: (0, 0)),
                  pl.BlockSpec((1, HA * DVA), lambda i, c: (0, 0)),
                  pl.BlockSpec((None, gb, half_pairs, LANES, DVA), lambda i, c: (layer, i, 0, 0, 0)),
                  pl.BlockSpec((1, half_pairs, gb, LANES), lambda i, c: (i, 0, 0, 0)),
                  pl.BlockSpec((1, gb, LANES), lambda i, c: (i, 0, 0))],
        out_specs=[blk(HA * DVA, 0),
                   pl.BlockSpec((gb, half_pairs, LANES, DVA), lambda i, c: (i, 0, 0, 0)),
                   pl.BlockSpec((1, half_pairs, gb, LANES), lambda i, c: (i, 0, 0, 0)),
                   pl.BlockSpec((1, gb, LANES), lambda i, c: (i, 0, 0))],
        out_shape=[jax.ShapeDtypeStruct((n, HA * DVA), BF16),
                   jax.ShapeDtypeStruct(c0r.shape[1:], F32),
                   jax.ShapeDtypeStruct(n0r.shape, F32),
                   jax.ShapeDtypeStruct(m0r.shape, F32)],
        scratch_shapes=[pltpu.VMEM((ROWS, LANES), F32), pltpu.VMEM((ROWS, LANES), F32)],
        compiler_params=_params(("parallel", "arbitrary")),
        name="mlstm_mixer",
    )(proj, proj, proj, proj, gates, gates, pad(b_i), pad(b_f), norm_g.reshape(1, -1), c0r, n0r, m0r)
    c1 = c1.reshape(bsz, HA, DKA, DVA)
    n1 = n1.transpose(0, 2, 1, 3).reshape(bsz, HA, DKA)
    m1 = m1.reshape(bsz, LANES)[:, :HA]
    return y, c1, n1, m1


def _gla_kernel(q_ref, k_ref, v_ref, r_ref, z_ref, wgu_ref, bg_ref, ng_ref, s0_ref,
                y_ref, s1_ref, bc_ref, b2_ref, kf_ref, *, gb, seg_len, t_valid, heads):
    chunk = pl.program_id(2)

    @pl.when(chunk == 0)
    def _():
        s1_ref[...] = s0_ref[...]

    width = heads * DKB
    row = lax.broadcasted_iota(jnp.int32, (ROWS, width), 0)
    tpos = row % seg_len
    lg = _log_sigmoid(jnp.dot(z_ref[...].astype(BF16), wgu_ref[...], preferred_element_type=F32)
                      + bg_ref[...]) / GLA_TAU
    k_all = k_ref[...].astype(F32)
    if t_valid < seg_len:
        valid = tpos < t_valid
        lg = jnp.where(valid, lg, 0.0)
        k_all = jnp.where(valid, k_all, 0.0)
    bc = _seg_cumsum(lg, seg_len)

    n_sub = ROWS // GLA_SUB
    sub_row = lax.broadcasted_iota(jnp.int32, (GLA_SUB, ROWS), 0)
    sub_col = lax.broadcasted_iota(jnp.int32, (GLA_SUB, ROWS), 1)

    for h in range(heads):
        kcols = slice(h * DKB, (h + 1) * DKB)
        vcols = slice(h * DVB, (h + 1) * DVB)
        qh = q_ref[:, kcols].astype(F32) * (DKB ** -0.5)
        kh = k_all[:, kcols]
        vh = v_ref[:, vcols]
        bch = bc[:, kcols]
        bc_ref[...] = bch
        b2_ref[...] = bch * LOG2_E
        kf_ref[...] = kh
        b_last_g = bc_ref[pl.ds(seg_len - 1, gb, stride=seg_len), :]
        b_last = _rep_rows(b_last_g, seg_len)
        s_h = s1_ref[:, h].reshape(gb * DKB, DVB)

        inter = _dot(_expand_blockdiag(qh * jnp.exp(bch), gb, seg_len), s_h)

        att_rows = []
        k_decayed = []
        for blk in range(n_sub):
            r0 = blk * GLA_SUB
            q_b = qh[r0:r0 + GLA_SUB]
            bc_b = bch[r0:r0 + GLA_SUB]
            b2_b = b2_ref[r0:r0 + GLA_SUB, :]
            att = jnp.zeros((GLA_SUB, ROWS), F32)
            for s in range(GLA_SUB):
                k_row = jnp.broadcast_to(kf_ref[r0 + s:r0 + s + 1, :], (GLA_SUB, DKB))
                b2_row = jnp.broadcast_to(b2_ref[r0 + s:r0 + s + 1, :], (GLA_SUB, DKB))
                cs = jnp.sum(q_b * k_row * jnp.exp2(b2_b - b2_row), axis=-1, keepdims=True)
                att = jnp.where(sub_col == r0 + s, cs, att)
            att = jnp.where(sub_col <= sub_row + r0, att, 0.0)
            if seg_len > GLA_SUB and blk > 0:
                b_edge = bch[r0 - 1:r0]
                if blk > 1:
                    step = jnp.exp(b_edge - bch[r0 - GLA_SUB - 1:r0 - GLA_SUB])
                    k_decayed = [kb * step for kb in k_decayed]
                k_decayed.append(kh[r0 - GLA_SUB:r0] * jnp.exp(b_edge - bch[r0 - GLA_SUB:r0]))
                k_t = jnp.concatenate(k_decayed + [jnp.zeros((ROWS - r0, DKB), F32)], axis=0)
                att = att + _dot_nt(q_b * jnp.exp(bc_b - b_edge), k_t)
            att_rows.append(att)
        att_full = jnp.concatenate(att_rows, axis=0)

        o = inter + _dot(att_full, vh)
        msq = jnp.mean(o * o, axis=-1, keepdims=True)
        on = o * lax.rsqrt(msq + EPS) * ng_ref[:, vcols]
        r = r_ref[:, vcols].astype(F32)
        y_ref[:, vcols] = (on * (r * _sigmoid(r))).astype(y_ref.dtype)

        k_d = kh * jnp.exp(b_last - bch)
        upd = _dot_tn(_expand_blockdiag(k_d, gb, seg_len), vh)
        e_last = jnp.exp(b_last_g)
        e_cols = jnp.concatenate(
            [jnp.broadcast_to(e_last[b:b + 1], (DKB, DKB)).T for b in range(gb)], axis=0)
        e_cols = jnp.concatenate([e_cols] * (DVB // DKB), axis=1)
        s1_ref[:, h] = (e_cols * s_h + upd).reshape(gb, DKB, DVB)


def _gla(proj, z, w_gate_up, b_gate, norm_g, s0, gb, seg_len, t_valid, n_chunks, heads):
    n = proj.shape[0]
    n_groups = n // (ROWS * n_chunks)
    n_hb = HB // heads
    wq, wv = heads * DKB, heads * DVB
    q_off, k_off = 0, (HB * DKB) // wq
    v_off, r_off = (2 * HB * DKB) // wv, (2 * HB * DKB + HB * DVB) // wv
    blk = lambda width, off: pl.BlockSpec((ROWS, width), lambda i, hb, c: (i * n_chunks + c, off + hb))
    state_spec = pl.BlockSpec((gb, heads, DKB, DVB), lambda i, hb, c: (i, hb, 0, 0))
    wgu = jnp.zeros((LANES, HB * DKB), BF16).at[:GLA_RANK].set(w_gate_up.astype(BF16))
    y, s1 = pl.pallas_call(
        functools.partial(_gla_kernel, gb=gb, seg_len=seg_len, t_valid=t_valid, heads=heads),
        grid=(n_groups, n_hb, n_chunks),
        in_specs=[blk(wq, q_off), blk(wq, k_off), blk(wv, v_off), blk(wv, r_off),
                  pl.BlockSpec((ROWS, LANES), lambda i, hb, c: (i * n_chunks + c, 0)),
                  pl.BlockSpec((LANES, wq), lambda i, hb, c: (0, hb)),
                  pl.BlockSpec((1, wq), lambda i, hb, c: (0, hb)),
                  pl.BlockSpec((1, wv), lambda i, hb, c: (0, hb)),
                  state_spec],
        out_specs=[blk(wv, 0), state_spec],
        out_shape=[jax.ShapeDtypeStruct((n, HB * DVB), BF16),
                   jax.ShapeDtypeStruct(s0.shape, F32)],
        scratch_shapes=[pltpu.VMEM((ROWS, DKB), F32)] * 3,
        compiler_params=_params(("parallel", "parallel", "arbitrary")),
        name="gla_mixer",
    )(proj, proj, proj, proj, z, wgu, b_gate.reshape(1, -1), norm_g.reshape(1, -1), s0)
    return y, s1


def _swa_kernel(q_ref, kc_ref, vc_ref, kp_ref, vp_ref, sink_ref, o_ref, *, units, q_len, prev_from_grid):
    group = HC // HKV
    rows4 = group * ROWS
    row = lax.broadcasted_iota(jnp.int32, (rows4, ROWS), 0) % ROWS
    key = lax.broadcasted_iota(jnp.int32, (rows4, ROWS), 1)
    q_pos = row % q_len
    mask_p = key >= q_pos
    if prev_from_grid:
        mask_p = mask_p & (pl.program_id(1) > 0)
    mask_c = ((key // q_len) == (row // q_len)) & ((key % q_len) <= q_pos)
    bias_p = jnp.where(mask_p, 0.0, NEG)
    bias_c = jnp.where(mask_c, 0.0, NEG)
    low = lax.broadcasted_iota(jnp.int32, (ROWS, LANES), 1) < HDC
    ones = jnp.ones((WINDOW, LANES), BF16)
    sinks = sink_ref[...]
    scale = HDC ** -0.5

    def per_unit(lhs, rhs_ref, cols, contract_rhs_rows):
        mm = _dot if contract_rhs_rows else _dot_nt
        if units == 1:
            return mm(lhs, rhs_ref[:, cols])
        pieces = [[None] * units for _ in range(group)]
        for u in range(units):
            rows_u = jnp.concatenate([lhs[gi * ROWS + u * q_len:gi * ROWS + (u + 1) * q_len]
                                      for gi in range(group)], axis=0)
            res = mm(rows_u, rhs_ref[u * WINDOW:(u + 1) * WINDOW, cols])
            for gi in range(group):
                pieces[gi][u] = res[gi * q_len:(gi + 1) * q_len]
        return jnp.concatenate([pieces[gi][u] for gi in range(group) for u in range(units)], axis=0)

    out_tiles = [None] * (HC // 2)
    for kk in range(HKV):
        kv_cols = slice((kk // 2) * LANES, (kk // 2 + 1) * LANES)
        k_half = kk % 2
        k_sel = low if k_half == 0 else ~low
        q_parts, sink_parts = [], []
        for gi in range(group):
            hq = kk * group + gi
            tile = q_ref[:, (hq // 2) * LANES:(hq // 2 + 1) * LANES] * scale
            if hq % 2 != k_half:
                tile = pltpu.roll(tile, HDC, axis=1)
            q_parts.append(jnp.where(k_sel, tile, 0.0))
            sink_parts.append(jnp.broadcast_to(_col(sinks, hq), (ROWS, LANES)))
        q4 = jnp.concatenate(q_parts, axis=0)
        sink = jnp.concatenate(sink_parts, axis=0)
        s_p = per_unit(q4, kp_ref, kv_cols, False) + bias_p
        s_c = _dot_nt(q4, kc_ref[:, kv_cols]) + bias_c
        row_max = jnp.max(jnp.maximum(s_p, s_c), axis=-1, keepdims=True)
        mx = jnp.maximum(jnp.broadcast_to(row_max, (rows4, LANES)), sink)
        p_p = jnp.exp(s_p - mx).astype(BF16)
        p_c = jnp.exp(s_c - mx).astype(BF16)
        den = (jnp.dot(p_p, ones, preferred_element_type=F32) + jnp.dot(p_c, ones, preferred_element_type=F32)
               + jnp.exp(sink - mx))
        out4 = (per_unit(p_p, vp_ref, kv_cols, True) + _dot(p_c, vc_ref[:, kv_cols])) / den
        for gi in range(group):
            hq = kk * group + gi
            part = out4[gi * ROWS:(gi + 1) * ROWS]
            if hq % 2 != k_half:
                part = pltpu.roll(part, HDC, axis=1)
            prev = out_tiles[hq // 2]
            out_tiles[hq // 2] = part if prev is None else jnp.where(low if hq % 2 == 0 else ~low, part, prev)
    for t in range(HC // 2):
        o_ref[:, t * LANES:(t + 1) * LANES] = out_tiles[t].astype(o_ref.dtype)


def _swa(proj, sinks, units, q_len, n_chunks, prev_k=None, prev_v=None):
    n = proj.shape[0]
    n_groups = n // (ROWS * n_chunks)
    kv_w = HKV * HDC
    k_blk, v_blk = (HC * HDC) // kv_w, (HC * HDC) // kv_w + 1
    prev_from_grid = prev_k is None
    cur = lambda blk: pl.BlockSpec((ROWS, kv_w), lambda i, c: (i * n_chunks + c, blk))
    if prev_from_grid:
        prev = lambda blk: pl.BlockSpec((ROWS, kv_w), lambda i, c: (i * n_chunks + jnp.maximum(c - 1, 0), blk))
        prev_specs = [prev(k_blk), prev(v_blk)]
        prev_args = [proj, proj]
    else:
        prev_specs = [pl.BlockSpec((units * WINDOW, kv_w), lambda i, c: (i, 0))] * 2
        prev_args = [prev_k, prev_v]
    sink_row = jnp.pad(sinks.astype(F32), (0, LANES - HC)).reshape(1, LANES)
    return pl.pallas_call(
        functools.partial(_swa_kernel, units=units, q_len=q_len, prev_from_grid=prev_from_grid),
        grid=(n_groups, n_chunks),
        in_specs=[pl.BlockSpec((ROWS, HC * HDC), lambda i, c: (i * n_chunks + c, 0)),
                  cur(k_blk), cur(v_blk)] + prev_specs +
                 [pl.BlockSpec((1, LANES), lambda i, c: (0, 0))],
        out_specs=pl.BlockSpec((ROWS, HC * HDC), lambda i, c: (i * n_chunks + c, 0)),
        out_shape=jax.ShapeDtypeStruct((n, HC * HDC), BF16),
        compiler_params=_params(("parallel", "arbitrary")),
        name="swa_mixer",
    )(proj, proj, proj, *prev_args, sink_row)


def _proj_spec(w, i):
    kind, j = i % N_MIXERS, i // N_MIXERS
    spec = {'g': (w['norm_mix'], i), 'bias': None, 'w_extra': None, 'out_dtype': BF16}
    if kind == 0:
        spec.update(w=(w['a_w_in'], j), e_main=2 * HA * (DKA + DVA), w_extra=(w['a_w_gates'], j))
    elif kind == 1:
        spec.update(w=(w['b_w_in'], j), e_main=2 * HB * (DKB + DVB), w_extra=(w['b_w_z'], j))
    else:
        spec.update(w=(w['c_w_in'], j), e_main=(HC + 2 * HKV) * HDC, bias=(w['c_b_in'], j), out_dtype=F32)
    return spec


def _trunk(x3, st, w, is_prompt):
    bsz, t_in, _ = x3.shape
    if is_prompt:
        t, t_valid, gb, seg_len = t_in, ROWS, 1, ROWS
        x = x3.reshape(bsz * t, D_MODEL)
    else:
        t, t_valid, gb, seg_len = SAMPLE_T_PAD, t_in, ROWS // SAMPLE_T_PAD, SAMPLE_T_PAD
        x = jnp.pad(x3, ((0, 0), (0, t - t_in), (0, 0))).reshape(bsz * t, D_MODEL)
    n_chunks = (gb * t) // ROWS
    new = {'a_c': [], 'a_n': [], 'a_m': [], 'b_s': [], 'c_k': [], 'c_v': [], 'f': []}

    projected = _norm_matmul(x, _proj_spec(w, 0))
    for i in range(DEPTH):
        kind, j = i % N_MIXERS, i // N_MIXERS
        if kind == 0:
            proj, gates = projected
            y, c1, n1, m1 = _mlstm(proj, gates, w['a_b_i'][j], w['a_b_f'][j], w['a_norm'][j],
                                   st['a_c'], j, st['a_n'][j], st['a_m'][j], gb, seg_len, t_valid, n_chunks)
            new['a_c'].append(c1)
            new['a_n'].append(n1)
            new['a_m'].append(m1)
            w_out = (w['a_w_out'], j)
        elif kind == 1:
            proj, z = projected
            y, s1 = _gla(proj, z, w['b_w_gate_up'][j], w['b_b_gate'][j], w['b_norm'][j], st['b_s'][j],
                         gb, seg_len, t_valid, n_chunks, heads=HB if is_prompt else 1)
            new['b_s'].append(s1)
            w_out = (w['b_w_out'], j)
        else:
            proj = projected[0]
            kv_w = HKV * HDC
            end = t if is_prompt else t_valid
            keep = min(WINDOW, end)
            newest = proj.reshape(bsz, t, -1)[:, end - keep:end, HC * HDC:]
            k_new = newest[:, :, :kv_w].reshape(bsz, keep, HKV, HDC)
            v_new = newest[:, :, kv_w:].reshape(bsz, keep, HKV, HDC)
            if is_prompt:
                y = _swa(proj, w['c_sinks'][j], 1, ROWS, n_chunks)
                new['c_k'].append(k_new)
                new['c_v'].append(v_new)
            else:
                k_buf, v_buf = st['c_k'][j], st['c_v'][j]
                y = _swa(proj, w['c_sinks'][j], gb, seg_len, n_chunks,
                         prev_k=k_buf.reshape(bsz * WINDOW, kv_w), prev_v=v_buf.reshape(bsz * WINDOW, kv_w))
                new['c_k'].append(jnp.concatenate([k_buf[:, t_valid:], k_new], axis=1))
                new['c_v'].append(jnp.concatenate([v_buf[:, t_valid:], v_new], axis=1))
            w_out = (w['c_w_out'], j)

        last = i == DEPTH - 1
        tail_args = (x, y, w_out, w['ffn'], i)
        tail_kw = dict(seq_len=t, final_g=w['norm_final'] if last else None,
                       next_proj=None if last else _proj_spec(w, i + 1), tm=TAIL_TM)
        if is_prompt:
            x, gate_tail, *projected = _layer_tail(*tail_args, **tail_kw)
            seq_tails = gate_tail.reshape(bsz, t // TAIL_TM, SUBLANES, D_FF)[:, -1]
            new['f'].append(seq_tails[:, SUBLANES - (CONV_W - 1):])
        else:
            state = st['f'][i]
            state_pad = jnp.pad(state, ((0, 0), (t - (CONV_W - 1), 0), (0, 0)))
            x, gate_all, *projected = _layer_tail(*tail_args, conv_state=state_pad, **tail_kw)
            g_ext = jnp.concatenate([state, gate_all.reshape(bsz, t, D_FF)[:, :t_valid]], axis=1)
            new['f'].append(g_ext[:, -(CONV_W - 1):])

    out = {name: jnp.stack(vals) for name, vals in new.items()}
    y = x.reshape(bsz, t, D_MODEL)[:, :t_in]
    return y, out


def _prepare_weights(norm_mix_g, norm_ffn_g, norm_final_g, a_w_in, a_b_i, a_b_f, a_norm_g, a_w_out, b_w_in,
                     b_w_gate_up, b_b_gate, b_norm_g, b_w_out, c_w_in, c_b_in, c_sinks, c_w_out, f_w_up,
                     f_conv_w, f_conv_b, f_w_down):
    n_a, n_b = a_w_in.shape[0], b_w_in.shape[0]
    e_a = 2 * HA * (DKA + DVA)
    e_b = 2 * HB * (DKB + DVB)
    a_w_gates = jnp.zeros((n_a, D_MODEL, 2 * LANES), BF16)
    a_w_gates = a_w_gates.at[:, :, :HA].set(a_w_in[:, :, e_a:e_a + HA].astype(BF16))
    a_w_gates = a_w_gates.at[:, :, LANES:LANES + HA].set(a_w_in[:, :, e_a + HA:].astype(BF16))
    b_w_z = jnp.zeros((n_b, D_MODEL, LANES), BF16).at[:, :, :GLA_RANK].set(b_w_in[:, :, e_b:].astype(BF16))
    ffn = {'g': norm_ffn_g[:, None, :], 'w_up': f_w_up.astype(BF16), 'conv_w': f_conv_w,
           'conv_b': f_conv_b[:, None, :], 'w_down': f_w_down.astype(BF16)}
    w = {'norm_mix': norm_mix_g[:, None, :], 'norm_final': norm_final_g[None, :], 'ffn': ffn,
         'a_w_in': a_w_in.astype(BF16), 'a_w_gates': a_w_gates, 'a_b_i': a_b_i, 'a_b_f': a_b_f,
         'a_norm': a_norm_g, 'a_w_out': a_w_out.astype(BF16),
         'b_w_in': b_w_in.astype(BF16), 'b_w_z': b_w_z, 'b_w_gate_up': b_w_gate_up, 'b_b_gate': b_b_gate,
         'b_norm': b_norm_g, 'b_w_out': b_w_out.astype(BF16),
         'c_w_in': c_w_in.astype(BF16), 'c_b_in': c_b_in[:, None, :], 'c_sinks': c_sinks,
         'c_w_out': c_w_out.astype(BF16)}
    return w


def kernel(x_prompt, x_sample, state_mlstm_c, state_mlstm_n, state_mlstm_m, state_gla, cache_swa_k, cache_swa_v, state_ffn_conv, norm_mix_g, norm_ffn_g, norm_final_g, a_w_in, a_b_i, a_b_f, a_norm_g, a_w_out, b_w_in, b_w_gate_up, b_b_gate, b_norm_g, b_w_out, c_w_in, c_b_in, c_sinks, c_w_out, f_w_up, f_conv_w, f_conv_b, f_w_down):
    w = _prepare_weights(norm_mix_g, norm_ffn_g, norm_final_g, a_w_in, a_b_i, a_b_f, a_norm_g, a_w_out, b_w_in,
                         b_w_gate_up, b_b_gate, b_norm_g, b_w_out, c_w_in, c_b_in, c_sinks, c_w_out, f_w_up,
                         f_conv_w, f_conv_b, f_w_down)
    n_a, n_b, n_c = a_w_in.shape[0], b_w_in.shape[0], c_w_in.shape[0]
    bp = x_prompt.shape[0]
    st_p = {'a_c': jnp.zeros((n_a, bp, HA, DKA, DVA), F32),
            'a_n': jnp.zeros((n_a, bp, HA, DKA), F32),
            'a_m': jnp.zeros((n_a, bp, HA), F32),
            'b_s': jnp.zeros((n_b, bp, HB, DKB, DVB), F32),
            'c_k': [None] * n_c, 'c_v': [None] * n_c, 'f': None}
    st_s = {'a_c': state_mlstm_c, 'a_n': state_mlstm_n, 'a_m': state_mlstm_m, 'b_s': state_gla,
            'c_k': cache_swa_k, 'c_v': cache_swa_v, 'f': state_ffn_conv}
    y_prompt, np_ = _trunk(x_prompt, st_p, w, True)
    y_sample, ns_ = _trunk(x_sample, st_s, w, False)
    return (y_prompt, y_sample,
            np_['a_c'], np_['a_n'], np_['a_m'], np_['b_s'], np_['c_k'], np_['c_v'], np_['f'],
            ns_['a_c'], ns_['a_n'], ns_['a_m'], ns_['b_s'], ns_['c_k'], ns_['c_v'], ns_['f'])
```

```python
import functools
import math

import jax
import jax.numpy as jnp
from jax import lax
from jax.experimental import pallas as pl
from jax.experimental.pallas import tpu as pltpu

F32 = jnp.float32
BF16 = jnp.bfloat16

D_MODEL = 1024
DEPTH = 4
N_MIXERS = 3
HA, DKA, DVA = 8, 64, 128
GATE_SOFTCAP = 15.0
HB, DKB, DVB = 4, 128, 256
GLA_RANK = 16
GLA_TAU = 16.0
HC, HKV, HDC = 16, 4, 64
WINDOW = 128
D_FF = 2816
CONV_W = 3
EPS = 1e-6
NEG = -1e30
LOG2_E = 1.4426950408889634

ROWS = 128
LANES = 128
SUBLANES = 8
SAMPLE_T_PAD = 8
GLA_SUB = 8
FFN_CHUNK = 256
PROJ_CHUNK = 512
MIXER_UNITS = 2
TAIL_TM_PROMPT = 512
TAIL_TM_SAMPLE = 256
VMEM_LIMIT = 56 * 1024 * 1024


def _params(sem):
    return pltpu.CompilerParams(dimension_semantics=sem, vmem_limit_bytes=VMEM_LIMIT)


def _dot(a, b):
    return jnp.dot(a.astype(BF16), b.astype(BF16), preferred_element_type=F32)


def _dot_nt(a, b):
    return lax.dot_general(a.astype(BF16), b.astype(BF16), (((1,), (1,)), ((), ())),
                           preferred_element_type=F32)


def _dot_tn(a, b):
    return lax.dot_general(a.astype(BF16), b.astype(BF16), (((0,), (0,)), ((), ())),
                           preferred_element_type=F32)


def _sigmoid(x):
    return 1.0 / (1.0 + jnp.exp(-x))


def _log_sigmoid(x):
    return jnp.minimum(x, 0.0) - jnp.log(1.0 + jnp.exp(-jnp.abs(x)))


def _softcap(z):
    return GATE_SOFTCAP * jnp.tanh(z / GATE_SOFTCAP)


def _col(x, h):
    lane = lax.broadcasted_iota(jnp.int32, x.shape, 1)
    return jnp.sum(jnp.where(lane == h, x, 0.0), axis=-1, keepdims=True)


def _expand_heads(x, n_heads, width, terms):
    lanes = x.shape[1]
    src = lax.broadcasted_iota(jnp.int32, (terms * lanes, n_heads * width), 0) % lanes
    dst = lax.broadcasted_iota(jnp.int32, (terms * lanes, n_heads * width), 1) // width
    pieces, rest = [], x
    for _ in range(terms):
        piece = rest.astype(BF16)
        pieces.append(piece)
        rest = rest - piece.astype(F32)
    return jnp.dot(jnp.concatenate(pieces, axis=1), (src == dst).astype(BF16), preferred_element_type=F32)


def _rep_rows(x, reps):
    g, c = x.shape
    if g == 1:
        return jnp.broadcast_to(x, (reps, c))
    return jnp.concatenate([jnp.broadcast_to(x[b:b + 1], (reps, c)) for b in range(g)], axis=0)


def _seg_scan(x, seg_len, op, tpos):
    s = 1
    while s < seg_len:
        shifted = pltpu.roll(x, s, axis=0)
        x = jnp.where(tpos >= s, op(x, shifted), x)
        s *= 2
    return x


def _seg_cumsum(x, seg_len):
    rows = x.shape[0]
    dst = lax.broadcasted_iota(jnp.int32, (rows, 3 * rows), 0)
    src = lax.broadcasted_iota(jnp.int32, (rows, 3 * rows), 1) % rows
    tri = ((src <= dst) & ((src // seg_len) == (dst // seg_len))).astype(BF16)
    pieces, rest = [], x
    for _ in range(3):
        piece = rest.astype(BF16)
        pieces.append(piece)
        rest = rest - piece.astype(F32)
    return jnp.dot(tri, jnp.concatenate(pieces, axis=0), preferred_element_type=F32)


def _expand_blockdiag(x, gb, seg_len):
    if gb == 1:
        return x
    seq = lax.broadcasted_iota(jnp.int32, x.shape, 0) // seg_len
    return jnp.concatenate([jnp.where(seq == b, x, 0.0) for b in range(gb)], axis=1)


def _resident(arr, layer=None):
    shape = arr.shape if layer is None else arr.shape[1:]
    block = shape if layer is None else (None,) + shape
    lead = () if layer is None else (layer,)
    return pl.BlockSpec(block, lambda *_: lead + (0,) * len(shape), pipeline_mode=pl.Buffered(1))


def _project(x, rows, gn_ref, wn_ref, bn_ref, wne_ref, p_ref, pe_ref, e_main):
    ms = jnp.mean(x * x, axis=-1, keepdims=True)
    xn = (x * lax.rsqrt(ms + EPS) * gn_ref[...]).astype(BF16)
    for c in range(e_main // PROJ_CHUNK):
        cols = slice(c * PROJ_CHUNK, (c + 1) * PROJ_CHUNK)
        acc = jnp.dot(xn, wn_ref[:, cols], preferred_element_type=F32)
        if bn_ref is not None:
            acc = acc + bn_ref[:, cols]
        p_ref[rows, cols] = acc.astype(p_ref.dtype)
    if wne_ref is not None:
        pe_ref[rows, :] = jnp.dot(xn, wne_ref[...], preferred_element_type=F32)


def _proj_operands(spec, n, tm):
    g, g_layer = spec['g']
    w, w_layer = spec['w']
    in_specs = [_resident(g, g_layer), _resident(w, w_layer)]
    args = [g, w]
    if spec['bias'] is not None:
        in_specs.append(_resident(*spec['bias']))
        args.append(spec['bias'][0])
    out_specs = [pl.BlockSpec((tm, spec['e_main']), lambda i: (i, 0))]
    out_shape = [jax.ShapeDtypeStruct((n, spec['e_main']), spec['out_dtype'])]
    if spec['w_extra'] is not None:
        in_specs.append(_resident(*spec['w_extra']))
        args.append(spec['w_extra'][0])
        ex = spec['w_extra'][0].shape[-1]
        out_specs.append(pl.BlockSpec((tm, ex), lambda i: (i, 0)))
        out_shape.append(jax.ShapeDtypeStruct((n, ex), F32))
    return in_specs, args, out_specs, out_shape


def _proj_kernel(*refs, e_main, has_bias, has_extra, tm):
    refs = list(refs)
    take = lambda cond=True: refs.pop(0) if cond else None
    x_ref, gn_ref, wn_ref = take(), take(), take()
    bn_ref, wne_ref = take(has_bias), take(has_extra)
    p_ref, pe_ref = take(), take(has_extra)
    for r in range(2):
        rows = slice(r * tm // 2, (r + 1) * tm // 2)
        _project(x_ref[rows, :], rows, gn_ref, wn_ref, bn_ref, wne_ref, p_ref, pe_ref, e_main)


def _norm_matmul(x, spec, tm=512):
    n, d = x.shape
    assert n % tm == 0 and spec['e_main'] % PROJ_CHUNK == 0
    in_specs, args, out_specs, out_shape = _proj_operands(spec, n, tm)
    return pl.pallas_call(
        functools.partial(_proj_kernel, e_main=spec['e_main'], has_bias=spec['bias'] is not None,
                          has_extra=spec['w_extra'] is not None, tm=tm),
        grid=(n // tm,),
        in_specs=[pl.BlockSpec((tm, d), lambda i: (i, 0))] + in_specs,
        out_specs=out_specs, out_shape=out_shape,
        compiler_params=_params(("parallel",)),
        name="norm_matmul",
    )(x, *args)


def _tail_kernel(*refs, tm, seq_len, carry_mode, apply_final, proj_main, proj_bias, proj_extra):
    refs = list(refs)
    take = lambda cond=True: refs.pop(0) if cond else None
    x_ref, y_ref, wo_ref, g_ref, wu_ref, cw_ref, cb_ref, wd_ref = [take() for _ in range(8)]
    st_ref = take(not carry_mode)
    fg_ref = take(apply_final)
    has_proj = proj_main > 0
    gn_ref, wn_ref = take(has_proj), take(has_proj)
    bn_ref, wne_ref = take(proj_bias), take(proj_extra)
    o_ref, go_ref = take(), take()
    p_ref, pe_ref = take(has_proj), take(proj_extra)
    h_ref = take()
    carry_ref = take(carry_mode)

    d_ff = wd_ref.shape[0]
    x2 = x_ref[...] + jnp.dot(y_ref[...].astype(BF16), wo_ref[...], preferred_element_type=F32)
    ms = jnp.mean(x2 * x2, axis=-1, keepdims=True)
    xn = (x2 * lax.rsqrt(ms + EPS) * g_ref[...]).astype(BF16)

    if carry_mode:
        @pl.when((pl.program_id(0) * tm) % seq_len == 0)
        def _():
            carry_ref[...] = jnp.zeros_like(carry_ref)
        row = lax.broadcasted_iota(jnp.int32, (tm, FFN_CHUNK), 0)
    else:
        n_seq = tm // seq_len
        tpos = lax.broadcasted_iota(jnp.int32, (n_seq, seq_len, FFN_CHUNK), 1)

    for c in range(d_ff // FFN_CHUNK):
        cols = slice(c * FFN_CHUNK, (c + 1) * FFN_CHUNK)
        up_cols = slice(d_ff + c * FFN_CHUNK, d_ff + (c + 1) * FFN_CHUNK)
        gate = jnp.dot(xn, wu_ref[:, cols], preferred_element_type=F32)
        up = jnp.dot(xn, wu_ref[:, up_cols], preferred_element_type=F32)
        if carry_mode:
            prev = carry_ref[:, cols]
            p1, p2 = prev[SUBLANES - 1:SUBLANES], prev[SUBLANES - 2:SUBLANES - 1]
            g1 = jnp.where(row == 0, p1, pltpu.roll(gate, 1, axis=0))
            g2 = jnp.where(row == 0, p2, jnp.where(row == 1, p1, pltpu.roll(gate, 2, axis=0)))
            tail = gate[tm - SUBLANES:tm]
            carry_ref[:, cols] = tail
            go_ref[0, :, cols] = tail
        else:
            gate3 = gate.reshape(n_seq, seq_len, FFN_CHUNK)
            st = st_ref[:, :, cols]
            g1 = jnp.where(tpos >= 1, pltpu.roll(gate3, 1, axis=1), pltpu.roll(st, 1, axis=1))
            g2 = jnp.where(tpos >= 2, pltpu.roll(gate3, 2, axis=1), pltpu.roll(st, 2, axis=1))
            g1 = g1.reshape(tm, FFN_CHUNK)
            g2 = g2.reshape(tm, FFN_CHUNK)
            go_ref[:, cols] = gate
        cw = cw_ref[:, cols]
        conv = cb_ref[:, cols] + cw[0:1] * g2 + cw[1:2] * g1 + cw[2:3] * gate
        h_ref[:, cols] = (conv * _sigmoid(conv) * up).astype(BF16)

    out = x2 + jnp.dot(h_ref[...], wd_ref[...], preferred_element_type=F32)
    if apply_final:
        ms = jnp.mean(out * out, axis=-1, keepdims=True)
        out = out * lax.rsqrt(ms + EPS) * fg_ref[...]
    o_ref[...] = out

    if has_proj:
        _project(out, slice(None), gn_ref, wn_ref, bn_ref, wne_ref, p_ref, pe_ref, proj_main)


def _layer_tail(x, y, w_out, ffn, layer, seq_len, conv_state=None, final_g=None, next_proj=None, tm=256):
    n, d = x.shape
    f = ffn['w_down'].shape[1]
    assert n % tm == 0 and f % FFN_CHUNK == 0
    carry_mode = conv_state is None
    apply_final = final_g is not None
    in_specs = [pl.BlockSpec((tm, d), lambda i: (i, 0)),
                pl.BlockSpec((tm, d), lambda i: (i, 0)),
                _resident(*w_out)]
    args = [x, y, w_out[0]]
    for name in ('g', 'w_up', 'conv_w', 'conv_b', 'w_down'):
        in_specs.append(_resident(ffn[name], layer))
        args.append(ffn[name])
    scratch = []
    if carry_mode:
        assert seq_len % tm == 0
        go_shape = jax.ShapeDtypeStruct((n // tm, SUBLANES, f), F32)
        go_spec = pl.BlockSpec((1, SUBLANES, f), lambda i: (i, 0, 0))
        scratch.append(pltpu.VMEM((SUBLANES, f), F32))
    else:
        assert seq_len == SUBLANES and tm % seq_len == 0
        in_specs.append(pl.BlockSpec((tm // seq_len, seq_len, f), lambda i: (i, 0, 0)))
        args.append(conv_state)
        go_shape = jax.ShapeDtypeStruct((n, f), F32)
        go_spec = pl.BlockSpec((tm, f), lambda i: (i, 0))
    if apply_final:
        in_specs.append(_resident(final_g))
        args.append(final_g)
    out_specs = [pl.BlockSpec((tm, d), lambda i: (i, 0)), go_spec]
    out_shape = [jax.ShapeDtypeStruct((n, d), F32), go_shape]
    proj_main, proj_bias, proj_extra = 0, False, False
    if next_proj is not None:
        proj_main = next_proj['e_main']
        proj_bias = next_proj['bias'] is not None
        proj_extra = next_proj['w_extra'] is not None
        assert proj_main % PROJ_CHUNK == 0
        p_in_specs, p_args, p_out_specs, p_out_shape = _proj_operands(next_proj, n, tm)
        in_specs += p_in_specs
        args += p_args
        out_specs += p_out_specs
        out_shape += p_out_shape
    scratch = [pltpu.VMEM((tm, f), BF16)] + scratch
    return pl.pallas_call(
        functools.partial(_tail_kernel, tm=tm, seq_len=seq_len, carry_mode=carry_mode, apply_final=apply_final,
                          proj_main=proj_main, proj_bias=proj_bias, proj_extra=proj_extra),
        grid=(n // tm,),
        in_specs=in_specs,
        out_specs=out_specs,
        out_shape=out_shape,
        scratch_shapes=scratch,
        compiler_params=_params(("arbitrary",)),
        name="layer_tail",
    )(*args)


def _mlstm_kernel(q_ref, k_ref, v_ref, o_ref, gi_ref, gf_ref, bi_ref, bf_ref, ng_ref,
                  c0_ref, n0_ref, m0_ref, y_ref, c1_ref, n1_ref, m1_ref, fs_ref, ms_ref,
                  *, units, gb, seg_len, t_valid):
    @pl.when(pl.program_id(1) == 0)
    def _():
        c1_ref[...] = c0_ref[...]
        n1_ref[...] = n0_ref[...]
        m1_ref[...] = m0_ref[...]

    for u in range(units):
        seqs = pl.ds(u * gb, gb)
        _mlstm_unit(q_ref.at[u], k_ref.at[u], v_ref.at[u], o_ref.at[u], gi_ref.at[u], gf_ref.at[u],
                    bi_ref, bf_ref, ng_ref, y_ref.at[u], c1_ref.at[seqs], n1_ref.at[u], m1_ref.at[u],
                    fs_ref.at[u], ms_ref.at[u], gb=gb, seg_len=seg_len, t_valid=t_valid)


def _mlstm_unit(q_ref, k_ref, v_ref, o_ref, gi_ref, gf_ref, bi_ref, bf_ref, ng_ref,
                y_ref, c1_ref, n1_ref, m1_ref, fs_ref, ms_ref, *, gb, seg_len, t_valid):
    shape = (ROWS, LANES)
    row = lax.broadcasted_iota(jnp.int32, shape, 0)
    lane = lax.broadcasted_iota(jnp.int32, shape, 1)
    tpos = row % seg_len
    low_half = lane < DKA

    ig = _softcap(gi_ref[...] + bi_ref[...])
    lf = _log_sigmoid(_softcap(gf_ref[...] + bf_ref[...]))
    if t_valid < seg_len:
        valid = tpos < t_valid
        ig = jnp.where(valid, ig, NEG)
        lf = jnp.where(valid, lf, 0.0)
    f_cum = _seg_scan(lf, seg_len, jnp.add, tpos)
    a = ig - f_cum
    cmax = _seg_scan(a, seg_len, jnp.maximum, tpos)
    m_prev_g = m1_ref[...]
    m_prev = _rep_rows(m_prev_g, seg_len)
    m_t = f_cum + jnp.maximum(m_prev, cmax)
    g = jnp.exp(jnp.minimum(f_cum + m_prev - m_t, 0.0))
    u = f_cum - m_t
    em = jnp.exp(-m_t)

    fs_ref[...] = f_cum
    ms_ref[...] = m_t
    f_last_g = fs_ref[pl.ds(seg_len - 1, gb, stride=seg_len), :]
    m_last_g = ms_ref[pl.ds(seg_len - 1, gb, stride=seg_len), :]
    f_last = _rep_rows(f_last_g, seg_len)
    m_last = _rep_rows(m_last_g, seg_len)
    w_last = jnp.exp(jnp.minimum(a + f_last - m_last, 0.0))
    g_last_g = jnp.exp(jnp.minimum(f_last_g + m_prev_g - m_last_g, 0.0))

    a_t = a.T
    u_w = _expand_heads(u, HA, DVA, 3)
    g_w = _expand_heads(g, HA, DVA, 2)
    em_w = _expand_heads(em, HA, DVA, 2)
    wl_w = _expand_heads(w_last, HA, DKA, 2)
    gl_rows = g_last_g if gb >= SUBLANES else jnp.broadcast_to(g_last_g, (SUBLANES, LANES))
    gl_w = _expand_heads(gl_rows, HA, DKA, 2)[:gb]

    col_i = lax.broadcasted_iota(jnp.int32, (ROWS, 2 * ROWS), 1) % ROWS
    row_i = lax.broadcasted_iota(jnp.int32, (ROWS, 2 * ROWS), 0)
    causal2 = (col_i <= row_i) & ((col_i // seg_len) == (row_i // seg_len))
    state_low = lax.broadcasted_iota(jnp.int32, (gb * LANES, LANES), 0) % LANES < DKA
    flat_low = lax.broadcasted_iota(jnp.int32, (LANES, gb * LANES), 1) % LANES < DKA
    sel = (lax.broadcasted_iota(jnp.int32, (max(gb, SUBLANES), ROWS), 1) // seg_len
           == lax.broadcasted_iota(jnp.int32, (max(gb, SUBLANES), ROWS), 0)).astype(BF16)
    ones_blk = (lax.broadcasted_iota(jnp.int32, (2 * ROWS, 2 * DVA), 0) // ROWS
                == lax.broadcasted_iota(jnp.int32, (2 * ROWS, 2 * DVA), 1) // DVA).astype(BF16)
    zeros_v = jnp.zeros((ROWS, DVA), v_ref.dtype)

    q_all = q_ref[...]
    k_all = k_ref[...] * (DKA ** -0.5)
    k2_all = k_all * wl_w

    for p in range(HA // 2):
        cols = slice(p * LANES, (p + 1) * LANES)
        wide = slice(2 * p * DVA, (2 * p + 2) * DVA)
        qp, kp, k2 = q_all[:, cols], k_all[:, cols], k2_all[:, cols]
        c_p = c1_ref[:, p].reshape(gb * LANES, DVA)
        n_p = n1_ref[p]
        v2 = v_ref[:, wide]
        v_a, v_b = v2[:, :DVA], v2[:, DVA:]

        k_sep = jnp.concatenate([jnp.where(low_half, kp, 0.0), jnp.where(low_half, 0.0, kp)], axis=0)
        s2 = _dot_nt(qp, k_sep)
        logw2 = jnp.concatenate([a_t[2 * p:2 * p + 1], a_t[2 * p + 1:2 * p + 2]], axis=1) + u_w[:, wide]
        qk2 = (s2 * jnp.where(causal2, jnp.exp(jnp.minimum(logw2, 0.0)), 0.0)).astype(BF16)
        v_bd = jnp.concatenate([jnp.concatenate([v_a, zeros_v], axis=1),
                                jnp.concatenate([zeros_v, v_b], axis=1)], axis=0)
        num2 = jnp.dot(qk2, v_bd, preferred_element_type=F32)
        dsum2 = jnp.dot(qk2, ones_blk, preferred_element_type=F32)

        qx = _expand_blockdiag(qp, gb, seg_len)
        c_sep = jnp.concatenate([jnp.where(state_low, c_p, 0.0), jnp.where(state_low, 0.0, c_p)], axis=1)
        inter2 = _dot(qx, c_sep)
        n_flat = n_p if gb == 1 else jnp.concatenate([n_p[b:b + 1] for b in range(gb)], axis=1)
        n_b = jnp.broadcast_to(n_flat, (LANES, gb * LANES))
        n_sep = jnp.concatenate([jnp.where(flat_low, n_b, 0.0), jnp.where(flat_low, 0.0, n_b)], axis=0)
        dint2 = _dot_nt(qx, n_sep)

        g2 = g_w[:, wide]
        hh2 = (num2 + g2 * inter2) / jnp.maximum(jnp.abs(dsum2 + g2 * dint2), em_w[:, wide])
        for half in range(2):
            hcols = slice((2 * p + half) * DVA, (2 * p + half + 1) * DVA)
            hh = hh2[:, half * DVA:(half + 1) * DVA]
            msq = jnp.mean(hh * hh, axis=-1, keepdims=True)
            hn = hh * lax.rsqrt(msq + EPS) * ng_ref[:, hcols]
            y_ref[:, hcols] = (hn * _sigmoid(o_ref[:, hcols].astype(F32))).astype(y_ref.dtype)

        upd2 = _dot_tn(_expand_blockdiag(k2, gb, seg_len), v2)
        upd = jnp.where(state_low, upd2[:, :DVA], upd2[:, DVA:])
        n_scale = gl_w[:, cols]
        scale = jnp.concatenate([jnp.broadcast_to(n_scale[b:b + 1], (LANES, LANES)).T for b in range(gb)],
                                axis=0)
        c1_ref[:, p] = (scale * c_p + upd).reshape(gb, LANES, DVA)
        n_upd = jnp.dot(sel, k2.astype(BF16), preferred_element_type=F32)[:gb]
        n1_ref[p] = n_scale * n_p + n_upd

    m1_ref[...] = m_last_g


def _mlstm(proj, gates, b_i, b_f, norm_g, c_all, layer, n0, m0, gb, seg_len, t_valid, n_chunks, units):
    n = proj.shape[0]
    n_blocks = n // (ROWS * n_chunks)
    assert n_blocks % units == 0
    bsz = n_blocks * gb
    half_pairs = HA // 2
    rows3 = lambda a: a.reshape(n_blocks, n_chunks * ROWS, a.shape[-1])
    proj3, gates3 = rows3(proj), rows3(gates)
    c0r = c_all.reshape(c_all.shape[0], bsz, half_pairs, LANES, DVA)
    n0r = n0.reshape(n_blocks, gb, half_pairs, LANES).transpose(0, 2, 1, 3)
    m0r = jnp.pad(m0, ((0, 0), (0, LANES - HA))).reshape(n_blocks, gb, LANES)
    pad = lambda b: jnp.pad(b, (0, LANES - HA)).reshape(1, LANES)
    blk = lambda width, cb: pl.BlockSpec((units, ROWS, width), lambda i, c: (i, c, cb))
    c_blk = (units * gb, half_pairs, LANES, DVA)
    y, c1, n1, m1 = pl.pallas_call(
        functools.partial(_mlstm_kernel, units=units, gb=gb, seg_len=seg_len, t_valid=t_valid),
        grid=(n_blocks // units, n_chunks),
        in_specs=[blk(HA * DKA, 0), blk(HA * DKA, 1), blk(HA * DVA, 1), blk(HA * DVA, 2),
                  blk(LANES, 0), blk(LANES, 1),
                  pl.BlockSpec((1, LANES), lambda i, c: (0, 0)),
                  pl.BlockSpec((1, LANES), lambda i, c: (0, 0)),
                  pl.BlockSpec((1, HA * DVA), lambda i, c: (0, 0)),
                  pl.BlockSpec((None,) + c_blk, lambda i, c: (layer, i, 0, 0, 0)),
                  pl.BlockSpec((units, half_pairs, gb, LANES), lambda i, c: (i, 0, 0, 0)),
                  pl.BlockSpec((units, gb, LANES), lambda i, c: (i, 0, 0))],
        out_specs=[blk(HA * DVA, 0),
                   pl.BlockSpec(c_blk, lambda i, c: (i, 0, 0, 0)),
                   pl.BlockSpec((units, half_pairs, gb, LANES), lambda i, c: (i, 0, 0, 0)),
                   pl.BlockSpec((units, gb, LANES), lambda i, c: (i, 0, 0))],
        out_shape=[jax.ShapeDtypeStruct((n_blocks, n_chunks * ROWS, HA * DVA), BF16),
                   jax.ShapeDtypeStruct(c0r.shape[1:], F32),
                   jax.ShapeDtypeStruct(n0r.shape, F32),
                   jax.ShapeDtypeStruct(m0r.shape, F32)],
        scratch_shapes=[pltpu.VMEM((units, ROWS, LANES), F32), pltpu.VMEM((units, ROWS, LANES), F32)],
        compiler_params=_params(("parallel", "arbitrary")),
        name="mlstm_mixer",
    )(proj3, proj3, proj3, proj3, gates3, gates3, pad(b_i), pad(b_f), norm_g.reshape(1, -1), c0r, n0r, m0r)
    y = y.reshape(n, HA * DVA)
    c1 = c1.reshape(bsz, HA, DKA, DVA)
    n1 = n1.transpose(0, 2, 1, 3).reshape(bsz, HA, DKA)
    m1 = m1.reshape(bsz, LANES)[:, :HA]
    return y, c1, n1, m1


def _gla_kernel(q_ref, k_ref, v_ref, r_ref, z_ref, wgu_ref, bg_ref, ng_ref, s0_ref,
                y_ref, s1_ref, bc_ref, b2_ref, kf_ref, *, units, gb, seg_len, t_valid, heads):
    @pl.when(pl.program_id(2) == 0)
    def _():
        s1_ref[...] = s0_ref[...]

    for u in range(units):
        _gla_unit(q_ref.at[u], k_ref.at[u], v_ref.at[u], r_ref.at[u], z_ref.at[u], wgu_ref, bg_ref, ng_ref,
                  y_ref.at[u], s1_ref.at[pl.ds(u * gb, gb)], bc_ref.at[u], b2_ref.at[u], kf_ref.at[u],
                  gb=gb, seg_len=seg_len, t_valid=t_valid, heads=heads)


def _gla_unit(q_ref, k_ref, v_ref, r_ref, z_ref, wgu_ref, bg_ref, ng_ref, y_ref, s1_ref, bc_ref, b2_ref, kf_ref,
              *, gb, seg_len, t_valid, heads):
    width = heads * DKB
    row = lax.broadcasted_iota(jnp.int32, (ROWS, width), 0)
    tpos = row % seg_len
    lg = _log_sigmoid(jnp.dot(z_ref[...].astype(BF16), wgu_ref[...], preferred_element_type=F32)
                      + bg_ref[...]) / GLA_TAU
    k_all = k_ref[...].astype(F32)
    if t_valid < seg_len:
        valid = tpos < t_valid
        lg = jnp.where(valid, lg, 0.0)
        k_all = jnp.where(valid, k_all, 0.0)
    bc = _seg_cumsum(lg, seg_len)

    n_sub = ROWS // GLA_SUB
    sub_row = lax.broadcasted_iota(jnp.int32, (GLA_SUB, ROWS), 0)
    sub_col = lax.broadcasted_iota(jnp.int32, (GLA_SUB, ROWS), 1)

    for h in range(heads):
        kcols = slice(h * DKB, (h + 1) * DKB)
        vcols = slice(h * DVB, (h + 1) * DVB)
        qh = q_ref[:, kcols].astype(F32) * (DKB ** -0.5)
        kh = k_all[:, kcols]
        vh = v_ref[:, vcols]
        bch = bc[:, kcols]
        bc_ref[...] = bch
        b2_ref[...] = bch * LOG2_E
        kf_ref[...] = kh
        b_last_g = bc_ref[pl.ds(seg_len - 1, gb, stride=seg_len), :]
        b_last = _rep_rows(b_last_g, seg_len)
        s_h = s1_ref[:, h].reshape(gb * DKB, DVB)

        inter = _dot(_expand_blockdiag(qh * jnp.exp(bch), gb, seg_len), s_h)

        att_rows = []
        k_decayed = []
        for blk in range(n_sub):
            r0 = blk * GLA_SUB
            q_b = qh[r0:r0 + GLA_SUB]
            bc_b = bch[r0:r0 + GLA_SUB]
            b2_b = b2_ref[r0:r0 + GLA_SUB, :]
            att = jnp.zeros((GLA_SUB, ROWS), F32)
            for s in range(GLA_SUB):
                k_row = jnp.broadcast_to(kf_ref[r0 + s:r0 + s + 1, :], (GLA_SUB, DKB))
                b2_row = jnp.broadcast_to(b2_ref[r0 + s:r0 + s + 1, :], (GLA_SUB, DKB))
                cs = jnp.sum(q_b * k_row * jnp.exp2(b2_b - b2_row), axis=-1, keepdims=True)
                att = jnp.where(sub_col == r0 + s, cs, att)
            att = jnp.where(sub_col <= sub_row + r0, att, 0.0)
            if seg_len > GLA_SUB and blk > 0:
                b_edge = bch[r0 - 1:r0]
                if blk > 1:
                    step = jnp.exp(b_edge - bch[r0 - GLA_SUB - 1:r0 - GLA_SUB])
                    k_decayed = [kb * step for kb in k_decayed]
                k_decayed.append(kh[r0 - GLA_SUB:r0] * jnp.exp(b_edge - bch[r0 - GLA_SUB:r0]))
                k_t = jnp.concatenate(k_decayed + [jnp.zeros((ROWS - r0, DKB), F32)], axis=0)
                att = att + _dot_nt(q_b * jnp.exp(bc_b - b_edge), k_t)
            att_rows.append(att)
        att_full = jnp.concatenate(att_rows, axis=0)

        o = inter + _dot(att_full, vh)
        msq = jnp.mean(o * o, axis=-1, keepdims=True)
        on = o * lax.rsqrt(msq + EPS) * ng_ref[:, vcols]
        r = r_ref[:, vcols].astype(F32)
        y_ref[:, vcols] = (on * (r * _sigmoid(r))).astype(y_ref.dtype)

        k_d = kh * jnp.exp(b_last - bch)
        upd = _dot_tn(_expand_blockdiag(k_d, gb, seg_len), vh)
        e_last = jnp.exp(b_last_g)
        e_cols = jnp.concatenate(
            [jnp.broadcast_to(e_last[b:b + 1], (DKB, DKB)).T for b in range(gb)], axis=0)
        e_cols = jnp.concatenate([e_cols] * (DVB // DKB), axis=1)
        s1_ref[:, h] = (e_cols * s_h + upd).reshape(gb, DKB, DVB)


def _gla(proj, z, w_gate_up, b_gate, norm_g, s0, gb, seg_len, t_valid, n_chunks, heads, units):
    n = proj.shape[0]
    n_blocks = n // (ROWS * n_chunks)
    assert n_blocks % units == 0 and seg_len % GLA_SUB == 0
    n_hb = HB // heads
    wq, wv = heads * DKB, heads * DVB
    q_off, k_off = 0, (HB * DKB) // wq
    v_off, r_off = (2 * HB * DKB) // wv, (2 * HB * DKB + HB * DVB) // wv
    rows3 = lambda a: a.reshape(n_blocks, n_chunks * ROWS, a.shape[-1])
    proj3, z3 = rows3(proj), rows3(z)
    blk = lambda width, off: pl.BlockSpec((units, ROWS, width), lambda i, hb, c: (i, c, off + hb))
    state_spec = pl.BlockSpec((units * gb, heads, DKB, DVB), lambda i, hb, c: (i, hb, 0, 0))
    wgu = jnp.zeros((LANES, HB * DKB), BF16).at[:GLA_RANK].set(w_gate_up.astype(BF16))
    y, s1 = pl.pallas_call(
        functools.partial(_gla_kernel, units=units, gb=gb, seg_len=seg_len, t_valid=t_valid, heads=heads),
        grid=(n_blocks // units, n_hb, n_chunks),
        in_specs=[blk(wq, q_off), blk(wq, k_off), blk(wv, v_off), blk(wv, r_off),
                  pl.BlockSpec((units, ROWS, LANES), lambda i, hb, c: (i, c, 0)),
                  pl.BlockSpec((LANES, wq), lambda i, hb, c: (0, hb)),
                  pl.BlockSpec((1, wq), lambda i, hb, c: (0, hb)),
                  pl.BlockSpec((1, wv), lambda i, hb, c: (0, hb)),
                  state_spec],
        out_specs=[blk(wv, 0), state_spec],
        out_shape=[jax.ShapeDtypeStruct((n_blocks, n_chunks * ROWS, HB * DVB), BF16),
                   jax.ShapeDtypeStruct(s0.shape, F32)],
        scratch_shapes=[pltpu.VMEM((units, ROWS, DKB), F32)] * 3,
        compiler_params=_params(("parallel", "parallel", "arbitrary")),
        name="gla_mixer",
    )(proj3, proj3, proj3, proj3, z3, wgu, b_gate.reshape(1, -1), norm_g.reshape(1, -1), s0)
    return y.reshape(n, HB * DVB), s1


def _swa_kernel(q_ref, kc_ref, vc_ref, kp_ref, vp_ref, sink_ref, o_ref, *, blocks, units, q_len, prev_from_grid):
    for b in range(blocks):
        _swa_block(q_ref.at[b], kc_ref.at[b], vc_ref.at[b], kp_ref.at[b], vp_ref.at[b], sink_ref, o_ref.at[b],
                   units=units, q_len=q_len, prev_from_grid=prev_from_grid)


def _swa_block(q_ref, kc_ref, vc_ref, kp_ref, vp_ref, sink_ref, o_ref, *, units, q_len, prev_from_grid):
    group = HC // HKV
    rows4 = group * ROWS
    row = lax.broadcasted_iota(jnp.int32, (rows4, ROWS), 0) % ROWS
    key = lax.broadcasted_iota(jnp.int32, (rows4, ROWS), 1)
    q_pos = row % q_len
    mask_p = key >= q_pos
    if prev_from_grid:
        mask_p = mask_p & (pl.program_id(1) > 0)
    mask_c = ((key // q_len) == (row // q_len)) & ((key % q_len) <= q_pos)
    bias_p = jnp.where(mask_p, 0.0, NEG)
    bias_c = jnp.where(mask_c, 0.0, NEG)
    low = lax.broadcasted_iota(jnp.int32, (ROWS, LANES), 1) < HDC
    ones = jnp.ones((WINDOW, LANES), BF16)
    sinks = sink_ref[...]
    scale = HDC ** -0.5

    def per_unit(lhs, rhs_ref, cols, contract_rhs_rows):
        mm = _dot if contract_rhs_rows else _dot_nt
        if units == 1:
            return mm(lhs, rhs_ref[:, cols])
        pieces = [[None] * units for _ in range(group)]
        for u in range(units):
            rows_u = jnp.concatenate([lhs[gi * ROWS + u * q_len:gi * ROWS + (u + 1) * q_len]
                                      for gi in range(group)], axis=0)
            res = mm(rows_u, rhs_ref[u * WINDOW:(u + 1) * WINDOW, cols])
            for gi in range(group):
                pieces[gi][u] = res[gi * q_len:(gi + 1) * q_len]
        return jnp.concatenate([pieces[gi][u] for gi in range(group) for u in range(units)], axis=0)

    out_tiles = [None] * (HC // 2)
    for kk in range(HKV):
        kv_cols = slice((kk // 2) * LANES, (kk // 2 + 1) * LANES)
        k_half = kk % 2
        k_sel = low if k_half == 0 else ~low
        q_parts, sink_parts = [], []
        for gi in range(group):
            hq = kk * group + gi
            tile = q_ref[:, (hq // 2) * LANES:(hq // 2 + 1) * LANES] * scale
            if hq % 2 != k_half:
                tile = pltpu.roll(tile, HDC, axis=1)
            q_parts.append(jnp.where(k_sel, tile, 0.0))
            sink_parts.append(jnp.broadcast_to(_col(sinks, hq), (ROWS, LANES)))
        q4 = jnp.concatenate(q_parts, axis=0)
        sink = jnp.concatenate(sink_parts, axis=0)
        s_p = per_unit(q4, kp_ref, kv_cols, False) + bias_p
        s_c = _dot_nt(q4, kc_ref[:, kv_cols]) + bias_c
        row_max = jnp.max(jnp.maximum(s_p, s_c), axis=-1, keepdims=True)
        mx = jnp.maximum(jnp.broadcast_to(row_max, (rows4, LANES)), sink)
        p_p = jnp.exp(s_p - mx).astype(BF16)
        p_c = jnp.exp(s_c - mx).astype(BF16)
        den = (jnp.dot(p_p, ones, preferred_element_type=F32) + jnp.dot(p_c, ones, preferred_element_type=F32)
               + jnp.exp(sink - mx))
        out4 = (per_unit(p_p, vp_ref, kv_cols, True) + _dot(p_c, vc_ref[:, kv_cols])) / den
        for gi in range(group):
            hq = kk * group + gi
            part = out4[gi * ROWS:(gi + 1) * ROWS]
            if hq % 2 != k_half:
                part = pltpu.roll(part, HDC, axis=1)
            prev = out_tiles[hq // 2]
            out_tiles[hq // 2] = part if prev is None else jnp.where(low if hq % 2 == 0 else ~low, part, prev)
    for t in range(HC // 2):
        o_ref[:, t * LANES:(t + 1) * LANES] = out_tiles[t].astype(o_ref.dtype)


def _swa(proj, sinks, units, q_len, n_chunks, blocks, prev_k=None, prev_v=None):
    n = proj.shape[0]
    n_chains = n // (ROWS * n_chunks)
    assert n_chains % blocks == 0
    kv_w = HKV * HDC
    k_blk, v_blk = (HC * HDC) // kv_w, (HC * HDC) // kv_w + 1
    prev_from_grid = prev_k is None
    proj3 = proj.reshape(n_chains, n_chunks * ROWS, proj.shape[-1])
    cur = lambda blk: pl.BlockSpec((blocks, ROWS, kv_w), lambda i, c: (i, c, blk))
    if prev_from_grid:
        prev = lambda blk: pl.BlockSpec((blocks, ROWS, kv_w), lambda i, c: (i, jnp.maximum(c - 1, 0), blk))
        prev_specs = [prev(k_blk), prev(v_blk)]
        prev_args = [proj3, proj3]
    else:
        prev_specs = [pl.BlockSpec((blocks, units * WINDOW, kv_w), lambda i, c: (i, 0, 0))] * 2
        prev_args = [prev_k.reshape(n_chains, units * WINDOW, kv_w), prev_v.reshape(n_chains, units * WINDOW, kv_w)]
    sink_row = jnp.pad(sinks.astype(F32), (0, LANES - HC)).reshape(1, LANES)
    y = pl.pallas_call(
        functools.partial(_swa_kernel, blocks=blocks, units=units, q_len=q_len, prev_from_grid=prev_from_grid),
        grid=(n_chains // blocks, n_chunks),
        in_specs=[pl.BlockSpec((blocks, ROWS, HC * HDC), lambda i, c: (i, c, 0)),
                  cur(k_blk), cur(v_blk)] + prev_specs +
                 [pl.BlockSpec((1, LANES), lambda i, c: (0, 0))],
        out_specs=pl.BlockSpec((blocks, ROWS, HC * HDC), lambda i, c: (i, c, 0)),
        out_shape=jax.ShapeDtypeStruct((n_chains, n_chunks * ROWS, HC * HDC), BF16),
        compiler_params=_params(("parallel", "arbitrary")),
        name="swa_mixer",
    )(proj3, proj3, proj3, *prev_args, sink_row)
    return y.reshape(n, HC * HDC)


def _proj_spec(w, i):
    kind, j = i % N_MIXERS, i // N_MIXERS
    spec = {'g': (w['norm_mix'], i), 'bias': None, 'w_extra': None, 'out_dtype': BF16}
    if kind == 0:
        spec.update(w=(w['a_w_in'], j), e_main=2 * HA * (DKA + DVA), w_extra=(w['a_w_gates'], j))
    elif kind == 1:
        spec.update(w=(w['b_w_in'], j), e_main=2 * HB * (DKB + DVB), w_extra=(w['b_w_z'], j))
    else:
        spec.update(w=(w['c_w_in'], j), e_main=(HC + 2 * HKV) * HDC, bias=(w['c_b_in'], j), out_dtype=F32)
    return spec


def _trunk(x3, st, w, is_prompt):
    bsz, t_in, _ = x3.shape
    if is_prompt:
        t, t_valid, gb, seg_len = t_in, ROWS, 1, ROWS
        x = x3.reshape(bsz * t, D_MODEL)
    else:
        t, t_valid, gb, seg_len = SAMPLE_T_PAD, t_in, ROWS // SAMPLE_T_PAD, SAMPLE_T_PAD
        x = jnp.pad(x3, ((0, 0), (0, t - t_in), (0, 0))).reshape(bsz * t, D_MODEL)
    n_chunks = (gb * t) // ROWS
    new = {'a_c': [], 'a_n': [], 'a_m': [], 'b_s': [], 'c_k': [], 'c_v': [], 'f': []}

    projected = _norm_matmul(x, _proj_spec(w, 0))
    for i in range(DEPTH):
        kind, j = i % N_MIXERS, i // N_MIXERS
        if kind == 0:
            proj, gates = projected
            y, c1, n1, m1 = _mlstm(proj, gates, w['a_b_i'][j], w['a_b_f'][j], w['a_norm'][j],
                                   st['a_c'], j, st['a_n'][j], st['a_m'][j], gb, seg_len, t_valid, n_chunks,
                                   units=MIXER_UNITS if is_prompt else 1)
            new['a_c'].append(c1)
            new['a_n'].append(n1)
            new['a_m'].append(m1)
            w_out = (w['a_w_out'], j)
        elif kind == 1:
            proj, z = projected
            y, s1 = _gla(proj, z, w['b_w_gate_up'][j], w['b_b_gate'][j], w['b_norm'][j], st['b_s'][j],
                         gb, seg_len, t_valid, n_chunks, heads=HB if is_prompt else 1,
                         units=MIXER_UNITS if is_prompt else 1)
            new['b_s'].append(s1)
            w_out = (w['b_w_out'], j)
        else:
            proj = projected[0]
            kv_w = HKV * HDC
            end = t if is_prompt else t_valid
            keep = min(WINDOW, end)
            newest = proj.reshape(bsz, t, -1)[:, end - keep:end, HC * HDC:]
            k_new = newest[:, :, :kv_w].reshape(bsz, keep, HKV, HDC)
            v_new = newest[:, :, kv_w:].reshape(bsz, keep, HKV, HDC)
            if is_prompt:
                y = _swa(proj, w['c_sinks'][j], 1, ROWS, n_chunks, MIXER_UNITS)
                new['c_k'].append(k_new)
                new['c_v'].append(v_new)
            else:
                k_buf, v_buf = st['c_k'][j], st['c_v'][j]
                y = _swa(proj, w['c_sinks'][j], gb, seg_len, n_chunks, 1,
                         prev_k=k_buf.reshape(bsz * WINDOW, kv_w), prev_v=v_buf.reshape(bsz * WINDOW, kv_w))
                new['c_k'].append(jnp.concatenate([k_buf[:, t_valid:], k_new], axis=1))
                new['c_v'].append(jnp.concatenate([v_buf[:, t_valid:], v_new], axis=1))
            w_out = (w['c_w_out'], j)

        last = i == DEPTH - 1
        tail_args = (x, y, w_out, w['ffn'], i)
        tail_kw = dict(seq_len=t, final_g=w['norm_final'] if last else None,
                       next_proj=None if last else _proj_spec(w, i + 1),
                       tm=TAIL_TM_PROMPT if is_prompt else TAIL_TM_SAMPLE)
        if is_prompt:
            x, gate_tail, *projected = _layer_tail(*tail_args, **tail_kw)
            seq_tails = gate_tail.reshape(bsz, t // TAIL_TM_PROMPT, SUBLANES, D_FF)[:, -1]
            new['f'].append(seq_tails[:, SUBLANES - (CONV_W - 1):])
        else:
            state = st['f'][i]
            state_pad = jnp.pad(state, ((0, 0), (t - (CONV_W - 1), 0), (0, 0)))
            x, gate_all, *projected = _layer_tail(*tail_args, conv_state=state_pad, **tail_kw)
            g_ext = jnp.concatenate([state, gate_all.reshape(bsz, t, D_FF)[:, :t_valid]], axis=1)
            new['f'].append(g_ext[:, -(CONV_W - 1):])

    out = {name: jnp.stack(vals) for name, vals in new.items()}
    y = x.reshape(bsz, t, D_MODEL)[:, :t_in]
    return y, out


def _prepare_weights(norm_mix_g, norm_ffn_g, norm_final_g, a_w_in, a_b_i, a_b_f, a_norm_g, a_w_out, b_w_in,
                     b_w_gate_up, b_b_gate, b_norm_g, b_w_out, c_w_in, c_b_in, c_sinks, c_w_out, f_w_up,
                     f_conv_w, f_conv_b, f_w_down):
    n_a, n_b = a_w_in.shape[0], b_w_in.shape[0]
    e_a = 2 * HA * (DKA + DVA)
    e_b = 2 * HB * (DKB + DVB)
    a_w_gates = jnp.zeros((n_a, D_MODEL, 2 * LANES), BF16)
    a_w_gates = a_w_gates.at[:, :, :HA].set(a_w_in[:, :, e_a:e_a + HA].astype(BF16))
    a_w_gates = a_w_gates.at[:, :, LANES:LANES + HA].set(a_w_in[:, :, e_a + HA:].astype(BF16))
    b_w_z = jnp.zeros((n_b, D_MODEL, LANES), BF16).at[:, :, :GLA_RANK].set(b_w_in[:, :, e_b:].astype(BF16))
    ffn = {'g': norm_ffn_g[:, None, :], 'w_up': f_w_up.astype(BF16), 'conv_w': f_conv_w,
           'conv_b': f_conv_b[:, None, :], 'w_down': f_w_down.astype(BF16)}
    w = {'norm_mix': norm_mix_g[:, None, :], 'norm_final': norm_final_g[None, :], 'ffn': ffn,
         'a_w_in': a_w_in.astype(BF16), 'a_w_gates': a_w_gates, 'a_b_i': a_b_i, 'a_b_f': a_b_f,
         'a_norm': a_norm_g, 'a_w_out': a_w_out.astype(BF16),
         'b_w_in': b_w_in.astype(BF16), 'b_w_z': b_w_z, 'b_w_gate_up': b_w_gate_up, 'b_b_gate': b_b_gate,
         'b_norm': b_norm_g, 'b_w_out': b_w_out.astype(BF16),
         'c_w_in': c_w_in.astype(BF16), 'c_b_in': c_b_in[:, None, :], 'c_sinks': c_sinks,
         'c_w_out': c_w_out.astype(BF16)}
    return w


def kernel(x_prompt, x_sample, state_mlstm_c, state_mlstm_n, state_mlstm_m, state_gla, cache_swa_k, cache_swa_v, state_ffn_conv, norm_mix_g, norm_ffn_g, norm_final_g, a_w_in, a_b_i, a_b_f, a_norm_g, a_w_out, b_w_in, b_w_gate_up, b_b_gate, b_norm_g, b_w_out, c_w_in, c_b_in, c_sinks, c_w_out, f_w_up, f_conv_w, f_conv_b, f_w_down):
    w = _prepare_weights(norm_mix_g, norm_ffn_g, norm_final_g, a_w_in, a_b_i, a_b_f, a_norm_g, a_w_out, b_w_in,
                         b_w_gate_up, b_b_gate, b_norm_g, b_w_out, c_w_in, c_b_in, c_sinks, c_w_out, f_w_up,
                         f_conv_w, f_conv_b, f_w_down)
    n_a, n_b, n_c = a_w_in.shape[0], b_w_in.shape[0], c_w_in.shape[0]
    bp = x_prompt.shape[0]
    st_p = {'a_c': jnp.zeros((n_a, bp, HA, DKA, DVA), F32),
            'a_n': jnp.zeros((n_a, bp, HA, DKA), F32),
            'a_m': jnp.zeros((n_a, bp, HA), F32),
            'b_s': jnp.zeros((n_b, bp, HB, DKB, DVB), F32),
            'c_k': [None] * n_c, 'c_v': [None] * n_c, 'f': None}
    st_s = {'a_c': state_mlstm_c, 'a_n': state_mlstm_n, 'a_m': state_mlstm_m, 'b_s': state_gla,
            'c_k': cache_swa_k, 'c_v': cache_swa_v, 'f': state_ffn_conv}
    y_prompt, np_ = _trunk(x_prompt, st_p, w, True)
    y_sample, ns_ = _trunk(x_sample, st_s, w, False)
    return (y_prompt, y_sample,
            np_['a_c'], np_['a_n'], np_['a_m'], np_['b_s'], np_['c_k'], np_['c_v'], np_['f'],
            ns_['a_c'], ns_['a_n'], ns_['a_m'], ns_['b_s'], ns_['c_k'], ns_['c_v'], ns_['f'])
```

```python
import functools
import math

import jax
import jax.numpy as jnp
from jax import lax
from jax.experimental import pallas as pl
from jax.experimental.pallas import tpu as pltpu

F32 = jnp.float32
BF16 = jnp.bfloat16

D_MODEL = 1024
DEPTH = 4
N_MIXERS = 3
HA, DKA, DVA = 8, 64, 128
GATE_SOFTCAP = 15.0
HB, DKB, DVB = 4, 128, 256
GLA_RANK = 16
GLA_TAU = 16.0
HC, HKV, HDC = 16, 4, 64
WINDOW = 128
D_FF = 2816
CONV_W = 3
EPS = 1e-6
NEG = -1e30
LOG2_E = 1.4426950408889634

ROWS = 128
LANES = 128
SUBLANES = 8
SAMPLE_T_PAD = 8
GLA_SUB = 8
FFN_CHUNK = 256
PROJ_CHUNK = 512
MIXER_UNITS = 4
TAIL_TM_PROMPT = 512
TAIL_TM_SAMPLE = 256
VMEM_LIMIT = 56 * 1024 * 1024


def _params(sem):
    return pltpu.CompilerParams(dimension_semantics=sem, vmem_limit_bytes=VMEM_LIMIT)


def _dot(a, b):
    return jnp.dot(a.astype(BF16), b.astype(BF16), preferred_element_type=F32)


def _dot_nt(a, b):
    return lax.dot_general(a.astype(BF16), b.astype(BF16), (((1,), (1,)), ((), ())),
                           preferred_element_type=F32)


def _dot_tn(a, b):
    return lax.dot_general(a.astype(BF16), b.astype(BF16), (((0,), (0,)), ((), ())),
                           preferred_element_type=F32)


def _sigmoid(x):
    return 1.0 / (1.0 + jnp.exp(-x))


def _log_sigmoid(x):
    return jnp.minimum(x, 0.0) - jnp.log(1.0 + jnp.exp(-jnp.abs(x)))


def _softcap(z):
    return GATE_SOFTCAP * jnp.tanh(z / GATE_SOFTCAP)


def _col(x, h):
    lane = lax.broadcasted_iota(jnp.int32, x.shape, 1)
    return jnp.sum(jnp.where(lane == h, x, 0.0), axis=-1, keepdims=True)


def _expand_heads(x, n_heads, width, terms):
    lanes = x.shape[1]
    src = lax.broadcasted_iota(jnp.int32, (terms * lanes, n_heads * width), 0) % lanes
    dst = lax.broadcasted_iota(jnp.int32, (terms * lanes, n_heads * width), 1) // width
    pieces, rest = [], x
    for _ in range(terms):
        piece = rest.astype(BF16)
        pieces.append(piece)
        rest = rest - piece.astype(F32)
    return jnp.dot(jnp.concatenate(pieces, axis=1), (src == dst).astype(BF16), preferred_element_type=F32)


def _rep_rows(x, reps):
    g, c = x.shape
    if g == 1:
        return jnp.broadcast_to(x, (reps, c))
    return jnp.concatenate([jnp.broadcast_to(x[b:b + 1], (reps, c)) for b in range(g)], axis=0)


def _seg_scan(x, seg_len, op, tpos):
    s = 1
    while s < seg_len:
        shifted = pltpu.roll(x, s, axis=0)
        x = jnp.where(tpos >= s, op(x, shifted), x)
        s *= 2
    return x


def _seg_cumsum(x, seg_len):
    rows = x.shape[0]
    dst = lax.broadcasted_iota(jnp.int32, (rows, 3 * rows), 0)
    src = lax.broadcasted_iota(jnp.int32, (rows, 3 * rows), 1) % rows
    tri = ((src <= dst) & ((src // seg_len) == (dst // seg_len))).astype(BF16)
    pieces, rest = [], x
    for _ in range(3):
        piece = rest.astype(BF16)
        pieces.append(piece)
        rest = rest - piece.astype(F32)
    return jnp.dot(tri, jnp.concatenate(pieces, axis=0), preferred_element_type=F32)


def _expand_blockdiag(x, gb, seg_len):
    if gb == 1:
        return x
    seq = lax.broadcasted_iota(jnp.int32, x.shape, 0) // seg_len
    return jnp.concatenate([jnp.where(seq == b, x, 0.0) for b in range(gb)], axis=1)


def _resident(arr, layer=None):
    shape = arr.shape if layer is None else arr.shape[1:]
    block = shape if layer is None else (None,) + shape
    lead = () if layer is None else (layer,)
    return pl.BlockSpec(block, lambda *_: lead + (0,) * len(shape), pipeline_mode=pl.Buffered(1))


def _project(x, rows, gn_ref, wn_ref, bn_ref, wne_ref, p_ref, pe_ref, e_main):
    ms = jnp.mean(x * x, axis=-1, keepdims=True)
    xn = (x * lax.rsqrt(ms + EPS) * gn_ref[...]).astype(BF16)
    for c in range(e_main // PROJ_CHUNK):
        cols = slice(c * PROJ_CHUNK, (c + 1) * PROJ_CHUNK)
        acc = jnp.dot(xn, wn_ref[:, cols], preferred_element_type=F32)
        if bn_ref is not None:
            acc = acc + bn_ref[:, cols]
        p_ref[rows, cols] = acc.astype(p_ref.dtype)
    if wne_ref is not None:
        pe_ref[rows, :] = jnp.dot(xn, wne_ref[...], preferred_element_type=F32)


def _proj_operands(spec, n, tm):
    g, g_layer = spec['g']
    w, w_layer = spec['w']
    in_specs = [_resident(g, g_layer), _resident(w, w_layer)]
    args = [g, w]
    if spec['bias'] is not None:
        in_specs.append(_resident(*spec['bias']))
        args.append(spec['bias'][0])
    out_specs = [pl.BlockSpec((tm, spec['e_main']), lambda i: (i, 0))]
    out_shape = [jax.ShapeDtypeStruct((n, spec['e_main']), spec['out_dtype'])]
    if spec['w_extra'] is not None:
        in_specs.append(_resident(*spec['w_extra']))
        args.append(spec['w_extra'][0])
        ex = spec['w_extra'][0].shape[-1]
        out_specs.append(pl.BlockSpec((tm, ex), lambda i: (i, 0)))
        out_shape.append(jax.ShapeDtypeStruct((n, ex), F32))
    return in_specs, args, out_specs, out_shape


def _proj_kernel(*refs, e_main, has_bias, has_extra, tm):
    refs = list(refs)
    take = lambda cond=True: refs.pop(0) if cond else None
    x_ref, gn_ref, wn_ref = take(), take(), take()
    bn_ref, wne_ref = take(has_bias), take(has_extra)
    p_ref, pe_ref = take(), take(has_extra)
    for r in range(2):
        rows = slice(r * tm // 2, (r + 1) * tm // 2)
        _project(x_ref[rows, :], rows, gn_ref, wn_ref, bn_ref, wne_ref, p_ref, pe_ref, e_main)


def _norm_matmul(x, spec, tm=512):
    n, d = x.shape
    assert n % tm == 0 and spec['e_main'] % PROJ_CHUNK == 0
    in_specs, args, out_specs, out_shape = _proj_operands(spec, n, tm)
    return pl.pallas_call(
        functools.partial(_proj_kernel, e_main=spec['e_main'], has_bias=spec['bias'] is not None,
                          has_extra=spec['w_extra'] is not None, tm=tm),
        grid=(n // tm,),
        in_specs=[pl.BlockSpec((tm, d), lambda i: (i, 0))] + in_specs,
        out_specs=out_specs, out_shape=out_shape,
        compiler_params=_params(("parallel",)),
        name="norm_matmul",
    )(x, *args)


def _tail_kernel(*refs, tm, seq_len, carry_mode, apply_final, proj_main, proj_bias, proj_extra):
    refs = list(refs)
    take = lambda cond=True: refs.pop(0) if cond else None
    x_ref, y_ref, wo_ref, g_ref, wu_ref, cw_ref, cb_ref, wd_ref = [take() for _ in range(8)]
    st_ref = take(not carry_mode)
    fg_ref = take(apply_final)
    has_proj = proj_main > 0
    gn_ref, wn_ref = take(has_proj), take(has_proj)
    bn_ref, wne_ref = take(proj_bias), take(proj_extra)
    o_ref, go_ref = take(), take()
    p_ref, pe_ref = take(has_proj), take(proj_extra)
    h_ref = take()
    carry_ref = take(carry_mode)

    d_ff = wd_ref.shape[0]
    if carry_mode:
        @pl.when((pl.program_id(0) * tm) % seq_len == 0)
        def _():
            carry_ref[...] = jnp.zeros_like(carry_ref)
        row = lax.broadcasted_iota(jnp.int32, (tm, FFN_CHUNK), 0)
    else:
        n_seq = tm // seq_len
        tpos = lax.broadcasted_iota(jnp.int32, (n_seq, seq_len, FFN_CHUNK), 1)

    x2_halves, xn_halves = [], []
    for r in range(2):
        rows = slice(r * tm // 2, (r + 1) * tm // 2)
        x2_r = x_ref[rows, :] + jnp.dot(y_ref[rows, :].astype(BF16), wo_ref[...], preferred_element_type=F32)
        ms = jnp.mean(x2_r * x2_r, axis=-1, keepdims=True)
        x2_halves.append(x2_r)
        xn_halves.append((x2_r * lax.rsqrt(ms + EPS) * g_ref[...]).astype(BF16))
    x2 = jnp.concatenate(x2_halves, axis=0)
    xn = jnp.concatenate(xn_halves, axis=0)

    def rows_dot(w, split):
        if split:
            return jnp.concatenate([jnp.dot(h, w, preferred_element_type=F32) for h in xn_halves], axis=0)
        return jnp.dot(xn, w, preferred_element_type=F32)

    for c in range(d_ff // FFN_CHUNK):
        cols = slice(c * FFN_CHUNK, (c + 1) * FFN_CHUNK)
        up_cols = slice(d_ff + c * FFN_CHUNK, d_ff + (c + 1) * FFN_CHUNK)
        gate = rows_dot(wu_ref[:, cols], c == 0)
        up = rows_dot(wu_ref[:, up_cols], c == 0)
        if carry_mode:
            prev = carry_ref[:, cols]
            p1, p2 = prev[SUBLANES - 1:SUBLANES], prev[SUBLANES - 2:SUBLANES - 1]
            g1 = jnp.where(row == 0, p1, pltpu.roll(gate, 1, axis=0))
            g2 = jnp.where(row == 0, p2, jnp.where(row == 1, p1, pltpu.roll(gate, 2, axis=0)))
            tail = gate[tm - SUBLANES:tm]
            carry_ref[:, cols] = tail
            go_ref[0, :, cols] = tail
        else:
            gate3 = gate.reshape(n_seq, seq_len, FFN_CHUNK)
            st = st_ref[:, :, cols]
            g1 = jnp.where(tpos >= 1, pltpu.roll(gate3, 1, axis=1), pltpu.roll(st, 1, axis=1))
            g2 = jnp.where(tpos >= 2, pltpu.roll(gate3, 2, axis=1), pltpu.roll(st, 2, axis=1))
            g1 = g1.reshape(tm, FFN_CHUNK)
            g2 = g2.reshape(tm, FFN_CHUNK)
            go_ref[:, cols] = gate
        cw = cw_ref[:, cols]
        conv = cb_ref[:, cols] + cw[0:1] * g2 + cw[1:2] * g1 + cw[2:3] * gate
        h_ref[:, cols] = (conv * _sigmoid(conv) * up).astype(BF16)

    out = x2 + jnp.dot(h_ref[...], wd_ref[...], preferred_element_type=F32)
    if apply_final:
        ms = jnp.mean(out * out, axis=-1, keepdims=True)
        out = out * lax.rsqrt(ms + EPS) * fg_ref[...]
    o_ref[...] = out

    if has_proj:
        _project(out, slice(None), gn_ref, wn_ref, bn_ref, wne_ref, p_ref, pe_ref, proj_main)


def _layer_tail(x, y, w_out, ffn, layer, seq_len, conv_state=None, final_g=None, next_proj=None, tm=256):
    n, d = x.shape
    f = ffn['w_down'].shape[1]
    assert n % tm == 0 and f % FFN_CHUNK == 0
    carry_mode = conv_state is None
    apply_final = final_g is not None
    in_specs = [pl.BlockSpec((tm, d), lambda i: (i, 0)),
                pl.BlockSpec((tm, d), lambda i: (i, 0)),
                _resident(*w_out)]
    args = [x, y, w_out[0]]
    for name in ('g', 'w_up', 'conv_w', 'conv_b', 'w_down'):
        in_specs.append(_resident(ffn[name], layer))
        args.append(ffn[name])
    scratch = []
    if carry_mode:
        assert seq_len % tm == 0
        go_shape = jax.ShapeDtypeStruct((n // tm, SUBLANES, f), F32)
        go_spec = pl.BlockSpec((1, SUBLANES, f), lambda i: (i, 0, 0))
        scratch.append(pltpu.VMEM((SUBLANES, f), F32))
    else:
        assert seq_len == SUBLANES and tm % seq_len == 0
        in_specs.append(pl.BlockSpec((tm // seq_len, seq_len, f), lambda i: (i, 0, 0)))
        args.append(conv_state)
        go_shape = jax.ShapeDtypeStruct((n, f), F32)
        go_spec = pl.BlockSpec((tm, f), lambda i: (i, 0))
    if apply_final:
        in_specs.append(_resident(final_g))
        args.append(final_g)
    out_specs = [pl.BlockSpec((tm, d), lambda i: (i, 0)), go_spec]
    out_shape = [jax.ShapeDtypeStruct((n, d), F32), go_shape]
    proj_main, proj_bias, proj_extra = 0, False, False
    if next_proj is not None:
        proj_main = next_proj['e_main']
        proj_bias = next_proj['bias'] is not None
        proj_extra = next_proj['w_extra'] is not None
        assert proj_main % PROJ_CHUNK == 0
        p_in_specs, p_args, p_out_specs, p_out_shape = _proj_operands(next_proj, n, tm)
        in_specs += p_in_specs
        args += p_args
        out_specs += p_out_specs
        out_shape += p_out_shape
    scratch = [pltpu.VMEM((tm, f), BF16)] + scratch
    return pl.pallas_call(
        functools.partial(_tail_kernel, tm=tm, seq_len=seq_len, carry_mode=carry_mode, apply_final=apply_final,
                          proj_main=proj_main, proj_bias=proj_bias, proj_extra=proj_extra),
        grid=(n // tm,),
        in_specs=in_specs,
        out_specs=out_specs,
        out_shape=out_shape,
        scratch_shapes=scratch,
        compiler_params=_params(("arbitrary",)),
        name="layer_tail",
    )(*args)


def _mlstm_kernel(q_ref, k_ref, v_ref, o_ref, gi_ref, gf_ref, bi_ref, bf_ref, ng_ref,
                  c0_ref, n0_ref, m0_ref, y_ref, c1_ref, n1_ref, m1_ref, fs_ref, ms_ref,
                  *, units, gb, seg_len, t_valid):
    @pl.when(pl.program_id(1) == 0)
    def _():
        c1_ref[...] = c0_ref[...]
        n1_ref[...] = n0_ref[...]
        m1_ref[...] = m0_ref[...]

    for u in range(units):
        seqs = pl.ds(u * gb, gb)
        _mlstm_unit(q_ref.at[u], k_ref.at[u], v_ref.at[u], o_ref.at[u], gi_ref.at[u], gf_ref.at[u],
                    bi_ref, bf_ref, ng_ref, y_ref.at[u], c1_ref.at[seqs], n1_ref.at[u], m1_ref.at[u],
                    fs_ref.at[u], ms_ref.at[u], gb=gb, seg_len=seg_len, t_valid=t_valid)


def _mlstm_unit(q_ref, k_ref, v_ref, o_ref, gi_ref, gf_ref, bi_ref, bf_ref, ng_ref,
                y_ref, c1_ref, n1_ref, m1_ref, fs_ref, ms_ref, *, gb, seg_len, t_valid):
    shape = (ROWS, LANES)
    row = lax.broadcasted_iota(jnp.int32, shape, 0)
    lane = lax.broadcasted_iota(jnp.int32, shape, 1)
    tpos = row % seg_len
    low_half = lane < DKA

    ig = _softcap(gi_ref[...] + bi_ref[...])
    lf = _log_sigmoid(_softcap(gf_ref[...] + bf_ref[...]))
    if t_valid < seg_len:
        valid = tpos < t_valid
        ig = jnp.where(valid, ig, NEG)
        lf = jnp.where(valid, lf, 0.0)
    f_cum = _seg_scan(lf, seg_len, jnp.add, tpos)
    a = ig - f_cum
    cmax = _seg_scan(a, seg_len, jnp.maximum, tpos)
    m_prev_g = m1_ref[...]
    m_prev = _rep_rows(m_prev_g, seg_len)
    m_t = f_cum + jnp.maximum(m_prev, cmax)
    g = jnp.exp(jnp.minimum(f_cum + m_prev - m_t, 0.0))
    u = f_cum - m_t
    em = jnp.exp(-m_t)

    fs_ref[...] = f_cum
    ms_ref[...] = m_t
    f_last_g = fs_ref[pl.ds(seg_len - 1, gb, stride=seg_len), :]
    m_last_g = ms_ref[pl.ds(seg_len - 1, gb, stride=seg_len), :]
    f_last = _rep_rows(f_last_g, seg_len)
    m_last = _rep_rows(m_last_g, seg_len)
    w_last = jnp.exp(jnp.minimum(a + f_last - m_last, 0.0))
    g_last_g = jnp.exp(jnp.minimum(f_last_g + m_prev_g - m_last_g, 0.0))

    a_t = a.T
    u_w = _expand_heads(u, HA, DVA, 3)
    g_w = _expand_heads(g, HA, DVA, 2)
    em_w = _expand_heads(em, HA, DVA, 2)
    wl_w = _expand_heads(w_last, HA, DKA, 2)
    gl_rows = g_last_g if gb >= SUBLANES else jnp.broadcast_to(g_last_g, (SUBLANES, LANES))
    gl_w = _expand_heads(gl_rows, HA, DKA, 2)[:gb]

    col_i = lax.broadcasted_iota(jnp.int32, (ROWS, 2 * ROWS), 1) % ROWS
    row_i = lax.broadcasted_iota(jnp.int32, (ROWS, 2 * ROWS), 0)
    causal2 = (col_i <= row_i) & ((col_i // seg_len) == (row_i // seg_len))
    state_low = lax.broadcasted_iota(jnp.int32, (gb * LANES, LANES), 0) % LANES < DKA
    flat_low = lax.broadcasted_iota(jnp.int32, (LANES, gb * LANES), 1) % LANES < DKA
    sel = (lax.broadcasted_iota(jnp.int32, (max(gb, SUBLANES), ROWS), 1) // seg_len
           == lax.broadcasted_iota(jnp.int32, (max(gb, SUBLANES), ROWS), 0)).astype(BF16)
    ones_blk = (lax.broadcasted_iota(jnp.int32, (2 * ROWS, 2 * DVA), 0) // ROWS
                == lax.broadcasted_iota(jnp.int32, (2 * ROWS, 2 * DVA), 1) // DVA).astype(BF16)
    zeros_v = jnp.zeros((ROWS, DVA), v_ref.dtype)

    q_all = q_ref[...]
    k_all = k_ref[...] * (DKA ** -0.5)
    k2_all = k_all * wl_w

    for p in range(HA // 2):
        cols = slice(p * LANES, (p + 1) * LANES)
        wide = slice(2 * p * DVA, (2 * p + 2) * DVA)
        qp, kp, k2 = q_all[:, cols], k_all[:, cols], k2_all[:, cols]
        c_p = c1_ref[:, p].reshape(gb * LANES, DVA)
        n_p = n1_ref[p]
        v2 = v_ref[:, wide]
        v_a, v_b = v2[:, :DVA], v2[:, DVA:]

        k_sep = jnp.concatenate([jnp.where(low_half, kp, 0.0), jnp.where(low_half, 0.0, kp)], axis=0)
        s2 = _dot_nt(qp, k_sep)
        logw2 = jnp.concatenate([a_t[2 * p:2 * p + 1], a_t[2 * p + 1:2 * p + 2]], axis=1) + u_w[:, wide]
        qk2 = (s2 * jnp.where(causal2, jnp.exp(jnp.minimum(logw2, 0.0)), 0.0)).astype(BF16)
        v_bd = jnp.concatenate([jnp.concatenate([v_a, zeros_v], axis=1),
                                jnp.concatenate([zeros_v, v_b], axis=1)], axis=0)
        num2 = jnp.dot(qk2, v_bd, preferred_element_type=F32)
        dsum2 = jnp.dot(qk2, ones_blk, preferred_element_type=F32)

        qx = _expand_blockdiag(qp, gb, seg_len)
        c_sep = jnp.concatenate([jnp.where(state_low, c_p, 0.0), jnp.where(state_low, 0.0, c_p)], axis=1)
        inter2 = _dot(qx, c_sep)
        n_flat = n_p if gb == 1 else jnp.concatenate([n_p[b:b + 1] for b in range(gb)], axis=1)
        n_b = jnp.broadcast_to(n_flat, (LANES, gb * LANES))
        n_sep = jnp.concatenate([jnp.where(flat_low, n_b, 0.0), jnp.where(flat_low, 0.0, n_b)], axis=0)
        dint2 = _dot_nt(qx, n_sep)

        g2 = g_w[:, wide]
        hh2 = (num2 + g2 * inter2) / jnp.maximum(jnp.abs(dsum2 + g2 * dint2), em_w[:, wide])
        for half in range(2):
            hcols = slice((2 * p + half) * DVA, (2 * p + half + 1) * DVA)
            hh = hh2[:, half * DVA:(half + 1) * DVA]
            msq = jnp.mean(hh * hh, axis=-1, keepdims=True)
            hn = hh * lax.rsqrt(msq + EPS) * ng_ref[:, hcols]
            y_ref[:, hcols] = (hn * _sigmoid(o_ref[:, hcols].astype(F32))).astype(y_ref.dtype)

        upd2 = _dot_tn(_expand_blockdiag(k2, gb, seg_len), v2)
        upd = jnp.where(state_low, upd2[:, :DVA], upd2[:, DVA:])
        n_scale = gl_w[:, cols]
        scale = jnp.concatenate([jnp.broadcast_to(n_scale[b:b + 1], (LANES, LANES)).T for b in range(gb)],
                                axis=0)
        c1_ref[:, p] = (scale * c_p + upd).reshape(gb, LANES, DVA)
        n_upd = jnp.dot(sel, k2.astype(BF16), preferred_element_type=F32)[:gb]
        n1_ref[p] = n_scale * n_p + n_upd

    m1_ref[...] = m_last_g


def _mlstm(proj, gates, b_i, b_f, norm_g, c_all, layer, n0, m0, gb, seg_len, t_valid, n_chunks, units):
    n = proj.shape[0]
    n_blocks = n // (ROWS * n_chunks)
    assert n_blocks % units == 0
    bsz = n_blocks * gb
    half_pairs = HA // 2
    rows3 = lambda a: a.reshape(n_blocks, n_chunks * ROWS, a.shape[-1])
    proj3, gates3 = rows3(proj), rows3(gates)
    c0r = c_all.reshape(c_all.shape[0], bsz, half_pairs, LANES, DVA)
    n0r = n0.reshape(n_blocks, gb, half_pairs, LANES).transpose(0, 2, 1, 3)
    m0r = jnp.pad(m0, ((0, 0), (0, LANES - HA))).reshape(n_blocks, gb, LANES)
    pad = lambda b: jnp.pad(b, (0, LANES - HA)).reshape(1, LANES)
    blk = lambda width, cb: pl.BlockSpec((units, ROWS, width), lambda i, c: (i, c, cb))
    c_blk = (units * gb, half_pairs, LANES, DVA)
    y, c1, n1, m1 = pl.pallas_call(
        functools.partial(_mlstm_kernel, units=units, gb=gb, seg_len=seg_len, t_valid=t_valid),
        grid=(n_blocks // units, n_chunks),
        in_specs=[blk(HA * DKA, 0), blk(HA * DKA, 1), blk(HA * DVA, 1), blk(HA * DVA, 2),
                  blk(LANES, 0), blk(LANES, 1),
                  pl.BlockSpec((1, LANES), lambda i, c: (0, 0)),
                  pl.BlockSpec((1, LANES), lambda i, c: (0, 0)),
                  pl.BlockSpec((1, HA * DVA), lambda i, c: (0, 0)),
                  pl.BlockSpec((None,) + c_blk, lambda i, c: (layer, i, 0, 0, 0)),
                  pl.BlockSpec((units, half_pairs, gb, LANES), lambda i, c: (i, 0, 0, 0)),
                  pl.BlockSpec((units, gb, LANES), lambda i, c: (i, 0, 0))],
        out_specs=[blk(HA * DVA, 0),
                   pl.BlockSpec(c_blk, lambda i, c: (i, 0, 0, 0)),
                   pl.BlockSpec((units, half_pairs, gb, LANES), lambda i, c: (i, 0, 0, 0)),
                   pl.BlockSpec((units, gb, LANES), lambda i, c: (i, 0, 0))],
        out_shape=[jax.ShapeDtypeStruct((n_blocks, n_chunks * ROWS, HA * DVA), BF16),
                   jax.ShapeDtypeStruct(c0r.shape[1:], F32),
                   jax.ShapeDtypeStruct(n0r.shape, F32),
                   jax.ShapeDtypeStruct(m0r.shape, F32)],
        scratch_shapes=[pltpu.VMEM((units, ROWS, LANES), F32), pltpu.VMEM((units, ROWS, LANES), F32)],
        compiler_params=_params(("parallel", "arbitrary")),
        name="mlstm_mixer",
    )(proj3, proj3, proj3, proj3, gates3, gates3, pad(b_i), pad(b_f), norm_g.reshape(1, -1), c0r, n0r, m0r)
    y = y.reshape(n, HA * DVA)
    c1 = c1.reshape(bsz, HA, DKA, DVA)
    n1 = n1.transpose(0, 2, 1, 3).reshape(bsz, HA, DKA)
    m1 = m1.reshape(bsz, LANES)[:, :HA]
    return y, c1, n1, m1


def _gla_kernel(q_ref, k_ref, v_ref, r_ref, z_ref, wgu_ref, bg_ref, ng_ref, s0_ref,
                y_ref, s1_ref, bc_ref, b2_ref, kf_ref, *, units, gb, seg_len, t_valid, heads):
    @pl.when(pl.program_id(2) == 0)
    def _():
        s1_ref[...] = s0_ref[...]

    for u in range(units):
        _gla_unit(q_ref.at[u], k_ref.at[u], v_ref.at[u], r_ref.at[u], z_ref.at[u], wgu_ref, bg_ref, ng_ref,
                  y_ref.at[u], s1_ref.at[pl.ds(u * gb, gb)], bc_ref.at[u], b2_ref.at[u], kf_ref.at[u],
                  gb=gb, seg_len=seg_len, t_valid=t_valid, heads=heads)


def _gla_unit(q_ref, k_ref, v_ref, r_ref, z_ref, wgu_ref, bg_ref, ng_ref, y_ref, s1_ref, bc_ref, b2_ref, kf_ref,
              *, gb, seg_len, t_valid, heads):
    width = heads * DKB
    row = lax.broadcasted_iota(jnp.int32, (ROWS, width), 0)
    tpos = row % seg_len
    lg = _log_sigmoid(jnp.dot(z_ref[...].astype(BF16), wgu_ref[...], preferred_element_type=F32)
                      + bg_ref[...]) / GLA_TAU
    k_all = k_ref[...].astype(F32)
    if t_valid < seg_len:
        valid = tpos < t_valid
        lg = jnp.where(valid, lg, 0.0)
        k_all = jnp.where(valid, k_all, 0.0)
    bc = _seg_cumsum(lg, seg_len)

    n_sub = ROWS // GLA_SUB
    sub_row = lax.broadcasted_iota(jnp.int32, (GLA_SUB, ROWS), 0)
    sub_col = lax.broadcasted_iota(jnp.int32, (GLA_SUB, ROWS), 1)

    for h in range(heads):
        kcols = slice(h * DKB, (h + 1) * DKB)
        vcols = slice(h * DVB, (h + 1) * DVB)
        qh = q_ref[:, kcols].astype(F32) * (DKB ** -0.5)
        kh = k_all[:, kcols]
        vh = v_ref[:, vcols]
        bch = bc[:, kcols]
        bc_ref[...] = bch
        b2_ref[...] = bch * LOG2_E
        kf_ref[...] = kh
        b_last_g = bc_ref[pl.ds(seg_len - 1, gb, stride=seg_len), :]
        b_last = _rep_rows(b_last_g, seg_len)
        s_h = s1_ref[:, h].reshape(gb * DKB, DVB)

        inter = _dot(_expand_blockdiag(qh * jnp.exp(bch), gb, seg_len), s_h)

        att_rows = []
        k_decayed = []
        for blk in range(n_sub):
            r0 = blk * GLA_SUB
            q_b = qh[r0:r0 + GLA_SUB]
            bc_b = bch[r0:r0 + GLA_SUB]
            b2_b = b2_ref[r0:r0 + GLA_SUB, :]
            att = jnp.zeros((GLA_SUB, ROWS), F32)
            for s in range(GLA_SUB):
                k_row = jnp.broadcast_to(kf_ref[r0 + s:r0 + s + 1, :], (GLA_SUB, DKB))
                b2_row = jnp.broadcast_to(b2_ref[r0 + s:r0 + s + 1, :], (GLA_SUB, DKB))
                cs = jnp.sum(q_b * k_row * jnp.exp2(b2_b - b2_row), axis=-1, keepdims=True)
                att = jnp.where(sub_col == r0 + s, cs, att)
            att = jnp.where(sub_col <= sub_row + r0, att, 0.0)
            if seg_len > GLA_SUB and blk > 0:
                b_edge = bch[r0 - 1:r0]
                if blk > 1:
                    step = jnp.exp(b_edge - bch[r0 - GLA_SUB - 1:r0 - GLA_SUB])
                    k_decayed = [kb * step for kb in k_decayed]
                k_decayed.append(kh[r0 - GLA_SUB:r0] * jnp.exp(b_edge - bch[r0 - GLA_SUB:r0]))
                k_t = jnp.concatenate(k_decayed + [jnp.zeros((ROWS - r0, DKB), F32)], axis=0)
                att = att + _dot_nt(q_b * jnp.exp(bc_b - b_edge), k_t)
            att_rows.append(att)
        att_full = jnp.concatenate(att_rows, axis=0)

        o = inter + _dot(att_full, vh)
        msq = jnp.mean(o * o, axis=-1, keepdims=True)
        on = o * lax.rsqrt(msq + EPS) * ng_ref[:, vcols]
        r = r_ref[:, vcols].astype(F32)
        y_ref[:, vcols] = (on * (r * _sigmoid(r))).astype(y_ref.dtype)

        k_d = kh * jnp.exp(b_last - bch)
        upd = _dot_tn(_expand_blockdiag(k_d, gb, seg_len), vh)
        e_last = jnp.exp(b_last_g)
        e_cols = jnp.concatenate(
            [jnp.broadcast_to(e_last[b:b + 1], (DKB, DKB)).T for b in range(gb)], axis=0)
        e_cols = jnp.concatenate([e_cols] * (DVB // DKB), axis=1)
        s1_ref[:, h] = (e_cols * s_h + upd).reshape(gb, DKB, DVB)


def _gla(proj, z, w_gate_up, b_gate, norm_g, s0, gb, seg_len, t_valid, n_chunks, heads, units):
    n = proj.shape[0]
    n_blocks = n // (ROWS * n_chunks)
    assert n_blocks % units == 0 and seg_len % GLA_SUB == 0
    n_hb = HB // heads
    wq, wv = heads * DKB, heads * DVB
    q_off, k_off = 0, (HB * DKB) // wq
    v_off, r_off = (2 * HB * DKB) // wv, (2 * HB * DKB + HB * DVB) // wv
    rows3 = lambda a: a.reshape(n_blocks, n_chunks * ROWS, a.shape[-1])
    proj3, z3 = rows3(proj), rows3(z)
    blk = lambda width, off: pl.BlockSpec((units, ROWS, width), lambda i, hb, c: (i, c, off + hb))
    state_spec = pl.BlockSpec((units * gb, heads, DKB, DVB), lambda i, hb, c: (i, hb, 0, 0))
    wgu = jnp.zeros((LANES, HB * DKB), BF16).at[:GLA_RANK].set(w_gate_up.astype(BF16))
    y, s1 = pl.pallas_call(
        functools.partial(_gla_kernel, units=units, gb=gb, seg_len=seg_len, t_valid=t_valid, heads=heads),
        grid=(n_blocks // units, n_hb, n_chunks),
        in_specs=[blk(wq, q_off), blk(wq, k_off), blk(wv, v_off), blk(wv, r_off),
                  pl.BlockSpec((units, ROWS, LANES), lambda i, hb, c: (i, c, 0)),
                  pl.BlockSpec((LANES, wq), lambda i, hb, c: (0, hb)),
                  pl.BlockSpec((1, wq), lambda i, hb, c: (0, hb)),
                  pl.BlockSpec((1, wv), lambda i, hb, c: (0, hb)),
                  state_spec],
        out_specs=[blk(wv, 0), state_spec],
        out_shape=[jax.ShapeDtypeStruct((n_blocks, n_chunks * ROWS, HB * DVB), BF16),
                   jax.ShapeDtypeStruct(s0.shape, F32)],
        scratch_shapes=[pltpu.VMEM((units, ROWS, DKB), F32)] * 3,
        compiler_params=_params(("parallel", "parallel", "arbitrary")),
        name="gla_mixer",
    )(proj3, proj3, proj3, proj3, z3, wgu, b_gate.reshape(1, -1), norm_g.reshape(1, -1), s0)
    return y.reshape(n, HB * DVB), s1


def _swa_kernel(q_ref, kc_ref, vc_ref, kp_ref, vp_ref, sink_ref, o_ref, *, blocks, units, q_len, prev_from_grid):
    for b in range(blocks):
        _swa_block(q_ref.at[b], kc_ref.at[b], vc_ref.at[b], kp_ref.at[b], vp_ref.at[b], sink_ref, o_ref.at[b],
                   units=units, q_len=q_len, prev_from_grid=prev_from_grid)


def _swa_block(q_ref, kc_ref, vc_ref, kp_ref, vp_ref, sink_ref, o_ref, *, units, q_len, prev_from_grid):
    group = HC // HKV
    rows4 = group * ROWS
    row = lax.broadcasted_iota(jnp.int32, (rows4, ROWS), 0) % ROWS
    key = lax.broadcasted_iota(jnp.int32, (rows4, ROWS), 1)
    q_pos = row % q_len
    mask_p = key >= q_pos
    if prev_from_grid:
        mask_p = mask_p & (pl.program_id(1) > 0)
    mask_c = ((key // q_len) == (row // q_len)) & ((key % q_len) <= q_pos)
    bias_p = jnp.where(mask_p, 0.0, NEG)
    bias_c = jnp.where(mask_c, 0.0, NEG)
    low = lax.broadcasted_iota(jnp.int32, (ROWS, LANES), 1) < HDC
    ones = jnp.ones((WINDOW, LANES), BF16)
    sinks = sink_ref[...]
    scale = HDC ** -0.5

    def per_unit(lhs, rhs_ref, cols, contract_rhs_rows):
        mm = _dot if contract_rhs_rows else _dot_nt
        if units == 1:
            return mm(lhs, rhs_ref[:, cols])
        pieces = [[None] * units for _ in range(group)]
        for u in range(units):
            rows_u = jnp.concatenate([lhs[gi * ROWS + u * q_len:gi * ROWS + (u + 1) * q_len]
                                      for gi in range(group)], axis=0)
            res = mm(rows_u, rhs_ref[u * WINDOW:(u + 1) * WINDOW, cols])
            for gi in range(group):
                pieces[gi][u] = res[gi * q_len:(gi + 1) * q_len]
        return jnp.concatenate([pieces[gi][u] for gi in range(group) for u in range(units)], axis=0)

    out_tiles = [None] * (HC // 2)
    for kk in range(HKV):
        kv_cols = slice((kk // 2) * LANES, (kk // 2 + 1) * LANES)
        k_half = kk % 2
        k_sel = low if k_half == 0 else ~low
        q_parts, sink_parts = [], []
        for gi in range(group):
            hq = kk * group + gi
            tile = q_ref[:, (hq // 2) * LANES:(hq // 2 + 1) * LANES] * scale
            if hq % 2 != k_half:
                tile = pltpu.roll(tile, HDC, axis=1)
            q_parts.append(jnp.where(k_sel, tile, 0.0))
            sink_parts.append(jnp.broadcast_to(_col(sinks, hq), (ROWS, LANES)))
        q4 = jnp.concatenate(q_parts, axis=0)
        sink = jnp.concatenate(sink_parts, axis=0)
        s_p = per_unit(q4, kp_ref, kv_cols, False) + bias_p
        s_c = _dot_nt(q4, kc_ref[:, kv_cols]) + bias_c
        row_max = jnp.max(jnp.maximum(s_p, s_c), axis=-1, keepdims=True)
        mx = jnp.maximum(jnp.broadcast_to(row_max, (rows4, LANES)), sink)
        p_p = jnp.exp(s_p - mx).astype(BF16)
        p_c = jnp.exp(s_c - mx).astype(BF16)
        den = (jnp.dot(p_p, ones, preferred_element_type=F32) + jnp.dot(p_c, ones, preferred_element_type=F32)
               + jnp.exp(sink - mx))
        out4 = (per_unit(p_p, vp_ref, kv_cols, True) + _dot(p_c, vc_ref[:, kv_cols])) / den
        for gi in range(group):
            hq = kk * group + gi
            part = out4[gi * ROWS:(gi + 1) * ROWS]
            if hq % 2 != k_half:
                part = pltpu.roll(part, HDC, axis=1)
            prev = out_tiles[hq // 2]
            out_tiles[hq // 2] = part if prev is None else jnp.where(low if hq % 2 == 0 else ~low, part, prev)
    for t in range(HC // 2):
        o_ref[:, t * LANES:(t + 1) * LANES] = out_tiles[t].astype(o_ref.dtype)


def _swa(proj, sinks, units, q_len, n_chunks, blocks, prev_k=None, prev_v=None):
    n = proj.shape[0]
    n_chains = n // (ROWS * n_chunks)
    assert n_chains % blocks == 0
    kv_w = HKV * HDC
    k_blk, v_blk = (HC * HDC) // kv_w, (HC * HDC) // kv_w + 1
    prev_from_grid = prev_k is None
    proj3 = proj.reshape(n_chains, n_chunks * ROWS, proj.shape[-1])
    cur = lambda blk: pl.BlockSpec((blocks, ROWS, kv_w), lambda i, c: (i, c, blk))
    if prev_from_grid:
        prev = lambda blk: pl.BlockSpec((blocks, ROWS, kv_w), lambda i, c: (i, jnp.maximum(c - 1, 0), blk))
        prev_specs = [prev(k_blk), prev(v_blk)]
        prev_args = [proj3, proj3]
    else:
        prev_specs = [pl.BlockSpec((blocks, units * WINDOW, kv_w), lambda i, c: (i, 0, 0))] * 2
        prev_args = [prev_k.reshape(n_chains, units * WINDOW, kv_w), prev_v.reshape(n_chains, units * WINDOW, kv_w)]
    sink_row = jnp.pad(sinks.astype(F32), (0, LANES - HC)).reshape(1, LANES)
    y = pl.pallas_call(
        functools.partial(_swa_kernel, blocks=blocks, units=units, q_len=q_len, prev_from_grid=prev_from_grid),
        grid=(n_chains // blocks, n_chunks),
        in_specs=[pl.BlockSpec((blocks, ROWS, HC * HDC), lambda i, c: (i, c, 0)),
                  cur(k_blk), cur(v_blk)] + prev_specs +
                 [pl.BlockSpec((1, LANES), lambda i, c: (0, 0))],
        out_specs=pl.BlockSpec((blocks, ROWS, HC * HDC), lambda i, c: (i, c, 0)),
        out_shape=jax.ShapeDtypeStruct((n_chains, n_chunks * ROWS, HC * HDC), BF16),
        compiler_params=_params(("parallel", "arbitrary")),
        name="swa_mixer",
    )(proj3, proj3, proj3, *prev_args, sink_row)
    return y.reshape(n, HC * HDC)


def _proj_spec(w, i):
    kind, j = i % N_MIXERS, i // N_MIXERS
    spec = {'g': (w['norm_mix'], i), 'bias': None, 'w_extra': None, 'out_dtype': BF16}
    if kind == 0:
        spec.update(w=(w['a_w_in'], j), e_main=2 * HA * (DKA + DVA), w_extra=(w['a_w_gates'], j))
    elif kind == 1:
        spec.update(w=(w['b_w_in'], j), e_main=2 * HB * (DKB + DVB), w_extra=(w['b_w_z'], j))
    else:
        spec.update(w=(w['c_w_in'], j), e_main=(HC + 2 * HKV) * HDC, bias=(w['c_b_in'], j), out_dtype=F32)
    return spec


def _trunk(x3, st, w, is_prompt):
    bsz, t_in, _ = x3.shape
    if is_prompt:
        t, t_valid, gb, seg_len = t_in, ROWS, 1, ROWS
        units = math.gcd(MIXER_UNITS, bsz)
        x = x3.reshape(bsz * t, D_MODEL)
    else:
        t, t_valid, gb, seg_len = SAMPLE_T_PAD, t_in, ROWS // SAMPLE_T_PAD, SAMPLE_T_PAD
        units = 1
        x = jnp.pad(x3, ((0, 0), (0, t - t_in), (0, 0))).reshape(bsz * t, D_MODEL)
    n_chunks = (gb * t) // ROWS
    new = {'a_c': [], 'a_n': [], 'a_m': [], 'b_s': [], 'c_k': [], 'c_v': [], 'f': []}

    projected = _norm_matmul(x, _proj_spec(w, 0))
    for i in range(DEPTH):
        kind, j = i % N_MIXERS, i // N_MIXERS
        if kind == 0:
            proj, gates = projected
            y, c1, n1, m1 = _mlstm(proj, gates, w['a_b_i'][j], w['a_b_f'][j], w['a_norm'][j],
                                   st['a_c'], j, st['a_n'][j], st['a_m'][j], gb, seg_len, t_valid, n_chunks,
                                   units=units)
            new['a_c'].append(c1)
            new['a_n'].append(n1)
            new['a_m'].append(m1)
            w_out = (w['a_w_out'], j)
        elif kind == 1:
            proj, z = projected
            y, s1 = _gla(proj, z, w['b_w_gate_up'][j], w['b_b_gate'][j], w['b_norm'][j], st['b_s'][j],
                         gb, seg_len, t_valid, n_chunks, heads=HB if is_prompt else 1,
                         units=units)
            new['b_s'].append(s1)
            w_out = (w['b_w_out'], j)
        else:
            proj = projected[0]
            kv_w = HKV * HDC
            end = t if is_prompt else t_valid
            keep = min(WINDOW, end)
            newest = proj.reshape(bsz, t, -1)[:, end - keep:end, HC * HDC:]
            k_new = newest[:, :, :kv_w].reshape(bsz, keep, HKV, HDC)
            v_new = newest[:, :, kv_w:].reshape(bsz, keep, HKV, HDC)
            if is_prompt:
                y = _swa(proj, w['c_sinks'][j], 1, ROWS, n_chunks, units)
                new['c_k'].append(k_new)
                new['c_v'].append(v_new)
            else:
                k_buf, v_buf = st['c_k'][j], st['c_v'][j]
                y = _swa(proj, w['c_sinks'][j], gb, seg_len, n_chunks, 1,
                         prev_k=k_buf.reshape(bsz * WINDOW, kv_w), prev_v=v_buf.reshape(bsz * WINDOW, kv_w))
                new['c_k'].append(jnp.concatenate([k_buf[:, t_valid:], k_new], axis=1))
                new['c_v'].append(jnp.concatenate([v_buf[:, t_valid:], v_new], axis=1))
            w_out = (w['c_w_out'], j)

        last = i == DEPTH - 1
        tail_args = (x, y, w_out, w['ffn'], i)
        tail_kw = dict(seq_len=t, final_g=w['norm_final'] if last else None,
                       next_proj=None if last else _proj_spec(w, i + 1),
                       tm=TAIL_TM_PROMPT if is_prompt else TAIL_TM_SAMPLE)
        if is_prompt:
            x, gate_tail, *projected = _layer_tail(*tail_args, **tail_kw)
            seq_tails = gate_tail.reshape(bsz, t // TAIL_TM_PROMPT, SUBLANES, D_FF)[:, -1]
            new['f'].append(seq_tails[:, SUBLANES - (CONV_W - 1):])
        else:
            state = st['f'][i]
            state_pad = jnp.pad(state, ((0, 0), (t - (CONV_W - 1), 0), (0, 0)))
            x, gate_all, *projected = _layer_tail(*tail_args, conv_state=state_pad, **tail_kw)
            g_ext = jnp.concatenate([state, gate_all.reshape(bsz, t, D_FF)[:, :t_valid]], axis=1)
            new['f'].append(g_ext[:, -(CONV_W - 1):])

    out = {name: jnp.stack(vals) for name, vals in new.items()}
    y = x.reshape(bsz, t, D_MODEL)[:, :t_in]
    return y, out


def _prepare_weights(norm_mix_g, norm_ffn_g, norm_final_g, a_w_in, a_b_i, a_b_f, a_norm_g, a_w_out, b_w_in,
                     b_w_gate_up, b_b_gate, b_norm_g, b_w_out, c_w_in, c_b_in, c_sinks, c_w_out, f_w_up,
                     f_conv_w, f_conv_b, f_w_down):
    n_a, n_b = a_w_in.shape[0], b_w_in.shape[0]
    e_a = 2 * HA * (DKA + DVA)
    e_b = 2 * HB * (DKB + DVB)
    a_w_gates = jnp.zeros((n_a, D_MODEL, 2 * LANES), BF16)
    a_w_gates = a_w_gates.at[:, :, :HA].set(a_w_in[:, :, e_a:e_a + HA].astype(BF16))
    a_w_gates = a_w_gates.at[:, :, LANES:LANES + HA].set(a_w_in[:, :, e_a + HA:].astype(BF16))
    b_w_z = jnp.zeros((n_b, D_MODEL, LANES), BF16).at[:, :, :GLA_RANK].set(b_w_in[:, :, e_b:].astype(BF16))
    ffn = {'g': norm_ffn_g[:, None, :], 'w_up': f_w_up.astype(BF16), 'conv_w': f_conv_w,
           'conv_b': f_conv_b[:, None, :], 'w_down': f_w_down.astype(BF16)}
    w = {'norm_mix': norm_mix_g[:, None, :], 'norm_final': norm_final_g[None, :], 'ffn': ffn,
         'a_w_in': a_w_in[:, :, :e_a].astype(BF16), 'a_w_gates': a_w_gates, 'a_b_i': a_b_i, 'a_b_f': a_b_f,
         'a_norm': a_norm_g, 'a_w_out': a_w_out.astype(BF16),
         'b_w_in': b_w_in[:, :, :e_b].astype(BF16), 'b_w_z': b_w_z, 'b_w_gate_up': b_w_gate_up, 'b_b_gate': b_b_gate,
         'b_norm': b_norm_g, 'b_w_out': b_w_out.astype(BF16),
         'c_w_in': c_w_in.astype(BF16), 'c_b_in': c_b_in[:, None, :], 'c_sinks': c_sinks,
         'c_w_out': c_w_out.astype(BF16)}
    return w


def kernel(x_prompt, x_sample, state_mlstm_c, state_mlstm_n, state_mlstm_m, state_gla, cache_swa_k, cache_swa_v, state_ffn_conv, norm_mix_g, norm_ffn_g, norm_final_g, a_w_in, a_b_i, a_b_f, a_norm_g, a_w_out, b_w_in, b_w_gate_up, b_b_gate, b_norm_g, b_w_out, c_w_in, c_b_in, c_sinks, c_w_out, f_w_up, f_conv_w, f_conv_b, f_w_down):
    w = _prepare_weights(norm_mix_g, norm_ffn_g, norm_final_g, a_w_in, a_b_i, a_b_f, a_norm_g, a_w_out, b_w_in,
                         b_w_gate_up, b_b_gate, b_norm_g, b_w_out, c_w_in, c_b_in, c_sinks, c_w_out, f_w_up,
                         f_conv_w, f_conv_b, f_w_down)
    n_a, n_b, n_c = a_w_in.shape[0], b_w_in.shape[0], c_w_in.shape[0]
    bp = x_prompt.shape[0]
    st_p = {'a_c': jnp.zeros((n_a, bp, HA, DKA, DVA), F32),
            'a_n': jnp.zeros((n_a, bp, HA, DKA), F32),
            'a_m': jnp.zeros((n_a, bp, HA), F32),
            'b_s': jnp.zeros((n_b, bp, HB, DKB, DVB), F32),
            'c_k': [None] * n_c, 'c_v': [None] * n_c, 'f': None}
    st_s = {'a_c': state_mlstm_c, 'a_n': state_mlstm_n, 'a_m': state_mlstm_m, 'b_s': state_gla,
            'c_k': cache_swa_k, 'c_v': cache_swa_v, 'f': state_ffn_conv}
    y_prompt, np_ = _trunk(x_prompt, st_p, w, True)
    y_sample, ns_ = _trunk(x_sample, st_s, w, False)
    return (y_prompt, y_sample,
            np_['a_c'], np_['a_n'], np_['a_m'], np_['b_s'], np_['c_k'], np_['c_v'], np_['f'],
            ns_['a_c'], ns_['a_n'], ns_['a_m'], ns_['b_s'], ns_['c_k'], ns_['c_v'], ns_['f'])
```

```python
import functools
import math

import jax
import jax.numpy as jnp
from jax import lax
from jax.experimental import pallas as pl
from jax.experimental.pallas import tpu as pltpu

F32 = jnp.float32
BF16 = jnp.bfloat16

D_MODEL = 1024
DEPTH = 4
N_MIXERS = 3
HA, DKA, DVA = 8, 64, 128
GATE_SOFTCAP = 15.0
HB, DKB, DVB = 4, 128, 256
GLA_RANK = 16
GLA_TAU = 16.0
HC, HKV, HDC = 16, 4, 64
WINDOW = 128
D_FF = 2816
CONV_W = 3
EPS = 1e-6
NEG = -1e30
LOG2_E = 1.4426950408889634

ROWS = 128
LANES = 128
SUBLANES = 8
SAMPLE_T_PAD = 8
GLA_SUB = 8
FFN_CHUNK = 256
PROJ_CHUNK = 512
MIXER_UNITS = 4
TAIL_TM_PROMPT = 512
TAIL_TM_SAMPLE = 256
VMEM_LIMIT = 56 * 1024 * 1024


def _params(sem):
    return pltpu.CompilerParams(dimension_semantics=sem, vmem_limit_bytes=VMEM_LIMIT)


def _dot(a, b):
    return jnp.dot(a.astype(BF16), b.astype(BF16), preferred_element_type=F32)


def _dot_nt(a, b):
    return lax.dot_general(a.astype(BF16), b.astype(BF16), (((1,), (1,)), ((), ())),
                           preferred_element_type=F32)


def _dot_tn(a, b):
    return lax.dot_general(a.astype(BF16), b.astype(BF16), (((0,), (0,)), ((), ())),
                           preferred_element_type=F32)


def _sigmoid(x):
    return 1.0 / (1.0 + jnp.exp(-x))


def _log_sigmoid(x):
    return jnp.minimum(x, 0.0) - jnp.log(1.0 + jnp.exp(-jnp.abs(x)))


def _softcap(z):
    return GATE_SOFTCAP * jnp.tanh(z / GATE_SOFTCAP)


def _col(x, h):
    lane = lax.broadcasted_iota(jnp.int32, x.shape, 1)
    return jnp.sum(jnp.where(lane == h, x, 0.0), axis=-1, keepdims=True)


def _expand_heads(x, n_heads, width, terms):
    lanes = x.shape[1]
    src = lax.broadcasted_iota(jnp.int32, (terms * lanes, n_heads * width), 0) % lanes
    dst = lax.broadcasted_iota(jnp.int32, (terms * lanes, n_heads * width), 1) // width
    pieces, rest = [], x
    for _ in range(terms):
        piece = rest.astype(BF16)
        pieces.append(piece)
        rest = rest - piece.astype(F32)
    return jnp.dot(jnp.concatenate(pieces, axis=1), (src == dst).astype(BF16), preferred_element_type=F32)


def _rep_rows(x, reps):
    g, c = x.shape
    if g == 1:
        return jnp.broadcast_to(x, (reps, c))
    return jnp.concatenate([jnp.broadcast_to(x[b:b + 1], (reps, c)) for b in range(g)], axis=0)


def _seg_scan(x, seg_len, op, tpos):
    s = 1
    while s < seg_len:
        shifted = pltpu.roll(x, s, axis=0)
        x = jnp.where(tpos >= s, op(x, shifted), x)
        s *= 2
    return x


def _seg_cumsum(x, seg_len):
    rows = x.shape[0]
    dst = lax.broadcasted_iota(jnp.int32, (rows, 3 * rows), 0)
    src = lax.broadcasted_iota(jnp.int32, (rows, 3 * rows), 1) % rows
    tri = ((src <= dst) & ((src // seg_len) == (dst // seg_len))).astype(BF16)
    pieces, rest = [], x
    for _ in range(3):
        piece = rest.astype(BF16)
        pieces.append(piece)
        rest = rest - piece.astype(F32)
    return jnp.dot(tri, jnp.concatenate(pieces, axis=0), preferred_element_type=F32)


def _expand_blockdiag(x, gb, seg_len):
    if gb == 1:
        return x
    seq = lax.broadcasted_iota(jnp.int32, x.shape, 0) // seg_len
    return jnp.concatenate([jnp.where(seq == b, x, 0.0) for b in range(gb)], axis=1)


def _resident(arr, layer=None):
    shape = arr.shape if layer is None else arr.shape[1:]
    block = shape if layer is None else (None,) + shape
    lead = () if layer is None else (layer,)
    return pl.BlockSpec(block, lambda *_: lead + (0,) * len(shape), pipeline_mode=pl.Buffered(1))


def _project(x, rows, gn_ref, wn_ref, bn_ref, wne_ref, p_ref, pe_ref, e_main):
    ms = jnp.mean(x * x, axis=-1, keepdims=True)
    xn = (x * lax.rsqrt(ms + EPS) * gn_ref[...]).astype(BF16)
    for c in range(e_main // PROJ_CHUNK):
        cols = slice(c * PROJ_CHUNK, (c + 1) * PROJ_CHUNK)
        acc = jnp.dot(xn, wn_ref[:, cols], preferred_element_type=F32)
        if bn_ref is not None:
            acc = acc + bn_ref[:, cols]
        p_ref[rows, cols] = acc.astype(p_ref.dtype)
    if wne_ref is not None:
        pe_ref[rows, :] = jnp.dot(xn, wne_ref[...], preferred_element_type=F32)


def _proj_operands(spec, n, tm):
    g, g_layer = spec['g']
    w, w_layer = spec['w']
    in_specs = [_resident(g, g_layer), _resident(w, w_layer)]
    args = [g, w]
    if spec['bias'] is not None:
        in_specs.append(_resident(*spec['bias']))
        args.append(spec['bias'][0])
    out_specs = [pl.BlockSpec((tm, spec['e_main']), lambda i: (i, 0))]
    out_shape = [jax.ShapeDtypeStruct((n, spec['e_main']), spec['out_dtype'])]
    if spec['w_extra'] is not None:
        in_specs.append(_resident(*spec['w_extra']))
        args.append(spec['w_extra'][0])
        ex = spec['w_extra'][0].shape[-1]
        out_specs.append(pl.BlockSpec((tm, ex), lambda i: (i, 0)))
        out_shape.append(jax.ShapeDtypeStruct((n, ex), F32))
    return in_specs, args, out_specs, out_shape


def _proj_kernel(*refs, e_main, has_bias, has_extra, tm):
    refs = list(refs)
    take = lambda cond=True: refs.pop(0) if cond else None
    x_ref, gn_ref, wn_ref = take(), take(), take()
    bn_ref, wne_ref = take(has_bias), take(has_extra)
    p_ref, pe_ref = take(), take(has_extra)
    for r in range(2):
        rows = slice(r * tm // 2, (r + 1) * tm // 2)
        _project(x_ref[rows, :], rows, gn_ref, wn_ref, bn_ref, wne_ref, p_ref, pe_ref, e_main)


def _norm_matmul(x, spec, tm=512):
    n, d = x.shape
    assert n % tm == 0 and spec['e_main'] % PROJ_CHUNK == 0
    in_specs, args, out_specs, out_shape = _proj_operands(spec, n, tm)
    return pl.pallas_call(
        functools.partial(_proj_kernel, e_main=spec['e_main'], has_bias=spec['bias'] is not None,
                          has_extra=spec['w_extra'] is not None, tm=tm),
        grid=(n // tm,),
        in_specs=[pl.BlockSpec((tm, d), lambda i: (i, 0))] + in_specs,
        out_specs=out_specs, out_shape=out_shape,
        compiler_params=_params(("parallel",)),
        name="norm_matmul",
    )(x, *args)


def _tail_kernel(*refs, tm, seq_len, carry_mode, apply_final, proj_main, proj_bias, proj_extra):
    refs = list(refs)
    take = lambda cond=True: refs.pop(0) if cond else None
    x_ref, y_ref, wo_ref, g_ref, wu_ref, cw_ref, cb_ref, wd_ref = [take() for _ in range(8)]
    st_ref = take(not carry_mode)
    fg_ref = take(apply_final)
    has_proj = proj_main > 0
    gn_ref, wn_ref = take(has_proj), take(has_proj)
    bn_ref, wne_ref = take(proj_bias), take(proj_extra)
    o_ref, go_ref = take(), take()
    p_ref, pe_ref = take(has_proj), take(proj_extra)
    h_ref = take()
    carry_ref = take(carry_mode)

    d_ff = wd_ref.shape[0]
    if carry_mode:
        @pl.when((pl.program_id(0) * tm) % seq_len == 0)
        def _():
            carry_ref[...] = jnp.zeros_like(carry_ref)
        row = lax.broadcasted_iota(jnp.int32, (tm, FFN_CHUNK), 0)
    else:
        n_seq = tm // seq_len
        tpos = lax.broadcasted_iota(jnp.int32, (n_seq, seq_len, FFN_CHUNK), 1)

    x2_halves, xn_halves = [], []
    for r in range(2):
        rows = slice(r * tm // 2, (r + 1) * tm // 2)
        x2_r = x_ref[rows, :] + jnp.dot(y_ref[rows, :].astype(BF16), wo_ref[...], preferred_element_type=F32)
        ms = jnp.mean(x2_r * x2_r, axis=-1, keepdims=True)
        x2_halves.append(x2_r)
        xn_halves.append((x2_r * lax.rsqrt(ms + EPS) * g_ref[...]).astype(BF16))
    x2 = jnp.concatenate(x2_halves, axis=0)
    xn = jnp.concatenate(xn_halves, axis=0)

    def rows_dot(w, split):
        if split:
            return jnp.concatenate([jnp.dot(h, w, preferred_element_type=F32) for h in xn_halves], axis=0)
        return jnp.dot(xn, w, preferred_element_type=F32)

    for c in range(d_ff // FFN_CHUNK):
        cols = slice(c * FFN_CHUNK, (c + 1) * FFN_CHUNK)
        up_cols = slice(d_ff + c * FFN_CHUNK, d_ff + (c + 1) * FFN_CHUNK)
        gate = rows_dot(wu_ref[:, cols], c == 0)
        up = rows_dot(wu_ref[:, up_cols], c == 0)
        if carry_mode:
            prev = carry_ref[:, cols]
            p1, p2 = prev[SUBLANES - 1:SUBLANES], prev[SUBLANES - 2:SUBLANES - 1]
            g1 = jnp.where(row == 0, p1, pltpu.roll(gate, 1, axis=0))
            g2 = jnp.where(row == 0, p2, jnp.where(row == 1, p1, pltpu.roll(gate, 2, axis=0)))
            tail = gate[tm - SUBLANES:tm]
            carry_ref[:, cols] = tail
            go_ref[0, :, cols] = tail
        else:
            gate3 = gate.reshape(n_seq, seq_len, FFN_CHUNK)
            st = st_ref[:, :, cols]
            g1 = jnp.where(tpos >= 1, pltpu.roll(gate3, 1, axis=1), pltpu.roll(st, 1, axis=1))
            g2 = jnp.where(tpos >= 2, pltpu.roll(gate3, 2, axis=1), pltpu.roll(st, 2, axis=1))
            g1 = g1.reshape(tm, FFN_CHUNK)
            g2 = g2.reshape(tm, FFN_CHUNK)
            go_ref[:, cols] = gate
        cw = cw_ref[:, cols]
        conv = cb_ref[:, cols] + cw[0:1] * g2 + cw[1:2] * g1 + cw[2:3] * gate
        h_ref[:, cols] = (conv * _sigmoid(conv) * up).astype(BF16)

    out = x2 + jnp.dot(h_ref[...], wd_ref[...], preferred_element_type=F32)
    if apply_final:
        ms = jnp.mean(out * out, axis=-1, keepdims=True)
        out = out * lax.rsqrt(ms + EPS) * fg_ref[...]
    o_ref[...] = out

    if has_proj:
        _project(out, slice(None), gn_ref, wn_ref, bn_ref, wne_ref, p_ref, pe_ref, proj_main)


def _layer_tail(x, y, w_out, ffn, layer, seq_len, conv_state=None, final_g=None, next_proj=None, tm=256):
    n, d = x.shape
    f = ffn['w_down'].shape[1]
    assert n % tm == 0 and f % FFN_CHUNK == 0
    carry_mode = conv_state is None
    apply_final = final_g is not None
    in_specs = [pl.BlockSpec((tm, d), lambda i: (i, 0)),
                pl.BlockSpec((tm, d), lambda i: (i, 0)),
                _resident(*w_out)]
    args = [x, y, w_out[0]]
    for name in ('g', 'w_up', 'conv_w', 'conv_b', 'w_down'):
        in_specs.append(_resident(ffn[name], layer))
        args.append(ffn[name])
    scratch = []
    if carry_mode:
        assert seq_len % tm == 0
        go_shape = jax.ShapeDtypeStruct((n // tm, SUBLANES, f), F32)
        go_spec = pl.BlockSpec((1, SUBLANES, f), lambda i: (i, 0, 0))
        scratch.append(pltpu.VMEM((SUBLANES, f), F32))
    else:
        assert seq_len == SUBLANES and tm % seq_len == 0
        in_specs.append(pl.BlockSpec((tm // seq_len, seq_len, f), lambda i: (i, 0, 0)))
        args.append(conv_state)
        go_shape = jax.ShapeDtypeStruct((n, f), F32)
        go_spec = pl.BlockSpec((tm, f), lambda i: (i, 0))
    if apply_final:
        in_specs.append(_resident(final_g))
        args.append(final_g)
    out_specs = [pl.BlockSpec((tm, d), lambda i: (i, 0)), go_spec]
    out_shape = [jax.ShapeDtypeStruct((n, d), F32), go_shape]
    proj_main, proj_bias, proj_extra = 0, False, False
    if next_proj is not None:
        proj_main = next_proj['e_main']
        proj_bias = next_proj['bias'] is not None
        proj_extra = next_proj['w_extra'] is not None
        assert proj_main % PROJ_CHUNK == 0
        p_in_specs, p_args, p_out_specs, p_out_shape = _proj_operands(next_proj, n, tm)
        in_specs += p_in_specs
        args += p_args
        out_specs += p_out_specs
        out_shape += p_out_shape
    scratch = [pltpu.VMEM((tm, f), BF16)] + scratch
    return pl.pallas_call(
        functools.partial(_tail_kernel, tm=tm, seq_len=seq_len, carry_mode=carry_mode, apply_final=apply_final,
                          proj_main=proj_main, proj_bias=proj_bias, proj_extra=proj_extra),
        grid=(n // tm,),
        in_specs=in_specs,
        out_specs=out_specs,
        out_shape=out_shape,
        scratch_shapes=scratch,
        compiler_params=_params(("arbitrary",)),
        name="layer_tail",
    )(*args)


def _mlstm_kernel(q_ref, k_ref, v_ref, o_ref, gi_ref, gf_ref, bi_ref, bf_ref, ng_ref,
                  c0_ref, n0_ref, m0_ref, y_ref, c1_ref, n1_ref, m1_ref, fs_ref, ms_ref,
                  *, units, gb, seg_len, t_valid):
    @pl.when(pl.program_id(1) == 0)
    def _():
        c1_ref[...] = c0_ref[...]
        n1_ref[...] = n0_ref[...]
        m1_ref[...] = m0_ref[...]

    for u in range(units):
        seqs = pl.ds(u * gb, gb)
        _mlstm_unit(q_ref.at[u], k_ref.at[u], v_ref.at[u], o_ref.at[u], gi_ref.at[u], gf_ref.at[u],
                    bi_ref, bf_ref, ng_ref, y_ref.at[u], c1_ref.at[seqs], n1_ref.at[u], m1_ref.at[u],
                    fs_ref.at[u], ms_ref.at[u], gb=gb, seg_len=seg_len, t_valid=t_valid)


def _mlstm_unit(q_ref, k_ref, v_ref, o_ref, gi_ref, gf_ref, bi_ref, bf_ref, ng_ref,
                y_ref, c1_ref, n1_ref, m1_ref, fs_ref, ms_ref, *, gb, seg_len, t_valid):
    shape = (ROWS, LANES)
    row = lax.broadcasted_iota(jnp.int32, shape, 0)
    lane = lax.broadcasted_iota(jnp.int32, shape, 1)
    tpos = row % seg_len
    low_half = lane < DKA

    ig = _softcap(gi_ref[...] + bi_ref[...])
    lf = _log_sigmoid(_softcap(gf_ref[...] + bf_ref[...]))
    if t_valid < seg_len:
        valid = tpos < t_valid
        ig = jnp.where(valid, ig, NEG)
        lf = jnp.where(valid, lf, 0.0)
    f_cum = _seg_scan(lf, seg_len, jnp.add, tpos)
    a = ig - f_cum
    cmax = _seg_scan(a, seg_len, jnp.maximum, tpos)
    m_prev_g = m1_ref[...]
    m_prev = _rep_rows(m_prev_g, seg_len)
    m_t = f_cum + jnp.maximum(m_prev, cmax)
    g = jnp.exp(jnp.minimum(f_cum + m_prev - m_t, 0.0))
    u = f_cum - m_t
    em = jnp.exp(-m_t)

    fs_ref[...] = f_cum
    ms_ref[...] = m_t
    f_last_g = fs_ref[pl.ds(seg_len - 1, gb, stride=seg_len), :]
    m_last_g = ms_ref[pl.ds(seg_len - 1, gb, stride=seg_len), :]
    f_last = _rep_rows(f_last_g, seg_len)
    m_last = _rep_rows(m_last_g, seg_len)
    w_last = jnp.exp(jnp.minimum(a + f_last - m_last, 0.0))
    g_last_g = jnp.exp(jnp.minimum(f_last_g + m_prev_g - m_last_g, 0.0))

    a_t = a.T
    u_w = _expand_heads(u, HA, DVA, 3)
    g_w = _expand_heads(g, HA, DVA, 2)
    em_w = _expand_heads(em, HA, DVA, 2)
    wl_w = _expand_heads(w_last, HA, DKA, 2)
    gl_rows = g_last_g if gb >= SUBLANES else jnp.broadcast_to(g_last_g, (SUBLANES, LANES))
    gl_w = _expand_heads(gl_rows, HA, DKA, 2)[:gb]

    col_i = lax.broadcasted_iota(jnp.int32, (ROWS, 2 * ROWS), 1) % ROWS
    row_i = lax.broadcasted_iota(jnp.int32, (ROWS, 2 * ROWS), 0)
    causal2 = (col_i <= row_i) & ((col_i // seg_len) == (row_i // seg_len))
    state_low = lax.broadcasted_iota(jnp.int32, (gb * LANES, LANES), 0) % LANES < DKA
    flat_low = lax.broadcasted_iota(jnp.int32, (LANES, gb * LANES), 1) % LANES < DKA
    sel = (lax.broadcasted_iota(jnp.int32, (max(gb, SUBLANES), ROWS), 1) // seg_len
           == lax.broadcasted_iota(jnp.int32, (max(gb, SUBLANES), ROWS), 0)).astype(BF16)
    ones_blk = (lax.broadcasted_iota(jnp.int32, (2 * ROWS, 2 * DVA), 0) // ROWS
                == lax.broadcasted_iota(jnp.int32, (2 * ROWS, 2 * DVA), 1) // DVA).astype(BF16)
    zeros_v = jnp.zeros((ROWS, DVA), v_ref.dtype)

    q_all = q_ref[...]
    k_all = k_ref[...] * (DKA ** -0.5)
    k2_all = k_all * wl_w

    for p in range(HA // 2):
        cols = slice(p * LANES, (p + 1) * LANES)
        wide = slice(2 * p * DVA, (2 * p + 2) * DVA)
        qp, kp, k2 = q_all[:, cols], k_all[:, cols], k2_all[:, cols]
        c_p = c1_ref[:, p].reshape(gb * LANES, DVA)
        n_p = n1_ref[p]
        v2 = v_ref[:, wide]
        v_a, v_b = v2[:, :DVA], v2[:, DVA:]

        k_sep = jnp.concatenate([jnp.where(low_half, kp, 0.0), jnp.where(low_half, 0.0, kp)], axis=0)
        s2 = _dot_nt(qp, k_sep)
        logw2 = jnp.concatenate([a_t[2 * p:2 * p + 1], a_t[2 * p + 1:2 * p + 2]], axis=1) + u_w[:, wide]
        qk2 = (s2 * jnp.where(causal2, jnp.exp(jnp.minimum(logw2, 0.0)), 0.0)).astype(BF16)
        v_bd = jnp.concatenate([jnp.concatenate([v_a, zeros_v], axis=1),
                                jnp.concatenate([zeros_v, v_b], axis=1)], axis=0)
        num2 = jnp.dot(qk2, v_bd, preferred_element_type=F32)
        dsum2 = jnp.dot(qk2, ones_blk, preferred_element_type=F32)

        qx = _expand_blockdiag(qp, gb, seg_len)
        c_sep = jnp.concatenate([jnp.where(state_low, c_p, 0.0), jnp.where(state_low, 0.0, c_p)], axis=1)
        inter2 = _dot(qx, c_sep)
        n_flat = n_p if gb == 1 else jnp.concatenate([n_p[b:b + 1] for b in range(gb)], axis=1)
        n_b = jnp.broadcast_to(n_flat, (LANES, gb * LANES))
        n_sep = jnp.concatenate([jnp.where(flat_low, n_b, 0.0), jnp.where(flat_low, 0.0, n_b)], axis=0)
        dint2 = _dot_nt(qx, n_sep)

        g2 = g_w[:, wide]
        hh2 = (num2 + g2 * inter2) / jnp.maximum(jnp.abs(dsum2 + g2 * dint2), em_w[:, wide])
        for half in range(2):
            hcols = slice((2 * p + half) * DVA, (2 * p + half + 1) * DVA)
            hh = hh2[:, half * DVA:(half + 1) * DVA]
            msq = jnp.mean(hh * hh, axis=-1, keepdims=True)
            hn = hh * lax.rsqrt(msq + EPS) * ng_ref[:, hcols]
            y_ref[:, hcols] = (hn * _sigmoid(o_ref[:, hcols].astype(F32))).astype(y_ref.dtype)

        upd2 = _dot_tn(_expand_blockdiag(k2, gb, seg_len), v2)
        upd = jnp.where(state_low, upd2[:, :DVA], upd2[:, DVA:])
        n_scale = gl_w[:, cols]
        scale = jnp.concatenate([jnp.broadcast_to(n_scale[b:b + 1], (LANES, LANES)).T for b in range(gb)],
                                axis=0)
        c1_ref[:, p] = (scale * c_p + upd).reshape(gb, LANES, DVA)
        n_upd = jnp.dot(sel, k2.astype(BF16), preferred_element_type=F32)[:gb]
        n1_ref[p] = n_scale * n_p + n_upd

    m1_ref[...] = m_last_g


def _mlstm(proj, gates, b_i, b_f, norm_g, c_all, layer, n0, m0, gb, seg_len, t_valid, n_chunks, units):
    n = proj.shape[0]
    n_blocks = n // (ROWS * n_chunks)
    assert n_blocks % units == 0
    bsz = n_blocks * gb
    half_pairs = HA // 2
    rows3 = lambda a: a.reshape(n_blocks, n_chunks * ROWS, a.shape[-1])
    proj3, gates3 = rows3(proj), rows3(gates)
    c0r = c_all.reshape(c_all.shape[0], bsz, half_pairs, LANES, DVA)
    n0r = n0.reshape(n_blocks, gb, half_pairs, LANES).transpose(0, 2, 1, 3)
    m0r = jnp.pad(m0, ((0, 0), (0, LANES - HA))).reshape(n_blocks, gb, LANES)
    pad = lambda b: jnp.pad(b, (0, LANES - HA)).reshape(1, LANES)
    blk = lambda width, cb: pl.BlockSpec((units, ROWS, width), lambda i, c: (i, c, cb))
    c_blk = (units * gb, half_pairs, LANES, DVA)
    y, c1, n1, m1 = pl.pallas_call(
        functools.partial(_mlstm_kernel, units=units, gb=gb, seg_len=seg_len, t_valid=t_valid),
        grid=(n_blocks // units, n_chunks),
        in_specs=[blk(HA * DKA, 0), blk(HA * DKA, 1), blk(HA * DVA, 1), blk(HA * DVA, 2),
                  blk(LANES, 0), blk(LANES, 1),
                  pl.BlockSpec((1, LANES), lambda i, c: (0, 0)),
                  pl.BlockSpec((1, LANES), lambda i, c: (0, 0)),
                  pl.BlockSpec((1, HA * DVA), lambda i, c: (0, 0)),
                  pl.BlockSpec((None,) + c_blk, lambda i, c: (layer, i, 0, 0, 0)),
                  pl.BlockSpec((units, half_pairs, gb, LANES), lambda i, c: (i, 0, 0, 0)),
                  pl.BlockSpec((units, gb, LANES), lambda i, c: (i, 0, 0))],
        out_specs=[blk(HA * DVA, 0),
                   pl.BlockSpec(c_blk, lambda i, c: (i, 0, 0, 0)),
                   pl.BlockSpec((units, half_pairs, gb, LANES), lambda i, c: (i, 0, 0, 0)),
                   pl.BlockSpec((units, gb, LANES), lambda i, c: (i, 0, 0))],
        out_shape=[jax.ShapeDtypeStruct((n_blocks, n_chunks * ROWS, HA * DVA), BF16),
                   jax.ShapeDtypeStruct(c0r.shape[1:], F32),
                   jax.ShapeDtypeStruct(n0r.shape, F32),
                   jax.ShapeDtypeStruct(m0r.shape, F32)],
        scratch_shapes=[pltpu.VMEM((units, ROWS, LANES), F32), pltpu.VMEM((units, ROWS, LANES), F32)],
        compiler_params=_params(("parallel", "arbitrary")),
        name="mlstm_mixer",
    )(proj3, proj3, proj3, proj3, gates3, gates3, pad(b_i), pad(b_f), norm_g.reshape(1, -1), c0r, n0r, m0r)
    y = y.reshape(n, HA * DVA)
    c1 = c1.reshape(bsz, HA, DKA, DVA)
    n1 = n1.transpose(0, 2, 1, 3).reshape(bsz, HA, DKA)
    m1 = m1.reshape(bsz, LANES)[:, :HA]
    return y, c1, n1, m1


def _gla_kernel(q_ref, k_ref, v_ref, r_ref, z_ref, wgu_ref, bg_ref, ng_ref, s0_ref,
                y_ref, s1_ref, bc_ref, b2_ref, kf_ref, *, units, gb, seg_len, t_valid, heads):
    @pl.when(pl.program_id(2) == 0)
    def _():
        s1_ref[...] = s0_ref[...]

    for u in range(units):
        _gla_unit(q_ref.at[u], k_ref.at[u], v_ref.at[u], r_ref.at[u], z_ref.at[u], wgu_ref, bg_ref, ng_ref,
                  y_ref.at[u], s1_ref.at[pl.ds(u * gb, gb)], bc_ref.at[u], b2_ref.at[u], kf_ref.at[u],
                  gb=gb, seg_len=seg_len, t_valid=t_valid, heads=heads)


def _gla_unit(q_ref, k_ref, v_ref, r_ref, z_ref, wgu_ref, bg_ref, ng_ref, y_ref, s1_ref, bc_ref, b2_ref, kf_ref,
              *, gb, seg_len, t_valid, heads):
    width = heads * DKB
    row = lax.broadcasted_iota(jnp.int32, (ROWS, width), 0)
    tpos = row % seg_len
    lg = _log_sigmoid(jnp.dot(z_ref[...].astype(BF16), wgu_ref[...], preferred_element_type=F32)
                      + bg_ref[...]) / GLA_TAU
    k_all = k_ref[...].astype(F32)
    if t_valid < seg_len:
        valid = tpos < t_valid
        lg = jnp.where(valid, lg, 0.0)
        k_all = jnp.where(valid, k_all, 0.0)
    bc = _seg_cumsum(lg, seg_len)

    n_sub = ROWS // GLA_SUB
    sub_row = lax.broadcasted_iota(jnp.int32, (GLA_SUB, ROWS), 0)
    sub_col = lax.broadcasted_iota(jnp.int32, (GLA_SUB, ROWS), 1)

    for h in range(heads):
        kcols = slice(h * DKB, (h + 1) * DKB)
        vcols = slice(h * DVB, (h + 1) * DVB)
        qh = q_ref[:, kcols].astype(F32) * (DKB ** -0.5)
        kh = k_all[:, kcols]
        vh = v_ref[:, vcols]
        bch = bc[:, kcols]
        bc_ref[...] = bch
        b2_ref[...] = bch * LOG2_E
        kf_ref[...] = kh
        b_last_g = bc_ref[pl.ds(seg_len - 1, gb, stride=seg_len), :]
        b_last = _rep_rows(b_last_g, seg_len)
        s_h = s1_ref[:, h].reshape(gb * DKB, DVB)

        inter = _dot(_expand_blockdiag(qh * jnp.exp(bch), gb, seg_len), s_h)

        att_rows = []
        k_decayed = []
        for blk in range(n_sub):
            r0 = blk * GLA_SUB
            q_b = qh[r0:r0 + GLA_SUB]
            bc_b = bch[r0:r0 + GLA_SUB]
            b2_b = b2_ref[r0:r0 + GLA_SUB, :]
            att = jnp.zeros((GLA_SUB, ROWS), F32)
            for s in range(GLA_SUB):
                k_row = jnp.broadcast_to(kf_ref[r0 + s:r0 + s + 1, :], (GLA_SUB, DKB))
                b2_row = jnp.broadcast_to(b2_ref[r0 + s:r0 + s + 1, :], (GLA_SUB, DKB))
                cs = jnp.sum(q_b * k_row * jnp.exp2(b2_b - b2_row), axis=-1, keepdims=True)
                att = jnp.where(sub_col == r0 + s, cs, att)
            att = jnp.where(sub_col <= sub_row + r0, att, 0.0)
            if seg_len > GLA_SUB and blk > 0:
                b_edge = bch[r0 - 1:r0]
                if blk > 1:
                    step = jnp.exp(b_edge - bch[r0 - GLA_SUB - 1:r0 - GLA_SUB])
                    k_decayed = [kb * step for kb in k_decayed]
                k_decayed.append(kh[r0 - GLA_SUB:r0] * jnp.exp(b_edge - bch[r0 - GLA_SUB:r0]))
                k_t = jnp.concatenate(k_decayed + [jnp.zeros((ROWS - r0, DKB), F32)], axis=0)
                att = att + _dot_nt(q_b * jnp.exp(bc_b - b_edge), k_t)
            att_rows.append(att)
        att_full = jnp.concatenate(att_rows, axis=0)

        o = inter + _dot(att_full, vh)
        msq = jnp.mean(o * o, axis=-1, keepdims=True)
        on = o * lax.rsqrt(msq + EPS) * ng_ref[:, vcols]
        r = r_ref[:, vcols].astype(F32)
        y_ref[:, vcols] = (on * (r * _sigmoid(r))).astype(y_ref.dtype)

        k_d = kh * jnp.exp(b_last - bch)
        upd = _dot_tn(_expand_blockdiag(k_d, gb, seg_len), vh)
        e_last = jnp.exp(b_last_g)
        e_cols = jnp.concatenate(
            [jnp.broadcast_to(e_last[b:b + 1], (DKB, DKB)).T for b in range(gb)], axis=0)
        e_cols = jnp.concatenate([e_cols] * (DVB // DKB), axis=1)
        s1_ref[:, h] = (e_cols * s_h + upd).reshape(gb, DKB, DVB)


def _gla(proj, z, w_gate_up, b_gate, norm_g, s0, gb, seg_len, t_valid, n_chunks, heads, units):
    n = proj.shape[0]
    n_blocks = n // (ROWS * n_chunks)
    assert n_blocks % units == 0 and seg_len % GLA_SUB == 0
    n_hb = HB // heads
    wq, wv = heads * DKB, heads * DVB
    q_off, k_off = 0, (HB * DKB) // wq
    v_off, r_off = (2 * HB * DKB) // wv, (2 * HB * DKB + HB * DVB) // wv
    rows3 = lambda a: a.reshape(n_blocks, n_chunks * ROWS, a.shape[-1])
    proj3, z3 = rows3(proj), rows3(z)
    blk = lambda width, off: pl.BlockSpec((units, ROWS, width), lambda i, hb, c: (i, c, off + hb))
    state_spec = pl.BlockSpec((units * gb, heads, DKB, DVB), lambda i, hb, c: (i, hb, 0, 0))
    wgu = jnp.zeros((LANES, HB * DKB), BF16).at[:GLA_RANK].set(w_gate_up.astype(BF16))
    y, s1 = pl.pallas_call(
        functools.partial(_gla_kernel, units=units, gb=gb, seg_len=seg_len, t_valid=t_valid, heads=heads),
        grid=(n_blocks // units, n_hb, n_chunks),
        in_specs=[blk(wq, q_off), blk(wq, k_off), blk(wv, v_off), blk(wv, r_off),
                  pl.BlockSpec((units, ROWS, LANES), lambda i, hb, c: (i, c, 0)),
                  pl.BlockSpec((LANES, wq), lambda i, hb, c: (0, hb)),
                  pl.BlockSpec((1, wq), lambda i, hb, c: (0, hb)),
                  pl.BlockSpec((1, wv), lambda i, hb, c: (0, hb)),
                  state_spec],
        out_specs=[blk(wv, 0), state_spec],
        out_shape=[jax.ShapeDtypeStruct((n_blocks, n_chunks * ROWS, HB * DVB), BF16),
                   jax.ShapeDtypeStruct(s0.shape, F32)],
        scratch_shapes=[pltpu.VMEM((units, ROWS, DKB), F32)] * 3,
        compiler_params=_params(("parallel", "parallel", "arbitrary")),
        name="gla_mixer",
    )(proj3, proj3, proj3, proj3, z3, wgu, b_gate.reshape(1, -1), norm_g.reshape(1, -1), s0)
    return y.reshape(n, HB * DVB), s1


def _swa_kernel(q_ref, kc_ref, vc_ref, kp_ref, vp_ref, sink_ref, o_ref, *, blocks, units, q_len, prev_from_grid):
    for b in range(blocks):
        _swa_block(q_ref.at[b], kc_ref.at[b], vc_ref.at[b], kp_ref.at[b], vp_ref.at[b], sink_ref, o_ref.at[b],
                   units=units, q_len=q_len, prev_from_grid=prev_from_grid)


def _swa_block(q_ref, kc_ref, vc_ref, kp_ref, vp_ref, sink_ref, o_ref, *, units, q_len, prev_from_grid):
    group = HC // HKV
    rows4 = group * ROWS
    row = lax.broadcasted_iota(jnp.int32, (rows4, ROWS), 0) % ROWS
    key = lax.broadcasted_iota(jnp.int32, (rows4, ROWS), 1)
    q_pos = row % q_len
    mask_p = key >= q_pos
    if prev_from_grid:
        mask_p = mask_p & (pl.program_id(1) > 0)
    mask_c = ((key // q_len) == (row // q_len)) & ((key % q_len) <= q_pos)
    bias_p = jnp.where(mask_p, 0.0, NEG)
    bias_c = jnp.where(mask_c, 0.0, NEG)
    low = lax.broadcasted_iota(jnp.int32, (ROWS, LANES), 1) < HDC
    ones = jnp.ones((WINDOW, LANES), BF16)
    sinks = sink_ref[...]
    scale = HDC ** -0.5

    def per_unit(lhs, rhs, contract_rhs_rows):
        mm = _dot if contract_rhs_rows else _dot_nt
        if units == 1:
            return mm(lhs, rhs)
        pieces = [[None] * units for _ in range(group)]
        for u in range(units):
            rows_u = jnp.concatenate([lhs[gi * ROWS + u * q_len:gi * ROWS + (u + 1) * q_len]
                                      for gi in range(group)], axis=0)
            res = mm(rows_u, rhs[u * WINDOW:(u + 1) * WINDOW])
            for gi in range(group):
                pieces[gi][u] = res[gi * q_len:(gi + 1) * q_len]
        return jnp.concatenate([pieces[gi][u] for gi in range(group) for u in range(units)], axis=0)

    out_tiles = [None] * (HC // 2)
    for kk in range(HKV):
        kv_cols = slice((kk // 2) * LANES, (kk // 2 + 1) * LANES)
        k_half = kk % 2
        k_sel = low if k_half == 0 else ~low
        q_parts, sink_parts = [], []
        for gi in range(group):
            hq = kk * group + gi
            tile = q_ref[:, (hq // 2) * LANES:(hq // 2 + 1) * LANES] * scale
            if hq % 2 != k_half:
                tile = pltpu.roll(tile, HDC, axis=1)
            q_parts.append(jnp.where(k_sel, tile, 0.0))
            sink_parts.append(jnp.broadcast_to(_col(sinks, hq), (ROWS, LANES)))
        q4 = jnp.concatenate(q_parts, axis=0)
        sink = jnp.concatenate(sink_parts, axis=0)
        s_p = per_unit(q4, kp_ref[:, kv_cols], False) + bias_p
        s_c = _dot_nt(q4, kc_ref[:, kv_cols]) + bias_c
        row_max = jnp.max(jnp.maximum(s_p, s_c), axis=-1, keepdims=True)
        mx = jnp.maximum(jnp.broadcast_to(row_max, (rows4, LANES)), sink)
        p_p = jnp.exp(s_p - mx).astype(BF16)
        p_c = jnp.exp(s_c - mx).astype(BF16)
        den = (jnp.dot(p_p, ones, preferred_element_type=F32) + jnp.dot(p_c, ones, preferred_element_type=F32)
               + jnp.exp(sink - mx))
        out4 = (per_unit(p_p, vp_ref[:, kv_cols], True) + _dot(p_c, vc_ref[:, kv_cols])) / den
        for gi in range(group):
            hq = kk * group + gi
            part = out4[gi * ROWS:(gi + 1) * ROWS]
            if hq % 2 != k_half:
                part = pltpu.roll(part, HDC, axis=1)
            prev = out_tiles[hq // 2]
            out_tiles[hq // 2] = part if prev is None else jnp.where(low if hq % 2 == 0 else ~low, part, prev)
    for t in range(HC // 2):
        o_ref[:, t * LANES:(t + 1) * LANES] = out_tiles[t].astype(o_ref.dtype)


def _swa(proj, sinks, units, q_len, n_chunks, blocks, prev_k=None, prev_v=None):
    n = proj.shape[0]
    n_chains = n // (ROWS * n_chunks)
    assert n_chains % blocks == 0
    kv_w = HKV * HDC
    k_blk, v_blk = (HC * HDC) // kv_w, (HC * HDC) // kv_w + 1
    prev_from_grid = prev_k is None
    proj3 = proj.reshape(n_chains, n_chunks * ROWS, proj.shape[-1])
    cur = lambda blk: pl.BlockSpec((blocks, ROWS, kv_w), lambda i, c: (i, c, blk))
    if prev_from_grid:
        prev = lambda blk: pl.BlockSpec((blocks, ROWS, kv_w), lambda i, c: (i, jnp.maximum(c - 1, 0), blk))
        prev_specs = [prev(k_blk), prev(v_blk)]
        prev_args = [proj3, proj3]
    else:
        prev_specs = [pl.BlockSpec((blocks, units * WINDOW, kv_w), lambda i, c: (i, 0, 0))] * 2
        prev_args = [prev_k.reshape(n_chains, units * WINDOW, kv_w), prev_v.reshape(n_chains, units * WINDOW, kv_w)]
    sink_row = jnp.pad(sinks.astype(F32), (0, LANES - HC)).reshape(1, LANES)
    y = pl.pallas_call(
        functools.partial(_swa_kernel, blocks=blocks, units=units, q_len=q_len, prev_from_grid=prev_from_grid),
        grid=(n_chains // blocks, n_chunks),
        in_specs=[pl.BlockSpec((blocks, ROWS, HC * HDC), lambda i, c: (i, c, 0)),
                  cur(k_blk), cur(v_blk)] + prev_specs +
                 [pl.BlockSpec((1, LANES), lambda i, c: (0, 0))],
        out_specs=pl.BlockSpec((blocks, ROWS, HC * HDC), lambda i, c: (i, c, 0)),
        out_shape=jax.ShapeDtypeStruct((n_chains, n_chunks * ROWS, HC * HDC), BF16),
        compiler_params=_params(("parallel", "arbitrary")),
        name="swa_mixer",
    )(proj3, proj3, proj3, *prev_args, sink_row)
    return y.reshape(n, HC * HDC)


def _proj_spec(w, i):
    kind, j = i % N_MIXERS, i // N_MIXERS
    spec = {'g': (w['norm_mix'], i), 'bias': None, 'w_extra': None, 'out_dtype': BF16}
    if kind == 0:
        spec.update(w=(w['a_w_in'], j), e_main=2 * HA * (DKA + DVA), w_extra=(w['a_w_gates'], j))
    elif kind == 1:
        spec.update(w=(w['b_w_in'], j), e_main=2 * HB * (DKB + DVB), w_extra=(w['b_w_z'], j))
    else:
        spec.update(w=(w['c_w_in'], j), e_main=(HC + 2 * HKV) * HDC, bias=(w['c_b_in'], j), out_dtype=F32)
    return spec


def _trunk(x3, st, w, is_prompt):
    bsz, t_in, _ = x3.shape
    if is_prompt:
        t, t_valid, gb, seg_len = t_in, ROWS, 1, ROWS
        units = math.gcd(MIXER_UNITS, bsz)
        x = x3.reshape(bsz * t, D_MODEL)
    else:
        t, t_valid, gb, seg_len = SAMPLE_T_PAD, t_in, ROWS // SAMPLE_T_PAD, SAMPLE_T_PAD
        units = 1
        x = jnp.pad(x3, ((0, 0), (0, t - t_in), (0, 0))).reshape(bsz * t, D_MODEL)
    n_chunks = (gb * t) // ROWS
    new = {'a_c': [], 'a_n': [], 'a_m': [], 'b_s': [], 'c_k': [], 'c_v': [], 'f': []}

    projected = _norm_matmul(x, _proj_spec(w, 0))
    for i in range(DEPTH):
        kind, j = i % N_MIXERS, i // N_MIXERS
        if kind == 0:
            proj, gates = projected
            y, c1, n1, m1 = _mlstm(proj, gates, w['a_b_i'][j], w['a_b_f'][j], w['a_norm'][j],
                                   st['a_c'], j, st['a_n'][j], st['a_m'][j], gb, seg_len, t_valid, n_chunks,
                                   units=units)
            new['a_c'].append(c1)
            new['a_n'].append(n1)
            new['a_m'].append(m1)
            w_out = (w['a_w_out'], j)
        elif kind == 1:
            proj, z = projected
            y, s1 = _gla(proj, z, w['b_w_gate_up'][j], w['b_b_gate'][j], w['b_norm'][j], st['b_s'][j],
                         gb, seg_len, t_valid, n_chunks, heads=HB if is_prompt else 1,
                         units=units)
            new['b_s'].append(s1)
            w_out = (w['b_w_out'], j)
        else:
            proj = projected[0]
            kv_w = HKV * HDC
            end = t if is_prompt else t_valid
            keep = min(WINDOW, end)
            newest = proj.reshape(bsz, t, -1)[:, end - keep:end, HC * HDC:]
            k_new = newest[:, :, :kv_w].reshape(bsz, keep, HKV, HDC)
            v_new = newest[:, :, kv_w:].reshape(bsz, keep, HKV, HDC)
            if is_prompt:
                y = _swa(proj, w['c_sinks'][j], 1, ROWS, n_chunks, units)
                new['c_k'].append(k_new)
                new['c_v'].append(v_new)
            else:
                k_buf, v_buf = st['c_k'][j], st['c_v'][j]
                y = _swa(proj, w['c_sinks'][j], gb, seg_len, n_chunks, 1,
                         prev_k=k_buf.reshape(bsz * WINDOW, kv_w), prev_v=v_buf.reshape(bsz * WINDOW, kv_w))
                new['c_k'].append(jnp.concatenate([k_buf[:, t_valid:], k_new], axis=1))
                new['c_v'].append(jnp.concatenate([v_buf[:, t_valid:], v_new], axis=1))
            w_out = (w['c_w_out'], j)

        last = i == DEPTH - 1
        tail_args = (x, y, w_out, w['ffn'], i)
        tail_kw = dict(seq_len=t, final_g=w['norm_final'] if last else None,
                       next_proj=None if last else _proj_spec(w, i + 1),
                       tm=TAIL_TM_PROMPT if is_prompt else TAIL_TM_SAMPLE)
        if is_prompt:
            x, gate_tail, *projected = _layer_tail(*tail_args, **tail_kw)
            seq_tails = gate_tail.reshape(bsz, t // TAIL_TM_PROMPT, SUBLANES, D_FF)[:, -1]
            new['f'].append(seq_tails[:, SUBLANES - (CONV_W - 1):])
        else:
            state = st['f'][i]
            state_pad = jnp.pad(state, ((0, 0), (t - (CONV_W - 1), 0), (0, 0)))
            x, gate_all, *projected = _layer_tail(*tail_args, conv_state=state_pad, **tail_kw)
            g_ext = jnp.concatenate([state, gate_all.reshape(bsz, t, D_FF)[:, :t_valid]], axis=1)
            new['f'].append(g_ext[:, -(CONV_W - 1):])

    out = {name: jnp.stack(vals) for name, vals in new.items()}
    y = x.reshape(bsz, t, D_MODEL)[:, :t_in]
    return y, out


def _prepare_weights(norm_mix_g, norm_ffn_g, norm_final_g, a_w_in, a_b_i, a_b_f, a_norm_g, a_w_out, b_w_in,
                     b_w_gate_up, b_b_gate, b_norm_g, b_w_out, c_w_in, c_b_in, c_sinks, c_w_out, f_w_up,
                     f_conv_w, f_conv_b, f_w_down):
    n_a, n_b = a_w_in.shape[0], b_w_in.shape[0]
    e_a = 2 * HA * (DKA + DVA)
    e_b = 2 * HB * (DKB + DVB)
    a_w_gates = jnp.zeros((n_a, D_MODEL, 2 * LANES), BF16)
    a_w_gates = a_w_gates.at[:, :, :HA].set(a_w_in[:, :, e_a:e_a + HA].astype(BF16))
    a_w_gates = a_w_gates.at[:, :, LANES:LANES + HA].set(a_w_in[:, :, e_a + HA:].astype(BF16))
    b_w_z = jnp.zeros((n_b, D_MODEL, LANES), BF16).at[:, :, :GLA_RANK].set(b_w_in[:, :, e_b:].astype(BF16))
    ffn = {'g': norm_ffn_g[:, None, :], 'w_up': f_w_up.astype(BF16), 'conv_w': f_conv_w,
           'conv_b': f_conv_b[:, None, :], 'w_down': f_w_down.astype(BF16)}
    w = {'norm_mix': norm_mix_g[:, None, :], 'norm_final': norm_final_g[None, :], 'ffn': ffn,
         'a_w_in': a_w_in.astype(BF16), 'a_w_gates': a_w_gates, 'a_b_i': a_b_i, 'a_b_f': a_b_f,
         'a_norm': a_norm_g, 'a_w_out': a_w_out.astype(BF16),
         'b_w_in': b_w_in.astype(BF16), 'b_w_z': b_w_z, 'b_w_gate_up': b_w_gate_up, 'b_b_gate': b_b_gate,
         'b_norm': b_norm_g, 'b_w_out': b_w_out.astype(BF16),
         'c_w_in': c_w_in.astype(BF16), 'c_b_in': c_b_in[:, None, :], 'c_sinks': c_sinks,
         'c_w_out': c_w_out.astype(BF16)}
    return w


def kernel(x_prompt, x_sample, state_mlstm_c, state_mlstm_n, state_mlstm_m, state_gla, cache_swa_k, cache_swa_v, state_ffn_conv, norm_mix_g, norm_ffn_g, norm_final_g, a_w_in, a_b_i, a_b_f, a_norm_g, a_w_out, b_w_in, b_w_gate_up, b_b_gate, b_norm_g, b_w_out, c_w_in, c_b_in, c_sinks, c_w_out, f_w_up, f_conv_w, f_conv_b, f_w_down):
    w = _prepare_weights(norm_mix_g, norm_ffn_g, norm_final_g, a_w_in, a_b_i, a_b_f, a_norm_g, a_w_out, b_w_in,
                         b_w_gate_up, b_b_gate, b_norm_g, b_w_out, c_w_in, c_b_in, c_sinks, c_w_out, f_w_up,
                         f_conv_w, f_conv_b, f_w_down)
    n_a, n_b, n_c = a_w_in.shape[0], b_w_in.shape[0], c_w_in.shape[0]
    bp = x_prompt.shape[0]
    st_p = {'a_c': jnp.zeros((n_a, bp, HA, DKA, DVA), F32),
            'a_n': jnp.zeros((n_a, bp, HA, DKA), F32),
            'a_m': jnp.zeros((n_a, bp, HA), F32),
            'b_s': jnp.zeros((n_b, bp, HB, DKB, DVB), F32),
            'c_k': [None] * n_c, 'c_v': [None] * n_c, 'f': None}
    st_s = {'a_c': state_mlstm_c, 'a_n': state_mlstm_n, 'a_m': state_mlstm_m, 'b_s': state_gla,
            'c_k': cache_swa_k, 'c_v': cache_swa_v, 'f': state_ffn_conv}
    y_prompt, np_ = _trunk(x_prompt, st_p, w, True)
    y_sample, ns_ = _trunk(x_sample, st_s, w, False)
    return (y_prompt, y_sample,
            np_['a_c'], np_['a_n'], np_['a_m'], np_['b_s'], np_['c_k'], np_['c_v'], np_['f'],
            ns_['a_c'], ns_['a_n'], ns_['a_m'], ns_['b_s'], ns_['c_k'], ns_['c_v'], ns_['f'])
```

```python
import functools
import math

import jax
import jax.numpy as jnp
from jax import lax
from jax.experimental import pallas as pl
from jax.experimental.pallas import tpu as pltpu

F32 = jnp.float32
BF16 = jnp.bfloat16

D_MODEL = 1024
DEPTH = 4
N_MIXERS = 3
HA, DKA, DVA = 8, 64, 128
GATE_SOFTCAP = 15.0
HB, DKB, DVB = 4, 128, 256
GLA_RANK = 16
GLA_TAU = 16.0
HC, HKV, HDC = 16, 4, 64
WINDOW = 128
D_FF = 2816
CONV_W = 3
EPS = 1e-6
NEG = -1e30
LOG2_E = 1.4426950408889634

ROWS = 128
LANES = 128
SUBLANES = 8
SAMPLE_T_PAD = 8
GLA_SUB = 8
FFN_CHUNK = 256
PROJ_CHUNK = 512
MIXER_UNITS = 4
TAIL_TM_PROMPT = 512
TAIL_TM_SAMPLE = 256
VMEM_LIMIT = 56 * 1024 * 1024


def _params(sem):
    return pltpu.CompilerParams(dimension_semantics=sem, vmem_limit_bytes=VMEM_LIMIT)


def _dot(a, b):
    return jnp.dot(a.astype(BF16), b.astype(BF16), preferred_element_type=F32)


def _dot_nt(a, b):
    return lax.dot_general(a.astype(BF16), b.astype(BF16), (((1,), (1,)), ((), ())),
                           preferred_element_type=F32)


def _dot_tn(a, b):
    return lax.dot_general(a.astype(BF16), b.astype(BF16), (((0,), (0,)), ((), ())),
                           preferred_element_type=F32)


def _sigmoid(x):
    return 1.0 / (1.0 + jnp.exp(-x))


def _log_sigmoid(x):
    return jnp.minimum(x, 0.0) - jnp.log(1.0 + jnp.exp(-jnp.abs(x)))


def _softcap(z):
    return GATE_SOFTCAP * jnp.tanh(z / GATE_SOFTCAP)


def _col(x, h):
    lane = lax.broadcasted_iota(jnp.int32, x.shape, 1)
    return jnp.sum(jnp.where(lane == h, x, 0.0), axis=-1, keepdims=True)


def _expand_heads(x, n_heads, width, terms):
    lanes = x.shape[1]
    src = lax.broadcasted_iota(jnp.int32, (terms * lanes, n_heads * width), 0) % lanes
    dst = lax.broadcasted_iota(jnp.int32, (terms * lanes, n_heads * width), 1) // width
    pieces, rest = [], x
    for _ in range(terms):
        piece = rest.astype(BF16)
        pieces.append(piece)
        rest = rest - piece.astype(F32)
    return jnp.dot(jnp.concatenate(pieces, axis=1), (src == dst).astype(BF16), preferred_element_type=F32)


def _rep_rows(x, reps):
    g, c = x.shape
    if g == 1:
        return jnp.broadcast_to(x, (reps, c))
    return jnp.concatenate([jnp.broadcast_to(x[b:b + 1], (reps, c)) for b in range(g)], axis=0)


def _seg_scan(x, seg_len, op, tpos):
    s = 1
    while s < seg_len:
        shifted = pltpu.roll(x, s, axis=0)
        x = jnp.where(tpos >= s, op(x, shifted), x)
        s *= 2
    return x


def _seg_cumsum(x, seg_len):
    rows = x.shape[0]
    dst = lax.broadcasted_iota(jnp.int32, (rows, 3 * rows), 0)
    src = lax.broadcasted_iota(jnp.int32, (rows, 3 * rows), 1) % rows
    tri = ((src <= dst) & ((src // seg_len) == (dst // seg_len))).astype(BF16)
    pieces, rest = [], x
    for _ in range(3):
        piece = rest.astype(BF16)
        pieces.append(piece)
        rest = rest - piece.astype(F32)
    return jnp.dot(tri, jnp.concatenate(pieces, axis=0), preferred_element_type=F32)


def _expand_blockdiag(x, gb, seg_len):
    if gb == 1:
        return x
    seq = lax.broadcasted_iota(jnp.int32, x.shape, 0) // seg_len
    return jnp.concatenate([jnp.where(seq == b, x, 0.0) for b in range(gb)], axis=1)


def _resident(arr, layer=None):
    shape = arr.shape if layer is None else arr.shape[1:]
    block = shape if layer is None else (None,) + shape
    lead = () if layer is None else (layer,)
    return pl.BlockSpec(block, lambda *_: lead + (0,) * len(shape), pipeline_mode=pl.Buffered(1))


def _project(x, rows, gn_ref, wn_ref, bn_ref, wne_ref, p_ref, pe_ref, e_main):
    ms = jnp.mean(x * x, axis=-1, keepdims=True)
    xn = (x * lax.rsqrt(ms + EPS) * gn_ref[...]).astype(BF16)
    for c in range(e_main // PROJ_CHUNK):
        cols = slice(c * PROJ_CHUNK, (c + 1) * PROJ_CHUNK)
        acc = jnp.dot(xn, wn_ref[:, cols], preferred_element_type=F32)
        if bn_ref is not None:
            acc = acc + bn_ref[:, cols]
        p_ref[rows, cols] = acc.astype(p_ref.dtype)
    if wne_ref is not None:
        pe_ref[rows, :] = jnp.dot(xn, wne_ref[...], preferred_element_type=F32)


def _proj_operands(spec, n, tm):
    g, g_layer = spec['g']
    w, w_layer = spec['w']
    in_specs = [_resident(g, g_layer), _resident(w, w_layer)]
    args = [g, w]
    if spec['bias'] is not None:
        in_specs.append(_resident(*spec['bias']))
        args.append(spec['bias'][0])
    out_specs = [pl.BlockSpec((tm, spec['e_main']), lambda i: (i, 0))]
    out_shape = [jax.ShapeDtypeStruct((n, spec['e_main']), spec['out_dtype'])]
    if spec['w_extra'] is not None:
        in_specs.append(_resident(*spec['w_extra']))
        args.append(spec['w_extra'][0])
        ex = spec['w_extra'][0].shape[-1]
        out_specs.append(pl.BlockSpec((tm, ex), lambda i: (i, 0)))
        out_shape.append(jax.ShapeDtypeStruct((n, ex), F32))
    return in_specs, args, out_specs, out_shape


def _proj_kernel(*refs, e_main, has_bias, has_extra, tm):
    refs = list(refs)
    take = lambda cond=True: refs.pop(0) if cond else None
    x_ref, gn_ref, wn_ref = take(), take(), take()
    bn_ref, wne_ref = take(has_bias), take(has_extra)
    p_ref, pe_ref = take(), take(has_extra)
    for r in range(2):
        rows = slice(r * tm // 2, (r + 1) * tm // 2)
        _project(x_ref[rows, :], rows, gn_ref, wn_ref, bn_ref, wne_ref, p_ref, pe_ref, e_main)


def _norm_matmul(x, spec, tm=512):
    n, d = x.shape
    assert n % tm == 0 and spec['e_main'] % PROJ_CHUNK == 0
    in_specs, args, out_specs, out_shape = _proj_operands(spec, n, tm)
    return pl.pallas_call(
        functools.partial(_proj_kernel, e_main=spec['e_main'], has_bias=spec['bias'] is not None,
                          has_extra=spec['w_extra'] is not None, tm=tm),
        grid=(n // tm,),
        in_specs=[pl.BlockSpec((tm, d), lambda i: (i, 0))] + in_specs,
        out_specs=out_specs, out_shape=out_shape,
        compiler_params=_params(("parallel",)),
        name="norm_matmul",
    )(x, *args)


def _tail_kernel(*refs, tm, seq_len, t_valid, carry_mode, apply_final, proj_main, proj_bias, proj_extra):
    refs = list(refs)
    take = lambda cond=True: refs.pop(0) if cond else None
    x_ref, y_ref, wo_ref, g_ref, wu_ref, cw_ref, cb_ref, wd_ref = [take() for _ in range(8)]
    st_ref = take(not carry_mode)
    fg_ref = take(apply_final)
    has_proj = proj_main > 0
    gn_ref, wn_ref = take(has_proj), take(has_proj)
    bn_ref, wne_ref = take(proj_bias), take(proj_extra)
    o_ref, go_ref = take(), take()
    p_ref, pe_ref = take(has_proj), take(proj_extra)
    h_ref = take()
    carry_ref = take(carry_mode)

    d_ff = wd_ref.shape[0]
    if carry_mode:
        @pl.when((pl.program_id(0) * tm) % seq_len == 0)
        def _():
            carry_ref[...] = jnp.zeros_like(carry_ref)
        row = lax.broadcasted_iota(jnp.int32, (tm, FFN_CHUNK), 0)
    else:
        n_seq = tm // seq_len
        tpos = lax.broadcasted_iota(jnp.int32, (n_seq, seq_len, FFN_CHUNK), 1)

    x2_halves, xn_halves = [], []
    for r in range(2):
        rows = slice(r * tm // 2, (r + 1) * tm // 2)
        x2_r = x_ref[rows, :] + jnp.dot(y_ref[rows, :].astype(BF16), wo_ref[...], preferred_element_type=F32)
        ms = jnp.mean(x2_r * x2_r, axis=-1, keepdims=True)
        x2_halves.append(x2_r)
        xn_halves.append((x2_r * lax.rsqrt(ms + EPS) * g_ref[...]).astype(BF16))
    x2 = jnp.concatenate(x2_halves, axis=0)
    xn = jnp.concatenate(xn_halves, axis=0)

    def rows_dot(w, split):
        if split:
            return jnp.concatenate([jnp.dot(h, w, preferred_element_type=F32) for h in xn_halves], axis=0)
        return jnp.dot(xn, w, preferred_element_type=F32)

    for c in range(d_ff // FFN_CHUNK):
        cols = slice(c * FFN_CHUNK, (c + 1) * FFN_CHUNK)
        up_cols = slice(d_ff + c * FFN_CHUNK, d_ff + (c + 1) * FFN_CHUNK)
        gate = rows_dot(wu_ref[:, cols], c == 0)
        up = rows_dot(wu_ref[:, up_cols], c == 0)
        if carry_mode:
            prev = carry_ref[:, cols]
            p1, p2 = prev[SUBLANES - 1:SUBLANES], prev[SUBLANES - 2:SUBLANES - 1]
            g1 = jnp.where(row == 0, p1, pltpu.roll(gate, 1, axis=0))
            g2 = jnp.where(row == 0, p2, jnp.where(row == 1, p1, pltpu.roll(gate, 2, axis=0)))
            tail = gate[tm - SUBLANES:tm]
            carry_ref[:, cols] = tail
            go_ref[0, :, cols] = tail
        else:
            gate3 = gate.reshape(n_seq, seq_len, FFN_CHUNK)
            st_2, st_1 = st_ref[:, 0:1, cols], st_ref[:, 1:2, cols]
            g1 = jnp.where(tpos >= 1, pltpu.roll(gate3, 1, axis=1), st_1)
            g2 = jnp.where(tpos >= 2, pltpu.roll(gate3, 2, axis=1), jnp.where(tpos == 0, st_2, st_1))
            g1 = g1.reshape(tm, FFN_CHUNK)
            g2 = g2.reshape(tm, FFN_CHUNK)
            go_ref[:, :, cols] = gate3[:, t_valid - (CONV_W - 1):t_valid, :]
        cw = cw_ref[:, cols]
        conv = cb_ref[:, cols] + cw[0:1] * g2 + cw[1:2] * g1 + cw[2:3] * gate
        h_ref[:, cols] = (conv * _sigmoid(conv) * up).astype(BF16)

    out = x2 + jnp.dot(h_ref[...], wd_ref[...], preferred_element_type=F32)
    if apply_final:
        ms = jnp.mean(out * out, axis=-1, keepdims=True)
        out = out * lax.rsqrt(ms + EPS) * fg_ref[...]
    o_ref[...] = out

    if has_proj:
        _project(out, slice(None), gn_ref, wn_ref, bn_ref, wne_ref, p_ref, pe_ref, proj_main)


def _layer_tail(x, y, w_out, ffn, layer, seq_len, conv_state=None, t_valid=None, final_g=None, next_proj=None,
                tm=256):
    n, d = x.shape
    f = ffn['w_down'].shape[1]
    assert n % tm == 0 and f % FFN_CHUNK == 0
    carry_mode = conv_state is None
    apply_final = final_g is not None
    in_specs = [pl.BlockSpec((tm, d), lambda i: (i, 0)),
                pl.BlockSpec((tm, d), lambda i: (i, 0)),
                _resident(*w_out)]
    args = [x, y, w_out[0]]
    for name in ('g', 'w_up', 'conv_w', 'conv_b', 'w_down'):
        in_specs.append(_resident(ffn[name], layer))
        args.append(ffn[name])
    scratch = []
    if carry_mode:
        assert seq_len % tm == 0
        go_shape = jax.ShapeDtypeStruct((n // tm, SUBLANES, f), F32)
        go_spec = pl.BlockSpec((1, SUBLANES, f), lambda i: (i, 0, 0))
        scratch.append(pltpu.VMEM((SUBLANES, f), F32))
    else:
        assert seq_len == SUBLANES and tm % seq_len == 0 and t_valid >= CONV_W - 1
        state_blk = (tm // seq_len, CONV_W - 1, f)
        in_specs.append(pl.BlockSpec((None,) + state_blk, lambda i: (layer, i, 0, 0)))
        args.append(conv_state)
        go_shape = jax.ShapeDtypeStruct((n // seq_len, CONV_W - 1, f), F32)
        go_spec = pl.BlockSpec(state_blk, lambda i: (i, 0, 0))
    if apply_final:
        in_specs.append(_resident(final_g))
        args.append(final_g)
    out_specs = [pl.BlockSpec((tm, d), lambda i: (i, 0)), go_spec]
    out_shape = [jax.ShapeDtypeStruct((n, d), F32), go_shape]
    proj_main, proj_bias, proj_extra = 0, False, False
    if next_proj is not None:
        proj_main = next_proj['e_main']
        proj_bias = next_proj['bias'] is not None
        proj_extra = next_proj['w_extra'] is not None
        assert proj_main % PROJ_CHUNK == 0
        p_in_specs, p_args, p_out_specs, p_out_shape = _proj_operands(next_proj, n, tm)
        in_specs += p_in_specs
        args += p_args
        out_specs += p_out_specs
        out_shape += p_out_shape
    scratch = [pltpu.VMEM((tm, f), BF16)] + scratch
    return pl.pallas_call(
        functools.partial(_tail_kernel, tm=tm, seq_len=seq_len, t_valid=t_valid, carry_mode=carry_mode,
                          apply_final=apply_final,
                          proj_main=proj_main, proj_bias=proj_bias, proj_extra=proj_extra),
        grid=(n // tm,),
        in_specs=in_specs,
        out_specs=out_specs,
        out_shape=out_shape,
        scratch_shapes=scratch,
        compiler_params=_params(("arbitrary",)),
        name="layer_tail",
    )(*args)


def _mlstm_kernel(q_ref, k_ref, v_ref, o_ref, gi_ref, gf_ref, bi_ref, bf_ref, ng_ref,
                  c0_ref, n0_ref, m0_ref, y_ref, c1_ref, n1_ref, m1_ref, fs_ref, ms_ref,
                  *, units, gb, seg_len, t_valid):
    @pl.when(pl.program_id(1) == 0)
    def _():
        c1_ref[...] = c0_ref[...]
        n1_ref[...] = n0_ref[...]
        m1_ref[...] = m0_ref[...]

    for u in range(units):
        seqs = pl.ds(u * gb, gb)
        _mlstm_unit(q_ref.at[u], k_ref.at[u], v_ref.at[u], o_ref.at[u], gi_ref.at[u], gf_ref.at[u],
                    bi_ref, bf_ref, ng_ref, y_ref.at[u], c1_ref.at[seqs], n1_ref.at[u], m1_ref.at[u],
                    fs_ref.at[u], ms_ref.at[u], gb=gb, seg_len=seg_len, t_valid=t_valid)


def _mlstm_unit(q_ref, k_ref, v_ref, o_ref, gi_ref, gf_ref, bi_ref, bf_ref, ng_ref,
                y_ref, c1_ref, n1_ref, m1_ref, fs_ref, ms_ref, *, gb, seg_len, t_valid):
    shape = (ROWS, LANES)
    row = lax.broadcasted_iota(jnp.int32, shape, 0)
    lane = lax.broadcasted_iota(jnp.int32, shape, 1)
    tpos = row % seg_len
    low_half = lane < DKA

    ig = _softcap(gi_ref[...] + bi_ref[...])
    lf = _log_sigmoid(_softcap(gf_ref[...] + bf_ref[...]))
    if t_valid < seg_len:
        valid = tpos < t_valid
        ig = jnp.where(valid, ig, NEG)
        lf = jnp.where(valid, lf, 0.0)
    f_cum = _seg_scan(lf, seg_len, jnp.add, tpos)
    a = ig - f_cum
    cmax = _seg_scan(a, seg_len, jnp.maximum, tpos)
    m_prev_g = m1_ref[...]
    m_prev = _rep_rows(m_prev_g, seg_len)
    m_t = f_cum + jnp.maximum(m_prev, cmax)
    g = jnp.exp(jnp.minimum(f_cum + m_prev - m_t, 0.0))
    u = f_cum - m_t
    em = jnp.exp(-m_t)

    fs_ref[...] = f_cum
    ms_ref[...] = m_t
    f_last_g = fs_ref[pl.ds(seg_len - 1, gb, stride=seg_len), :]
    m_last_g = ms_ref[pl.ds(seg_len - 1, gb, stride=seg_len), :]
    f_last = _rep_rows(f_last_g, seg_len)
    m_last = _rep_rows(m_last_g, seg_len)
    w_last = jnp.exp(jnp.minimum(a + f_last - m_last, 0.0))
    g_last_g = jnp.exp(jnp.minimum(f_last_g + m_prev_g - m_last_g, 0.0))

    a_t = a.T
    u_w = _expand_heads(u, HA, DVA, 3)
    g_w = _expand_heads(g, HA, DVA, 2)
    em_w = _expand_heads(em, HA, DVA, 2)
    wl_w = _expand_heads(w_last, HA, DKA, 2)
    gl_rows = g_last_g if gb >= SUBLANES else jnp.broadcast_to(g_last_g, (SUBLANES, LANES))
    gl_w = _expand_heads(gl_rows, HA, DKA, 2)[:gb]

    col_i = lax.broadcasted_iota(jnp.int32, (ROWS, 2 * ROWS), 1) % ROWS
    row_i = lax.broadcasted_iota(jnp.int32, (ROWS, 2 * ROWS), 0)
    causal2 = (col_i <= row_i) & ((col_i // seg_len) == (row_i // seg_len))
    state_low = lax.broadcasted_iota(jnp.int32, (gb * LANES, LANES), 0) % LANES < DKA
    flat_low = lax.broadcasted_iota(jnp.int32, (LANES, gb * LANES), 1) % LANES < DKA
    sel = (lax.broadcasted_iota(jnp.int32, (max(gb, SUBLANES), ROWS), 1) // seg_len
           == lax.broadcasted_iota(jnp.int32, (max(gb, SUBLANES), ROWS), 0)).astype(BF16)
    ones_blk = (lax.broadcasted_iota(jnp.int32, (2 * ROWS, 2 * DVA), 0) // ROWS
                == lax.broadcasted_iota(jnp.int32, (2 * ROWS, 2 * DVA), 1) // DVA).astype(BF16)
    zeros_v = jnp.zeros((ROWS, DVA), v_ref.dtype)

    q_all = q_ref[...]
    k_all = k_ref[...] * (DKA ** -0.5)
    k2_all = k_all * wl_w

    for p in range(HA // 2):
        cols = slice(p * LANES, (p + 1) * LANES)
        wide = slice(2 * p * DVA, (2 * p + 2) * DVA)
        qp, kp, k2 = q_all[:, cols], k_all[:, cols], k2_all[:, cols]
        c_p = c1_ref[:, p].reshape(gb * LANES, DVA)
        n_p = n1_ref[p]
        v2 = v_ref[:, wide]
        v_a, v_b = v2[:, :DVA], v2[:, DVA:]

        k_sep = jnp.concatenate([jnp.where(low_half, kp, 0.0), jnp.where(low_half, 0.0, kp)], axis=0)
        s2 = _dot_nt(qp, k_sep)
        logw2 = jnp.concatenate([a_t[2 * p:2 * p + 1], a_t[2 * p + 1:2 * p + 2]], axis=1) + u_w[:, wide]
        qk2 = (s2 * jnp.where(causal2, jnp.exp(jnp.minimum(logw2, 0.0)), 0.0)).astype(BF16)
        v_bd = jnp.concatenate([jnp.concatenate([v_a, zeros_v], axis=1),
                                jnp.concatenate([zeros_v, v_b], axis=1)], axis=0)
        num2 = jnp.dot(qk2, v_bd, preferred_element_type=F32)
        dsum2 = jnp.dot(qk2, ones_blk, preferred_element_type=F32)

        qx = _expand_blockdiag(qp, gb, seg_len)
        c_sep = jnp.concatenate([jnp.where(state_low, c_p, 0.0), jnp.where(state_low, 0.0, c_p)], axis=1)
        inter2 = _dot(qx, c_sep)
        n_flat = n_p if gb == 1 else jnp.concatenate([n_p[b:b + 1] for b in range(gb)], axis=1)
        n_b = jnp.broadcast_to(n_flat, (LANES, gb * LANES))
        n_sep = jnp.concatenate([jnp.where(flat_low, n_b, 0.0), jnp.where(flat_low, 0.0, n_b)], axis=0)
        dint2 = _dot_nt(qx, n_sep)

        g2 = g_w[:, wide]
        hh2 = (num2 + g2 * inter2) / jnp.maximum(jnp.abs(dsum2 + g2 * dint2), em_w[:, wide])
        for half in range(2):
            hcols = slice((2 * p + half) * DVA, (2 * p + half + 1) * DVA)
            hh = hh2[:, half * DVA:(half + 1) * DVA]
            msq = jnp.mean(hh * hh, axis=-1, keepdims=True)
            hn = hh * lax.rsqrt(msq + EPS) * ng_ref[:, hcols]
            y_ref[:, hcols] = (hn * _sigmoid(o_ref[:, hcols].astype(F32))).astype(y_ref.dtype)

        upd2 = _dot_tn(_expand_blockdiag(k2, gb, seg_len), v2)
        upd = jnp.where(state_low, upd2[:, :DVA], upd2[:, DVA:])
        n_scale = gl_w[:, cols]
        scale = jnp.concatenate([jnp.broadcast_to(n_scale[b:b + 1], (LANES, LANES)).T for b in range(gb)],
                                axis=0)
        c1_ref[:, p] = (scale * c_p + upd).reshape(gb, LANES, DVA)
        n_upd = jnp.dot(sel, k2.astype(BF16), preferred_element_type=F32)[:gb]
        n1_ref[p] = n_scale * n_p + n_upd

    m1_ref[...] = m_last_g


def _mlstm(proj, gates, b_i, b_f, norm_g, c_all, layer, n0, m0, gb, seg_len, t_valid, n_chunks, units):
    n = proj.shape[0]
    n_blocks = n // (ROWS * n_chunks)
    assert n_blocks % units == 0
    bsz = n_blocks * gb
    half_pairs = HA // 2
    rows3 = lambda a: a.reshape(n_blocks, n_chunks * ROWS, a.shape[-1])
    proj3, gates3 = rows3(proj), rows3(gates)
    c0r = c_all.reshape(c_all.shape[0], bsz, half_pairs, LANES, DVA)
    n0r = n0.reshape(n_blocks, gb, half_pairs, LANES).transpose(0, 2, 1, 3)
    m0r = jnp.pad(m0, ((0, 0), (0, LANES - HA))).reshape(n_blocks, gb, LANES)
    pad = lambda b: jnp.pad(b, (0, LANES - HA)).reshape(1, LANES)
    blk = lambda width, cb: pl.BlockSpec((units, ROWS, width), lambda i, c: (i, c, cb))
    c_blk = (units * gb, half_pairs, LANES, DVA)
    y, c1, n1, m1 = pl.pallas_call(
        functools.partial(_mlstm_kernel, units=units, gb=gb, seg_len=seg_len, t_valid=t_valid),
        grid=(n_blocks // units, n_chunks),
        in_specs=[blk(HA * DKA, 0), blk(HA * DKA, 1), blk(HA * DVA, 1), blk(HA * DVA, 2),
                  blk(LANES, 0), blk(LANES, 1),
                  pl.BlockSpec((1, LANES), lambda i, c: (0, 0)),
                  pl.BlockSpec((1, LANES), lambda i, c: (0, 0)),
                  pl.BlockSpec((1, HA * DVA), lambda i, c: (0, 0)),
                  pl.BlockSpec((None,) + c_blk, lambda i, c: (layer, i, 0, 0, 0)),
                  pl.BlockSpec((units, half_pairs, gb, LANES), lambda i, c: (i, 0, 0, 0)),
                  pl.BlockSpec((units, gb, LANES), lambda i, c: (i, 0, 0))],
        out_specs=[blk(HA * DVA, 0),
                   pl.BlockSpec(c_blk, lambda i, c: (i, 0, 0, 0)),
                   pl.BlockSpec((units, half_pairs, gb, LANES), lambda i, c: (i, 0, 0, 0)),
                   pl.BlockSpec((units, gb, LANES), lambda i, c: (i, 0, 0))],
        out_shape=[jax.ShapeDtypeStruct((n_blocks, n_chunks * ROWS, HA * DVA), BF16),
                   jax.ShapeDtypeStruct(c0r.shape[1:], F32),
                   jax.ShapeDtypeStruct(n0r.shape, F32),
                   jax.ShapeDtypeStruct(m0r.shape, F32)],
        scratch_shapes=[pltpu.VMEM((units, ROWS, LANES), F32), pltpu.VMEM((units, ROWS, LANES), F32)],
        compiler_params=_params(("parallel", "arbitrary")),
        name="mlstm_mixer",
    )(proj3, proj3, proj3, proj3, gates3, gates3, pad(b_i), pad(b_f), norm_g.reshape(1, -1), c0r, n0r, m0r)
    y = y.reshape(n, HA * DVA)
    c1 = c1.reshape(bsz, HA, DKA, DVA)
    n1 = n1.transpose(0, 2, 1, 3).reshape(bsz, HA, DKA)
    m1 = m1.reshape(bsz, LANES)[:, :HA]
    return y, c1, n1, m1


def _gla_kernel(q_ref, k_ref, v_ref, r_ref, z_ref, wgu_ref, bg_ref, ng_ref, s0_ref,
                y_ref, s1_ref, bc_ref, b2_ref, kf_ref, *, units, gb, seg_len, t_valid, heads):
    @pl.when(pl.program_id(2) == 0)
    def _():
        s1_ref[...] = s0_ref[...]

    for u in range(units):
        _gla_unit(q_ref.at[u], k_ref.at[u], v_ref.at[u], r_ref.at[u], z_ref.at[u], wgu_ref, bg_ref, ng_ref,
                  y_ref.at[u], s1_ref.at[pl.ds(u * gb, gb)], bc_ref.at[u], b2_ref.at[u], kf_ref.at[u],
                  gb=gb, seg_len=seg_len, t_valid=t_valid, heads=heads)


def _gla_unit(q_ref, k_ref, v_ref, r_ref, z_ref, wgu_ref, bg_ref, ng_ref, y_ref, s1_ref, bc_ref, b2_ref, kf_ref,
              *, gb, seg_len, t_valid, heads):
    width = heads * DKB
    row = lax.broadcasted_iota(jnp.int32, (ROWS, width), 0)
    tpos = row % seg_len
    lg = _log_sigmoid(jnp.dot(z_ref[...].astype(BF16), wgu_ref[...], preferred_element_type=F32)
                      + bg_ref[...]) / GLA_TAU
    k_all = k_ref[...].astype(F32)
    if t_valid < seg_len:
        valid = tpos < t_valid
        lg = jnp.where(valid, lg, 0.0)
        k_all = jnp.where(valid, k_all, 0.0)
    bc = _seg_cumsum(lg, seg_len)

    n_sub = ROWS // GLA_SUB
    sub_row = lax.broadcasted_iota(jnp.int32, (GLA_SUB, ROWS), 0)
    sub_col = lax.broadcasted_iota(jnp.int32, (GLA_SUB, ROWS), 1)

    for h in range(heads):
        kcols = slice(h * DKB, (h + 1) * DKB)
        vcols = slice(h * DVB, (h + 1) * DVB)
        qh = q_ref[:, kcols].astype(F32) * (DKB ** -0.5)
        kh = k_all[:, kcols]
        vh = v_ref[:, vcols]
        bch = bc[:, kcols]
        bc_ref[...] = bch
        b2_ref[...] = bch * LOG2_E
        kf_ref[...] = kh
        b_last_g = bc_ref[pl.ds(seg_len - 1, gb, stride=seg_len), :]
        b_last = _rep_rows(b_last_g, seg_len)
        s_h = s1_ref[:, h].reshape(gb * DKB, DVB)

        inter = _dot(_expand_blockdiag(qh * jnp.exp(bch), gb, seg_len), s_h)

        att_rows = []
        k_decayed = []
        for blk in range(n_sub):
            r0 = blk * GLA_SUB
            q_b = qh[r0:r0 + GLA_SUB]
            bc_b = bch[r0:r0 + GLA_SUB]
            b2_b = b2_ref[r0:r0 + GLA_SUB, :]
            att = jnp.zeros((GLA_SUB, ROWS), F32)
            for s in range(GLA_SUB):
                k_row = jnp.broadcast_to(kf_ref[r0 + s:r0 + s + 1, :], (GLA_SUB, DKB))
                b2_row = jnp.broadcast_to(b2_ref[r0 + s:r0 + s + 1, :], (GLA_SUB, DKB))
                cs = jnp.sum(q_b * k_row * jnp.exp2(b2_b - b2_row), axis=-1, keepdims=True)
                att = jnp.where(sub_col == r0 + s, cs, att)
            att = jnp.where(sub_col <= sub_row + r0, att, 0.0)
            if seg_len > GLA_SUB and blk > 0:
                b_edge = bch[r0 - 1:r0]
                if blk > 1:
                    step = jnp.exp(b_edge - bch[r0 - GLA_SUB - 1:r0 - GLA_SUB])
                    k_decayed = [kb * step for kb in k_decayed]
                k_decayed.append(kh[r0 - GLA_SUB:r0] * jnp.exp(b_edge - bch[r0 - GLA_SUB:r0]))
                k_t = jnp.concatenate(k_decayed + [jnp.zeros((ROWS - r0, DKB), F32)], axis=0)
                att = att + _dot_nt(q_b * jnp.exp(bc_b - b_edge), k_t)
            att_rows.append(att)
        att_full = jnp.concatenate(att_rows, axis=0)

        o = inter + _dot(att_full, vh)
        msq = jnp.mean(o * o, axis=-1, keepdims=True)
        on = o * lax.rsqrt(msq + EPS) * ng_ref[:, vcols]
        r = r_ref[:, vcols].astype(F32)
        y_ref[:, vcols] = (on * (r * _sigmoid(r))).astype(y_ref.dtype)

        k_d = kh * jnp.exp(b_last - bch)
        upd = _dot_tn(_expand_blockdiag(k_d, gb, seg_len), vh)
        e_last = jnp.exp(b_last_g)
        e_cols = jnp.concatenate(
            [jnp.broadcast_to(e_last[b:b + 1], (DKB, DKB)).T for b in range(gb)], axis=0)
        e_cols = jnp.concatenate([e_cols] * (DVB // DKB), axis=1)
        s1_ref[:, h] = (e_cols * s_h + upd).reshape(gb, DKB, DVB)


def _gla(proj, z, w_gate_up, b_gate, norm_g, s0, gb, seg_len, t_valid, n_chunks, heads, units):
    n = proj.shape[0]
    n_blocks = n // (ROWS * n_chunks)
    assert n_blocks % units == 0 and seg_len % GLA_SUB == 0
    n_hb = HB // heads
    wq, wv = heads * DKB, heads * DVB
    q_off, k_off = 0, (HB * DKB) // wq
    v_off, r_off = (2 * HB * DKB) // wv, (2 * HB * DKB + HB * DVB) // wv
    rows3 = lambda a: a.reshape(n_blocks, n_chunks * ROWS, a.shape[-1])
    proj3, z3 = rows3(proj), rows3(z)
    blk = lambda width, off: pl.BlockSpec((units, ROWS, width), lambda i, hb, c: (i, c, off + hb))
    state_spec = pl.BlockSpec((units * gb, heads, DKB, DVB), lambda i, hb, c: (i, hb, 0, 0))
    wgu = jnp.zeros((LANES, HB * DKB), BF16).at[:GLA_RANK].set(w_gate_up.astype(BF16))
    y, s1 = pl.pallas_call(
        functools.partial(_gla_kernel, units=units, gb=gb, seg_len=seg_len, t_valid=t_valid, heads=heads),
        grid=(n_blocks // units, n_hb, n_chunks),
        in_specs=[blk(wq, q_off), blk(wq, k_off), blk(wv, v_off), blk(wv, r_off),
                  pl.BlockSpec((units, ROWS, LANES), lambda i, hb, c: (i, c, 0)),
                  pl.BlockSpec((LANES, wq), lambda i, hb, c: (0, hb)),
                  pl.BlockSpec((1, wq), lambda i, hb, c: (0, hb)),
                  pl.BlockSpec((1, wv), lambda i, hb, c: (0, hb)),
                  state_spec],
        out_specs=[blk(wv, 0), state_spec],
        out_shape=[jax.ShapeDtypeStruct((n_blocks, n_chunks * ROWS, HB * DVB), BF16),
                   jax.ShapeDtypeStruct(s0.shape, F32)],
        scratch_shapes=[pltpu.VMEM((units, ROWS, DKB), F32)] * 3,
        compiler_params=_params(("parallel", "parallel", "arbitrary")),
        name="gla_mixer",
    )(proj3, proj3, proj3, proj3, z3, wgu, b_gate.reshape(1, -1), norm_g.reshape(1, -1), s0)
    return y.reshape(n, HB * DVB), s1


def _swa_kernel(q_ref, kc_ref, vc_ref, kp_ref, vp_ref, sink_ref, o_ref, *, blocks, units, q_len, prev_from_grid):
    for b in range(blocks):
        _swa_block(q_ref.at[b], kc_ref.at[b], vc_ref.at[b], kp_ref.at[b], vp_ref.at[b], sink_ref, o_ref.at[b],
                   units=units, q_len=q_len, prev_from_grid=prev_from_grid)


def _swa_block(q_ref, kc_ref, vc_ref, kp_ref, vp_ref, sink_ref, o_ref, *, units, q_len, prev_from_grid):
    group = HC // HKV
    rows4 = group * ROWS
    row = lax.broadcasted_iota(jnp.int32, (rows4, ROWS), 0) % ROWS
    key = lax.broadcasted_iota(jnp.int32, (rows4, ROWS), 1)
    q_pos = row % q_len
    mask_p = key >= q_pos
    if prev_from_grid:
        mask_p = mask_p & (pl.program_id(1) > 0)
    mask_c = ((key // q_len) == (row // q_len)) & ((key % q_len) <= q_pos)
    bias_p = jnp.where(mask_p, 0.0, NEG)
    bias_c = jnp.where(mask_c, 0.0, NEG)
    low = lax.broadcasted_iota(jnp.int32, (ROWS, LANES), 1) < HDC
    ones = jnp.ones((WINDOW, LANES), BF16)
    sinks = sink_ref[...]
    scale = HDC ** -0.5

    def per_unit(lhs, rhs, contract_rhs_rows):
        mm = _dot if contract_rhs_rows else _dot_nt
        if units == 1:
            return mm(lhs, rhs)
        pieces = [[None] * units for _ in range(group)]
        for u in range(units):
            rows_u = jnp.concatenate([lhs[gi * ROWS + u * q_len:gi * ROWS + (u + 1) * q_len]
                                      for gi in range(group)], axis=0)
            res = mm(rows_u, rhs[u * WINDOW:(u + 1) * WINDOW])
            for gi in range(group):
                pieces[gi][u] = res[gi * q_len:(gi + 1) * q_len]
        return jnp.concatenate([pieces[gi][u] for gi in range(group) for u in range(units)], axis=0)

    out_tiles = [None] * (HC // 2)
    for kk in range(HKV):
        kv_cols = slice((kk // 2) * LANES, (kk // 2 + 1) * LANES)
        k_half = kk % 2
        k_sel = low if k_half == 0 else ~low
        q_parts, sink_parts = [], []
        for gi in range(group):
            hq = kk * group + gi
            tile = q_ref[:, (hq // 2) * LANES:(hq // 2 + 1) * LANES] * scale
            if hq % 2 != k_half:
                tile = pltpu.roll(tile, HDC, axis=1)
            q_parts.append(jnp.where(k_sel, tile, 0.0))
            sink_parts.append(jnp.broadcast_to(_col(sinks, hq), (ROWS, LANES)))
        q4 = jnp.concatenate(q_parts, axis=0)
        sink = jnp.concatenate(sink_parts, axis=0)
        s_p = per_unit(q4, kp_ref[:, kv_cols], False) + bias_p
        s_c = _dot_nt(q4, kc_ref[:, kv_cols]) + bias_c
        row_max = jnp.max(jnp.maximum(s_p, s_c), axis=-1, keepdims=True)
        mx = jnp.maximum(jnp.broadcast_to(row_max, (rows4, LANES)), sink)
        p_p = jnp.exp(s_p - mx).astype(BF16)
        p_c = jnp.exp(s_c - mx).astype(BF16)
        den = (jnp.dot(p_p, ones, preferred_element_type=F32) + jnp.dot(p_c, ones, preferred_element_type=F32)
               + jnp.exp(sink - mx))
        out4 = (per_unit(p_p, vp_ref[:, kv_cols], True) + _dot(p_c, vc_ref[:, kv_cols])) / den
        for gi in range(group):
            hq = kk * group + gi
            part = out4[gi * ROWS:(gi + 1) * ROWS]
            if hq % 2 != k_half:
                part = pltpu.roll(part, HDC, axis=1)
            prev = out_tiles[hq // 2]
            out_tiles[hq // 2] = part if prev is None else jnp.where(low if hq % 2 == 0 else ~low, part, prev)
    for t in range(HC // 2):
        o_ref[:, t * LANES:(t + 1) * LANES] = out_tiles[t].astype(o_ref.dtype)


def _swa(proj, sinks, units, q_len, n_chunks, blocks, prev_k=None, prev_v=None):
    n = proj.shape[0]
    n_chains = n // (ROWS * n_chunks)
    assert n_chains % blocks == 0
    kv_w = HKV * HDC
    k_blk, v_blk = (HC * HDC) // kv_w, (HC * HDC) // kv_w + 1
    prev_from_grid = prev_k is None
    proj3 = proj.reshape(n_chains, n_chunks * ROWS, proj.shape[-1])
    cur = lambda blk: pl.BlockSpec((blocks, ROWS, kv_w), lambda i, c: (i, c, blk))
    if prev_from_grid:
        prev = lambda blk: pl.BlockSpec((blocks, ROWS, kv_w), lambda i, c: (i, jnp.maximum(c - 1, 0), blk))
        prev_specs = [prev(k_blk), prev(v_blk)]
        prev_args = [proj3, proj3]
    else:
        prev_specs = [pl.BlockSpec((blocks, units * WINDOW, kv_w), lambda i, c: (i, 0, 0))] * 2
        prev_args = [prev_k.reshape(n_chains, units * WINDOW, kv_w), prev_v.reshape(n_chains, units * WINDOW, kv_w)]
    sink_row = jnp.pad(sinks.astype(F32), (0, LANES - HC)).reshape(1, LANES)
    y = pl.pallas_call(
        functools.partial(_swa_kernel, blocks=blocks, units=units, q_len=q_len, prev_from_grid=prev_from_grid),
        grid=(n_chains // blocks, n_chunks),
        in_specs=[pl.BlockSpec((blocks, ROWS, HC * HDC), lambda i, c: (i, c, 0)),
                  cur(k_blk), cur(v_blk)] + prev_specs +
                 [pl.BlockSpec((1, LANES), lambda i, c: (0, 0))],
        out_specs=pl.BlockSpec((blocks, ROWS, HC * HDC), lambda i, c: (i, c, 0)),
        out_shape=jax.ShapeDtypeStruct((n_chains, n_chunks * ROWS, HC * HDC), BF16),
        compiler_params=_params(("parallel", "arbitrary")),
        name="swa_mixer",
    )(proj3, proj3, proj3, *prev_args, sink_row)
    return y.reshape(n, HC * HDC)


def _proj_spec(w, i):
    kind, j = i % N_MIXERS, i // N_MIXERS
    spec = {'g': (w['norm_mix'], i), 'bias': None, 'w_extra': None, 'out_dtype': BF16}
    if kind == 0:
        spec.update(w=(w['a_w_in'], j), e_main=2 * HA * (DKA + DVA), w_extra=(w['a_w_gates'], j))
    elif kind == 1:
        spec.update(w=(w['b_w_in'], j), e_main=2 * HB * (DKB + DVB), w_extra=(w['b_w_z'], j))
    else:
        spec.update(w=(w['c_w_in'], j), e_main=(HC + 2 * HKV) * HDC, bias=(w['c_b_in'], j), out_dtype=F32)
    return spec


def _trunk(x3, st, w, is_prompt):
    bsz, t_in, _ = x3.shape
    if is_prompt:
        t, t_valid, gb, seg_len = t_in, ROWS, 1, ROWS
        units = math.gcd(MIXER_UNITS, bsz)
        x = x3.reshape(bsz * t, D_MODEL)
    else:
        t, t_valid, gb, seg_len = SAMPLE_T_PAD, t_in, ROWS // SAMPLE_T_PAD, SAMPLE_T_PAD
        units = 1
        x = jnp.pad(x3, ((0, 0), (0, t - t_in), (0, 0))).reshape(bsz * t, D_MODEL)
    n_chunks = (gb * t) // ROWS
    new = {'a_c': [], 'a_n': [], 'a_m': [], 'b_s': [], 'c_k': [], 'c_v': [], 'f': []}

    projected = _norm_matmul(x, _proj_spec(w, 0))
    for i in range(DEPTH):
        kind, j = i % N_MIXERS, i // N_MIXERS
        if kind == 0:
            proj, gates = projected
            y, c1, n1, m1 = _mlstm(proj, gates, w['a_b_i'][j], w['a_b_f'][j], w['a_norm'][j],
                                   st['a_c'], j, st['a_n'][j], st['a_m'][j], gb, seg_len, t_valid, n_chunks,
                                   units=units)
            new['a_c'].append(c1)
            new['a_n'].append(n1)
            new['a_m'].append(m1)
            w_out = (w['a_w_out'], j)
        elif kind == 1:
            proj, z = projected
            y, s1 = _gla(proj, z, w['b_w_gate_up'][j], w['b_b_gate'][j], w['b_norm'][j], st['b_s'][j],
                         gb, seg_len, t_valid, n_chunks, heads=HB if is_prompt else 1,
                         units=units)
            new['b_s'].append(s1)
            w_out = (w['b_w_out'], j)
        else:
            proj = projected[0]
            kv_w = HKV * HDC
            end = t if is_prompt else t_valid
            keep = min(WINDOW, end)
            newest = proj.reshape(bsz, t, -1)[:, end - keep:end, HC * HDC:]
            k_new = newest[:, :, :kv_w].reshape(bsz, keep, HKV, HDC)
            v_new = newest[:, :, kv_w:].reshape(bsz, keep, HKV, HDC)
            if is_prompt:
                y = _swa(proj, w['c_sinks'][j], 1, ROWS, n_chunks, units)
                new['c_k'].append(k_new)
                new['c_v'].append(v_new)
            else:
                k_buf, v_buf = st['c_k'][j], st['c_v'][j]
                y = _swa(proj, w['c_sinks'][j], gb, seg_len, n_chunks, 1,
                         prev_k=k_buf.reshape(bsz * WINDOW, kv_w), prev_v=v_buf.reshape(bsz * WINDOW, kv_w))
                new['c_k'].append(jnp.concatenate([k_buf[:, t_valid:], k_new], axis=1))
                new['c_v'].append(jnp.concatenate([v_buf[:, t_valid:], v_new], axis=1))
            w_out = (w['c_w_out'], j)

        last = i == DEPTH - 1
        tail_args = (x, y, w_out, w['ffn'], i)
        tail_kw = dict(seq_len=t, final_g=w['norm_final'] if last else None,
                       next_proj=None if last else _proj_spec(w, i + 1),
                       tm=TAIL_TM_PROMPT if is_prompt else TAIL_TM_SAMPLE)
        if is_prompt:
            x, gate_tail, *projected = _layer_tail(*tail_args, **tail_kw)
            seq_tails = gate_tail.reshape(bsz, t // TAIL_TM_PROMPT, SUBLANES, D_FF)[:, -1]
            new['f'].append(seq_tails[:, SUBLANES - (CONV_W - 1):])
        else:
            x, new_state, *projected = _layer_tail(*tail_args, conv_state=st['f'], t_valid=t_valid, **tail_kw)
            new['f'].append(new_state)

    out = {name: jnp.stack(vals) for name, vals in new.items()}
    y = x.reshape(bsz, t, D_MODEL)[:, :t_in]
    return y, out


def _prepare_weights(norm_mix_g, norm_ffn_g, norm_final_g, a_w_in, a_b_i, a_b_f, a_norm_g, a_w_out, b_w_in,
                     b_w_gate_up, b_b_gate, b_norm_g, b_w_out, c_w_in, c_b_in, c_sinks, c_w_out, f_w_up,
                     f_conv_w, f_conv_b, f_w_down):
    n_a, n_b = a_w_in.shape[0], b_w_in.shape[0]
    e_a = 2 * HA * (DKA + DVA)
    e_b = 2 * HB * (DKB + DVB)
    a_w_gates = jnp.zeros((n_a, D_MODEL, 2 * LANES), BF16)
    a_w_gates = a_w_gates.at[:, :, :HA].set(a_w_in[:, :, e_a:e_a + HA].astype(BF16))
    a_w_gates = a_w_gates.at[:, :, LANES:LANES + HA].set(a_w_in[:, :, e_a + HA:].astype(BF16))
    b_w_z = jnp.zeros((n_b, D_MODEL, LANES), BF16).at[:, :, :GLA_RANK].set(b_w_in[:, :, e_b:].astype(BF16))
    ffn = {'g': norm_ffn_g[:, None, :], 'w_up': f_w_up.astype(BF16), 'conv_w': f_conv_w,
           'conv_b': f_conv_b[:, None, :], 'w_down': f_w_down.astype(BF16)}
    w = {'norm_mix': norm_mix_g[:, None, :], 'norm_final': norm_final_g[None, :], 'ffn': ffn,
         'a_w_in': a_w_in.astype(BF16), 'a_w_gates': a_w_gates, 'a_b_i': a_b_i, 'a_b_f': a_b_f,
         'a_norm': a_norm_g, 'a_w_out': a_w_out.astype(BF16),
         'b_w_in': b_w_in.astype(BF16), 'b_w_z': b_w_z, 'b_w_gate_up': b_w_gate_up, 'b_b_gate': b_b_gate,
         'b_norm': b_norm_g, 'b_w_out': b_w_out.astype(BF16),
         'c_w_in': c_w_in.astype(BF16), 'c_b_in': c_b_in[:, None, :], 'c_sinks': c_sinks,
         'c_w_out': c_w_out.astype(BF16)}
    return w


def kernel(x_prompt, x_sample, state_mlstm_c, state_mlstm_n, state_mlstm_m, state_gla, cache_swa_k, cache_swa_v, state_ffn_conv, norm_mix_g, norm_ffn_g, norm_final_g, a_w_in, a_b_i, a_b_f, a_norm_g, a_w_out, b_w_in, b_w_gate_up, b_b_gate, b_norm_g, b_w_out, c_w_in, c_b_in, c_sinks, c_w_out, f_w_up, f_conv_w, f_conv_b, f_w_down):
    w = _prepare_weights(norm_mix_g, norm_ffn_g, norm_final_g, a_w_in, a_b_i, a_b_f, a_norm_g, a_w_out, b_w_in,
                         b_w_gate_up, b_b_gate, b_norm_g, b_w_out, c_w_in, c_b_in, c_sinks, c_w_out, f_w_up,
                         f_conv_w, f_conv_b, f_w_down)
    n_a, n_b, n_c = a_w_in.shape[0], b_w_in.shape[0], c_w_in.shape[0]
    bp = x_prompt.shape[0]
    st_p = {'a_c': jnp.zeros((n_a, bp, HA, DKA, DVA), F32),
            'a_n': jnp.zeros((n_a, bp, HA, DKA), F32),
            'a_m': jnp.zeros((n_a, bp, HA), F32),
            'b_s': jnp.zeros((n_b, bp, HB, DKB, DVB), F32),
            'c_k': [None] * n_c, 'c_v': [None] * n_c, 'f': None}
    st_s = {'a_c': state_mlstm_c, 'a_n': state_mlstm_n, 'a_m': state_mlstm_m, 'b_s': state_gla,
            'c_k': cache_swa_k, 'c_v': cache_swa_v, 'f': state_ffn_conv}
    y_prompt, np_ = _trunk(x_prompt, st_p, w, True)
    y_sample, ns_ = _trunk(x_sample, st_s, w, False)
    return (y_prompt, y_sample,
            np_['a_c'], np_['a_n'], np_['a_m'], np_['b_s'], np_['c_k'], np_['c_v'], np_['f'],
            ns_['a_c'], ns_['a_n'], ns_['a_m'], ns_['b_s'], ns_['c_k'], ns_['c_v'], ns_['f'])
```

```python
import functools
import math

import jax
import jax.numpy as jnp
from jax import lax
from jax.experimental import pallas as pl
from jax.experimental.pallas import tpu as pltpu

F32 = jnp.float32
BF16 = jnp.bfloat16

D_MODEL = 1024
DEPTH = 4
N_MIXERS = 3
HA, DKA, DVA = 8, 64, 128
GATE_SOFTCAP = 15.0
HB, DKB, DVB = 4, 128, 256
GLA_RANK = 16
GLA_TAU = 16.0
HC, HKV, HDC = 16, 4, 64
WINDOW = 128
D_FF = 2816
CONV_W = 3
EPS = 1e-6
NEG = -1e30
LOG2_E = 1.4426950408889634

ROWS = 128
LANES = 128
SUBLANES = 8
SAMPLE_T_PAD = 8
GLA_SUB = 8
FFN_CHUNK = 256
PROJ_CHUNK = 512
MIXER_UNITS = 4
TAIL_TM_PROMPT = 512
TAIL_TM_SAMPLE = 256
VMEM_LIMIT = 56 * 1024 * 1024


def _params(sem):
    return pltpu.CompilerParams(dimension_semantics=sem, vmem_limit_bytes=VMEM_LIMIT)


def _dot(a, b):
    return jnp.dot(a.astype(BF16), b.astype(BF16), preferred_element_type=F32)


def _dot_nt(a, b):
    return lax.dot_general(a.astype(BF16), b.astype(BF16), (((1,), (1,)), ((), ())),
                           preferred_element_type=F32)


def _dot_tn(a, b):
    return lax.dot_general(a.astype(BF16), b.astype(BF16), (((0,), (0,)), ((), ())),
                           preferred_element_type=F32)


def _sigmoid(x):
    return 1.0 / (1.0 + jnp.exp(-x))


def _log_sigmoid(x):
    return jnp.minimum(x, 0.0) - jnp.log(1.0 + jnp.exp(-jnp.abs(x)))


def _softcap(z):
    return GATE_SOFTCAP * jnp.tanh(z / GATE_SOFTCAP)


def _col(x, h):
    lane = lax.broadcasted_iota(jnp.int32, x.shape, 1)
    return jnp.sum(jnp.where(lane == h, x, 0.0), axis=-1, keepdims=True)


def _expand_heads(x, n_heads, width, terms):
    lanes = x.shape[1]
    src = lax.broadcasted_iota(jnp.int32, (terms * lanes, n_heads * width), 0) % lanes
    dst = lax.broadcasted_iota(jnp.int32, (terms * lanes, n_heads * width), 1) // width
    pieces, rest = [], x
    for _ in range(terms):
        piece = rest.astype(BF16)
        pieces.append(piece)
        rest = rest - piece.astype(F32)
    return jnp.dot(jnp.concatenate(pieces, axis=1), (src == dst).astype(BF16), preferred_element_type=F32)


def _rep_rows(x, reps):
    g, c = x.shape
    if g == 1:
        return jnp.broadcast_to(x, (reps, c))
    return jnp.concatenate([jnp.broadcast_to(x[b:b + 1], (reps, c)) for b in range(g)], axis=0)


def _seg_scan(x, seg_len, op, tpos):
    s = 1
    while s < seg_len:
        shifted = pltpu.roll(x, s, axis=0)
        x = jnp.where(tpos >= s, op(x, shifted), x)
        s *= 2
    return x


def _seg_cumsum(x, seg_len):
    rows = x.shape[0]
    dst = lax.broadcasted_iota(jnp.int32, (rows, 3 * rows), 0)
    src = lax.broadcasted_iota(jnp.int32, (rows, 3 * rows), 1) % rows
    tri = ((src <= dst) & ((src // seg_len) == (dst // seg_len))).astype(BF16)
    pieces, rest = [], x
    for _ in range(3):
        piece = rest.astype(BF16)
        pieces.append(piece)
        rest = rest - piece.astype(F32)
    return jnp.dot(tri, jnp.concatenate(pieces, axis=0), preferred_element_type=F32)


def _expand_blockdiag(x, gb, seg_len):
    if gb == 1:
        return x
    seq = lax.broadcasted_iota(jnp.int32, x.shape, 0) // seg_len
    return jnp.concatenate([jnp.where(seq == b, x, 0.0) for b in range(gb)], axis=1)


def _resident(arr, layer=None):
    shape = arr.shape if layer is None else arr.shape[1:]
    block = shape if layer is None else (None,) + shape
    lead = () if layer is None else (layer,)
    return pl.BlockSpec(block, lambda *_: lead + (0,) * len(shape), pipeline_mode=pl.Buffered(1))


def _project(x, rows, gn_ref, wn_ref, bn_ref, wne_ref, p_ref, pe_ref, e_main):
    ms = jnp.mean(x * x, axis=-1, keepdims=True)
    xn = (x * lax.rsqrt(ms + EPS) * gn_ref[...]).astype(BF16)
    for c in range(e_main // PROJ_CHUNK):
        cols = slice(c * PROJ_CHUNK, (c + 1) * PROJ_CHUNK)
        acc = jnp.dot(xn, wn_ref[:, cols], preferred_element_type=F32)
        if bn_ref is not None:
            acc = acc + bn_ref[:, cols]
        p_ref[rows, cols] = acc.astype(p_ref.dtype)
    if wne_ref is not None:
        pe_ref[rows, :] = jnp.dot(xn, wne_ref[...], preferred_element_type=F32)


def _proj_operands(spec, n, tm):
    g, g_layer = spec['g']
    w, w_layer = spec['w']
    in_specs = [_resident(g, g_layer), _resident(w, w_layer)]
    args = [g, w]
    if spec['bias'] is not None:
        in_specs.append(_resident(*spec['bias']))
        args.append(spec['bias'][0])
    out_specs = [pl.BlockSpec((tm, spec['e_main']), lambda i: (i, 0))]
    out_shape = [jax.ShapeDtypeStruct((n, spec['e_main']), spec['out_dtype'])]
    if spec['w_extra'] is not None:
        in_specs.append(_resident(*spec['w_extra']))
        args.append(spec['w_extra'][0])
        ex = spec['w_extra'][0].shape[-1]
        out_specs.append(pl.BlockSpec((tm, ex), lambda i: (i, 0)))
        out_shape.append(jax.ShapeDtypeStruct((n, ex), F32))
    return in_specs, args, out_specs, out_shape


def _proj_kernel(*refs, e_main, has_bias, has_extra, tm):
    refs = list(refs)
    take = lambda cond=True: refs.pop(0) if cond else None
    x_ref, gn_ref, wn_ref = take(), take(), take()
    bn_ref, wne_ref = take(has_bias), take(has_extra)
    p_ref, pe_ref = take(), take(has_extra)
    for r in range(2):
        rows = slice(r * tm // 2, (r + 1) * tm // 2)
        _project(x_ref[rows, :], rows, gn_ref, wn_ref, bn_ref, wne_ref, p_ref, pe_ref, e_main)


def _norm_matmul(x, spec, tm=512):
    n, d = x.shape
    assert n % tm == 0 and spec['e_main'] % PROJ_CHUNK == 0
    in_specs, args, out_specs, out_shape = _proj_operands(spec, n, tm)
    return pl.pallas_call(
        functools.partial(_proj_kernel, e_main=spec['e_main'], has_bias=spec['bias'] is not None,
                          has_extra=spec['w_extra'] is not None, tm=tm),
        grid=(n // tm,),
        in_specs=[pl.BlockSpec((tm, d), lambda i: (i, 0))] + in_specs,
        out_specs=out_specs, out_shape=out_shape,
        compiler_params=_params(("parallel",)),
        name="norm_matmul",
    )(x, *args)


def _tail_kernel(*refs, tm, seq_len, t_valid, carry_mode, apply_final, proj_main, proj_bias, proj_extra):
    refs = list(refs)
    take = lambda cond=True: refs.pop(0) if cond else None
    x_ref, y_ref, wo_ref, g_ref, wu_ref, cw_ref, cb_ref, wd_ref = [take() for _ in range(8)]
    st_ref = take(not carry_mode)
    fg_ref = take(apply_final)
    has_proj = proj_main > 0
    gn_ref, wn_ref = take(has_proj), take(has_proj)
    bn_ref, wne_ref = take(proj_bias), take(proj_extra)
    o_ref, go_ref = take(), take()
    p_ref, pe_ref = take(has_proj), take(proj_extra)
    h_ref = take()
    carry_ref = take(carry_mode)

    d_ff = wd_ref.shape[0]
    if carry_mode:
        @pl.when((pl.program_id(0) * tm) % seq_len == 0)
        def _():
            carry_ref[...] = jnp.zeros_like(carry_ref)
        row = lax.broadcasted_iota(jnp.int32, (tm, FFN_CHUNK), 0)
    else:
        n_seq = tm // seq_len
        tpos = lax.broadcasted_iota(jnp.int32, (n_seq, seq_len, FFN_CHUNK), 1)

    x2_halves, xn_halves = [], []
    for r in range(2):
        rows = slice(r * tm // 2, (r + 1) * tm // 2)
        x2_r = x_ref[rows, :] + jnp.dot(y_ref[rows, :].astype(BF16), wo_ref[...], preferred_element_type=F32)
        ms = jnp.mean(x2_r * x2_r, axis=-1, keepdims=True)
        x2_halves.append(x2_r)
        xn_halves.append((x2_r * lax.rsqrt(ms + EPS) * g_ref[...]).astype(BF16))
    x2 = jnp.concatenate(x2_halves, axis=0)
    xn = jnp.concatenate(xn_halves, axis=0)

    def rows_dot(w, split):
        if split:
            return jnp.concatenate([jnp.dot(h, w, preferred_element_type=F32) for h in xn_halves], axis=0)
        return jnp.dot(xn, w, preferred_element_type=F32)

    for c in range(d_ff // FFN_CHUNK):
        cols = slice(c * FFN_CHUNK, (c + 1) * FFN_CHUNK)
        up_cols = slice(d_ff + c * FFN_CHUNK, d_ff + (c + 1) * FFN_CHUNK)
        gate = rows_dot(wu_ref[:, cols], c == 0)
        up = rows_dot(wu_ref[:, up_cols], c == 0)
        if carry_mode:
            prev = carry_ref[:, cols]
            p1, p2 = prev[SUBLANES - 1:SUBLANES], prev[SUBLANES - 2:SUBLANES - 1]
            g1 = jnp.where(row == 0, p1, pltpu.roll(gate, 1, axis=0))
            g2 = jnp.where(row == 0, p2, jnp.where(row == 1, p1, pltpu.roll(gate, 2, axis=0)))
            tail = gate[tm - SUBLANES:tm]
            carry_ref[:, cols] = tail
            go_ref[0, :, cols] = tail
        else:
            gate3 = gate.reshape(n_seq, seq_len, FFN_CHUNK)
            st_2, st_1 = st_ref[:, 0:1, cols], st_ref[:, 1:2, cols]
            g1 = jnp.where(tpos >= 1, pltpu.roll(gate3, 1, axis=1), st_1)
            g2 = jnp.where(tpos >= 2, pltpu.roll(gate3, 2, axis=1), jnp.where(tpos == 0, st_2, st_1))
            g1 = g1.reshape(tm, FFN_CHUNK)
            g2 = g2.reshape(tm, FFN_CHUNK)
            go_ref[:, :, cols] = gate3[:, t_valid - (CONV_W - 1):t_valid, :]
        cw = cw_ref[:, cols]
        conv = cb_ref[:, cols] + cw[0:1] * g2 + cw[1:2] * g1 + cw[2:3] * gate
        h_ref[:, cols] = (conv * _sigmoid(conv) * up).astype(BF16)

    out = x2 + jnp.dot(h_ref[...], wd_ref[...], preferred_element_type=F32)
    if apply_final:
        ms = jnp.mean(out * out, axis=-1, keepdims=True)
        out = out * lax.rsqrt(ms + EPS) * fg_ref[...]
    o_ref[...] = out

    if has_proj:
        _project(out, slice(None), gn_ref, wn_ref, bn_ref, wne_ref, p_ref, pe_ref, proj_main)


def _layer_tail(x, y, w_out, ffn, layer, seq_len, conv_state=None, t_valid=None, final_g=None, next_proj=None,
                tm=256):
    n, d = x.shape
    f = ffn['w_down'].shape[1]
    assert n % tm == 0 and f % FFN_CHUNK == 0
    carry_mode = conv_state is None
    apply_final = final_g is not None
    in_specs = [pl.BlockSpec((tm, d), lambda i: (i, 0)),
                pl.BlockSpec((tm, d), lambda i: (i, 0)),
                _resident(*w_out)]
    args = [x, y, w_out[0]]
    for name in ('g', 'w_up', 'conv_w', 'conv_b', 'w_down'):
        in_specs.append(_resident(ffn[name], layer))
        args.append(ffn[name])
    scratch = []
    if carry_mode:
        assert seq_len % tm == 0
        go_shape = jax.ShapeDtypeStruct((n // tm, SUBLANES, f), F32)
        go_spec = pl.BlockSpec((1, SUBLANES, f), lambda i: (i, 0, 0))
        scratch.append(pltpu.VMEM((SUBLANES, f), F32))
    else:
        assert seq_len == SUBLANES and tm % seq_len == 0 and t_valid >= CONV_W - 1
        state_blk = (tm // seq_len, CONV_W - 1, f)
        in_specs.append(pl.BlockSpec((None,) + state_blk, lambda i: (layer, i, 0, 0)))
        args.append(conv_state)
        go_shape = jax.ShapeDtypeStruct((n // seq_len, CONV_W - 1, f), F32)
        go_spec = pl.BlockSpec(state_blk, lambda i: (i, 0, 0))
    if apply_final:
        in_specs.append(_resident(final_g))
        args.append(final_g)
    out_specs = [pl.BlockSpec((tm, d), lambda i: (i, 0)), go_spec]
    out_shape = [jax.ShapeDtypeStruct((n, d), F32), go_shape]
    proj_main, proj_bias, proj_extra = 0, False, False
    if next_proj is not None:
        proj_main = next_proj['e_main']
        proj_bias = next_proj['bias'] is not None
        proj_extra = next_proj['w_extra'] is not None
        assert proj_main % PROJ_CHUNK == 0
        p_in_specs, p_args, p_out_specs, p_out_shape = _proj_operands(next_proj, n, tm)
        in_specs += p_in_specs
        args += p_args
        out_specs += p_out_specs
        out_shape += p_out_shape
    scratch = [pltpu.VMEM((tm, f), BF16)] + scratch
    return pl.pallas_call(
        functools.partial(_tail_kernel, tm=tm, seq_len=seq_len, t_valid=t_valid, carry_mode=carry_mode,
                          apply_final=apply_final,
                          proj_main=proj_main, proj_bias=proj_bias, proj_extra=proj_extra),
        grid=(n // tm,),
        in_specs=in_specs,
        out_specs=out_specs,
        out_shape=out_shape,
        scratch_shapes=scratch,
        compiler_params=_params(("arbitrary",)),
        name="layer_tail",
    )(*args)


def _mlstm_kernel(q_ref, k_ref, v_ref, o_ref, gi_ref, gf_ref, bi_ref, bf_ref, ng_ref,
                  c0_ref, n0_ref, m0_ref, y_ref, c1_ref, n1_ref, m1_ref, fs_ref, ms_ref,
                  *, units, gb, seg_len, t_valid):
    @pl.when(pl.program_id(1) == 0)
    def _():
        c1_ref[...] = c0_ref[...]
        n1_ref[...] = n0_ref[...]
        m1_ref[...] = m0_ref[...]

    for u in range(units):
        seqs = pl.ds(u * gb, gb)
        _mlstm_unit(q_ref.at[u], k_ref.at[u], v_ref.at[u], o_ref.at[u], gi_ref.at[u], gf_ref.at[u],
                    bi_ref, bf_ref, ng_ref, y_ref.at[u], c1_ref.at[seqs], n1_ref.at[u], m1_ref.at[u],
                    fs_ref.at[u], ms_ref.at[u], gb=gb, seg_len=seg_len, t_valid=t_valid)


def _mlstm_unit(q_ref, k_ref, v_ref, o_ref, gi_ref, gf_ref, bi_ref, bf_ref, ng_ref,
                y_ref, c1_ref, n1_ref, m1_ref, fs_ref, ms_ref, *, gb, seg_len, t_valid):
    shape = (ROWS, LANES)
    row = lax.broadcasted_iota(jnp.int32, shape, 0)
    lane = lax.broadcasted_iota(jnp.int32, shape, 1)
    tpos = row % seg_len
    low_half = lane < DKA

    ig = _softcap(gi_ref[...] + bi_ref[...])
    lf = _log_sigmoid(_softcap(gf_ref[...] + bf_ref[...]))
    if t_valid < seg_len:
        valid = tpos < t_valid
        ig = jnp.where(valid, ig, NEG)
        lf = jnp.where(valid, lf, 0.0)
    f_cum = _seg_scan(lf, seg_len, jnp.add, tpos)
    a = ig - f_cum
    cmax = _seg_scan(a, seg_len, jnp.maximum, tpos)
    m_prev_g = m1_ref[...]
    m_prev = _rep_rows(m_prev_g, seg_len)
    m_t = f_cum + jnp.maximum(m_prev, cmax)
    g = jnp.exp(jnp.minimum(f_cum + m_prev - m_t, 0.0))
    u = f_cum - m_t
    em = jnp.exp(-m_t)

    fs_ref[...] = f_cum
    ms_ref[...] = m_t
    f_last_g = fs_ref[pl.ds(seg_len - 1, gb, stride=seg_len), :]
    m_last_g = ms_ref[pl.ds(seg_len - 1, gb, stride=seg_len), :]
    f_last = _rep_rows(f_last_g, seg_len)
    m_last = _rep_rows(m_last_g, seg_len)
    w_last = jnp.exp(jnp.minimum(a + f_last - m_last, 0.0))
    g_last_g = jnp.exp(jnp.minimum(f_last_g + m_prev_g - m_last_g, 0.0))

    a_t = a.T
    u_w = _expand_heads(u, HA, DVA, 3)
    g_w = _expand_heads(g, HA, DVA, 2)
    em_w = _expand_heads(em, HA, DVA, 2)
    wl_w = _expand_heads(w_last, HA, DKA, 2)
    gl_rows = g_last_g if gb >= SUBLANES else jnp.broadcast_to(g_last_g, (SUBLANES, LANES))
    gl_w = _expand_heads(gl_rows, HA, DKA, 2)[:gb]

    col_i = lax.broadcasted_iota(jnp.int32, (ROWS, 2 * ROWS), 1) % ROWS
    row_i = lax.broadcasted_iota(jnp.int32, (ROWS, 2 * ROWS), 0)
    causal2 = (col_i <= row_i) & ((col_i // seg_len) == (row_i // seg_len))
    state_low = lax.broadcasted_iota(jnp.int32, (gb * LANES, LANES), 0) % LANES < DKA
    flat_low = lax.broadcasted_iota(jnp.int32, (LANES, gb * LANES), 1) % LANES < DKA
    sel = (lax.broadcasted_iota(jnp.int32, (max(gb, SUBLANES), ROWS), 1) // seg_len
           == lax.broadcasted_iota(jnp.int32, (max(gb, SUBLANES), ROWS), 0)).astype(BF16)
    ones_blk = (lax.broadcasted_iota(jnp.int32, (2 * ROWS, 2 * DVA), 0) // ROWS
                == lax.broadcasted_iota(jnp.int32, (2 * ROWS, 2 * DVA), 1) // DVA).astype(BF16)
    zeros_v = jnp.zeros((ROWS, DVA), v_ref.dtype)

    q_all = q_ref[...]
    k_all = k_ref[...] * (DKA ** -0.5)
    k2_all = k_all * wl_w

    for p in range(HA // 2):
        cols = slice(p * LANES, (p + 1) * LANES)
        wide = slice(2 * p * DVA, (2 * p + 2) * DVA)
        qp, kp, k2 = q_all[:, cols], k_all[:, cols], k2_all[:, cols]
        c_p = c1_ref[:, p].reshape(gb * LANES, DVA)
        n_p = n1_ref[p]
        v2 = v_ref[:, wide]
        v_a, v_b = v2[:, :DVA], v2[:, DVA:]

        k_sep = jnp.concatenate([jnp.where(low_half, kp, 0.0), jnp.where(low_half, 0.0, kp)], axis=0)
        s2 = _dot_nt(qp, k_sep)
        logw2 = jnp.concatenate([a_t[2 * p:2 * p + 1], a_t[2 * p + 1:2 * p + 2]], axis=1) + u_w[:, wide]
        qk2 = (s2 * jnp.where(causal2, jnp.exp(jnp.minimum(logw2, 0.0)), 0.0)).astype(BF16)
        v_bd = jnp.concatenate([jnp.concatenate([v_a, zeros_v], axis=1),
                                jnp.concatenate([zeros_v, v_b], axis=1)], axis=0)
        num2 = jnp.dot(qk2, v_bd, preferred_element_type=F32)
        dsum2 = jnp.dot(qk2, ones_blk, preferred_element_type=F32)

        qx = _expand_blockdiag(qp, gb, seg_len)
        c_sep = jnp.concatenate([jnp.where(state_low, c_p, 0.0), jnp.where(state_low, 0.0, c_p)], axis=1)
        inter2 = _dot(qx, c_sep)
        n_flat = n_p if gb == 1 else jnp.concatenate([n_p[b:b + 1] for b in range(gb)], axis=1)
        n_b = jnp.broadcast_to(n_flat, (LANES, gb * LANES))
        n_sep = jnp.concatenate([jnp.where(flat_low, n_b, 0.0), jnp.where(flat_low, 0.0, n_b)], axis=0)
        dint2 = _dot_nt(qx, n_sep)

        g2 = g_w[:, wide]
        hh2 = (num2 + g2 * inter2) / jnp.maximum(jnp.abs(dsum2 + g2 * dint2), em_w[:, wide])
        for half in range(2):
            hcols = slice((2 * p + half) * DVA, (2 * p + half + 1) * DVA)
            hh = hh2[:, half * DVA:(half + 1) * DVA]
            msq = jnp.mean(hh * hh, axis=-1, keepdims=True)
            hn = hh * lax.rsqrt(msq + EPS) * ng_ref[:, hcols]
            y_ref[:, hcols] = (hn * _sigmoid(o_ref[:, hcols].astype(F32))).astype(y_ref.dtype)

        upd2 = _dot_tn(_expand_blockdiag(k2, gb, seg_len), v2)
        upd = jnp.where(state_low, upd2[:, :DVA], upd2[:, DVA:])
        n_scale = gl_w[:, cols]
        scale = jnp.concatenate([jnp.broadcast_to(n_scale[b:b + 1], (LANES, LANES)).T for b in range(gb)],
                                axis=0)
        c1_ref[:, p] = (scale * c_p + upd).reshape(gb, LANES, DVA)
        n_upd = jnp.dot(sel, k2.astype(BF16), preferred_element_type=F32)[:gb]
        n1_ref[p] = n_scale * n_p + n_upd

    m1_ref[...] = m_last_g


def _mlstm(proj, gates, b_i, b_f, norm_g, c_all, layer, n0, m0, gb, seg_len, t_valid, n_chunks, units):
    n = proj.shape[0]
    n_blocks = n // (ROWS * n_chunks)
    assert n_blocks % units == 0
    bsz = n_blocks * gb
    half_pairs = HA // 2
    rows3 = lambda a: a.reshape(n_blocks, n_chunks * ROWS, a.shape[-1])
    proj3, gates3 = rows3(proj), rows3(gates)
    c0r = c_all.reshape(c_all.shape[0], bsz, half_pairs, LANES, DVA)
    n0r = n0.reshape(n_blocks, gb, half_pairs, LANES).transpose(0, 2, 1, 3)
    m0r = jnp.pad(m0, ((0, 0), (0, LANES - HA))).reshape(n_blocks, gb, LANES)
    pad = lambda b: jnp.pad(b, (0, LANES - HA)).reshape(1, LANES)
    blk = lambda width, cb: pl.BlockSpec((units, ROWS, width), lambda i, c: (i, c, cb))
    c_blk = (units * gb, half_pairs, LANES, DVA)
    y, c1, n1, m1 = pl.pallas_call(
        functools.partial(_mlstm_kernel, units=units, gb=gb, seg_len=seg_len, t_valid=t_valid),
        grid=(n_blocks // units, n_chunks),
        in_specs=[blk(HA * DKA, 0), blk(HA * DKA, 1), blk(HA * DVA, 1), blk(HA * DVA, 2),
                  blk(LANES, 0), blk(LANES, 1),
                  pl.BlockSpec((1, LANES), lambda i, c: (0, 0)),
                  pl.BlockSpec((1, LANES), lambda i, c: (0, 0)),
                  pl.BlockSpec((1, HA * DVA), lambda i, c: (0, 0)),
                  pl.BlockSpec((None,) + c_blk, lambda i, c: (layer, i, 0, 0, 0)),
                  pl.BlockSpec((units, half_pairs, gb, LANES), lambda i, c: (i, 0, 0, 0)),
                  pl.BlockSpec((units, gb, LANES), lambda i, c: (i, 0, 0))],
        out_specs=[blk(HA * DVA, 0),
                   pl.BlockSpec(c_blk, lambda i, c: (i, 0, 0, 0)),
                   pl.BlockSpec((units, half_pairs, gb, LANES), lambda i, c: (i, 0, 0, 0)),
                   pl.BlockSpec((units, gb, LANES), lambda i, c: (i, 0, 0))],
        out_shape=[jax.ShapeDtypeStruct((n_blocks, n_chunks * ROWS, HA * DVA), BF16),
                   jax.ShapeDtypeStruct(c0r.shape[1:], F32),
                   jax.ShapeDtypeStruct(n0r.shape, F32),
                   jax.ShapeDtypeStruct(m0r.shape, F32)],
        scratch_shapes=[pltpu.VMEM((units, ROWS, LANES), F32), pltpu.VMEM((units, ROWS, LANES), F32)],
        compiler_params=_params(("parallel", "arbitrary")),
        name="mlstm_mixer",
    )(proj3, proj3, proj3, proj3, gates3, gates3, pad(b_i), pad(b_f), norm_g.reshape(1, -1), c0r, n0r, m0r)
    y = y.reshape(n, HA * DVA)
    c1 = c1.reshape(bsz, HA, DKA, DVA)
    n1 = n1.transpose(0, 2, 1, 3).reshape(bsz, HA, DKA)
    m1 = m1.reshape(bsz, LANES)[:, :HA]
    return y, c1, n1, m1


def _gla_kernel(q_ref, k_ref, v_ref, r_ref, z_ref, wgu_ref, bg_ref, ng_ref, s0_ref,
                y_ref, s1_ref, bc_ref, b2_ref, kf_ref, *, units, gb, seg_len, t_valid, heads):
    @pl.when(pl.program_id(2) == 0)
    def _():
        s1_ref[...] = s0_ref[...]

    for u in range(units):
        _gla_unit(q_ref.at[u], k_ref.at[u], v_ref.at[u], r_ref.at[u], z_ref.at[u], wgu_ref, bg_ref, ng_ref,
                  y_ref.at[u], s1_ref.at[pl.ds(u * gb, gb)], bc_ref.at[u], b2_ref.at[u], kf_ref.at[u],
                  gb=gb, seg_len=seg_len, t_valid=t_valid, heads=heads)


def _gla_unit(q_ref, k_ref, v_ref, r_ref, z_ref, wgu_ref, bg_ref, ng_ref, y_ref, s1_ref, bc_ref, b2_ref, kf_ref,
              *, gb, seg_len, t_valid, heads):
    width = heads * DKB
    row = lax.broadcasted_iota(jnp.int32, (ROWS, width), 0)
    tpos = row % seg_len
    lg = _log_sigmoid(jnp.dot(z_ref[...].astype(BF16), wgu_ref[...], preferred_element_type=F32)
                      + bg_ref[...]) / GLA_TAU
    k_all = k_ref[...].astype(F32)
    if t_valid < seg_len:
        valid = tpos < t_valid
        lg = jnp.where(valid, lg, 0.0)
        k_all = jnp.where(valid, k_all, 0.0)
    bc = _seg_cumsum(lg, seg_len)

    n_sub = ROWS // GLA_SUB
    sub_row = lax.broadcasted_iota(jnp.int32, (GLA_SUB, ROWS), 0)
    sub_col = lax.broadcasted_iota(jnp.int32, (GLA_SUB, ROWS), 1)

    for h in range(heads):
        kcols = slice(h * DKB, (h + 1) * DKB)
        vcols = slice(h * DVB, (h + 1) * DVB)
        qh = q_ref[:, kcols].astype(F32) * (DKB ** -0.5)
        kh = k_all[:, kcols]
        vh = v_ref[:, vcols]
        bch = bc[:, kcols]
        bc_ref[...] = bch
        b2_ref[...] = bch * LOG2_E
        kf_ref[...] = kh
        b_last_g = bc_ref[pl.ds(seg_len - 1, gb, stride=seg_len), :]
        b_last = _rep_rows(b_last_g, seg_len)
        s_h = s1_ref[:, h].reshape(gb * DKB, DVB)

        inter = _dot(_expand_blockdiag(qh * jnp.exp(bch), gb, seg_len), s_h)

        att_rows = []
        k_decayed = []
        for blk in range(n_sub):
            r0 = blk * GLA_SUB
            q_b = qh[r0:r0 + GLA_SUB]
            bc_b = bch[r0:r0 + GLA_SUB]
            b2_b = b2_ref[r0:r0 + GLA_SUB, :]
            att = jnp.zeros((GLA_SUB, ROWS), F32)
            for s in range(GLA_SUB):
                k_row = jnp.broadcast_to(kf_ref[r0 + s:r0 + s + 1, :], (GLA_SUB, DKB))
                b2_row = jnp.broadcast_to(b2_ref[r0 + s:r0 + s + 1, :], (GLA_SUB, DKB))
                cs = jnp.sum(q_b * k_row * jnp.exp2(b2_b - b2_row), axis=-1, keepdims=True)
                att = jnp.where(sub_col == r0 + s, cs, att)
            att = jnp.where(sub_col <= sub_row + r0, att, 0.0)
            if seg_len > GLA_SUB and blk > 0:
                b_edge = bch[r0 - 1:r0]
                if blk > 1:
                    step = jnp.exp(b_edge - bch[r0 - GLA_SUB - 1:r0 - GLA_SUB])
                    k_decayed = [kb * step for kb in k_decayed]
                k_decayed.append(kh[r0 - GLA_SUB:r0] * jnp.exp(b_edge - bch[r0 - GLA_SUB:r0]))
                k_t = jnp.concatenate(k_decayed + [jnp.zeros((ROWS - r0, DKB), F32)], axis=0)
                att = att + _dot_nt(q_b * jnp.exp(bc_b - b_edge), k_t)
            att_rows.append(att)
        att_full = jnp.concatenate(att_rows, axis=0)

        o = inter + _dot(att_full, vh)
        msq = jnp.mean(o * o, axis=-1, keepdims=True)
        on = o * lax.rsqrt(msq + EPS) * ng_ref[:, vcols]
        r = r_ref[:, vcols].astype(F32)
        y_ref[:, vcols] = (on * (r * _sigmoid(r))).astype(y_ref.dtype)

        k_d = kh * jnp.exp(b_last - bch)
        upd = _dot_tn(_expand_blockdiag(k_d, gb, seg_len), vh)
        e_last = jnp.exp(b_last_g)
        e_cols = jnp.concatenate(
            [jnp.broadcast_to(e_last[b:b + 1], (DKB, DKB)).T for b in range(gb)], axis=0)
        e_cols = jnp.concatenate([e_cols] * (DVB // DKB), axis=1)
        s1_ref[:, h] = (e_cols * s_h + upd).reshape(gb, DKB, DVB)


def _gla(proj, z, w_gate_up, b_gate, norm_g, s0, gb, seg_len, t_valid, n_chunks, heads, units):
    n = proj.shape[0]
    n_blocks = n // (ROWS * n_chunks)
    assert n_blocks % units == 0 and seg_len % GLA_SUB == 0
    n_hb = HB // heads
    wq, wv = heads * DKB, heads * DVB
    q_off, k_off = 0, (HB * DKB) // wq
    v_off, r_off = (2 * HB * DKB) // wv, (2 * HB * DKB + HB * DVB) // wv
    rows3 = lambda a: a.reshape(n_blocks, n_chunks * ROWS, a.shape[-1])
    proj3, z3 = rows3(proj), rows3(z)
    blk = lambda width, off: pl.BlockSpec((units, ROWS, width), lambda i, hb, c: (i, c, off + hb))
    state_spec = pl.BlockSpec((units * gb, heads, DKB, DVB), lambda i, hb, c: (i, hb, 0, 0))
    wgu = jnp.zeros((LANES, HB * DKB), BF16).at[:GLA_RANK].set(w_gate_up.astype(BF16))
    y, s1 = pl.pallas_call(
        functools.partial(_gla_kernel, units=units, gb=gb, seg_len=seg_len, t_valid=t_valid, heads=heads),
        grid=(n_blocks // units, n_hb, n_chunks),
        in_specs=[blk(wq, q_off), blk(wq, k_off), blk(wv, v_off), blk(wv, r_off),
                  pl.BlockSpec((units, ROWS, LANES), lambda i, hb, c: (i, c, 0)),
                  pl.BlockSpec((LANES, wq), lambda i, hb, c: (0, hb)),
                  pl.BlockSpec((1, wq), lambda i, hb, c: (0, hb)),
                  pl.BlockSpec((1, wv), lambda i, hb, c: (0, hb)),
                  state_spec],
        out_specs=[blk(wv, 0), state_spec],
        out_shape=[jax.ShapeDtypeStruct((n_blocks, n_chunks * ROWS, HB * DVB), BF16),
                   jax.ShapeDtypeStruct(s0.shape, F32)],
        scratch_shapes=[pltpu.VMEM((units, ROWS, DKB), F32)] * 3,
        compiler_params=_params(("parallel", "parallel", "arbitrary")),
        name="gla_mixer",
    )(proj3, proj3, proj3, proj3, z3, wgu, b_gate.reshape(1, -1), norm_g.reshape(1, -1), s0)
    return y.reshape(n, HB * DVB), s1


def _swa_kernel(q_ref, kc_ref, vc_ref, kp_ref, vp_ref, sink_ref, o_ref, *, blocks, units, q_len, prev_from_grid):
    for b in range(blocks):
        _swa_block(q_ref.at[b], kc_ref.at[b], vc_ref.at[b], kp_ref.at[b], vp_ref.at[b], sink_ref, o_ref.at[b],
                   units=units, q_len=q_len, prev_from_grid=prev_from_grid)


def _swa_block(q_ref, kc_ref, vc_ref, kp_ref, vp_ref, sink_ref, o_ref, *, units, q_len, prev_from_grid):
    group = HC // HKV
    rows4 = group * ROWS
    row = lax.broadcasted_iota(jnp.int32, (rows4, ROWS), 0) % ROWS
    key = lax.broadcasted_iota(jnp.int32, (rows4, ROWS), 1)
    q_pos = row % q_len
    mask_p = key >= q_pos
    if prev_from_grid:
        mask_p = mask_p & (pl.program_id(1) > 0)
    mask_c = ((key // q_len) == (row // q_len)) & ((key % q_len) <= q_pos)
    bias_p = jnp.where(mask_p, 0.0, NEG)
    bias_c = jnp.where(mask_c, 0.0, NEG)
    low = lax.broadcasted_iota(jnp.int32, (ROWS, LANES), 1) < HDC
    ones = jnp.ones((WINDOW, LANES), BF16)
    sinks = sink_ref[...]
    scale = HDC ** -0.5

    def per_unit(lhs, rhs, contract_rhs_rows):
        mm = _dot if contract_rhs_rows else _dot_nt
        if units == 1:
            return mm(lhs, rhs)
        pieces = [[None] * units for _ in range(group)]
        for u in range(units):
            rows_u = jnp.concatenate([lhs[gi * ROWS + u * q_len:gi * ROWS + (u + 1) * q_len]
                                      for gi in range(group)], axis=0)
            res = mm(rows_u, rhs[u * WINDOW:(u + 1) * WINDOW])
            for gi in range(group):
                pieces[gi][u] = res[gi * q_len:(gi + 1) * q_len]
        return jnp.concatenate([pieces[gi][u] for gi in range(group) for u in range(units)], axis=0)

    out_tiles = [None] * (HC // 2)
    for kk in range(HKV):
        kv_cols = slice((kk // 2) * LANES, (kk // 2 + 1) * LANES)
        k_half = kk % 2
        k_sel = low if k_half == 0 else ~low
        q_parts, sink_parts = [], []
        for gi in range(group):
            hq = kk * group + gi
            tile = q_ref[:, (hq // 2) * LANES:(hq // 2 + 1) * LANES] * scale
            if hq % 2 != k_half:
                tile = pltpu.roll(tile, HDC, axis=1)
            q_parts.append(jnp.where(k_sel, tile, 0.0))
            sink_parts.append(jnp.broadcast_to(_col(sinks, hq), (ROWS, LANES)))
        q4 = jnp.concatenate(q_parts, axis=0)
        sink = jnp.concatenate(sink_parts, axis=0)
        s_p = per_unit(q4, kp_ref[:, kv_cols], False) + bias_p
        s_c = _dot_nt(q4, kc_ref[:, kv_cols]) + bias_c
        row_max = jnp.max(jnp.maximum(s_p, s_c), axis=-1, keepdims=True)
        mx = jnp.maximum(jnp.broadcast_to(row_max, (rows4, LANES)), sink)
        p_p = jnp.exp(s_p - mx).astype(BF16)
        p_c = jnp.exp(s_c - mx).astype(BF16)
        den = (jnp.dot(p_p, ones, preferred_element_type=F32) + jnp.dot(p_c, ones, preferred_element_type=F32)
               + jnp.exp(sink - mx))
        out4 = (per_unit(p_p, vp_ref[:, kv_cols], True) + _dot(p_c, vc_ref[:, kv_cols])) / den
        for gi in range(group):
            hq = kk * group + gi
            part = out4[gi * ROWS:(gi + 1) * ROWS]
            if hq % 2 != k_half:
                part = pltpu.roll(part, HDC, axis=1)
            prev = out_tiles[hq // 2]
            out_tiles[hq // 2] = part if prev is None else jnp.where(low if hq % 2 == 0 else ~low, part, prev)
    for t in range(HC // 2):
        o_ref[:, t * LANES:(t + 1) * LANES] = out_tiles[t].astype(o_ref.dtype)


def _swa(proj, sinks, units, q_len, n_chunks, blocks, prev_k=None, prev_v=None):
    n = proj.shape[0]
    n_chains = n // (ROWS * n_chunks)
    assert n_chains % blocks == 0
    kv_w = HKV * HDC
    k_blk, v_blk = (HC * HDC) // kv_w, (HC * HDC) // kv_w + 1
    prev_from_grid = prev_k is None
    proj3 = proj.reshape(n_chains, n_chunks * ROWS, proj.shape[-1])
    cur = lambda blk: pl.BlockSpec((blocks, ROWS, kv_w), lambda i, c: (i, c, blk))
    if prev_from_grid:
        prev = lambda blk: pl.BlockSpec((blocks, ROWS, kv_w), lambda i, c: (i, jnp.maximum(c - 1, 0), blk))
        prev_specs = [prev(k_blk), prev(v_blk)]
        prev_args = [proj3, proj3]
    else:
        prev_specs = [pl.BlockSpec((blocks, units * WINDOW, kv_w), lambda i, c: (i, 0, 0))] * 2
        prev_args = [prev_k.reshape(n_chains, units * WINDOW, kv_w), prev_v.reshape(n_chains, units * WINDOW, kv_w)]
    sink_row = jnp.pad(sinks.astype(F32), (0, LANES - HC)).reshape(1, LANES)
    y = pl.pallas_call(
        functools.partial(_swa_kernel, blocks=blocks, units=units, q_len=q_len, prev_from_grid=prev_from_grid),
        grid=(n_chains // blocks, n_chunks),
        in_specs=[pl.BlockSpec((blocks, ROWS, HC * HDC), lambda i, c: (i, c, 0)),
                  cur(k_blk), cur(v_blk)] + prev_specs +
                 [pl.BlockSpec((1, LANES), lambda i, c: (0, 0))],
        out_specs=pl.BlockSpec((blocks, ROWS, HC * HDC), lambda i, c: (i, c, 0)),
        out_shape=jax.ShapeDtypeStruct((n_chains, n_chunks * ROWS, HC * HDC), BF16),
        compiler_params=_params(("parallel", "arbitrary")),
        name="swa_mixer",
    )(proj3, proj3, proj3, *prev_args, sink_row)
    return y.reshape(n, HC * HDC)


def _proj_spec(w, i):
    kind, j = i % N_MIXERS, i // N_MIXERS
    spec = {'g': (w['norm_mix'], i), 'bias': None, 'w_extra': None, 'out_dtype': BF16}
    if kind == 0:
        spec.update(w=(w['a_w_in'], j), e_main=2 * HA * (DKA + DVA), w_extra=(w['a_w_gates'], j))
    elif kind == 1:
        spec.update(w=(w['b_w_in'], j), e_main=2 * HB * (DKB + DVB), w_extra=(w['b_w_z'], j))
    else:
        spec.update(w=(w['c_w_in'], j), e_main=(HC + 2 * HKV) * HDC, bias=(w['c_b_in'], j), out_dtype=F32)
    return spec


def _trunk(x3, st, w, is_prompt):
    bsz, t_in, _ = x3.shape
    if is_prompt:
        t, t_valid, gb, seg_len = t_in, ROWS, 1, ROWS
        units = math.gcd(MIXER_UNITS, bsz)
        x = x3.reshape(bsz * t, D_MODEL)
    else:
        t, t_valid, gb, seg_len = SAMPLE_T_PAD, t_in, ROWS // SAMPLE_T_PAD, SAMPLE_T_PAD
        units = 1
        x = jnp.pad(x3, ((0, 0), (0, t - t_in), (0, 0))).reshape(bsz * t, D_MODEL)
    n_chunks = (gb * t) // ROWS
    new = {'a_c': [], 'a_n': [], 'a_m': [], 'b_s': [], 'c_k': [], 'c_v': [], 'f': []}

    projected = _norm_matmul(x, _proj_spec(w, 0))
    for i in range(DEPTH):
        kind, j = i % N_MIXERS, i // N_MIXERS
        if kind == 0:
            proj, gates = projected
            y, c1, n1, m1 = _mlstm(proj, gates, w['a_b_i'][j], w['a_b_f'][j], w['a_norm'][j],
                                   st['a_c'], j, st['a_n'][j], st['a_m'][j], gb, seg_len, t_valid, n_chunks,
                                   units=units)
            new['a_c'].append(c1)
            new['a_n'].append(n1)
            new['a_m'].append(m1)
            w_out = (w['a_w_out'], j)
        elif kind == 1:
            proj, z = projected
            y, s1 = _gla(proj, z, w['b_w_gate_up'][j], w['b_b_gate'][j], w['b_norm'][j], st['b_s'][j],
                         gb, seg_len, t_valid, n_chunks, heads=HB,
                         units=units)
            new['b_s'].append(s1)
            w_out = (w['b_w_out'], j)
        else:
            proj = projected[0]
            kv_w = HKV * HDC
            end = t if is_prompt else t_valid
            keep = min(WINDOW, end)
            newest = proj.reshape(bsz, t, -1)[:, end - keep:end, HC * HDC:]
            k_new = newest[:, :, :kv_w].reshape(bsz, keep, HKV, HDC)
            v_new = newest[:, :, kv_w:].reshape(bsz, keep, HKV, HDC)
            if is_prompt:
                y = _swa(proj, w['c_sinks'][j], 1, ROWS, n_chunks, units)
                new['c_k'].append(k_new)
                new['c_v'].append(v_new)
            else:
                k_buf, v_buf = st['c_k'][j], st['c_v'][j]
                y = _swa(proj, w['c_sinks'][j], gb, seg_len, n_chunks, 1,
                         prev_k=k_buf.reshape(bsz * WINDOW, kv_w), prev_v=v_buf.reshape(bsz * WINDOW, kv_w))
                new['c_k'].append(jnp.concatenate([k_buf[:, t_valid:], k_new], axis=1))
                new['c_v'].append(jnp.concatenate([v_buf[:, t_valid:], v_new], axis=1))
            w_out = (w['c_w_out'], j)

        last = i == DEPTH - 1
        tail_args = (x, y, w_out, w['ffn'], i)
        tail_kw = dict(seq_len=t, final_g=w['norm_final'] if last else None,
                       next_proj=None if last else _proj_spec(w, i + 1),
                       tm=TAIL_TM_PROMPT if is_prompt else TAIL_TM_SAMPLE)
        if is_prompt:
            x, gate_tail, *projected = _layer_tail(*tail_args, **tail_kw)
            seq_tails = gate_tail.reshape(bsz, t // TAIL_TM_PROMPT, SUBLANES, D_FF)[:, -1]
            new['f'].append(seq_tails[:, SUBLANES - (CONV_W - 1):])
        else:
            x, new_state, *projected = _layer_tail(*tail_args, conv_state=st['f'], t_valid=t_valid, **tail_kw)
            new['f'].append(new_state)

    out = {name: jnp.stack(vals) for name, vals in new.items()}
    y = x.reshape(bsz, t, D_MODEL)[:, :t_in]
    return y, out


def _prepare_weights(norm_mix_g, norm_ffn_g, norm_final_g, a_w_in, a_b_i, a_b_f, a_norm_g, a_w_out, b_w_in,
                     b_w_gate_up, b_b_gate, b_norm_g, b_w_out, c_w_in, c_b_in, c_sinks, c_w_out, f_w_up,
                     f_conv_w, f_conv_b, f_w_down):
    n_a, n_b = a_w_in.shape[0], b_w_in.shape[0]
    e_a = 2 * HA * (DKA + DVA)
    e_b = 2 * HB * (DKB + DVB)
    a_w_gates = jnp.zeros((n_a, D_MODEL, 2 * LANES), BF16)
    a_w_gates = a_w_gates.at[:, :, :HA].set(a_w_in[:, :, e_a:e_a + HA].astype(BF16))
    a_w_gates = a_w_gates.at[:, :, LANES:LANES + HA].set(a_w_in[:, :, e_a + HA:].astype(BF16))
    b_w_z = jnp.zeros((n_b, D_MODEL, LANES), BF16).at[:, :, :GLA_RANK].set(b_w_in[:, :, e_b:].astype(BF16))
    ffn = {'g': norm_ffn_g[:, None, :], 'w_up': f_w_up.astype(BF16), 'conv_w': f_conv_w,
           'conv_b': f_conv_b[:, None, :], 'w_down': f_w_down.astype(BF16)}
    w = {'norm_mix': norm_mix_g[:, None, :], 'norm_final': norm_final_g[None, :], 'ffn': ffn,
         'a_w_in': a_w_in.astype(BF16), 'a_w_gates': a_w_gates, 'a_b_i': a_b_i, 'a_b_f': a_b_f,
         'a_norm': a_norm_g, 'a_w_out': a_w_out.astype(BF16),
         'b_w_in': b_w_in.astype(BF16), 'b_w_z': b_w_z, 'b_w_gate_up': b_w_gate_up, 'b_b_gate': b_b_gate,
         'b_norm': b_norm_g, 'b_w_out': b_w_out.astype(BF16),
         'c_w_in': c_w_in.astype(BF16), 'c_b_in': c_b_in[:, None, :], 'c_sinks': c_sinks,
         'c_w_out': c_w_out.astype(BF16)}
    return w


def kernel(x_prompt, x_sample, state_mlstm_c, state_mlstm_n, state_mlstm_m, state_gla, cache_swa_k, cache_swa_v, state_ffn_conv, norm_mix_g, norm_ffn_g, norm_final_g, a_w_in, a_b_i, a_b_f, a_norm_g, a_w_out, b_w_in, b_w_gate_up, b_b_gate, b_norm_g, b_w_out, c_w_in, c_b_in, c_sinks, c_w_out, f_w_up, f_conv_w, f_conv_b, f_w_down):
    w = _prepare_weights(norm_mix_g, norm_ffn_g, norm_final_g, a_w_in, a_b_i, a_b_f, a_norm_g, a_w_out, b_w_in,
                         b_w_gate_up, b_b_gate, b_norm_g, b_w_out, c_w_in, c_b_in, c_sinks, c_w_out, f_w_up,
                         f_conv_w, f_conv_b, f_w_down)
    n_a, n_b, n_c = a_w_in.shape[0], b_w_in.shape[0], c_w_in.shape[0]
    bp = x_prompt.shape[0]
    st_p = {'a_c': jnp.zeros((n_a, bp, HA, DKA, DVA), F32),
            'a_n': jnp.zeros((n_a, bp, HA, DKA), F32),
            'a_m': jnp.zeros((n_a, bp, HA), F32),
            'b_s': jnp.zeros((n_b, bp, HB, DKB, DVB), F32),
            'c_k': [None] * n_c, 'c_v': [None] * n_c, 'f': None}
    st_s = {'a_c': state_mlstm_c, 'a_n': state_mlstm_n, 'a_m': state_mlstm_m, 'b_s': state_gla,
            'c_k': cache_swa_k, 'c_v': cache_swa_v, 'f': state_ffn_conv}
    y_prompt, np_ = _trunk(x_prompt, st_p, w, True)
    y_sample, ns_ = _trunk(x_sample, st_s, w, False)
    return (y_prompt, y_sample,
            np_['a_c'], np_['a_n'], np_['a_m'], np_['b_s'], np_['c_k'], np_['c_v'], np_['f'],
            ns_['a_c'], ns_['a_n'], ns_['a_m'], ns_['b_s'], ns_['c_k'], ns_['c_v'], ns_['f'])
```

```python
import functools
import math

import jax
import jax.numpy as jnp
from jax import lax
from jax.experimental import pallas as pl
from jax.experimental.pallas import tpu as pltpu

F32 = jnp.float32
BF16 = jnp.bfloat16

D_MODEL = 1024
DEPTH = 4
N_MIXERS = 3
HA, DKA, DVA = 8, 64, 128
GATE_SOFTCAP = 15.0
HB, DKB, DVB = 4, 128, 256
GLA_RANK = 16
GLA_TAU = 16.0
HC, HKV, HDC = 16, 4, 64
WINDOW = 128
D_FF = 2816
CONV_W = 3
EPS = 1e-6
NEG = -1e30
LOG2_E = 1.4426950408889634

ROWS = 128
LANES = 128
SUBLANES = 8
SAMPLE_T_PAD = 8
GLA_SUB = 8
FFN_CHUNK = 256
PROJ_CHUNK = 512
MIXER_UNITS = 4
TAIL_TM_PROMPT = 512
TAIL_TM_SAMPLE = 256
VMEM_LIMIT = 56 * 1024 * 1024


def _params(sem):
    return pltpu.CompilerParams(dimension_semantics=sem, vmem_limit_bytes=VMEM_LIMIT)


def _dot(a, b):
    return jnp.dot(a.astype(BF16), b.astype(BF16), preferred_element_type=F32)


def _dot_nt(a, b):
    return lax.dot_general(a.astype(BF16), b.astype(BF16), (((1,), (1,)), ((), ())),
                           preferred_element_type=F32)


def _dot_tn(a, b):
    return lax.dot_general(a.astype(BF16), b.astype(BF16), (((0,), (0,)), ((), ())),
                           preferred_element_type=F32)


def _sigmoid(x):
    return 1.0 / (1.0 + jnp.exp(-x))


def _log_sigmoid(x):
    return jnp.minimum(x, 0.0) - jnp.log(1.0 + jnp.exp(-jnp.abs(x)))


def _softcap(z):
    return GATE_SOFTCAP * jnp.tanh(z / GATE_SOFTCAP)


def _col(x, h):
    lane = lax.broadcasted_iota(jnp.int32, x.shape, 1)
    return jnp.sum(jnp.where(lane == h, x, 0.0), axis=-1, keepdims=True)


def _expand_heads(x, n_heads, width, terms):
    lanes = x.shape[1]
    src = lax.broadcasted_iota(jnp.int32, (terms * lanes, n_heads * width), 0) % lanes
    dst = lax.broadcasted_iota(jnp.int32, (terms * lanes, n_heads * width), 1) // width
    pieces, rest = [], x
    for _ in range(terms):
        piece = rest.astype(BF16)
        pieces.append(piece)
        rest = rest - piece.astype(F32)
    return jnp.dot(jnp.concatenate(pieces, axis=1), (src == dst).astype(BF16), preferred_element_type=F32)


def _rep_rows(x, reps):
    g, c = x.shape
    if g == 1:
        return jnp.broadcast_to(x, (reps, c))
    return jnp.concatenate([jnp.broadcast_to(x[b:b + 1], (reps, c)) for b in range(g)], axis=0)


def _seg_scan(x, seg_len, op, tpos):
    s = 1
    while s < seg_len:
        shifted = pltpu.roll(x, s, axis=0)
        x = jnp.where(tpos >= s, op(x, shifted), x)
        s *= 2
    return x


def _seg_cumsum(x, seg_len):
    rows = x.shape[0]
    dst = lax.broadcasted_iota(jnp.int32, (rows, 3 * rows), 0)
    src = lax.broadcasted_iota(jnp.int32, (rows, 3 * rows), 1) % rows
    tri = ((src <= dst) & ((src // seg_len) == (dst // seg_len))).astype(BF16)
    pieces, rest = [], x
    for _ in range(3):
        piece = rest.astype(BF16)
        pieces.append(piece)
        rest = rest - piece.astype(F32)
    return jnp.dot(tri, jnp.concatenate(pieces, axis=0), preferred_element_type=F32)


def _expand_blockdiag(x, gb, seg_len):
    if gb == 1:
        return x
    seq = lax.broadcasted_iota(jnp.int32, x.shape, 0) // seg_len
    return jnp.concatenate([jnp.where(seq == b, x, 0.0) for b in range(gb)], axis=1)


def _resident(arr, layer=None):
    shape = arr.shape if layer is None else arr.shape[1:]
    block = shape if layer is None else (None,) + shape
    lead = () if layer is None else (layer,)
    return pl.BlockSpec(block, lambda *_: lead + (0,) * len(shape), pipeline_mode=pl.Buffered(1))


def _project(x, rows, gn_ref, wn_ref, bn_ref, wne_ref, p_ref, pe_ref, e_main):
    ms = jnp.mean(x * x, axis=-1, keepdims=True)
    xn = (x * lax.rsqrt(ms + EPS) * gn_ref[...]).astype(BF16)
    for c in range(e_main // PROJ_CHUNK):
        cols = slice(c * PROJ_CHUNK, (c + 1) * PROJ_CHUNK)
        acc = jnp.dot(xn, wn_ref[:, cols], preferred_element_type=F32)
        if bn_ref is not None:
            acc = acc + bn_ref[:, cols]
        p_ref[rows, cols] = acc.astype(p_ref.dtype)
    if wne_ref is not None:
        pe_ref[rows, :] = jnp.dot(xn, wne_ref[...], preferred_element_type=F32)


def _proj_operands(spec, n, tm):
    g, g_layer = spec['g']
    w, w_layer = spec['w']
    in_specs = [_resident(g, g_layer), _resident(w, w_layer)]
    args = [g, w]
    if spec['bias'] is not None:
        in_specs.append(_resident(*spec['bias']))
        args.append(spec['bias'][0])
    out_specs = [pl.BlockSpec((tm, spec['e_main']), lambda i: (i, 0))]
    out_shape = [jax.ShapeDtypeStruct((n, spec['e_main']), spec['out_dtype'])]
    if spec['w_extra'] is not None:
        in_specs.append(_resident(*spec['w_extra']))
        args.append(spec['w_extra'][0])
        ex = spec['w_extra'][0].shape[-1]
        out_specs.append(pl.BlockSpec((tm, ex), lambda i: (i, 0)))
        out_shape.append(jax.ShapeDtypeStruct((n, ex), F32))
    return in_specs, args, out_specs, out_shape


def _proj_kernel(*refs, e_main, has_bias, has_extra, tm):
    refs = list(refs)
    take = lambda cond=True: refs.pop(0) if cond else None
    x_ref, gn_ref, wn_ref = take(), take(), take()
    bn_ref, wne_ref = take(has_bias), take(has_extra)
    p_ref, pe_ref = take(), take(has_extra)
    for r in range(2):
        rows = slice(r * tm // 2, (r + 1) * tm // 2)
        _project(x_ref[rows, :], rows, gn_ref, wn_ref, bn_ref, wne_ref, p_ref, pe_ref, e_main)


def _norm_matmul(x, spec, tm=512):
    n, d = x.shape
    assert n % tm == 0 and spec['e_main'] % PROJ_CHUNK == 0
    in_specs, args, out_specs, out_shape = _proj_operands(spec, n, tm)
    return pl.pallas_call(
        functools.partial(_proj_kernel, e_main=spec['e_main'], has_bias=spec['bias'] is not None,
                          has_extra=spec['w_extra'] is not None, tm=tm),
        grid=(n // tm,),
        in_specs=[pl.BlockSpec((tm, d), lambda i: (i, 0))] + in_specs,
        out_specs=out_specs, out_shape=out_shape,
        compiler_params=_params(("parallel",)),
        name="norm_matmul",
    )(x, *args)


def _tail_kernel(*refs, tm, seq_len, t_valid, carry_mode, apply_final, proj_main, proj_bias, proj_extra):
    refs = list(refs)
    take = lambda cond=True: refs.pop(0) if cond else None
    x_ref, y_ref, wo_ref, g_ref, wu_ref, cw_ref, cb_ref, wd_ref = [take() for _ in range(8)]
    st_ref = take(not carry_mode)
    fg_ref = take(apply_final)
    has_proj = proj_main > 0
    gn_ref, wn_ref = take(has_proj), take(has_proj)
    bn_ref, wne_ref = take(proj_bias), take(proj_extra)
    o_ref, go_ref = take(), take()
    p_ref, pe_ref = take(has_proj), take(proj_extra)
    h_ref = take()
    carry_ref = take(carry_mode)

    d_ff = wd_ref.shape[0]
    if carry_mode:
        @pl.when((pl.program_id(0) * tm) % seq_len == 0)
        def _():
            carry_ref[...] = jnp.zeros_like(carry_ref)
        row = lax.broadcasted_iota(jnp.int32, (tm, FFN_CHUNK), 0)
    else:
        n_seq = tm // seq_len
        tpos = lax.broadcasted_iota(jnp.int32, (n_seq, seq_len, FFN_CHUNK), 1)

    x2_halves, xn_halves = [], []
    for r in range(2):
        rows = slice(r * tm // 2, (r + 1) * tm // 2)
        x2_r = x_ref[rows, :] + jnp.dot(y_ref[rows, :].astype(BF16), wo_ref[...], preferred_element_type=F32)
        ms = jnp.mean(x2_r * x2_r, axis=-1, keepdims=True)
        x2_halves.append(x2_r)
        xn_halves.append((x2_r * lax.rsqrt(ms + EPS) * g_ref[...]).astype(BF16))
    x2 = jnp.concatenate(x2_halves, axis=0)
    xn = jnp.concatenate(xn_halves, axis=0)

    def rows_dot(w, split):
        if split:
            return jnp.concatenate([jnp.dot(h, w, preferred_element_type=F32) for h in xn_halves], axis=0)
        return jnp.dot(xn, w, preferred_element_type=F32)

    for c in range(d_ff // FFN_CHUNK):
        cols = slice(c * FFN_CHUNK, (c + 1) * FFN_CHUNK)
        up_cols = slice(d_ff + c * FFN_CHUNK, d_ff + (c + 1) * FFN_CHUNK)
        gate = rows_dot(wu_ref[:, cols], c == 0)
        up = rows_dot(wu_ref[:, up_cols], c == 0)
        if carry_mode:
            prev = carry_ref[:, cols]
            p1, p2 = prev[SUBLANES - 1:SUBLANES], prev[SUBLANES - 2:SUBLANES - 1]
            g1 = jnp.where(row == 0, p1, pltpu.roll(gate, 1, axis=0))
            g2 = jnp.where(row == 0, p2, jnp.where(row == 1, p1, pltpu.roll(gate, 2, axis=0)))
            tail = gate[tm - SUBLANES:tm]
            carry_ref[:, cols] = tail
            go_ref[0, :, cols] = tail
        else:
            gate3 = gate.reshape(n_seq, seq_len, FFN_CHUNK)
            st_2, st_1 = st_ref[:, 0:1, cols], st_ref[:, 1:2, cols]
            g1 = jnp.where(tpos >= 1, pltpu.roll(gate3, 1, axis=1), st_1)
            g2 = jnp.where(tpos >= 2, pltpu.roll(gate3, 2, axis=1), jnp.where(tpos == 0, st_2, st_1))
            g1 = g1.reshape(tm, FFN_CHUNK)
            g2 = g2.reshape(tm, FFN_CHUNK)
            go_ref[:, :, cols] = gate3[:, t_valid - (CONV_W - 1):t_valid, :]
        cw = cw_ref[:, cols]
        conv = cb_ref[:, cols] + cw[0:1] * g2 + cw[1:2] * g1 + cw[2:3] * gate
        h_ref[:, cols] = (conv * _sigmoid(conv) * up).astype(BF16)

    out = x2 + jnp.dot(h_ref[...], wd_ref[...], preferred_element_type=F32)
    if apply_final:
        ms = jnp.mean(out * out, axis=-1, keepdims=True)
        out = out * lax.rsqrt(ms + EPS) * fg_ref[...]
    o_ref[...] = out

    if has_proj:
        _project(out, slice(None), gn_ref, wn_ref, bn_ref, wne_ref, p_ref, pe_ref, proj_main)


def _layer_tail(x, y, w_out, ffn, layer, seq_len, conv_state=None, t_valid=None, final_g=None, next_proj=None,
                tm=256):
    n, d = x.shape
    f = ffn['w_down'].shape[1]
    assert n % tm == 0 and f % FFN_CHUNK == 0
    carry_mode = conv_state is None
    apply_final = final_g is not None
    in_specs = [pl.BlockSpec((tm, d), lambda i: (i, 0)),
                pl.BlockSpec((tm, d), lambda i: (i, 0)),
                _resident(*w_out)]
    args = [x, y, w_out[0]]
    for name in ('g', 'w_up', 'conv_w', 'conv_b', 'w_down'):
        in_specs.append(_resident(ffn[name], layer))
        args.append(ffn[name])
    scratch = []
    if carry_mode:
        assert seq_len % tm == 0
        go_shape = jax.ShapeDtypeStruct((n // tm, SUBLANES, f), F32)
        go_spec = pl.BlockSpec((1, SUBLANES, f), lambda i: (i, 0, 0))
        scratch.append(pltpu.VMEM((SUBLANES, f), F32))
    else:
        assert seq_len == SUBLANES and tm % seq_len == 0 and t_valid >= CONV_W - 1
        state_blk = (tm // seq_len, CONV_W - 1, f)
        in_specs.append(pl.BlockSpec((None,) + state_blk, lambda i: (layer, i, 0, 0)))
        args.append(conv_state)
        go_shape = jax.ShapeDtypeStruct((n // seq_len, CONV_W - 1, f), F32)
        go_spec = pl.BlockSpec(state_blk, lambda i: (i, 0, 0))
    if apply_final:
        in_specs.append(_resident(final_g))
        args.append(final_g)
    out_specs = [pl.BlockSpec((tm, d), lambda i: (i, 0)), go_spec]
    out_shape = [jax.ShapeDtypeStruct((n, d), F32), go_shape]
    proj_main, proj_bias, proj_extra = 0, False, False
    if next_proj is not None:
        proj_main = next_proj['e_main']
        proj_bias = next_proj['bias'] is not None
        proj_extra = next_proj['w_extra'] is not None
        assert proj_main % PROJ_CHUNK == 0
        p_in_specs, p_args, p_out_specs, p_out_shape = _proj_operands(next_proj, n, tm)
        in_specs += p_in_specs
        args += p_args
        out_specs += p_out_specs
        out_shape += p_out_shape
    scratch = [pltpu.VMEM((tm, f), BF16)] + scratch
    return pl.pallas_call(
        functools.partial(_tail_kernel, tm=tm, seq_len=seq_len, t_valid=t_valid, carry_mode=carry_mode,
                          apply_final=apply_final,
                          proj_main=proj_main, proj_bias=proj_bias, proj_extra=proj_extra),
        grid=(n // tm,),
        in_specs=in_specs,
        out_specs=out_specs,
        out_shape=out_shape,
        scratch_shapes=scratch,
        compiler_params=_params(("arbitrary",)),
        name="layer_tail",
    )(*args)


def _mlstm_kernel(*refs, units, gb, seg_len, t_valid, n_prev):
    q_ref, k_ref, v_ref, o_ref, gi_ref, gf_ref, bi_ref, bf_ref, ng_ref, c0_ref, n0_ref, m0_ref = refs[:12]
    prev_refs = refs[12:12 + n_prev]
    y_ref, c_out_ref, n1_ref, m1_ref, fs_ref, ms_ref = refs[12 + n_prev:]
    c1_ref = c_out_ref.at[n_prev]

    @pl.when(pl.program_id(1) == 0)
    def _():
        for k in range(n_prev):
            c_out_ref[k] = prev_refs[k][...]
        c1_ref[...] = c0_ref[...]
        n1_ref[...] = n0_ref[...]
        m1_ref[...] = m0_ref[...]

    for u in range(units):
        seqs = pl.ds(u * gb, gb)
        _mlstm_unit(q_ref.at[u], k_ref.at[u], v_ref.at[u], o_ref.at[u], gi_ref.at[u], gf_ref.at[u],
                    bi_ref, bf_ref, ng_ref, y_ref.at[u], c1_ref.at[seqs], n1_ref.at[u], m1_ref.at[u],
                    fs_ref.at[u], ms_ref.at[u], gb=gb, seg_len=seg_len, t_valid=t_valid)


def _mlstm_unit(q_ref, k_ref, v_ref, o_ref, gi_ref, gf_ref, bi_ref, bf_ref, ng_ref,
                y_ref, c1_ref, n1_ref, m1_ref, fs_ref, ms_ref, *, gb, seg_len, t_valid):
    shape = (ROWS, LANES)
    row = lax.broadcasted_iota(jnp.int32, shape, 0)
    lane = lax.broadcasted_iota(jnp.int32, shape, 1)
    tpos = row % seg_len
    low_half = lane < DKA

    ig = _softcap(gi_ref[...] + bi_ref[...])
    lf = _log_sigmoid(_softcap(gf_ref[...] + bf_ref[...]))
    if t_valid < seg_len:
        valid = tpos < t_valid
        ig = jnp.where(valid, ig, NEG)
        lf = jnp.where(valid, lf, 0.0)
    f_cum = _seg_scan(lf, seg_len, jnp.add, tpos)
    a = ig - f_cum
    cmax = _seg_scan(a, seg_len, jnp.maximum, tpos)
    m_prev_g = m1_ref[...]
    m_prev = _rep_rows(m_prev_g, seg_len)
    m_t = f_cum + jnp.maximum(m_prev, cmax)
    g = jnp.exp(jnp.minimum(f_cum + m_prev - m_t, 0.0))
    u = f_cum - m_t
    em = jnp.exp(-m_t)

    fs_ref[...] = f_cum
    ms_ref[...] = m_t
    f_last_g = fs_ref[pl.ds(seg_len - 1, gb, stride=seg_len), :]
    m_last_g = ms_ref[pl.ds(seg_len - 1, gb, stride=seg_len), :]
    f_last = _rep_rows(f_last_g, seg_len)
    m_last = _rep_rows(m_last_g, seg_len)
    w_last = jnp.exp(jnp.minimum(a + f_last - m_last, 0.0))
    g_last_g = jnp.exp(jnp.minimum(f_last_g + m_prev_g - m_last_g, 0.0))

    a_t = a.T
    u_w = _expand_heads(u, HA, DVA, 3)
    g_w = _expand_heads(g, HA, DVA, 2)
    em_w = _expand_heads(em, HA, DVA, 2)
    wl_w = _expand_heads(w_last, HA, DKA, 2)
    gl_rows = g_last_g if gb >= SUBLANES else jnp.broadcast_to(g_last_g, (SUBLANES, LANES))
    gl_w = _expand_heads(gl_rows, HA, DKA, 2)[:gb]

    col_i = lax.broadcasted_iota(jnp.int32, (ROWS, 2 * ROWS), 1) % ROWS
    row_i = lax.broadcasted_iota(jnp.int32, (ROWS, 2 * ROWS), 0)
    causal2 = (col_i <= row_i) & ((col_i // seg_len) == (row_i // seg_len))
    state_low = lax.broadcasted_iota(jnp.int32, (gb * LANES, LANES), 0) % LANES < DKA
    flat_low = lax.broadcasted_iota(jnp.int32, (LANES, gb * LANES), 1) % LANES < DKA
    sel = (lax.broadcasted_iota(jnp.int32, (max(gb, SUBLANES), ROWS), 1) // seg_len
           == lax.broadcasted_iota(jnp.int32, (max(gb, SUBLANES), ROWS), 0)).astype(BF16)
    ones_blk = (lax.broadcasted_iota(jnp.int32, (2 * ROWS, 2 * DVA), 0) // ROWS
                == lax.broadcasted_iota(jnp.int32, (2 * ROWS, 2 * DVA), 1) // DVA).astype(BF16)
    zeros_v = jnp.zeros((ROWS, DVA), v_ref.dtype)

    q_all = q_ref[...]
    k_all = k_ref[...] * (DKA ** -0.5)
    k2_all = k_all * wl_w

    for p in range(HA // 2):
        cols = slice(p * LANES, (p + 1) * LANES)
        wide = slice(2 * p * DVA, (2 * p + 2) * DVA)
        qp, kp, k2 = q_all[:, cols], k_all[:, cols], k2_all[:, cols]
        c_p = c1_ref[:, p].reshape(gb * LANES, DVA)
        n_p = n1_ref[p]
        v2 = v_ref[:, wide]
        v_a, v_b = v2[:, :DVA], v2[:, DVA:]

        k_sep = jnp.concatenate([jnp.where(low_half, kp, 0.0), jnp.where(low_half, 0.0, kp)], axis=0)
        s2 = _dot_nt(qp, k_sep)
        logw2 = jnp.concatenate([a_t[2 * p:2 * p + 1], a_t[2 * p + 1:2 * p + 2]], axis=1) + u_w[:, wide]
        qk2 = (s2 * jnp.where(causal2, jnp.exp(jnp.minimum(logw2, 0.0)), 0.0)).astype(BF16)
        v_bd = jnp.concatenate([jnp.concatenate([v_a, zeros_v], axis=1),
                                jnp.concatenate([zeros_v, v_b], axis=1)], axis=0)
        num2 = jnp.dot(qk2, v_bd, preferred_element_type=F32)
        dsum2 = jnp.dot(qk2, ones_blk, preferred_element_type=F32)

        qx = _expand_blockdiag(qp, gb, seg_len)
        c_sep = jnp.concatenate([jnp.where(state_low, c_p, 0.0), jnp.where(state_low, 0.0, c_p)], axis=1)
        inter2 = _dot(qx, c_sep)
        n_flat = n_p if gb == 1 else jnp.concatenate([n_p[b:b + 1] for b in range(gb)], axis=1)
        n_b = jnp.broadcast_to(n_flat, (LANES, gb * LANES))
        n_sep = jnp.concatenate([jnp.where(flat_low, n_b, 0.0), jnp.where(flat_low, 0.0, n_b)], axis=0)
        dint2 = _dot_nt(qx, n_sep)

        g2 = g_w[:, wide]
        hh2 = (num2 + g2 * inter2) / jnp.maximum(jnp.abs(dsum2 + g2 * dint2), em_w[:, wide])
        for half in range(2):
            hcols = slice((2 * p + half) * DVA, (2 * p + half + 1) * DVA)
            hh = hh2[:, half * DVA:(half + 1) * DVA]
            msq = jnp.mean(hh * hh, axis=-1, keepdims=True)
            hn = hh * lax.rsqrt(msq + EPS) * ng_ref[:, hcols]
            y_ref[:, hcols] = (hn * _sigmoid(o_ref[:, hcols].astype(F32))).astype(y_ref.dtype)

        upd2 = _dot_tn(_expand_blockdiag(k2, gb, seg_len), v2)
        upd = jnp.where(state_low, upd2[:, :DVA], upd2[:, DVA:])
        n_scale = gl_w[:, cols]
        scale = jnp.concatenate([jnp.broadcast_to(n_scale[b:b + 1], (LANES, LANES)).T for b in range(gb)],
                                axis=0)
        c1_ref[:, p] = (scale * c_p + upd).reshape(gb, LANES, DVA)
        n_upd = jnp.dot(sel, k2.astype(BF16), preferred_element_type=F32)[:gb]
        n1_ref[p] = n_scale * n_p + n_upd

    m1_ref[...] = m_last_g


def _mlstm(proj, gates, b_i, b_f, norm_g, c_all, layer, c_prev, n0, m0, gb, seg_len, t_valid, n_chunks, units):
    n = proj.shape[0]
    n_blocks = n // (ROWS * n_chunks)
    assert n_blocks % units == 0
    bsz = n_blocks * gb
    half_pairs = HA // 2
    rows3 = lambda a: a.reshape(n_blocks, n_chunks * ROWS, a.shape[-1])
    proj3, gates3 = rows3(proj), rows3(gates)
    c0r = c_all.reshape(c_all.shape[0], bsz, half_pairs, LANES, DVA)
    n0r = n0.reshape(n_blocks, gb, half_pairs, LANES).transpose(0, 2, 1, 3)
    m0r = jnp.pad(m0, ((0, 0), (0, LANES - HA))).reshape(n_blocks, gb, LANES)
    pad = lambda b: jnp.pad(b, (0, LANES - HA)).reshape(1, LANES)
    blk = lambda width, cb: pl.BlockSpec((units, ROWS, width), lambda i, c: (i, c, cb))
    c_blk = (units * gb, half_pairs, LANES, DVA)
    n_prev = len(c_prev)
    c_prev = [c.reshape(bsz, half_pairs, LANES, DVA) for c in c_prev]
    y, c1, n1, m1 = pl.pallas_call(
        functools.partial(_mlstm_kernel, units=units, gb=gb, seg_len=seg_len, t_valid=t_valid, n_prev=n_prev),
        grid=(n_blocks // units, n_chunks),
        in_specs=[blk(HA * DKA, 0), blk(HA * DKA, 1), blk(HA * DVA, 1), blk(HA * DVA, 2),
                  blk(LANES, 0), blk(LANES, 1),
                  pl.BlockSpec((1, LANES), lambda i, c: (0, 0)),
                  pl.BlockSpec((1, LANES), lambda i, c: (0, 0)),
                  pl.BlockSpec((1, HA * DVA), lambda i, c: (0, 0)),
                  pl.BlockSpec((None,) + c_blk, lambda i, c: (layer, i, 0, 0, 0)),
                  pl.BlockSpec((units, half_pairs, gb, LANES), lambda i, c: (i, 0, 0, 0)),
                  pl.BlockSpec((units, gb, LANES), lambda i, c: (i, 0, 0))]
                 + [pl.BlockSpec(c_blk, lambda i, c: (i, 0, 0, 0))] * n_prev,
        out_specs=[blk(HA * DVA, 0),
                   pl.BlockSpec((n_prev + 1,) + c_blk, lambda i, c: (0, i, 0, 0, 0)),
                   pl.BlockSpec((units, half_pairs, gb, LANES), lambda i, c: (i, 0, 0, 0)),
                   pl.BlockSpec((units, gb, LANES), lambda i, c: (i, 0, 0))],
        out_shape=[jax.ShapeDtypeStruct((n_blocks, n_chunks * ROWS, HA * DVA), BF16),
                   jax.ShapeDtypeStruct((n_prev + 1,) + c0r.shape[1:], F32),
                   jax.ShapeDtypeStruct(n0r.shape, F32),
                   jax.ShapeDtypeStruct(m0r.shape, F32)],
        scratch_shapes=[pltpu.VMEM((units, ROWS, LANES), F32), pltpu.VMEM((units, ROWS, LANES), F32)],
        compiler_params=_params(("parallel", "arbitrary")),
        name="mlstm_mixer",
    )(proj3, proj3, proj3, proj3, gates3, gates3, pad(b_i), pad(b_f), norm_g.reshape(1, -1), c0r, n0r, m0r,
      *c_prev)
    y = y.reshape(n, HA * DVA)
    c1 = c1.reshape(n_prev + 1, bsz, HA, DKA, DVA)
    n1 = n1.transpose(0, 2, 1, 3).reshape(bsz, HA, DKA)
    m1 = m1.reshape(bsz, LANES)[:, :HA]
    return y, c1, n1, m1


def _gla_kernel(q_ref, k_ref, v_ref, r_ref, z_ref, wgu_ref, bg_ref, ng_ref, s0_ref,
                y_ref, s1_ref, bc_ref, b2_ref, kf_ref, *, units, gb, seg_len, t_valid, heads):
    @pl.when(pl.program_id(2) == 0)
    def _():
        s1_ref[...] = s0_ref[...]

    for u in range(units):
        _gla_unit(q_ref.at[u], k_ref.at[u], v_ref.at[u], r_ref.at[u], z_ref.at[u], wgu_ref, bg_ref, ng_ref,
                  y_ref.at[u], s1_ref.at[pl.ds(u * gb, gb)], bc_ref.at[u], b2_ref.at[u], kf_ref.at[u],
                  gb=gb, seg_len=seg_len, t_valid=t_valid, heads=heads)


def _gla_unit(q_ref, k_ref, v_ref, r_ref, z_ref, wgu_ref, bg_ref, ng_ref, y_ref, s1_ref, bc_ref, b2_ref, kf_ref,
              *, gb, seg_len, t_valid, heads):
    width = heads * DKB
    row = lax.broadcasted_iota(jnp.int32, (ROWS, width), 0)
    tpos = row % seg_len
    lg = _log_sigmoid(jnp.dot(z_ref[...].astype(BF16), wgu_ref[...], preferred_element_type=F32)
                      + bg_ref[...]) / GLA_TAU
    k_all = k_ref[...].astype(F32)
    if t_valid < seg_len:
        valid = tpos < t_valid
        lg = jnp.where(valid, lg, 0.0)
        k_all = jnp.where(valid, k_all, 0.0)
    bc = _seg_cumsum(lg, seg_len)

    n_sub = ROWS // GLA_SUB
    sub_row = lax.broadcasted_iota(jnp.int32, (GLA_SUB, ROWS), 0)
    sub_col = lax.broadcasted_iota(jnp.int32, (GLA_SUB, ROWS), 1)

    for h in range(heads):
        kcols = slice(h * DKB, (h + 1) * DKB)
        vcols = slice(h * DVB, (h + 1) * DVB)
        qh = q_ref[:, kcols].astype(F32) * (DKB ** -0.5)
        kh = k_all[:, kcols]
        vh = v_ref[:, vcols]
        bch = bc[:, kcols]
        bc_ref[...] = bch
        b2_ref[...] = bch * LOG2_E
        kf_ref[...] = kh
        b_last_g = bc_ref[pl.ds(seg_len - 1, gb, stride=seg_len), :]
        b_last = _rep_rows(b_last_g, seg_len)
        s_h = s1_ref[:, h].reshape(gb * DKB, DVB)

        inter = _dot(_expand_blockdiag(qh * jnp.exp(bch), gb, seg_len), s_h)

        att_rows = []
        k_decayed = []
        for blk in range(n_sub):
            r0 = blk * GLA_SUB
            q_b = qh[r0:r0 + GLA_SUB]
            bc_b = bch[r0:r0 + GLA_SUB]
            b2_b = b2_ref[r0:r0 + GLA_SUB, :]
            att = jnp.zeros((GLA_SUB, ROWS), F32)
            for s in range(GLA_SUB):
                k_row = jnp.broadcast_to(kf_ref[r0 + s:r0 + s + 1, :], (GLA_SUB, DKB))
                b2_row = jnp.broadcast_to(b2_ref[r0 + s:r0 + s + 1, :], (GLA_SUB, DKB))
                cs = jnp.sum(q_b * k_row * jnp.exp2(b2_b - b2_row), axis=-1, keepdims=True)
                att = jnp.where(sub_col == r0 + s, cs, att)
            att = jnp.where(sub_col <= sub_row + r0, att, 0.0)
            if seg_len > GLA_SUB and blk > 0:
                b_edge = bch[r0 - 1:r0]
                if blk > 1:
                    step = jnp.exp(b_edge - bch[r0 - GLA_SUB - 1:r0 - GLA_SUB])
                    k_decayed = [kb * step for kb in k_decayed]
                k_decayed.append(kh[r0 - GLA_SUB:r0] * jnp.exp(b_edge - bch[r0 - GLA_SUB:r0]))
                k_t = jnp.concatenate(k_decayed + [jnp.zeros((ROWS - r0, DKB), F32)], axis=0)
                att = att + _dot_nt(q_b * jnp.exp(bc_b - b_edge), k_t)
            att_rows.append(att)
        att_full = jnp.concatenate(att_rows, axis=0)

        o = inter + _dot(att_full, vh)
        msq = jnp.mean(o * o, axis=-1, keepdims=True)
        on = o * lax.rsqrt(msq + EPS) * ng_ref[:, vcols]
        r = r_ref[:, vcols].astype(F32)
        y_ref[:, vcols] = (on * (r * _sigmoid(r))).astype(y_ref.dtype)

        k_d = kh * jnp.exp(b_last - bch)
        upd = _dot_tn(_expand_blockdiag(k_d, gb, seg_len), vh)
        e_last = jnp.exp(b_last_g)
        e_cols = jnp.concatenate(
            [jnp.broadcast_to(e_last[b:b + 1], (DKB, DKB)).T for b in range(gb)], axis=0)
        e_cols = jnp.concatenate([e_cols] * (DVB // DKB), axis=1)
        s1_ref[:, h] = (e_cols * s_h + upd).reshape(gb, DKB, DVB)


def _gla(proj, z, w_gate_up, b_gate, norm_g, s0, gb, seg_len, t_valid, n_chunks, heads, units):
    n = proj.shape[0]
    n_blocks = n // (ROWS * n_chunks)
    assert n_blocks % units == 0 and seg_len % GLA_SUB == 0
    n_hb = HB // heads
    wq, wv = heads * DKB, heads * DVB
    q_off, k_off = 0, (HB * DKB) // wq
    v_off, r_off = (2 * HB * DKB) // wv, (2 * HB * DKB + HB * DVB) // wv
    rows3 = lambda a: a.reshape(n_blocks, n_chunks * ROWS, a.shape[-1])
    proj3, z3 = rows3(proj), rows3(z)
    blk = lambda width, off: pl.BlockSpec((units, ROWS, width), lambda i, hb, c: (i, c, off + hb))
    state_spec = pl.BlockSpec((units * gb, heads, DKB, DVB), lambda i, hb, c: (i, hb, 0, 0))
    wgu = jnp.zeros((LANES, HB * DKB), BF16).at[:GLA_RANK].set(w_gate_up.astype(BF16))
    y, s1 = pl.pallas_call(
        functools.partial(_gla_kernel, units=units, gb=gb, seg_len=seg_len, t_valid=t_valid, heads=heads),
        grid=(n_blocks // units, n_hb, n_chunks),
        in_specs=[blk(wq, q_off), blk(wq, k_off), blk(wv, v_off), blk(wv, r_off),
                  pl.BlockSpec((units, ROWS, LANES), lambda i, hb, c: (i, c, 0)),
                  pl.BlockSpec((LANES, wq), lambda i, hb, c: (0, hb)),
                  pl.BlockSpec((1, wq), lambda i, hb, c: (0, hb)),
                  pl.BlockSpec((1, wv), lambda i, hb, c: (0, hb)),
                  state_spec],
        out_specs=[blk(wv, 0), state_spec],
        out_shape=[jax.ShapeDtypeStruct((n_blocks, n_chunks * ROWS, HB * DVB), BF16),
                   jax.ShapeDtypeStruct(s0.shape, F32)],
        scratch_shapes=[pltpu.VMEM((units, ROWS, DKB), F32)] * 3,
        compiler_params=_params(("parallel", "parallel", "arbitrary")),
        name="gla_mixer",
    )(proj3, proj3, proj3, proj3, z3, wgu, b_gate.reshape(1, -1), norm_g.reshape(1, -1), s0)
    return y.reshape(n, HB * DVB), s1


def _swa_kernel(q_ref, kc_ref, vc_ref, kp_ref, vp_ref, sink_ref, o_ref, *, blocks, units, q_len, prev_from_grid):
    for b in range(blocks):
        _swa_block(q_ref.at[b], kc_ref.at[b], vc_ref.at[b], kp_ref.at[b], vp_ref.at[b], sink_ref, o_ref.at[b],
                   units=units, q_len=q_len, prev_from_grid=prev_from_grid)


def _swa_block(q_ref, kc_ref, vc_ref, kp_ref, vp_ref, sink_ref, o_ref, *, units, q_len, prev_from_grid):
    group = HC // HKV
    rows4 = group * ROWS
    row = lax.broadcasted_iota(jnp.int32, (rows4, ROWS), 0) % ROWS
    key = lax.broadcasted_iota(jnp.int32, (rows4, ROWS), 1)
    q_pos = row % q_len
    mask_p = key >= q_pos
    if prev_from_grid:
        mask_p = mask_p & (pl.program_id(1) > 0)
    mask_c = ((key // q_len) == (row // q_len)) & ((key % q_len) <= q_pos)
    bias_p = jnp.where(mask_p, 0.0, NEG)
    bias_c = jnp.where(mask_c, 0.0, NEG)
    low = lax.broadcasted_iota(jnp.int32, (ROWS, LANES), 1) < HDC
    ones = jnp.ones((WINDOW, LANES), BF16)
    sinks = sink_ref[...]
    scale = HDC ** -0.5

    def per_unit(lhs, rhs, contract_rhs_rows):
        mm = _dot if contract_rhs_rows else _dot_nt
        if units == 1:
            return mm(lhs, rhs)
        pieces = [[None] * units for _ in range(group)]
        for u in range(units):
            rows_u = jnp.concatenate([lhs[gi * ROWS + u * q_len:gi * ROWS + (u + 1) * q_len]
                                      for gi in range(group)], axis=0)
            res = mm(rows_u, rhs[u * WINDOW:(u + 1) * WINDOW])
            for gi in range(group):
                pieces[gi][u] = res[gi * q_len:(gi + 1) * q_len]
        return jnp.concatenate([pieces[gi][u] for gi in range(group) for u in range(units)], axis=0)

    out_tiles = [None] * (HC // 2)
    for kk in range(HKV):
        kv_cols = slice((kk // 2) * LANES, (kk // 2 + 1) * LANES)
        k_half = kk % 2
        k_sel = low if k_half == 0 else ~low
        q_parts, sink_parts = [], []
        for gi in range(group):
            hq = kk * group + gi
            tile = q_ref[:, (hq // 2) * LANES:(hq // 2 + 1) * LANES] * scale
            if hq % 2 != k_half:
                tile = pltpu.roll(tile, HDC, axis=1)
            q_parts.append(jnp.where(k_sel, tile, 0.0))
            sink_parts.append(jnp.broadcast_to(_col(sinks, hq), (ROWS, LANES)))
        q4 = jnp.concatenate(q_parts, axis=0)
        sink = jnp.concatenate(sink_parts, axis=0)
        s_p = per_unit(q4, kp_ref[:, kv_cols], False) + bias_p
        s_c = _dot_nt(q4, kc_ref[:, kv_cols]) + bias_c
        row_max = jnp.max(jnp.maximum(s_p, s_c), axis=-1, keepdims=True)
        mx = jnp.maximum(jnp.broadcast_to(row_max, (rows4, LANES)), sink)
        p_p = jnp.exp(s_p - mx).astype(BF16)
        p_c = jnp.exp(s_c - mx).astype(BF16)
        den = (jnp.dot(p_p, ones, preferred_element_type=F32) + jnp.dot(p_c, ones, preferred_element_type=F32)
               + jnp.exp(sink - mx))
        out4 = (per_unit(p_p, vp_ref[:, kv_cols], True) + _dot(p_c, vc_ref[:, kv_cols])) / den
        for gi in range(group):
            hq = kk * group + gi
            part = out4[gi * ROWS:(gi + 1) * ROWS]
            if hq % 2 != k_half:
                part = pltpu.roll(part, HDC, axis=1)
            prev = out_tiles[hq // 2]
            out_tiles[hq // 2] = part if prev is None else jnp.where(low if hq % 2 == 0 else ~low, part, prev)
    for t in range(HC // 2):
        o_ref[:, t * LANES:(t + 1) * LANES] = out_tiles[t].astype(o_ref.dtype)


def _swa(proj, sinks, units, q_len, n_chunks, blocks, prev_k=None, prev_v=None):
    n = proj.shape[0]
    n_chains = n // (ROWS * n_chunks)
    assert n_chains % blocks == 0
    kv_w = HKV * HDC
    k_blk, v_blk = (HC * HDC) // kv_w, (HC * HDC) // kv_w + 1
    prev_from_grid = prev_k is None
    proj3 = proj.reshape(n_chains, n_chunks * ROWS, proj.shape[-1])
    cur = lambda blk: pl.BlockSpec((blocks, ROWS, kv_w), lambda i, c: (i, c, blk))
    if prev_from_grid:
        prev = lambda blk: pl.BlockSpec((blocks, ROWS, kv_w), lambda i, c: (i, jnp.maximum(c - 1, 0), blk))
        prev_specs = [prev(k_blk), prev(v_blk)]
        prev_args = [proj3, proj3]
    else:
        prev_specs = [pl.BlockSpec((blocks, units * WINDOW, kv_w), lambda i, c: (i, 0, 0))] * 2
        prev_args = [prev_k.reshape(n_chains, units * WINDOW, kv_w), prev_v.reshape(n_chains, units * WINDOW, kv_w)]
    sink_row = jnp.pad(sinks.astype(F32), (0, LANES - HC)).reshape(1, LANES)
    y = pl.pallas_call(
        functools.partial(_swa_kernel, blocks=blocks, units=units, q_len=q_len, prev_from_grid=prev_from_grid),
        grid=(n_chains // blocks, n_chunks),
        in_specs=[pl.BlockSpec((blocks, ROWS, HC * HDC), lambda i, c: (i, c, 0)),
                  cur(k_blk), cur(v_blk)] + prev_specs +
                 [pl.BlockSpec((1, LANES), lambda i, c: (0, 0))],
        out_specs=pl.BlockSpec((blocks, ROWS, HC * HDC), lambda i, c: (i, c, 0)),
        out_shape=jax.ShapeDtypeStruct((n_chains, n_chunks * ROWS, HC * HDC), BF16),
        compiler_params=_params(("parallel", "arbitrary")),
        name="swa_mixer",
    )(proj3, proj3, proj3, *prev_args, sink_row)
    return y.reshape(n, HC * HDC)


def _proj_spec(w, i):
    kind, j = i % N_MIXERS, i // N_MIXERS
    spec = {'g': (w['norm_mix'], i), 'bias': None, 'w_extra': None, 'out_dtype': BF16}
    if kind == 0:
        spec.update(w=(w['a_w_in'], j), e_main=2 * HA * (DKA + DVA), w_extra=(w['a_w_gates'], j))
    elif kind == 1:
        spec.update(w=(w['b_w_in'], j), e_main=2 * HB * (DKB + DVB), w_extra=(w['b_w_z'], j))
    else:
        spec.update(w=(w['c_w_in'], j), e_main=(HC + 2 * HKV) * HDC, bias=(w['c_b_in'], j), out_dtype=F32)
    return spec


def _trunk(x3, st, w, is_prompt):
    bsz, t_in, _ = x3.shape
    if is_prompt:
        t, t_valid, gb, seg_len = t_in, ROWS, 1, ROWS
        units = math.gcd(MIXER_UNITS, bsz)
        x = x3.reshape(bsz * t, D_MODEL)
    else:
        t, t_valid, gb, seg_len = SAMPLE_T_PAD, t_in, ROWS // SAMPLE_T_PAD, SAMPLE_T_PAD
        units = 1
        x = jnp.pad(x3, ((0, 0), (0, t - t_in), (0, 0))).reshape(bsz * t, D_MODEL)
    n_chunks = (gb * t) // ROWS
    new = {'a_n': [], 'a_m': [], 'b_s': [], 'c_k': [], 'c_v': [], 'f': []}
    c_layers = []

    projected = _norm_matmul(x, _proj_spec(w, 0))
    for i in range(DEPTH):
        kind, j = i % N_MIXERS, i // N_MIXERS
        if kind == 0:
            proj, gates = projected
            last_a = j == st['a_c'].shape[0] - 1
            y, c1, n1, m1 = _mlstm(proj, gates, w['a_b_i'][j], w['a_b_f'][j], w['a_norm'][j],
                                   st['a_c'], j, c_layers if last_a else [], st['a_n'][j], st['a_m'][j],
                                   gb, seg_len, t_valid, n_chunks, units=units)
            if last_a:
                c_final = c1
            else:
                c_layers.append(c1[0])
            new['a_n'].append(n1)
            new['a_m'].append(m1)
            w_out = (w['a_w_out'], j)
        elif kind == 1:
            proj, z = projected
            y, s1 = _gla(proj, z, w['b_w_gate_up'][j], w['b_b_gate'][j], w['b_norm'][j], st['b_s'][j],
                         gb, seg_len, t_valid, n_chunks, heads=HB,
                         units=units)
            new['b_s'].append(s1)
            w_out = (w['b_w_out'], j)
        else:
            proj = projected[0]
            kv_w = HKV * HDC
            end = t if is_prompt else t_valid
            keep = min(WINDOW, end)
            newest = proj.reshape(bsz, t, -1)[:, end - keep:end, HC * HDC:]
            k_new = newest[:, :, :kv_w].reshape(bsz, keep, HKV, HDC)
            v_new = newest[:, :, kv_w:].reshape(bsz, keep, HKV, HDC)
            if is_prompt:
                y = _swa(proj, w['c_sinks'][j], 1, ROWS, n_chunks, units)
                new['c_k'].append(k_new)
                new['c_v'].append(v_new)
            else:
                k_buf, v_buf = st['c_k'][j], st['c_v'][j]
                y = _swa(proj, w['c_sinks'][j], gb, seg_len, n_chunks, 1,
                         prev_k=k_buf.reshape(bsz * WINDOW, kv_w), prev_v=v_buf.reshape(bsz * WINDOW, kv_w))
                new['c_k'].append(jnp.concatenate([k_buf[:, t_valid:], k_new], axis=1))
                new['c_v'].append(jnp.concatenate([v_buf[:, t_valid:], v_new], axis=1))
            w_out = (w['c_w_out'], j)

        last = i == DEPTH - 1
        tail_args = (x, y, w_out, w['ffn'], i)
        tail_kw = dict(seq_len=t, final_g=w['norm_final'] if last else None,
                       next_proj=None if last else _proj_spec(w, i + 1),
                       tm=TAIL_TM_PROMPT if is_prompt else TAIL_TM_SAMPLE)
        if is_prompt:
            x, gate_tail, *projected = _layer_tail(*tail_args, **tail_kw)
            seq_tails = gate_tail.reshape(bsz, t // TAIL_TM_PROMPT, SUBLANES, D_FF)[:, -1]
            new['f'].append(seq_tails[:, SUBLANES - (CONV_W - 1):])
        else:
            x, new_state, *projected = _layer_tail(*tail_args, conv_state=st['f'], t_valid=t_valid, **tail_kw)
            new['f'].append(new_state)

    out = {name: jnp.stack(vals) for name, vals in new.items()}
    out['a_c'] = c_final
    y = x.reshape(bsz, t, D_MODEL)[:, :t_in]
    return y, out


def _prepare_weights(norm_mix_g, norm_ffn_g, norm_final_g, a_w_in, a_b_i, a_b_f, a_norm_g, a_w_out, b_w_in,
                     b_w_gate_up, b_b_gate, b_norm_g, b_w_out, c_w_in, c_b_in, c_sinks, c_w_out, f_w_up,
                     f_conv_w, f_conv_b, f_w_down):
    n_a, n_b = a_w_in.shape[0], b_w_in.shape[0]
    e_a = 2 * HA * (DKA + DVA)
    e_b = 2 * HB * (DKB + DVB)
    a_w_gates = jnp.zeros((n_a, D_MODEL, 2 * LANES), BF16)
    a_w_gates = a_w_gates.at[:, :, :HA].set(a_w_in[:, :, e_a:e_a + HA].astype(BF16))
    a_w_gates = a_w_gates.at[:, :, LANES:LANES + HA].set(a_w_in[:, :, e_a + HA:].astype(BF16))
    b_w_z = jnp.zeros((n_b, D_MODEL, LANES), BF16).at[:, :, :GLA_RANK].set(b_w_in[:, :, e_b:].astype(BF16))
    ffn = {'g': norm_ffn_g[:, None, :], 'w_up': f_w_up.astype(BF16), 'conv_w': f_conv_w,
           'conv_b': f_conv_b[:, None, :], 'w_down': f_w_down.astype(BF16)}
    w = {'norm_mix': norm_mix_g[:, None, :], 'norm_final': norm_final_g[None, :], 'ffn': ffn,
         'a_w_in': a_w_in.astype(BF16), 'a_w_gates': a_w_gates, 'a_b_i': a_b_i, 'a_b_f': a_b_f,
         'a_norm': a_norm_g, 'a_w_out': a_w_out.astype(BF16),
         'b_w_in': b_w_in.astype(BF16), 'b_w_z': b_w_z, 'b_w_gate_up': b_w_gate_up, 'b_b_gate': b_b_gate,
         'b_norm': b_norm_g, 'b_w_out': b_w_out.astype(BF16),
         'c_w_in': c_w_in.astype(BF16), 'c_b_in': c_b_in[:, None, :], 'c_sinks': c_sinks,
         'c_w_out': c_w_out.astype(BF16)}
    return w


def kernel(x_prompt, x_sample, state_mlstm_c, state_mlstm_n, state_mlstm_m, state_gla, cache_swa_k, cache_swa_v, state_ffn_conv, norm_mix_g, norm_ffn_g, norm_final_g, a_w_in, a_b_i, a_b_f, a_norm_g, a_w_out, b_w_in, b_w_gate_up, b_b_gate, b_norm_g, b_w_out, c_w_in, c_b_in, c_sinks, c_w_out, f_w_up, f_conv_w, f_conv_b, f_w_down):
    w = _prepare_weights(norm_mix_g, norm_ffn_g, norm_final_g, a_w_in, a_b_i, a_b_f, a_norm_g, a_w_out, b_w_in,
                         b_w_gate_up, b_b_gate, b_norm_g, b_w_out, c_w_in, c_b_in, c_sinks, c_w_out, f_w_up,
                         f_conv_w, f_conv_b, f_w_down)
    n_a, n_b, n_c = a_w_in.shape[0], b_w_in.shape[0], c_w_in.shape[0]
    bp = x_prompt.shape[0]
    st_p = {'a_c': jnp.zeros((n_a, bp, HA, DKA, DVA), F32),
            'a_n': jnp.zeros((n_a, bp, HA, DKA), F32),
            'a_m': jnp.zeros((n_a, bp, HA), F32),
            'b_s': jnp.zeros((n_b, bp, HB, DKB, DVB), F32),
            'c_k': [None] * n_c, 'c_v': [None] * n_c, 'f': None}
    st_s = {'a_c': state_mlstm_c, 'a_n': state_mlstm_n, 'a_m': state_mlstm_m, 'b_s': state_gla,
            'c_k': cache_swa_k, 'c_v': cache_swa_v, 'f': state_ffn_conv}
    y_prompt, np_ = _trunk(x_prompt, st_p, w, True)
    y_sample, ns_ = _trunk(x_sample, st_s, w, False)
    return (y_prompt, y_sample,
            np_['a_c'], np_['a_n'], np_['a_m'], np_['b_s'], np_['c_k'], np_['c_v'], np_['f'],
            ns_['a_c'], ns_['a_n'], ns_['a_m'], ns_['b_s'], ns_['c_k'], ns_['c_v'], ns_['f'])
```

```python
import functools
import math

import jax
import jax.numpy as jnp
from jax import lax
from jax.experimental import pallas as pl
from jax.experimental.pallas import tpu as pltpu

F32 = jnp.float32
BF16 = jnp.bfloat16

D_MODEL = 1024
DEPTH = 4
N_MIXERS = 3
HA, DKA, DVA = 8, 64, 128
GATE_SOFTCAP = 15.0
HB, DKB, DVB = 4, 128, 256
GLA_RANK = 16
GLA_TAU = 16.0
HC, HKV, HDC = 16, 4, 64
WINDOW = 128
D_FF = 2816
CONV_W = 3
EPS = 1e-6
NEG = -1e30
LOG2_E = 1.4426950408889634

ROWS = 128
LANES = 128
SUBLANES = 8
SAMPLE_T_PAD = 8
GLA_SUB = 8
FFN_CHUNK = 256
PROJ_CHUNK = 512
MIXER_UNITS = 4
TAIL_TM_PROMPT = 512
TAIL_TM_SAMPLE = 256
VMEM_LIMIT = 56 * 1024 * 1024


def _params(sem):
    return pltpu.CompilerParams(dimension_semantics=sem, vmem_limit_bytes=VMEM_LIMIT)


def _dot(a, b):
    return jnp.dot(a.astype(BF16), b.astype(BF16), preferred_element_type=F32)


def _dot_nt(a, b):
    return lax.dot_general(a.astype(BF16), b.astype(BF16), (((1,), (1,)), ((), ())),
                           preferred_element_type=F32)


def _dot_tn(a, b):
    return lax.dot_general(a.astype(BF16), b.astype(BF16), (((0,), (0,)), ((), ())),
                           preferred_element_type=F32)


def _sigmoid(x):
    return 1.0 / (1.0 + jnp.exp(-x))


def _log_sigmoid(x):
    return jnp.minimum(x, 0.0) - jnp.log(1.0 + jnp.exp(-jnp.abs(x)))


def _softcap(z):
    return GATE_SOFTCAP * jnp.tanh(z / GATE_SOFTCAP)


def _col(x, h):
    lane = lax.broadcasted_iota(jnp.int32, x.shape, 1)
    return jnp.sum(jnp.where(lane == h, x, 0.0), axis=-1, keepdims=True)


def _expand_heads(x, n_heads, width, terms):
    lanes = x.shape[1]
    src = lax.broadcasted_iota(jnp.int32, (terms * lanes, n_heads * width), 0) % lanes
    dst = lax.broadcasted_iota(jnp.int32, (terms * lanes, n_heads * width), 1) // width
    pieces, rest = [], x
    for _ in range(terms):
        piece = rest.astype(BF16)
        pieces.append(piece)
        rest = rest - piece.astype(F32)
    return jnp.dot(jnp.concatenate(pieces, axis=1), (src == dst).astype(BF16), preferred_element_type=F32)


def _rep_rows(x, reps):
    g, c = x.shape
    if g == 1:
        return jnp.broadcast_to(x, (reps, c))
    return jnp.concatenate([jnp.broadcast_to(x[b:b + 1], (reps, c)) for b in range(g)], axis=0)


def _seg_scan(x, seg_len, op, tpos):
    s = 1
    while s < seg_len:
        shifted = pltpu.roll(x, s, axis=0)
        x = jnp.where(tpos >= s, op(x, shifted), x)
        s *= 2
    return x


def _seg_cumsum(x, seg_len):
    rows = x.shape[0]
    dst = lax.broadcasted_iota(jnp.int32, (rows, 3 * rows), 0)
    src = lax.broadcasted_iota(jnp.int32, (rows, 3 * rows), 1) % rows
    tri = ((src <= dst) & ((src // seg_len) == (dst // seg_len))).astype(BF16)
    pieces, rest = [], x
    for _ in range(3):
        piece = rest.astype(BF16)
        pieces.append(piece)
        rest = rest - piece.astype(F32)
    return jnp.dot(tri, jnp.concatenate(pieces, axis=0), preferred_element_type=F32)


def _expand_blockdiag(x, gb, seg_len):
    if gb == 1:
        return x
    seq = lax.broadcasted_iota(jnp.int32, x.shape, 0) // seg_len
    return jnp.concatenate([jnp.where(seq == b, x, 0.0) for b in range(gb)], axis=1)


def _resident(arr, layer=None):
    shape = arr.shape if layer is None else arr.shape[1:]
    block = shape if layer is None else (None,) + shape
    lead = () if layer is None else (layer,)
    return pl.BlockSpec(block, lambda *_: lead + (0,) * len(shape), pipeline_mode=pl.Buffered(1))


def _project(x, rows, gn_ref, wn_ref, bn_ref, wne_ref, p_ref, pe_ref, e_main):
    ms = jnp.mean(x * x, axis=-1, keepdims=True)
    xn = (x * lax.rsqrt(ms + EPS) * gn_ref[...]).astype(BF16)
    for c in range(e_main // PROJ_CHUNK):
        cols = slice(c * PROJ_CHUNK, (c + 1) * PROJ_CHUNK)
        acc = jnp.dot(xn, wn_ref[:, cols], preferred_element_type=F32)
        if bn_ref is not None:
            acc = acc + bn_ref[:, cols]
        p_ref[rows, cols] = acc.astype(p_ref.dtype)
    if wne_ref is not None:
        pe_ref[rows, :] = jnp.dot(xn, wne_ref[...], preferred_element_type=F32)


def _proj_operands(spec, n, tm):
    g, g_layer = spec['g']
    w, w_layer = spec['w']
    in_specs = [_resident(g, g_layer), _resident(w, w_layer)]
    args = [g, w]
    if spec['bias'] is not None:
        in_specs.append(_resident(*spec['bias']))
        args.append(spec['bias'][0])
    out_specs = [pl.BlockSpec((tm, spec['e_main']), lambda i: (i, 0))]
    out_shape = [jax.ShapeDtypeStruct((n, spec['e_main']), spec['out_dtype'])]
    if spec['w_extra'] is not None:
        in_specs.append(_resident(*spec['w_extra']))
        args.append(spec['w_extra'][0])
        ex = spec['w_extra'][0].shape[-1]
        out_specs.append(pl.BlockSpec((tm, ex), lambda i: (i, 0)))
        out_shape.append(jax.ShapeDtypeStruct((n, ex), F32))
    return in_specs, args, out_specs, out_shape


def _proj_kernel(*refs, e_main, has_bias, has_extra, tm):
    refs = list(refs)
    take = lambda cond=True: refs.pop(0) if cond else None
    x_ref, gn_ref, wn_ref = take(), take(), take()
    bn_ref, wne_ref = take(has_bias), take(has_extra)
    p_ref, pe_ref = take(), take(has_extra)
    for r in range(2):
        rows = slice(r * tm // 2, (r + 1) * tm // 2)
        _project(x_ref[rows, :], rows, gn_ref, wn_ref, bn_ref, wne_ref, p_ref, pe_ref, e_main)


def _norm_matmul(x, spec, tm=512):
    n, d = x.shape
    assert n % tm == 0 and spec['e_main'] % PROJ_CHUNK == 0
    in_specs, args, out_specs, out_shape = _proj_operands(spec, n, tm)
    return pl.pallas_call(
        functools.partial(_proj_kernel, e_main=spec['e_main'], has_bias=spec['bias'] is not None,
                          has_extra=spec['w_extra'] is not None, tm=tm),
        grid=(n // tm,),
        in_specs=[pl.BlockSpec((tm, d), lambda i: (i, 0))] + in_specs,
        out_specs=out_specs, out_shape=out_shape,
        compiler_params=_params(("parallel",)),
        name="norm_matmul",
    )(x, *args)


def _tail_kernel(*refs, tm, seq_len, t_valid, carry_mode, apply_final, proj_main, proj_bias, proj_extra):
    refs = list(refs)
    take = lambda cond=True: refs.pop(0) if cond else None
    x_ref, y_ref, wo_ref, g_ref, wu_ref, cw_ref, cb_ref, wd_ref = [take() for _ in range(8)]
    st_ref = take(not carry_mode)
    fg_ref = take(apply_final)
    has_proj = proj_main > 0
    gn_ref, wn_ref = take(has_proj), take(has_proj)
    bn_ref, wne_ref = take(proj_bias), take(proj_extra)
    o_ref, go_ref = take(), take()
    p_ref, pe_ref = take(has_proj), take(proj_extra)
    h_ref = take()
    carry_ref = take(carry_mode)

    d_ff = wd_ref.shape[0]
    if carry_mode:
        @pl.when((pl.program_id(0) * tm) % seq_len == 0)
        def _():
            carry_ref[...] = jnp.zeros_like(carry_ref)
        row = lax.broadcasted_iota(jnp.int32, (tm, FFN_CHUNK), 0)
    else:
        n_seq = tm // seq_len
        tpos = lax.broadcasted_iota(jnp.int32, (n_seq, seq_len, FFN_CHUNK), 1)

    x2_halves, xn_halves = [], []
    for r in range(2):
        rows = slice(r * tm // 2, (r + 1) * tm // 2)
        x2_r = x_ref[rows, :] + jnp.dot(y_ref[rows, :].astype(BF16), wo_ref[...], preferred_element_type=F32)
        ms = jnp.mean(x2_r * x2_r, axis=-1, keepdims=True)
        x2_halves.append(x2_r)
        xn_halves.append((x2_r * lax.rsqrt(ms + EPS) * g_ref[...]).astype(BF16))
    x2 = jnp.concatenate(x2_halves, axis=0)
    xn = jnp.concatenate(xn_halves, axis=0)

    def rows_dot(w, split):
        if split:
            return jnp.concatenate([jnp.dot(h, w, preferred_element_type=F32) for h in xn_halves], axis=0)
        return jnp.dot(xn, w, preferred_element_type=F32)

    for c in range(d_ff // FFN_CHUNK):
        cols = slice(c * FFN_CHUNK, (c + 1) * FFN_CHUNK)
        up_cols = slice(d_ff + c * FFN_CHUNK, d_ff + (c + 1) * FFN_CHUNK)
        gate = rows_dot(wu_ref[:, cols], c == 0)
        up = rows_dot(wu_ref[:, up_cols], c == 0)
        if carry_mode:
            prev = carry_ref[:, cols]
            p1, p2 = prev[SUBLANES - 1:SUBLANES], prev[SUBLANES - 2:SUBLANES - 1]
            g1 = jnp.where(row == 0, p1, pltpu.roll(gate, 1, axis=0))
            g2 = jnp.where(row == 0, p2, jnp.where(row == 1, p1, pltpu.roll(gate, 2, axis=0)))
            tail = gate[tm - SUBLANES:tm]
            carry_ref[:, cols] = tail
            go_ref[0, :, cols] = tail
        else:
            gate3 = gate.reshape(n_seq, seq_len, FFN_CHUNK)
            st_2, st_1 = st_ref[:, 0:1, cols], st_ref[:, 1:2, cols]
            g1 = jnp.where(tpos >= 1, pltpu.roll(gate3, 1, axis=1), st_1)
            g2 = jnp.where(tpos >= 2, pltpu.roll(gate3, 2, axis=1), jnp.where(tpos == 0, st_2, st_1))
            g1 = g1.reshape(tm, FFN_CHUNK)
            g2 = g2.reshape(tm, FFN_CHUNK)
            go_ref[:, :, cols] = gate3[:, t_valid - (CONV_W - 1):t_valid, :]
        cw = cw_ref[:, cols]
        conv = cb_ref[:, cols] + cw[0:1] * g2 + cw[1:2] * g1 + cw[2:3] * gate
        h_ref[:, cols] = (conv * _sigmoid(conv) * up).astype(BF16)

    out = x2 + jnp.dot(h_ref[...], wd_ref[...], preferred_element_type=F32)
    if apply_final:
        ms = jnp.mean(out * out, axis=-1, keepdims=True)
        out = out * lax.rsqrt(ms + EPS) * fg_ref[...]
    o_ref[...] = out

    if has_proj:
        _project(out, slice(None), gn_ref, wn_ref, bn_ref, wne_ref, p_ref, pe_ref, proj_main)


def _layer_tail(x, y, w_out, ffn, layer, seq_len, conv_state=None, t_valid=None, final_g=None, next_proj=None,
                tm=256):
    n, d = x.shape
    f = ffn['w_down'].shape[1]
    assert n % tm == 0 and f % FFN_CHUNK == 0
    carry_mode = conv_state is None
    apply_final = final_g is not None
    in_specs = [pl.BlockSpec((tm, d), lambda i: (i, 0)),
                pl.BlockSpec((tm, d), lambda i: (i, 0)),
                _resident(*w_out)]
    args = [x, y, w_out[0]]
    for name in ('g', 'w_up', 'conv_w', 'conv_b', 'w_down'):
        in_specs.append(_resident(ffn[name], layer))
        args.append(ffn[name])
    scratch = []
    if carry_mode:
        assert seq_len % tm == 0
        go_shape = jax.ShapeDtypeStruct((n // tm, SUBLANES, f), F32)
        go_spec = pl.BlockSpec((1, SUBLANES, f), lambda i: (i, 0, 0))
        scratch.append(pltpu.VMEM((SUBLANES, f), F32))
    else:
        assert seq_len == SUBLANES and tm % seq_len == 0 and t_valid >= CONV_W - 1
        state_blk = (tm // seq_len, CONV_W - 1, f)
        in_specs.append(pl.BlockSpec((None,) + state_blk, lambda i: (layer, i, 0, 0)))
        args.append(conv_state)
        go_shape = jax.ShapeDtypeStruct((n // seq_len, CONV_W - 1, f), F32)
        go_spec = pl.BlockSpec(state_blk, lambda i: (i, 0, 0))
    if apply_final:
        in_specs.append(_resident(final_g))
        args.append(final_g)
    out_specs = [pl.BlockSpec((tm, d), lambda i: (i, 0)), go_spec]
    out_shape = [jax.ShapeDtypeStruct((n, d), F32), go_shape]
    proj_main, proj_bias, proj_extra = 0, False, False
    if next_proj is not None:
        proj_main = next_proj['e_main']
        proj_bias = next_proj['bias'] is not None
        proj_extra = next_proj['w_extra'] is not None
        assert proj_main % PROJ_CHUNK == 0
        p_in_specs, p_args, p_out_specs, p_out_shape = _proj_operands(next_proj, n, tm)
        in_specs += p_in_specs
        args += p_args
        out_specs += p_out_specs
        out_shape += p_out_shape
    scratch = [pltpu.VMEM((tm, f), BF16)] + scratch
    return pl.pallas_call(
        functools.partial(_tail_kernel, tm=tm, seq_len=seq_len, t_valid=t_valid, carry_mode=carry_mode,
                          apply_final=apply_final,
                          proj_main=proj_main, proj_bias=proj_bias, proj_extra=proj_extra),
        grid=(n // tm,),
        in_specs=in_specs,
        out_specs=out_specs,
        out_shape=out_shape,
        scratch_shapes=scratch,
        compiler_params=_params(("arbitrary",)),
        name="layer_tail",
    )(*args)


def _mlstm_kernel(*refs, units, gb, seg_len, t_valid, n_prev):
    q_ref, k_ref, v_ref, o_ref, gi_ref, gf_ref, bi_ref, bf_ref, ng_ref, c0_ref, n0_ref, m0_ref = refs[:12]
    prev_refs = refs[12:12 + n_prev]
    y_ref, c_out_ref, n1_ref, m1_ref, fs_ref, ms_ref = refs[12 + n_prev:]
    c1_ref = c_out_ref.at[n_prev]

    @pl.when(pl.program_id(1) == 0)
    def _():
        for k in range(n_prev):
            c_out_ref[k] = prev_refs[k][...]
        c1_ref[...] = c0_ref[...]
        n1_ref[...] = n0_ref[...]
        m1_ref[...] = m0_ref[...]

    for u in range(units):
        seqs = pl.ds(u * gb, gb)
        _mlstm_unit(q_ref.at[u], k_ref.at[u], v_ref.at[u], o_ref.at[u], gi_ref.at[u], gf_ref.at[u],
                    bi_ref, bf_ref, ng_ref, y_ref.at[u], c1_ref.at[seqs], n1_ref.at[u], m1_ref.at[u],
                    fs_ref.at[u], ms_ref.at[u], gb=gb, seg_len=seg_len, t_valid=t_valid)


def _mlstm_unit(q_ref, k_ref, v_ref, o_ref, gi_ref, gf_ref, bi_ref, bf_ref, ng_ref,
                y_ref, c1_ref, n1_ref, m1_ref, fs_ref, ms_ref, *, gb, seg_len, t_valid):
    shape = (ROWS, LANES)
    row = lax.broadcasted_iota(jnp.int32, shape, 0)
    lane = lax.broadcasted_iota(jnp.int32, shape, 1)
    tpos = row % seg_len
    low_half = lane < DKA

    ig = _softcap(gi_ref[...] + bi_ref[...])
    lf = _log_sigmoid(_softcap(gf_ref[...] + bf_ref[...]))
    if t_valid < seg_len:
        valid = tpos < t_valid
        ig = jnp.where(valid, ig, NEG)
        lf = jnp.where(valid, lf, 0.0)
    f_cum = _seg_scan(lf, seg_len, jnp.add, tpos)
    a = ig - f_cum
    cmax = _seg_scan(a, seg_len, jnp.maximum, tpos)
    m_prev_g = m1_ref[...]
    m_prev = _rep_rows(m_prev_g, seg_len)
    m_t = f_cum + jnp.maximum(m_prev, cmax)
    g = jnp.exp(jnp.minimum(f_cum + m_prev - m_t, 0.0))
    u = f_cum - m_t
    em = jnp.exp(-m_t)

    fs_ref[...] = f_cum
    ms_ref[...] = m_t
    f_last_g = fs_ref[pl.ds(seg_len - 1, gb, stride=seg_len), :]
    m_last_g = ms_ref[pl.ds(seg_len - 1, gb, stride=seg_len), :]
    f_last = _rep_rows(f_last_g, seg_len)
    m_last = _rep_rows(m_last_g, seg_len)
    w_last = jnp.exp(jnp.minimum(a + f_last - m_last, 0.0))
    g_last_g = jnp.exp(jnp.minimum(f_last_g + m_prev_g - m_last_g, 0.0))

    a_t = a.T
    u_w = _expand_heads(u, HA, DVA, 3)
    g_w = _expand_heads(g, HA, DVA, 2)
    em_w = _expand_heads(em, HA, DVA, 2)
    wl_w = _expand_heads(w_last, HA, DKA, 2)
    gl_rows = g_last_g if gb >= SUBLANES else jnp.broadcast_to(g_last_g, (SUBLANES, LANES))
    gl_w = _expand_heads(gl_rows, HA, DKA, 2)[:gb]

    col_i = lax.broadcasted_iota(jnp.int32, (ROWS, 2 * ROWS), 1) % ROWS
    row_i = lax.broadcasted_iota(jnp.int32, (ROWS, 2 * ROWS), 0)
    causal2 = (col_i <= row_i) & ((col_i // seg_len) == (row_i // seg_len))
    state_low = lax.broadcasted_iota(jnp.int32, (gb * LANES, LANES), 0) % LANES < DKA
    flat_low = lax.broadcasted_iota(jnp.int32, (LANES, gb * LANES), 1) % LANES < DKA
    sel = (lax.broadcasted_iota(jnp.int32, (max(gb, SUBLANES), ROWS), 1) // seg_len
           == lax.broadcasted_iota(jnp.int32, (max(gb, SUBLANES), ROWS), 0)).astype(BF16)
    ones_blk = (lax.broadcasted_iota(jnp.int32, (2 * ROWS, 2 * DVA), 0) // ROWS
                == lax.broadcasted_iota(jnp.int32, (2 * ROWS, 2 * DVA), 1) // DVA).astype(BF16)
    zeros_v = jnp.zeros((ROWS, DVA), v_ref.dtype)

    q_all = q_ref[...]
    k_all = k_ref[...] * (DKA ** -0.5)
    k2_all = k_all * wl_w

    for p in range(HA // 2):
        cols = slice(p * LANES, (p + 1) * LANES)
        wide = slice(2 * p * DVA, (2 * p + 2) * DVA)
        qp, kp, k2 = q_all[:, cols], k_all[:, cols], k2_all[:, cols]
        c_p = c1_ref[:, p].reshape(gb * LANES, DVA)
        n_p = n1_ref[p]
        v2 = v_ref[:, wide]
        v_a, v_b = v2[:, :DVA], v2[:, DVA:]

        k_sep = jnp.concatenate([jnp.where(low_half, kp, 0.0), jnp.where(low_half, 0.0, kp)], axis=0)
        s2 = _dot_nt(qp, k_sep)
        logw2 = jnp.concatenate([a_t[2 * p:2 * p + 1], a_t[2 * p + 1:2 * p + 2]], axis=1) + u_w[:, wide]
        qk2 = (s2 * jnp.where(causal2, jnp.exp(jnp.minimum(logw2, 0.0)), 0.0)).astype(BF16)
        v_bd = jnp.concatenate([jnp.concatenate([v_a, zeros_v], axis=1),
                                jnp.concatenate([zeros_v, v_b], axis=1)], axis=0)
        num2 = jnp.dot(qk2, v_bd, preferred_element_type=F32)
        dsum2 = jnp.dot(qk2, ones_blk, preferred_element_type=F32)

        qx = _expand_blockdiag(qp, gb, seg_len)
        c_sep = jnp.concatenate([jnp.where(state_low, c_p, 0.0), jnp.where(state_low, 0.0, c_p)], axis=1)
        inter2 = _dot(qx, c_sep)
        n_flat = n_p if gb == 1 else jnp.concatenate([n_p[b:b + 1] for b in range(gb)], axis=1)
        n_b = jnp.broadcast_to(n_flat, (LANES, gb * LANES))
        n_sep = jnp.concatenate([jnp.where(flat_low, n_b, 0.0), jnp.where(flat_low, 0.0, n_b)], axis=0)
        dint2 = _dot_nt(qx, n_sep)

        g2 = g_w[:, wide]
        hh2 = (num2 + g2 * inter2) / jnp.maximum(jnp.abs(dsum2 + g2 * dint2), em_w[:, wide])
        for half in range(2):
            hcols = slice((2 * p + half) * DVA, (2 * p + half + 1) * DVA)
            hh = hh2[:, half * DVA:(half + 1) * DVA]
            msq = jnp.mean(hh * hh, axis=-1, keepdims=True)
            hn = hh * lax.rsqrt(msq + EPS) * ng_ref[:, hcols]
            y_ref[:, hcols] = (hn * _sigmoid(o_ref[:, hcols].astype(F32))).astype(y_ref.dtype)

        upd2 = _dot_tn(_expand_blockdiag(k2, gb, seg_len), v2)
        upd = jnp.where(state_low, upd2[:, :DVA], upd2[:, DVA:])
        n_scale = gl_w[:, cols]
        scale = jnp.concatenate([jnp.broadcast_to(n_scale[b:b + 1], (LANES, LANES)).T for b in range(gb)],
                                axis=0)
        c1_ref[:, p] = (scale * c_p + upd).reshape(gb, LANES, DVA)
        n_upd = jnp.dot(sel, k2.astype(BF16), preferred_element_type=F32)[:gb]
        n1_ref[p] = n_scale * n_p + n_upd

    m1_ref[...] = m_last_g


def _mlstm(proj, gates, b_i, b_f, norm_g, c_all, layer, c_prev, n0, m0, gb, seg_len, t_valid, n_chunks, units):
    n = proj.shape[0]
    n_blocks = n // (ROWS * n_chunks)
    assert n_blocks % units == 0
    bsz = n_blocks * gb
    half_pairs = HA // 2
    rows3 = lambda a: a.reshape(n_blocks, n_chunks * ROWS, a.shape[-1])
    proj3, gates3 = rows3(proj), rows3(gates)
    c0r = c_all.reshape(c_all.shape[0], bsz, half_pairs, LANES, DVA)
    n0r = n0.reshape(n_blocks, gb, half_pairs, LANES).transpose(0, 2, 1, 3)
    m0r = jnp.pad(m0, ((0, 0), (0, LANES - HA))).reshape(n_blocks, gb, LANES)
    pad = lambda b: jnp.pad(b, (0, LANES - HA)).reshape(1, LANES)
    blk = lambda width, cb: pl.BlockSpec((units, ROWS, width), lambda i, c: (i, c, cb))
    c_blk = (units * gb, half_pairs, LANES, DVA)
    n_prev = len(c_prev)
    c_prev = [c.reshape(bsz, half_pairs, LANES, DVA) for c in c_prev]
    y, c1, n1, m1 = pl.pallas_call(
        functools.partial(_mlstm_kernel, units=units, gb=gb, seg_len=seg_len, t_valid=t_valid, n_prev=n_prev),
        grid=(n_blocks // units, n_chunks),
        in_specs=[blk(HA * DKA, 0), blk(HA * DKA, 1), blk(HA * DVA, 1), blk(HA * DVA, 2),
                  blk(LANES, 0), blk(LANES, 1),
                  pl.BlockSpec((1, LANES), lambda i, c: (0, 0)),
                  pl.BlockSpec((1, LANES), lambda i, c: (0, 0)),
                  pl.BlockSpec((1, HA * DVA), lambda i, c: (0, 0)),
                  pl.BlockSpec((None,) + c_blk, lambda i, c: (layer, i, 0, 0, 0)),
                  pl.BlockSpec((units, half_pairs, gb, LANES), lambda i, c: (i, 0, 0, 0)),
                  pl.BlockSpec((units, gb, LANES), lambda i, c: (i, 0, 0))]
                 + [pl.BlockSpec(c_blk, lambda i, c: (i, 0, 0, 0))] * n_prev,
        out_specs=[blk(HA * DVA, 0),
                   pl.BlockSpec((n_prev + 1,) + c_blk, lambda i, c: (0, i, 0, 0, 0)),
                   pl.BlockSpec((units, half_pairs, gb, LANES), lambda i, c: (i, 0, 0, 0)),
                   pl.BlockSpec((units, gb, LANES), lambda i, c: (i, 0, 0))],
        out_shape=[jax.ShapeDtypeStruct((n_blocks, n_chunks * ROWS, HA * DVA), BF16),
                   jax.ShapeDtypeStruct((n_prev + 1,) + c0r.shape[1:], F32),
                   jax.ShapeDtypeStruct(n0r.shape, F32),
                   jax.ShapeDtypeStruct(m0r.shape, F32)],
        scratch_shapes=[pltpu.VMEM((units, ROWS, LANES), F32), pltpu.VMEM((units, ROWS, LANES), F32)],
        compiler_params=_params(("parallel", "arbitrary")),
        name="mlstm_mixer",
    )(proj3, proj3, proj3, proj3, gates3, gates3, pad(b_i), pad(b_f), norm_g.reshape(1, -1), c0r, n0r, m0r,
      *c_prev)
    y = y.reshape(n, HA * DVA)
    c1 = c1.reshape(n_prev + 1, bsz, HA, DKA, DVA)
    n1 = n1.transpose(0, 2, 1, 3).reshape(bsz, HA, DKA)
    m1 = m1.reshape(bsz, LANES)[:, :HA]
    return y, c1, n1, m1


def _gla_kernel(q_ref, k_ref, v_ref, r_ref, z_ref, wgu_ref, bg_ref, ng_ref, s0_ref,
                y_ref, s1_ref, bc_ref, b2_ref, kf_ref, *, units, gb, seg_len, t_valid, heads):
    @pl.when(pl.program_id(2) == 0)
    def _():
        s1_ref[...] = s0_ref[...]

    for u in range(units):
        _gla_unit(q_ref.at[u], k_ref.at[u], v_ref.at[u], r_ref.at[u], z_ref.at[u], wgu_ref, bg_ref, ng_ref,
                  y_ref.at[u], s1_ref.at[pl.ds(u * gb, gb)], bc_ref.at[u], b2_ref.at[u], kf_ref.at[u],
                  gb=gb, seg_len=seg_len, t_valid=t_valid, heads=heads)


def _gla_unit(q_ref, k_ref, v_ref, r_ref, z_ref, wgu_ref, bg_ref, ng_ref, y_ref, s1_ref, bc_ref, b2_ref, kf_ref,
              *, gb, seg_len, t_valid, heads):
    width = heads * DKB
    row = lax.broadcasted_iota(jnp.int32, (ROWS, width), 0)
    tpos = row % seg_len
    lg = _log_sigmoid(jnp.dot(z_ref[...].astype(BF16), wgu_ref[...], preferred_element_type=F32)
                      + bg_ref[...]) / GLA_TAU
    k_all = k_ref[...].astype(F32)
    if t_valid < seg_len:
        valid = tpos < t_valid
        lg = jnp.where(valid, lg, 0.0)
        k_all = jnp.where(valid, k_all, 0.0)
    bc = _seg_cumsum(lg, seg_len)

    n_sub = ROWS // GLA_SUB
    sub_row = lax.broadcasted_iota(jnp.int32, (GLA_SUB, ROWS), 0)
    sub_col = lax.broadcasted_iota(jnp.int32, (GLA_SUB, ROWS), 1)

    for h in range(heads):
        kcols = slice(h * DKB, (h + 1) * DKB)
        vcols = slice(h * DVB, (h + 1) * DVB)
        qh = q_ref[:, kcols].astype(F32) * (DKB ** -0.5)
        kh = k_all[:, kcols]
        vh = v_ref[:, vcols]
        bch = bc[:, kcols]
        bc_ref[...] = bch
        b2_ref[...] = bch * LOG2_E
        kf_ref[...] = kh
        b_last_g = bc_ref[pl.ds(seg_len - 1, gb, stride=seg_len), :]
        b_last = _rep_rows(b_last_g, seg_len)
        s_h = s1_ref[:, h].reshape(gb * DKB, DVB)

        inter = _dot(_expand_blockdiag(qh * jnp.exp(bch), gb, seg_len), s_h)

        att_rows = []
        k_decayed = []
        for blk in range(n_sub):
            r0 = blk * GLA_SUB
            q_b = qh[r0:r0 + GLA_SUB]
            bc_b = bch[r0:r0 + GLA_SUB]
            b2_b = b2_ref[r0:r0 + GLA_SUB, :]
            att = jnp.zeros((GLA_SUB, ROWS), F32)
            for s in range(GLA_SUB):
                k_row = jnp.broadcast_to(kf_ref[r0 + s:r0 + s + 1, :], (GLA_SUB, DKB))
                b2_row = jnp.broadcast_to(b2_ref[r0 + s:r0 + s + 1, :], (GLA_SUB, DKB))
                cs = jnp.sum(q_b * k_row * jnp.exp2(b2_b - b2_row), axis=-1, keepdims=True)
                att = jnp.where(sub_col == r0 + s, cs, att)
            att = jnp.where(sub_col <= sub_row + r0, att, 0.0)
            if seg_len > GLA_SUB and blk > 0:
                b_edge = bch[r0 - 1:r0]
                if blk > 1:
                    step = jnp.exp(b_edge - bch[r0 - GLA_SUB - 1:r0 - GLA_SUB])
                    k_decayed = [kb * step for kb in k_decayed]
                k_decayed.append(kh[r0 - GLA_SUB:r0] * jnp.exp(b_edge - bch[r0 - GLA_SUB:r0]))
                k_t = jnp.concatenate(k_decayed + [jnp.zeros((ROWS - r0, DKB), F32)], axis=0)
                att = att + _dot_nt(q_b * jnp.exp(bc_b - b_edge), k_t)
            att_rows.append(att)
        att_full = jnp.concatenate(att_rows, axis=0)

        o = inter + _dot(att_full, vh)
        msq = jnp.mean(o * o, axis=-1, keepdims=True)
        on = o * lax.rsqrt(msq + EPS) * ng_ref[:, vcols]
        r = r_ref[:, vcols].astype(F32)
        y_ref[:, vcols] = (on * (r * _sigmoid(r))).astype(y_ref.dtype)

        k_d = kh * jnp.exp(b_last - bch)
        upd = _dot_tn(_expand_blockdiag(k_d, gb, seg_len), vh)
        e_last = jnp.exp(b_last_g)
        e_cols = jnp.concatenate(
            [jnp.broadcast_to(e_last[b:b + 1], (DKB, DKB)).T for b in range(gb)], axis=0)
        e_cols = jnp.concatenate([e_cols] * (DVB // DKB), axis=1)
        s1_ref[:, h] = (e_cols * s_h + upd).reshape(gb, DKB, DVB)


def _gla(proj, z, w_gate_up, b_gate, norm_g, s0, gb, seg_len, t_valid, n_chunks, heads, units):
    n = proj.shape[0]
    n_blocks = n // (ROWS * n_chunks)
    assert n_blocks % units == 0 and seg_len % GLA_SUB == 0
    n_hb = HB // heads
    wq, wv = heads * DKB, heads * DVB
    q_off, k_off = 0, (HB * DKB) // wq
    v_off, r_off = (2 * HB * DKB) // wv, (2 * HB * DKB + HB * DVB) // wv
    rows3 = lambda a: a.reshape(n_blocks, n_chunks * ROWS, a.shape[-1])
    proj3, z3 = rows3(proj), rows3(z)
    blk = lambda width, off: pl.BlockSpec((units, ROWS, width), lambda i, hb, c: (i, c, off + hb))
    state_spec = pl.BlockSpec((units * gb, heads, DKB, DVB), lambda i, hb, c: (i, hb, 0, 0))
    wgu = jnp.zeros((LANES, HB * DKB), BF16).at[:GLA_RANK].set(w_gate_up.astype(BF16))
    y, s1 = pl.pallas_call(
        functools.partial(_gla_kernel, units=units, gb=gb, seg_len=seg_len, t_valid=t_valid, heads=heads),
        grid=(n_blocks // units, n_hb, n_chunks),
        in_specs=[blk(wq, q_off), blk(wq, k_off), blk(wv, v_off), blk(wv, r_off),
                  pl.BlockSpec((units, ROWS, LANES), lambda i, hb, c: (i, c, 0)),
                  pl.BlockSpec((LANES, wq), lambda i, hb, c: (0, hb)),
                  pl.BlockSpec((1, wq), lambda i, hb, c: (0, hb)),
                  pl.BlockSpec((1, wv), lambda i, hb, c: (0, hb)),
                  state_spec],
        out_specs=[blk(wv, 0), state_spec],
        out_shape=[jax.ShapeDtypeStruct((n_blocks, n_chunks * ROWS, HB * DVB), BF16),
                   jax.ShapeDtypeStruct(s0.shape, F32)],
        scratch_shapes=[pltpu.VMEM((units, ROWS, DKB), F32)] * 3,
        compiler_params=_params(("parallel", "parallel", "arbitrary")),
        name="gla_mixer",
    )(proj3, proj3, proj3, proj3, z3, wgu, b_gate.reshape(1, -1), norm_g.reshape(1, -1), s0)
    return y.reshape(n, HB * DVB), s1


def _swa_kernel(*refs, blocks, units, q_len, prev_from_grid, shift):
    q_ref, kc_ref, vc_ref, kp_ref, vp_ref, sink_ref = refs[:6]
    o_ref = refs[8 if shift else 6]
    if shift:
        for src, dst in ((refs[6], refs[9]), (refs[7], refs[10])):
            dst[:, 0:WINDOW - shift] = src[:, shift:WINDOW]
            dst[:, WINDOW - shift:WINDOW] = jnp.zeros((units, shift, HKV, HDC), F32)
    for b in range(blocks):
        _swa_block(q_ref.at[b], kc_ref.at[b], vc_ref.at[b], kp_ref.at[b], vp_ref.at[b], sink_ref, o_ref.at[b],
                   units=units, q_len=q_len, prev_from_grid=prev_from_grid)


def _swa_block(q_ref, kc_ref, vc_ref, kp_ref, vp_ref, sink_ref, o_ref, *, units, q_len, prev_from_grid):
    group = HC // HKV
    rows4 = group * ROWS
    row = lax.broadcasted_iota(jnp.int32, (rows4, ROWS), 0) % ROWS
    key = lax.broadcasted_iota(jnp.int32, (rows4, ROWS), 1)
    q_pos = row % q_len
    mask_p = key >= q_pos
    if prev_from_grid:
        mask_p = mask_p & (pl.program_id(1) > 0)
    mask_c = ((key // q_len) == (row // q_len)) & ((key % q_len) <= q_pos)
    bias_p = jnp.where(mask_p, 0.0, NEG)
    bias_c = jnp.where(mask_c, 0.0, NEG)
    low = lax.broadcasted_iota(jnp.int32, (ROWS, LANES), 1) < HDC
    ones = jnp.ones((WINDOW, LANES), BF16)
    sinks = sink_ref[...]
    scale = HDC ** -0.5

    def per_unit(lhs, rhs, contract_rhs_rows):
        mm = _dot if contract_rhs_rows else _dot_nt
        if units == 1:
            return mm(lhs, rhs)
        pieces = [[None] * units for _ in range(group)]
        for u in range(units):
            rows_u = jnp.concatenate([lhs[gi * ROWS + u * q_len:gi * ROWS + (u + 1) * q_len]
                                      for gi in range(group)], axis=0)
            res = mm(rows_u, rhs[u * WINDOW:(u + 1) * WINDOW])
            for gi in range(group):
                pieces[gi][u] = res[gi * q_len:(gi + 1) * q_len]
        return jnp.concatenate([pieces[gi][u] for gi in range(group) for u in range(units)], axis=0)

    out_tiles = [None] * (HC // 2)
    for kk in range(HKV):
        kv_cols = slice((kk // 2) * LANES, (kk // 2 + 1) * LANES)
        k_half = kk % 2
        k_sel = low if k_half == 0 else ~low
        q_parts, sink_parts = [], []
        for gi in range(group):
            hq = kk * group + gi
            tile = q_ref[:, (hq // 2) * LANES:(hq // 2 + 1) * LANES] * scale
            if hq % 2 != k_half:
                tile = pltpu.roll(tile, HDC, axis=1)
            q_parts.append(jnp.where(k_sel, tile, 0.0))
            sink_parts.append(jnp.broadcast_to(_col(sinks, hq), (ROWS, LANES)))
        q4 = jnp.concatenate(q_parts, axis=0)
        sink = jnp.concatenate(sink_parts, axis=0)
        s_p = per_unit(q4, kp_ref[:, kv_cols], False) + bias_p
        s_c = _dot_nt(q4, kc_ref[:, kv_cols]) + bias_c
        row_max = jnp.max(jnp.maximum(s_p, s_c), axis=-1, keepdims=True)
        mx = jnp.maximum(jnp.broadcast_to(row_max, (rows4, LANES)), sink)
        p_p = jnp.exp(s_p - mx).astype(BF16)
        p_c = jnp.exp(s_c - mx).astype(BF16)
        den = (jnp.dot(p_p, ones, preferred_element_type=F32) + jnp.dot(p_c, ones, preferred_element_type=F32)
               + jnp.exp(sink - mx))
        out4 = (per_unit(p_p, vp_ref[:, kv_cols], True) + _dot(p_c, vc_ref[:, kv_cols])) / den
        for gi in range(group):
            hq = kk * group + gi
            part = out4[gi * ROWS:(gi + 1) * ROWS]
            if hq % 2 != k_half:
                part = pltpu.roll(part, HDC, axis=1)
            prev = out_tiles[hq // 2]
            out_tiles[hq // 2] = part if prev is None else jnp.where(low if hq % 2 == 0 else ~low, part, prev)
    for t in range(HC // 2):
        o_ref[:, t * LANES:(t + 1) * LANES] = out_tiles[t].astype(o_ref.dtype)


def _swa(proj, sinks, units, q_len, n_chunks, blocks, caches=None, layer=0, shift=0):
    n = proj.shape[0]
    n_chains = n // (ROWS * n_chunks)
    assert n_chains % blocks == 0
    kv_w = HKV * HDC
    k_blk, v_blk = (HC * HDC) // kv_w, (HC * HDC) // kv_w + 1
    prev_from_grid = caches is None
    proj3 = proj.reshape(n_chains, n_chunks * ROWS, proj.shape[-1])
    cur = lambda blk: pl.BlockSpec((blocks, ROWS, kv_w), lambda i, c: (i, c, blk))
    if prev_from_grid:
        prev = lambda blk: pl.BlockSpec((blocks, ROWS, kv_w), lambda i, c: (i, jnp.maximum(c - 1, 0), blk))
        prev_specs = [prev(k_blk), prev(v_blk)]
        prev_args = [proj3, proj3]
    else:
        assert blocks == 1 and n_chunks == 1
        prev_specs = [pl.BlockSpec((blocks, units * WINDOW, kv_w), lambda i, c: (i, 0, 0))] * 2
        prev_args = [cache[layer].reshape(n_chains, units * WINDOW, kv_w) for cache in caches]
    sink_row = jnp.pad(sinks.astype(F32), (0, LANES - HC)).reshape(1, LANES)
    in_specs = [pl.BlockSpec((blocks, ROWS, HC * HDC), lambda i, c: (i, c, 0)),
                cur(k_blk), cur(v_blk)] + prev_specs + [pl.BlockSpec((1, LANES), lambda i, c: (0, 0))]
    args = [proj3, proj3, proj3, *prev_args, sink_row]
    out_specs = [pl.BlockSpec((blocks, ROWS, HC * HDC), lambda i, c: (i, c, 0))]
    out_shape = [jax.ShapeDtypeStruct((n_chains, n_chunks * ROWS, HC * HDC), BF16)]
    if shift:
        cache_blk = (units, WINDOW, HKV, HDC)
        in_specs += [pl.BlockSpec((None,) + cache_blk, lambda i, c: (layer, i, 0, 0, 0))] * 2
        args += list(caches)
        out_specs += [pl.BlockSpec(cache_blk, lambda i, c: (i, 0, 0, 0))] * 2
        out_shape += [jax.ShapeDtypeStruct(caches[0].shape[1:], F32)] * 2
    y, *moved = pl.pallas_call(
        functools.partial(_swa_kernel, blocks=blocks, units=units, q_len=q_len, prev_from_grid=prev_from_grid,
                          shift=shift),
        grid=(n_chains // blocks, n_chunks),
        in_specs=in_specs, out_specs=out_specs, out_shape=out_shape,
        compiler_params=_params(("parallel", "arbitrary")),
        name="swa_mixer",
    )(*args)
    return (y.reshape(n, HC * HDC), *moved)


def _proj_spec(w, i):
    kind, j = i % N_MIXERS, i // N_MIXERS
    spec = {'g': (w['norm_mix'], i), 'bias': None, 'w_extra': None, 'out_dtype': BF16}
    if kind == 0:
        spec.update(w=(w['a_w_in'], j), e_main=2 * HA * (DKA + DVA), w_extra=(w['a_w_gates'], j))
    elif kind == 1:
        spec.update(w=(w['b_w_in'], j), e_main=2 * HB * (DKB + DVB), w_extra=(w['b_w_z'], j))
    else:
        spec.update(w=(w['c_w_in'], j), e_main=(HC + 2 * HKV) * HDC, bias=(w['c_b_in'], j), out_dtype=F32)
    return spec


def _trunk(x3, st, w, is_prompt):
    bsz, t_in, _ = x3.shape
    if is_prompt:
        t, t_valid, gb, seg_len = t_in, ROWS, 1, ROWS
        units = math.gcd(MIXER_UNITS, bsz)
        x = x3.reshape(bsz * t, D_MODEL)
    else:
        t, t_valid, gb, seg_len = SAMPLE_T_PAD, t_in, ROWS // SAMPLE_T_PAD, SAMPLE_T_PAD
        units = 1
        x = jnp.pad(x3, ((0, 0), (0, t - t_in), (0, 0))).reshape(bsz * t, D_MODEL)
    n_chunks = (gb * t) // ROWS
    new = {'a_n': [], 'a_m': [], 'b_s': [], 'c_k': [], 'c_v': [], 'f': []}
    c_layers = []

    projected = _norm_matmul(x, _proj_spec(w, 0))
    for i in range(DEPTH):
        kind, j = i % N_MIXERS, i // N_MIXERS
        if kind == 0:
            proj, gates = projected
            last_a = j == st['a_c'].shape[0] - 1
            y, c1, n1, m1 = _mlstm(proj, gates, w['a_b_i'][j], w['a_b_f'][j], w['a_norm'][j],
                                   st['a_c'], j, c_layers if last_a else [], st['a_n'][j], st['a_m'][j],
                                   gb, seg_len, t_valid, n_chunks, units=units)
            if last_a:
                c_final = c1
            else:
                c_layers.append(c1[0])
            new['a_n'].append(n1)
            new['a_m'].append(m1)
            w_out = (w['a_w_out'], j)
        elif kind == 1:
            proj, z = projected
            y, s1 = _gla(proj, z, w['b_w_gate_up'][j], w['b_b_gate'][j], w['b_norm'][j], st['b_s'][j],
                         gb, seg_len, t_valid, n_chunks, heads=HB,
                         units=units)
            new['b_s'].append(s1)
            w_out = (w['b_w_out'], j)
        else:
            proj = projected[0]
            kv_w = HKV * HDC
            end = t if is_prompt else t_valid
            keep = min(WINDOW, end)
            newest = proj.reshape(bsz, t, -1)[:, end - keep:end, HC * HDC:]
            k_new = newest[:, :, :kv_w].reshape(bsz, keep, HKV, HDC)
            v_new = newest[:, :, kv_w:].reshape(bsz, keep, HKV, HDC)
            if is_prompt:
                y, = _swa(proj, w['c_sinks'][j], 1, ROWS, n_chunks, units)
                new['c_k'].append(k_new)
                new['c_v'].append(v_new)
            else:
                y, k_moved, v_moved = _swa(proj, w['c_sinks'][j], gb, seg_len, n_chunks, 1,
                                           caches=(st['c_k'], st['c_v']), layer=j, shift=t_valid)
                new['c_k'].append(lax.dynamic_update_slice(k_moved, k_new, (0, WINDOW - t_valid, 0, 0)))
                new['c_v'].append(lax.dynamic_update_slice(v_moved, v_new, (0, WINDOW - t_valid, 0, 0)))
            w_out = (w['c_w_out'], j)

        last = i == DEPTH - 1
        tail_args = (x, y, w_out, w['ffn'], i)
        tail_kw = dict(seq_len=t, final_g=w['norm_final'] if last else None,
                       next_proj=None if last else _proj_spec(w, i + 1),
                       tm=TAIL_TM_PROMPT if is_prompt else TAIL_TM_SAMPLE)
        if is_prompt:
            x, gate_tail, *projected = _layer_tail(*tail_args, **tail_kw)
            seq_tails = gate_tail.reshape(bsz, t // TAIL_TM_PROMPT, SUBLANES, D_FF)[:, -1]
            new['f'].append(seq_tails[:, SUBLANES - (CONV_W - 1):])
        else:
            x, new_state, *projected = _layer_tail(*tail_args, conv_state=st['f'], t_valid=t_valid, **tail_kw)
            new['f'].append(new_state)

    out = {name: jnp.stack(vals) for name, vals in new.items()}
    out['a_c'] = c_final
    y = x.reshape(bsz, t, D_MODEL)[:, :t_in]
    return y, out


def _prepare_weights(norm_mix_g, norm_ffn_g, norm_final_g, a_w_in, a_b_i, a_b_f, a_norm_g, a_w_out, b_w_in,
                     b_w_gate_up, b_b_gate, b_norm_g, b_w_out, c_w_in, c_b_in, c_sinks, c_w_out, f_w_up,
                     f_conv_w, f_conv_b, f_w_down):
    n_a, n_b = a_w_in.shape[0], b_w_in.shape[0]
    e_a = 2 * HA * (DKA + DVA)
    e_b = 2 * HB * (DKB + DVB)
    a_w_gates = jnp.zeros((n_a, D_MODEL, 2 * LANES), BF16)
    a_w_gates = a_w_gates.at[:, :, :HA].set(a_w_in[:, :, e_a:e_a + HA].astype(BF16))
    a_w_gates = a_w_gates.at[:, :, LANES:LANES + HA].set(a_w_in[:, :, e_a + HA:].astype(BF16))
    b_w_z = jnp.zeros((n_b, D_MODEL, LANES), BF16).at[:, :, :GLA_RANK].set(b_w_in[:, :, e_b:].astype(BF16))
    ffn = {'g': norm_ffn_g[:, None, :], 'w_up': f_w_up.astype(BF16), 'conv_w': f_conv_w,
           'conv_b': f_conv_b[:, None, :], 'w_down': f_w_down.astype(BF16)}
    w = {'norm_mix': norm_mix_g[:, None, :], 'norm_final': norm_final_g[None, :], 'ffn': ffn,
         'a_w_in': a_w_in.astype(BF16), 'a_w_gates': a_w_gates, 'a_b_i': a_b_i, 'a_b_f': a_b_f,
         'a_norm': a_norm_g, 'a_w_out': a_w_out.astype(BF16),
         'b_w_in': b_w_in.astype(BF16), 'b_w_z': b_w_z, 'b_w_gate_up': b_w_gate_up, 'b_b_gate': b_b_gate,
         'b_norm': b_norm_g, 'b_w_out': b_w_out.astype(BF16),
         'c_w_in': c_w_in.astype(BF16), 'c_b_in': c_b_in[:, None, :], 'c_sinks': c_sinks,
         'c_w_out': c_w_out.astype(BF16)}
    return w


def kernel(x_prompt, x_sample, state_mlstm_c, state_mlstm_n, state_mlstm_m, state_gla, cache_swa_k, cache_swa_v, state_ffn_conv, norm_mix_g, norm_ffn_g, norm_final_g, a_w_in, a_b_i, a_b_f, a_norm_g, a_w_out, b_w_in, b_w_gate_up, b_b_gate, b_norm_g, b_w_out, c_w_in, c_b_in, c_sinks, c_w_out, f_w_up, f_conv_w, f_conv_b, f_w_down):
    w = _prepare_weights(norm_mix_g, norm_ffn_g, norm_final_g, a_w_in, a_b_i, a_b_f, a_norm_g, a_w_out, b_w_in,
                         b_w_gate_up, b_b_gate, b_norm_g, b_w_out, c_w_in, c_b_in, c_sinks, c_w_out, f_w_up,
                         f_conv_w, f_conv_b, f_w_down)
    n_a, n_b, n_c = a_w_in.shape[0], b_w_in.shape[0], c_w_in.shape[0]
    bp = x_prompt.shape[0]
    st_p = {'a_c': jnp.zeros((n_a, bp, HA, DKA, DVA), F32),
            'a_n': jnp.zeros((n_a, bp, HA, DKA), F32),
            'a_m': jnp.zeros((n_a, bp, HA), F32),
            'b_s': jnp.zeros((n_b, bp, HB, DKB, DVB), F32),
            'c_k': [None] * n_c, 'c_v': [None] * n_c, 'f': None}
    st_s = {'a_c': state_mlstm_c, 'a_n': state_mlstm_n, 'a_m': state_mlstm_m, 'b_s': state_gla,
            'c_k': cache_swa_k, 'c_v': cache_swa_v, 'f': state_ffn_conv}
    y_prompt, np_ = _trunk(x_prompt, st_p, w, True)
    y_sample, ns_ = _trunk(x_sample, st_s, w, False)
    return (y_prompt, y_sample,
            np_['a_c'], np_['a_n'], np_['a_m'], np_['b_s'], np_['c_k'], np_['c_v'], np_['f'],
            ns_['a_c'], ns_['a_n'], ns_['a_m'], ns_['b_s'], ns_['c_k'], ns_['c_v'], ns_['f'])
```

```python
import functools
import math

import jax
import jax.numpy as jnp
from jax import lax
from jax.experimental import pallas as pl
from jax.experimental.pallas import tpu as pltpu

F32 = jnp.float32
BF16 = jnp.bfloat16

D_MODEL = 1024
DEPTH = 4
N_MIXERS = 3
HA, DKA, DVA = 8, 64, 128
GATE_SOFTCAP = 15.0
HB, DKB, DVB = 4, 128, 256
GLA_RANK = 16
GLA_TAU = 16.0
HC, HKV, HDC = 16, 4, 64
WINDOW = 128
D_FF = 2816
CONV_W = 3
EPS = 1e-6
NEG = -1e30
LOG2_E = 1.4426950408889634

ROWS = 128
LANES = 128
SUBLANES = 8
SAMPLE_T_PAD = 8
GLA_SUB = 8
FFN_CHUNK = 256
PROJ_CHUNK = 512
MIXER_UNITS = 4
TAIL_TM_PROMPT = 512
TAIL_TM_SAMPLE = 256
TAIL_PIECES = 4
VMEM_LIMIT = 56 * 1024 * 1024


def _params(sem):
    return pltpu.CompilerParams(dimension_semantics=sem, vmem_limit_bytes=VMEM_LIMIT)


def _dot(a, b):
    return jnp.dot(a.astype(BF16), b.astype(BF16), preferred_element_type=F32)


def _dot_nt(a, b):
    return lax.dot_general(a.astype(BF16), b.astype(BF16), (((1,), (1,)), ((), ())),
                           preferred_element_type=F32)


def _dot_tn(a, b):
    return lax.dot_general(a.astype(BF16), b.astype(BF16), (((0,), (0,)), ((), ())),
                           preferred_element_type=F32)


def _sigmoid(x):
    return 1.0 / (1.0 + jnp.exp(-x))


def _log_sigmoid(x):
    return jnp.minimum(x, 0.0) - jnp.log(1.0 + jnp.exp(-jnp.abs(x)))


def _softcap(z):
    return GATE_SOFTCAP * jnp.tanh(z / GATE_SOFTCAP)


def _col(x, h):
    lane = lax.broadcasted_iota(jnp.int32, x.shape, 1)
    return jnp.sum(jnp.where(lane == h, x, 0.0), axis=-1, keepdims=True)


def _expand_heads(x, n_heads, width, terms):
    lanes = x.shape[1]
    src = lax.broadcasted_iota(jnp.int32, (terms * lanes, n_heads * width), 0) % lanes
    dst = lax.broadcasted_iota(jnp.int32, (terms * lanes, n_heads * width), 1) // width
    pieces, rest = [], x
    for _ in range(terms):
        piece = rest.astype(BF16)
        pieces.append(piece)
        rest = rest - piece.astype(F32)
    return jnp.dot(jnp.concatenate(pieces, axis=1), (src == dst).astype(BF16), preferred_element_type=F32)


def _rep_rows(x, reps):
    g, c = x.shape
    if g == 1:
        return jnp.broadcast_to(x, (reps, c))
    return jnp.concatenate([jnp.broadcast_to(x[b:b + 1], (reps, c)) for b in range(g)], axis=0)


def _seg_scan(x, seg_len, op, tpos):
    s = 1
    while s < seg_len:
        shifted = pltpu.roll(x, s, axis=0)
        x = jnp.where(tpos >= s, op(x, shifted), x)
        s *= 2
    return x


def _seg_cumsum(x, seg_len):
    rows = x.shape[0]
    dst = lax.broadcasted_iota(jnp.int32, (rows, 3 * rows), 0)
    src = lax.broadcasted_iota(jnp.int32, (rows, 3 * rows), 1) % rows
    tri = ((src <= dst) & ((src // seg_len) == (dst // seg_len))).astype(BF16)
    pieces, rest = [], x
    for _ in range(3):
        piece = rest.astype(BF16)
        pieces.append(piece)
        rest = rest - piece.astype(F32)
    return jnp.dot(tri, jnp.concatenate(pieces, axis=0), preferred_element_type=F32)


def _expand_blockdiag(x, gb, seg_len):
    if gb == 1:
        return x
    seq = lax.broadcasted_iota(jnp.int32, x.shape, 0) // seg_len
    return jnp.concatenate([jnp.where(seq == b, x, 0.0) for b in range(gb)], axis=1)


def _resident(arr, layer=None):
    shape = arr.shape if layer is None else arr.shape[1:]
    block = shape if layer is None else (None,) + shape
    lead = () if layer is None else (layer,)
    return pl.BlockSpec(block, lambda *_: lead + (0,) * len(shape), pipeline_mode=pl.Buffered(1))


def _project(x, rows, gn_ref, wn_ref, bn_ref, wne_ref, p_ref, pe_ref, e_main):
    ms = jnp.mean(x * x, axis=-1, keepdims=True)
    xn = (x * lax.rsqrt(ms + EPS) * gn_ref[...]).astype(BF16)
    for c in range(e_main // PROJ_CHUNK):
        cols = slice(c * PROJ_CHUNK, (c + 1) * PROJ_CHUNK)
        acc = jnp.dot(xn, wn_ref[:, cols], preferred_element_type=F32)
        if bn_ref is not None:
            acc = acc + bn_ref[:, cols]
        p_ref[rows, cols] = acc.astype(p_ref.dtype)
    if wne_ref is not None:
        pe_ref[rows, :] = jnp.dot(xn, wne_ref[...], preferred_element_type=F32)


def _proj_operands(spec, n, tm):
    g, g_layer = spec['g']
    w, w_layer = spec['w']
    in_specs = [_resident(g, g_layer), _resident(w, w_layer)]
    args = [g, w]
    if spec['bias'] is not None:
        in_specs.append(_resident(*spec['bias']))
        args.append(spec['bias'][0])
    out_specs = [pl.BlockSpec((tm, spec['e_main']), lambda i: (i, 0))]
    out_shape = [jax.ShapeDtypeStruct((n, spec['e_main']), spec['out_dtype'])]
    if spec['w_extra'] is not None:
        in_specs.append(_resident(*spec['w_extra']))
        args.append(spec['w_extra'][0])
        ex = spec['w_extra'][0].shape[-1]
        out_specs.append(pl.BlockSpec((tm, ex), lambda i: (i, 0)))
        out_shape.append(jax.ShapeDtypeStruct((n, ex), F32))
    return in_specs, args, out_specs, out_shape


def _proj_kernel(*refs, e_main, has_bias, has_extra, tm):
    refs = list(refs)
    take = lambda cond=True: refs.pop(0) if cond else None
    x_ref, gn_ref, wn_ref = take(), take(), take()
    bn_ref, wne_ref = take(has_bias), take(has_extra)
    p_ref, pe_ref = take(), take(has_extra)
    for r in range(2):
        rows = slice(r * tm // 2, (r + 1) * tm // 2)
        _project(x_ref[rows, :], rows, gn_ref, wn_ref, bn_ref, wne_ref, p_ref, pe_ref, e_main)


def _norm_matmul(x, spec, tm=512):
    n, d = x.shape
    assert n % tm == 0 and spec['e_main'] % PROJ_CHUNK == 0
    in_specs, args, out_specs, out_shape = _proj_operands(spec, n, tm)
    return pl.pallas_call(
        functools.partial(_proj_kernel, e_main=spec['e_main'], has_bias=spec['bias'] is not None,
                          has_extra=spec['w_extra'] is not None, tm=tm),
        grid=(n // tm,),
        in_specs=[pl.BlockSpec((tm, d), lambda i: (i, 0))] + in_specs,
        out_specs=out_specs, out_shape=out_shape,
        compiler_params=_params(("parallel",)),
        name="norm_matmul",
    )(x, *args)


def _tail_kernel(*refs, tm, seq_len, t_valid, carry_mode, apply_final, proj_main, proj_bias, proj_extra):
    refs = list(refs)
    take = lambda cond=True: refs.pop(0) if cond else None
    x_ref, y_ref, wo_ref, g_ref, wu_ref, cw_ref, cb_ref, wd_ref = [take() for _ in range(8)]
    st_ref = take(not carry_mode)
    fg_ref = take(apply_final)
    has_proj = proj_main > 0
    gn_ref, wn_ref = take(has_proj), take(has_proj)
    bn_ref, wne_ref = take(proj_bias), take(proj_extra)
    o_ref, go_ref = take(), take()
    p_ref, pe_ref = take(has_proj), take(proj_extra)
    h_ref = take()
    carry_ref = take(carry_mode)

    d_ff = wd_ref.shape[0]
    if carry_mode:
        @pl.when((pl.program_id(0) * tm) % seq_len == 0)
        def _():
            carry_ref[...] = jnp.zeros_like(carry_ref)
        row = lax.broadcasted_iota(jnp.int32, (tm, FFN_CHUNK), 0)
    else:
        n_seq = tm // seq_len
        tpos = lax.broadcasted_iota(jnp.int32, (n_seq, seq_len, FFN_CHUNK), 1)

    x2_halves, xn_halves = [], []
    for r in range(TAIL_PIECES):
        rows = slice(r * tm // TAIL_PIECES, (r + 1) * tm // TAIL_PIECES)
        x2_r = x_ref[rows, :] + jnp.dot(y_ref[rows, :].astype(BF16), wo_ref[...], preferred_element_type=F32)
        ms = jnp.mean(x2_r * x2_r, axis=-1, keepdims=True)
        x2_halves.append(x2_r)
        xn_halves.append((x2_r * lax.rsqrt(ms + EPS) * g_ref[...]).astype(BF16))
    x2 = jnp.concatenate(x2_halves, axis=0)
    xn = jnp.concatenate(xn_halves, axis=0)

    def rows_dot(w, split):
        if split:
            return jnp.concatenate([jnp.dot(h, w, preferred_element_type=F32) for h in xn_halves], axis=0)
        return jnp.dot(xn, w, preferred_element_type=F32)

    for c in range(d_ff // FFN_CHUNK):
        cols = slice(c * FFN_CHUNK, (c + 1) * FFN_CHUNK)
        up_cols = slice(d_ff + c * FFN_CHUNK, d_ff + (c + 1) * FFN_CHUNK)
        gate = rows_dot(wu_ref[:, cols], c == 0)
        up = rows_dot(wu_ref[:, up_cols], c == 0)
        if carry_mode:
            prev = carry_ref[:, cols]
            p1, p2 = prev[SUBLANES - 1:SUBLANES], prev[SUBLANES - 2:SUBLANES - 1]
            g1 = jnp.where(row == 0, p1, pltpu.roll(gate, 1, axis=0))
            g2 = jnp.where(row == 0, p2, jnp.where(row == 1, p1, pltpu.roll(gate, 2, axis=0)))
            tail = gate[tm - SUBLANES:tm]
            carry_ref[:, cols] = tail
            go_ref[0, :, cols] = tail
        else:
            gate3 = gate.reshape(n_seq, seq_len, FFN_CHUNK)
            st_2, st_1 = st_ref[:, 0:1, cols], st_ref[:, 1:2, cols]
            g1 = jnp.where(tpos >= 1, pltpu.roll(gate3, 1, axis=1), st_1)
            g2 = jnp.where(tpos >= 2, pltpu.roll(gate3, 2, axis=1), jnp.where(tpos == 0, st_2, st_1))
            g1 = g1.reshape(tm, FFN_CHUNK)
            g2 = g2.reshape(tm, FFN_CHUNK)
            go_ref[:, :, cols] = gate3[:, t_valid - (CONV_W - 1):t_valid, :]
        cw = cw_ref[:, cols]
        conv = cb_ref[:, cols] + cw[0:1] * g2 + cw[1:2] * g1 + cw[2:3] * gate
        h_ref[:, cols] = (conv * _sigmoid(conv) * up).astype(BF16)

    out = x2 + jnp.dot(h_ref[...], wd_ref[...], preferred_element_type=F32)
    if apply_final:
        ms = jnp.mean(out * out, axis=-1, keepdims=True)
        out = out * lax.rsqrt(ms + EPS) * fg_ref[...]
    o_ref[...] = out

    if has_proj:
        _project(out, slice(None), gn_ref, wn_ref, bn_ref, wne_ref, p_ref, pe_ref, proj_main)


def _layer_tail(x, y, w_out, ffn, layer, seq_len, conv_state=None, t_valid=None, final_g=None, next_proj=None,
                tm=256):
    n, d = x.shape
    f = ffn['w_down'].shape[1]
    assert n % tm == 0 and f % FFN_CHUNK == 0
    carry_mode = conv_state is None
    apply_final = final_g is not None
    in_specs = [pl.BlockSpec((tm, d), lambda i: (i, 0)),
                pl.BlockSpec((tm, d), lambda i: (i, 0)),
                _resident(*w_out)]
    args = [x, y, w_out[0]]
    for name in ('g', 'w_up', 'conv_w', 'conv_b', 'w_down'):
        in_specs.append(_resident(ffn[name], layer))
        args.append(ffn[name])
    scratch = []
    if carry_mode:
        assert seq_len % tm == 0
        go_shape = jax.ShapeDtypeStruct((n // tm, SUBLANES, f), F32)
        go_spec = pl.BlockSpec((1, SUBLANES, f), lambda i: (i, 0, 0))
        scratch.append(pltpu.VMEM((SUBLANES, f), F32))
    else:
        assert seq_len == SUBLANES and tm % seq_len == 0 and t_valid >= CONV_W - 1
        state_blk = (tm // seq_len, CONV_W - 1, f)
        in_specs.append(pl.BlockSpec((None,) + state_blk, lambda i: (layer, i, 0, 0)))
        args.append(conv_state)
        go_shape = jax.ShapeDtypeStruct((n // seq_len, CONV_W - 1, f), F32)
        go_spec = pl.BlockSpec(state_blk, lambda i: (i, 0, 0))
    if apply_final:
        in_specs.append(_resident(final_g))
        args.append(final_g)
    out_specs = [pl.BlockSpec((tm, d), lambda i: (i, 0)), go_spec]
    out_shape = [jax.ShapeDtypeStruct((n, d), F32), go_shape]
    proj_main, proj_bias, proj_extra = 0, False, False
    if next_proj is not None:
        proj_main = next_proj['e_main']
        proj_bias = next_proj['bias'] is not None
        proj_extra = next_proj['w_extra'] is not None
        assert proj_main % PROJ_CHUNK == 0
        p_in_specs, p_args, p_out_specs, p_out_shape = _proj_operands(next_proj, n, tm)
        in_specs += p_in_specs
        args += p_args
        out_specs += p_out_specs
        out_shape += p_out_shape
    scratch = [pltpu.VMEM((tm, f), BF16)] + scratch
    return pl.pallas_call(
        functools.partial(_tail_kernel, tm=tm, seq_len=seq_len, t_valid=t_valid, carry_mode=carry_mode,
                          apply_final=apply_final,
                          proj_main=proj_main, proj_bias=proj_bias, proj_extra=proj_extra),
        grid=(n // tm,),
        in_specs=in_specs,
        out_specs=out_specs,
        out_shape=out_shape,
        scratch_shapes=scratch,
        compiler_params=_params(("arbitrary",)),
        name="layer_tail",
    )(*args)


def _mlstm_kernel(*refs, units, gb, seg_len, t_valid, n_prev):
    q_ref, k_ref, v_ref, o_ref, gi_ref, gf_ref, bi_ref, bf_ref, ng_ref, c0_ref, n0_ref, m0_ref = refs[:12]
    prev_refs = refs[12:12 + n_prev]
    y_ref, c_out_ref, n1_ref, m1_ref, fs_ref, ms_ref = refs[12 + n_prev:]
    c1_ref = c_out_ref.at[n_prev]

    @pl.when(pl.program_id(1) == 0)
    def _():
        for k in range(n_prev):
            c_out_ref[k] = prev_refs[k][...]
        c1_ref[...] = c0_ref[...]
        n1_ref[...] = n0_ref[...]
        m1_ref[...] = m0_ref[...]

    for u in range(units):
        seqs = pl.ds(u * gb, gb)
        _mlstm_unit(q_ref.at[u], k_ref.at[u], v_ref.at[u], o_ref.at[u], gi_ref.at[u], gf_ref.at[u],
                    bi_ref, bf_ref, ng_ref, y_ref.at[u], c1_ref.at[seqs], n1_ref.at[u], m1_ref.at[u],
                    fs_ref.at[u], ms_ref.at[u], gb=gb, seg_len=seg_len, t_valid=t_valid)


def _mlstm_unit(q_ref, k_ref, v_ref, o_ref, gi_ref, gf_ref, bi_ref, bf_ref, ng_ref,
                y_ref, c1_ref, n1_ref, m1_ref, fs_ref, ms_ref, *, gb, seg_len, t_valid):
    shape = (ROWS, LANES)
    row = lax.broadcasted_iota(jnp.int32, shape, 0)
    lane = lax.broadcasted_iota(jnp.int32, shape, 1)
    tpos = row % seg_len
    low_half = lane < DKA

    ig = _softcap(gi_ref[...] + bi_ref[...])
    lf = _log_sigmoid(_softcap(gf_ref[...] + bf_ref[...]))
    if t_valid < seg_len:
        valid = tpos < t_valid
        ig = jnp.where(valid, ig, NEG)
        lf = jnp.where(valid, lf, 0.0)
    f_cum = _seg_scan(lf, seg_len, jnp.add, tpos)
    a = ig - f_cum
    cmax = _seg_scan(a, seg_len, jnp.maximum, tpos)
    m_prev_g = m1_ref[...]
    m_prev = _rep_rows(m_prev_g, seg_len)
    m_t = f_cum + jnp.maximum(m_prev, cmax)
    g = jnp.exp(jnp.minimum(f_cum + m_prev - m_t, 0.0))
    u = f_cum - m_t
    em = jnp.exp(-m_t)

    fs_ref[...] = f_cum
    ms_ref[...] = m_t
    f_last_g = fs_ref[pl.ds(seg_len - 1, gb, stride=seg_len), :]
    m_last_g = ms_ref[pl.ds(seg_len - 1, gb, stride=seg_len), :]
    f_last = _rep_rows(f_last_g, seg_len)
    m_last = _rep_rows(m_last_g, seg_len)
    w_last = jnp.exp(jnp.minimum(a + f_last - m_last, 0.0))
    g_last_g = jnp.exp(jnp.minimum(f_last_g + m_prev_g - m_last_g, 0.0))

    a_t = a.T
    u_w = _expand_heads(u, HA, DVA, 3)
    g_w = _expand_heads(g, HA, DVA, 2)
    em_w = _expand_heads(em, HA, DVA, 2)
    wl_w = _expand_heads(w_last, HA, DKA, 2)
    gl_rows = g_last_g if gb >= SUBLANES else jnp.broadcast_to(g_last_g, (SUBLANES, LANES))
    gl_w = _expand_heads(gl_rows, HA, DKA, 2)[:gb]

    col_i = lax.broadcasted_iota(jnp.int32, (ROWS, 2 * ROWS), 1) % ROWS
    row_i = lax.broadcasted_iota(jnp.int32, (ROWS, 2 * ROWS), 0)
    causal2 = (col_i <= row_i) & ((col_i // seg_len) == (row_i // seg_len))
    state_low = lax.broadcasted_iota(jnp.int32, (gb * LANES, LANES), 0) % LANES < DKA
    half_ones = ((lax.broadcasted_iota(jnp.int32, (LANES, 2 * DVA), 0) < DKA)
                 == (lax.broadcasted_iota(jnp.int32, (LANES, 2 * DVA), 1) < DVA)).astype(BF16)
    sel = (lax.broadcasted_iota(jnp.int32, (max(gb, SUBLANES), ROWS), 1) // seg_len
           == lax.broadcasted_iota(jnp.int32, (max(gb, SUBLANES), ROWS), 0)).astype(BF16)
    ones_blk = (lax.broadcasted_iota(jnp.int32, (2 * ROWS, 2 * DVA), 0) // ROWS
                == lax.broadcasted_iota(jnp.int32, (2 * ROWS, 2 * DVA), 1) // DVA).astype(BF16)
    zeros_v = jnp.zeros((ROWS, DVA), v_ref.dtype)

    q_all = q_ref[...]
    k_all = k_ref[...] * (DKA ** -0.5)
    k2_all = k_all * wl_w

    for p in range(HA // 2):
        cols = slice(p * LANES, (p + 1) * LANES)
        wide = slice(2 * p * DVA, (2 * p + 2) * DVA)
        qp, kp, k2 = q_all[:, cols], k_all[:, cols], k2_all[:, cols]
        c_p = c1_ref[:, p].reshape(gb * LANES, DVA)
        n_p = n1_ref[p]
        v2 = v_ref[:, wide]
        v_a, v_b = v2[:, :DVA], v2[:, DVA:]

        k_sep = jnp.concatenate([jnp.where(low_half, kp, 0.0), jnp.where(low_half, 0.0, kp)], axis=0)
        s2 = _dot_nt(qp, k_sep)
        logw2 = jnp.concatenate([a_t[2 * p:2 * p + 1], a_t[2 * p + 1:2 * p + 2]], axis=1) + u_w[:, wide]
        qk2 = (s2 * jnp.where(causal2, jnp.exp(logw2), 0.0)).astype(BF16)
        v_bd = jnp.concatenate([jnp.concatenate([v_a, zeros_v], axis=1),
                                jnp.concatenate([zeros_v, v_b], axis=1)], axis=0)
        num2 = jnp.dot(qk2, v_bd, preferred_element_type=F32)
        dsum2 = jnp.dot(qk2, ones_blk, preferred_element_type=F32)

        qx = _expand_blockdiag(qp, gb, seg_len)
        c_sep = jnp.concatenate([jnp.where(state_low, c_p, 0.0), jnp.where(state_low, 0.0, c_p)], axis=1)
        inter2 = _dot(qx, c_sep)
        dint2 = _dot(qp * _rep_rows(n_p, seg_len), half_ones)

        g2 = g_w[:, wide]
        hh2 = (num2 + g2 * inter2) / jnp.maximum(jnp.abs(dsum2 + g2 * dint2), em_w[:, wide])
        for half in range(2):
            hcols = slice((2 * p + half) * DVA, (2 * p + half + 1) * DVA)
            hh = hh2[:, half * DVA:(half + 1) * DVA]
            msq = jnp.mean(hh * hh, axis=-1, keepdims=True)
            hn = hh * lax.rsqrt(msq + EPS) * ng_ref[:, hcols]
            y_ref[:, hcols] = (hn * _sigmoid(o_ref[:, hcols].astype(F32))).astype(y_ref.dtype)

        upd2 = _dot_tn(_expand_blockdiag(k2, gb, seg_len), v2)
        upd = jnp.where(state_low, upd2[:, :DVA], upd2[:, DVA:])
        n_scale = gl_w[:, cols]
        scale = jnp.concatenate([jnp.broadcast_to(n_scale[b:b + 1], (LANES, LANES)).T for b in range(gb)],
                                axis=0)
        c1_ref[:, p] = (scale * c_p + upd).reshape(gb, LANES, DVA)
        n_upd = jnp.dot(sel, k2.astype(BF16), preferred_element_type=F32)[:gb]
        n1_ref[p] = n_scale * n_p + n_upd

    m1_ref[...] = m_last_g


def _mlstm(proj, gates, b_i, b_f, norm_g, c_all, layer, c_prev, n0, m0, gb, seg_len, t_valid, n_chunks, units):
    n = proj.shape[0]
    n_blocks = n // (ROWS * n_chunks)
    assert n_blocks % units == 0
    bsz = n_blocks * gb
    half_pairs = HA // 2
    rows3 = lambda a: a.reshape(n_blocks, n_chunks * ROWS, a.shape[-1])
    proj3, gates3 = rows3(proj), rows3(gates)
    c0r = c_all.reshape(c_all.shape[0], bsz, half_pairs, LANES, DVA)
    n0r = n0.reshape(n_blocks, gb, half_pairs, LANES).transpose(0, 2, 1, 3)
    m0r = jnp.pad(m0, ((0, 0), (0, LANES - HA))).reshape(n_blocks, gb, LANES)
    pad = lambda b: jnp.pad(b, (0, LANES - HA)).reshape(1, LANES)
    blk = lambda width, cb: pl.BlockSpec((units, ROWS, width), lambda i, c: (i, c, cb))
    c_blk = (units * gb, half_pairs, LANES, DVA)
    n_prev = len(c_prev)
    c_prev = [c.reshape(bsz, half_pairs, LANES, DVA) for c in c_prev]
    y, c1, n1, m1 = pl.pallas_call(
        functools.partial(_mlstm_kernel, units=units, gb=gb, seg_len=seg_len, t_valid=t_valid, n_prev=n_prev),
        grid=(n_blocks // units, n_chunks),
        in_specs=[blk(HA * DKA, 0), blk(HA * DKA, 1), blk(HA * DVA, 1), blk(HA * DVA, 2),
                  blk(LANES, 0), blk(LANES, 1),
                  pl.BlockSpec((1, LANES), lambda i, c: (0, 0)),
                  pl.BlockSpec((1, LANES), lambda i, c: (0, 0)),
                  pl.BlockSpec((1, HA * DVA), lambda i, c: (0, 0)),
                  pl.BlockSpec((None,) + c_blk, lambda i, c: (layer, i, 0, 0, 0)),
                  pl.BlockSpec((units, half_pairs, gb, LANES), lambda i, c: (i, 0, 0, 0)),
                  pl.BlockSpec((units, gb, LANES), lambda i, c: (i, 0, 0))]
                 + [pl.BlockSpec(c_blk, lambda i, c: (i, 0, 0, 0))] * n_prev,
        out_specs=[blk(HA * DVA, 0),
                   pl.BlockSpec((n_prev + 1,) + c_blk, lambda i, c: (0, i, 0, 0, 0)),
                   pl.BlockSpec((units, half_pairs, gb, LANES), lambda i, c: (i, 0, 0, 0)),
                   pl.BlockSpec((units, gb, LANES), lambda i, c: (i, 0, 0))],
        out_shape=[jax.ShapeDtypeStruct((n_blocks, n_chunks * ROWS, HA * DVA), BF16),
                   jax.ShapeDtypeStruct((n_prev + 1,) + c0r.shape[1:], F32),
                   jax.ShapeDtypeStruct(n0r.shape, F32),
                   jax.ShapeDtypeStruct(m0r.shape, F32)],
        scratch_shapes=[pltpu.VMEM((units, ROWS, LANES), F32), pltpu.VMEM((units, ROWS, LANES), F32)],
        compiler_params=_params(("parallel", "arbitrary")),
        name="mlstm_mixer",
    )(proj3, proj3, proj3, proj3, gates3, gates3, pad(b_i), pad(b_f), norm_g.reshape(1, -1), c0r, n0r, m0r,
      *c_prev)
    y = y.reshape(n, HA * DVA)
    c1 = c1.reshape(n_prev + 1, bsz, HA, DKA, DVA)
    n1 = n1.transpose(0, 2, 1, 3).reshape(bsz, HA, DKA)
    m1 = m1.reshape(bsz, LANES)[:, :HA]
    return y, c1, n1, m1


def _gla_kernel(q_ref, k_ref, v_ref, r_ref, z_ref, wgu_ref, bg_ref, ng_ref, s0_ref,
                y_ref, s1_ref, bc_ref, b2_ref, kf_ref, *, units, gb, seg_len, t_valid, heads):
    @pl.when(pl.program_id(2) == 0)
    def _():
        s1_ref[...] = s0_ref[...]

    for u in range(units):
        _gla_unit(q_ref.at[u], k_ref.at[u], v_ref.at[u], r_ref.at[u], z_ref.at[u], wgu_ref, bg_ref, ng_ref,
                  y_ref.at[u], s1_ref.at[pl.ds(u * gb, gb)], bc_ref.at[u], b2_ref.at[u], kf_ref.at[u],
                  gb=gb, seg_len=seg_len, t_valid=t_valid, heads=heads)


def _gla_unit(q_ref, k_ref, v_ref, r_ref, z_ref, wgu_ref, bg_ref, ng_ref, y_ref, s1_ref, bc_ref, b2_ref, kf_ref,
              *, gb, seg_len, t_valid, heads):
    width = heads * DKB
    row = lax.broadcasted_iota(jnp.int32, (ROWS, width), 0)
    tpos = row % seg_len
    lg = _log_sigmoid(jnp.dot(z_ref[...].astype(BF16), wgu_ref[...], preferred_element_type=F32)
                      + bg_ref[...]) / GLA_TAU
    k_all = k_ref[...].astype(F32)
    if t_valid < seg_len:
        valid = tpos < t_valid
        lg = jnp.where(valid, lg, 0.0)
        k_all = jnp.where(valid, k_all, 0.0)
    bc = _seg_cumsum(lg, seg_len)

    n_sub = ROWS // GLA_SUB
    sub_row = lax.broadcasted_iota(jnp.int32, (GLA_SUB, ROWS), 0)
    sub_col = lax.broadcasted_iota(jnp.int32, (GLA_SUB, ROWS), 1)

    for h in range(heads):
        kcols = slice(h * DKB, (h + 1) * DKB)
        vcols = slice(h * DVB, (h + 1) * DVB)
        qh = q_ref[:, kcols].astype(F32) * (DKB ** -0.5)
        kh = k_all[:, kcols]
        vh = v_ref[:, vcols]
        bch = bc[:, kcols]
        bc_ref[...] = bch
        b2_ref[...] = bch * LOG2_E
        kf_ref[...] = kh
        b_last_g = bc_ref[pl.ds(seg_len - 1, gb, stride=seg_len), :]
        b_last = _rep_rows(b_last_g, seg_len)
        s_h = s1_ref[:, h].reshape(gb * DKB, DVB)

        inter = _dot(_expand_blockdiag(qh * jnp.exp(bch), gb, seg_len), s_h)

        att_rows = []
        k_decayed = []
        for blk in range(n_sub):
            r0 = blk * GLA_SUB
            q_b = qh[r0:r0 + GLA_SUB]
            bc_b = bch[r0:r0 + GLA_SUB]
            b2_b = b2_ref[r0:r0 + GLA_SUB, :]
            att = jnp.zeros((GLA_SUB, ROWS), F32)
            for s in range(GLA_SUB):
                k_row = jnp.broadcast_to(kf_ref[r0 + s:r0 + s + 1, :], (GLA_SUB, DKB))
                b2_row = jnp.broadcast_to(b2_ref[r0 + s:r0 + s + 1, :], (GLA_SUB, DKB))
                cs = jnp.sum(q_b * k_row * jnp.exp2(b2_b - b2_row), axis=-1, keepdims=True)
                att = jnp.where(sub_col == r0 + s, cs, att)
            att = jnp.where(sub_col <= sub_row + r0, att, 0.0)
            if seg_len > GLA_SUB and blk > 0:
                b_edge = bch[r0 - 1:r0]
                if blk > 1:
                    step = jnp.exp(b_edge - bch[r0 - GLA_SUB - 1:r0 - GLA_SUB])
                    k_decayed = [kb * step for kb in k_decayed]
                k_decayed.append(kh[r0 - GLA_SUB:r0] * jnp.exp(b_edge - bch[r0 - GLA_SUB:r0]))
                k_t = jnp.concatenate(k_decayed + [jnp.zeros((ROWS - r0, DKB), F32)], axis=0)
                att = att + _dot_nt(q_b * jnp.exp(bc_b - b_edge), k_t)
            att_rows.append(att)
        att_full = jnp.concatenate(att_rows, axis=0)

        o = inter + _dot(att_full, vh)
        msq = jnp.mean(o * o, axis=-1, keepdims=True)
        on = o * lax.rsqrt(msq + EPS) * ng_ref[:, vcols]
        r = r_ref[:, vcols].astype(F32)
        y_ref[:, vcols] = (on * (r * _sigmoid(r))).astype(y_ref.dtype)

        k_d = kh * jnp.exp(b_last - bch)
        upd = _dot_tn(_expand_blockdiag(k_d, gb, seg_len), vh)
        e_last = jnp.exp(b_last_g)
        e_cols = jnp.concatenate(
            [jnp.broadcast_to(e_last[b:b + 1], (DKB, DKB)).T for b in range(gb)], axis=0)
        e_cols = jnp.concatenate([e_cols] * (DVB // DKB), axis=1)
        s1_ref[:, h] = (e_cols * s_h + upd).reshape(gb, DKB, DVB)


def _gla(proj, z, w_gate_up, b_gate, norm_g, s0, gb, seg_len, t_valid, n_chunks, heads, units):
    n = proj.shape[0]
    n_blocks = n // (ROWS * n_chunks)
    assert n_blocks % units == 0 and seg_len % GLA_SUB == 0
    n_hb = HB // heads
    wq, wv = heads * DKB, heads * DVB
    q_off, k_off = 0, (HB * DKB) // wq
    v_off, r_off = (2 * HB * DKB) // wv, (2 * HB * DKB + HB * DVB) // wv
    rows3 = lambda a: a.reshape(n_blocks, n_chunks * ROWS, a.shape[-1])
    proj3, z3 = rows3(proj), rows3(z)
    blk = lambda width, off: pl.BlockSpec((units, ROWS, width), lambda i, hb, c: (i, c, off + hb))
    state_spec = pl.BlockSpec((units * gb, heads, DKB, DVB), lambda i, hb, c: (i, hb, 0, 0))
    wgu = jnp.zeros((LANES, HB * DKB), BF16).at[:GLA_RANK].set(w_gate_up.astype(BF16))
    y, s1 = pl.pallas_call(
        functools.partial(_gla_kernel, units=units, gb=gb, seg_len=seg_len, t_valid=t_valid, heads=heads),
        grid=(n_blocks // units, n_hb, n_chunks),
        in_specs=[blk(wq, q_off), blk(wq, k_off), blk(wv, v_off), blk(wv, r_off),
                  pl.BlockSpec((units, ROWS, LANES), lambda i, hb, c: (i, c, 0)),
                  pl.BlockSpec((LANES, wq), lambda i, hb, c: (0, hb)),
                  pl.BlockSpec((1, wq), lambda i, hb, c: (0, hb)),
                  pl.BlockSpec((1, wv), lambda i, hb, c: (0, hb)),
                  state_spec],
        out_specs=[blk(wv, 0), state_spec],
        out_shape=[jax.ShapeDtypeStruct((n_blocks, n_chunks * ROWS, HB * DVB), BF16),
                   jax.ShapeDtypeStruct(s0.shape, F32)],
        scratch_shapes=[pltpu.VMEM((units, ROWS, DKB), F32)] * 3,
        compiler_params=_params(("parallel", "parallel", "arbitrary")),
        name="gla_mixer",
    )(proj3, proj3, proj3, proj3, z3, wgu, b_gate.reshape(1, -1), norm_g.reshape(1, -1), s0)
    return y.reshape(n, HB * DVB), s1


def _swa_kernel(q_ref, kc_ref, vc_ref, kp_ref, vp_ref, sink_ref, o_ref, *, blocks, units, q_len, prev_from_grid):
    for b in range(blocks):
        _swa_block(q_ref.at[b], kc_ref.at[b], vc_ref.at[b], kp_ref.at[b], vp_ref.at[b], sink_ref, o_ref.at[b],
                   units=units, q_len=q_len, prev_from_grid=prev_from_grid)


def _swa_block(q_ref, kc_ref, vc_ref, kp_ref, vp_ref, sink_ref, o_ref, *, units, q_len, prev_from_grid):
    group = HC // HKV
    rows4 = group * ROWS
    row = lax.broadcasted_iota(jnp.int32, (rows4, ROWS), 0) % ROWS
    key = lax.broadcasted_iota(jnp.int32, (rows4, ROWS), 1)
    q_pos = row % q_len
    mask_p = key >= q_pos
    if prev_from_grid:
        mask_p = mask_p & (pl.program_id(1) > 0)
    mask_c = ((key // q_len) == (row // q_len)) & ((key % q_len) <= q_pos)
    bias_p = jnp.where(mask_p, 0.0, NEG)
    bias_c = jnp.where(mask_c, 0.0, NEG)
    low = lax.broadcasted_iota(jnp.int32, (ROWS, LANES), 1) < HDC
    ones = jnp.ones((WINDOW, LANES), BF16)
    sinks = sink_ref[...]
    scale = HDC ** -0.5

    def per_unit(lhs, rhs, contract_rhs_rows):
        mm = _dot if contract_rhs_rows else _dot_nt
        if units == 1:
            return mm(lhs, rhs)
        pieces = [[None] * units for _ in range(group)]
        for u in range(units):
            rows_u = jnp.concatenate([lhs[gi * ROWS + u * q_len:gi * ROWS + (u + 1) * q_len]
                                      for gi in range(group)], axis=0)
            res = mm(rows_u, rhs[u * WINDOW:(u + 1) * WINDOW])
            for gi in range(group):
                pieces[gi][u] = res[gi * q_len:(gi + 1) * q_len]
        return jnp.concatenate([pieces[gi][u] for gi in range(group) for u in range(units)], axis=0)

    out_tiles = [None] * (HC // 2)
    for kk in range(HKV):
        kv_cols = slice((kk // 2) * LANES, (kk // 2 + 1) * LANES)
        k_half = kk % 2
        k_sel = low if k_half == 0 else ~low
        q_parts, sink_parts = [], []
        for gi in range(group):
            hq = kk * group + gi
            tile = q_ref[:, (hq // 2) * LANES:(hq // 2 + 1) * LANES] * scale
            if hq % 2 != k_half:
                tile = pltpu.roll(tile, HDC, axis=1)
            q_parts.append(jnp.where(k_sel, tile, 0.0))
            sink_parts.append(jnp.broadcast_to(_col(sinks, hq), (ROWS, LANES)))
        q4 = jnp.concatenate(q_parts, axis=0)
        sink = jnp.concatenate(sink_parts, axis=0)
        s_p = per_unit(q4, kp_ref[:, kv_cols], False) + bias_p
        s_c = _dot_nt(q4, kc_ref[:, kv_cols]) + bias_c
        row_max = jnp.max(jnp.maximum(s_p, s_c), axis=-1, keepdims=True)
        mx = jnp.maximum(jnp.broadcast_to(row_max, (rows4, LANES)), sink)
        p_p = jnp.exp(s_p - mx).astype(BF16)
        p_c = jnp.exp(s_c - mx).astype(BF16)
        den = (jnp.dot(p_p, ones, preferred_element_type=F32) + jnp.dot(p_c, ones, preferred_element_type=F32)
               + jnp.exp(sink - mx))
        out4 = (per_unit(p_p, vp_ref[:, kv_cols], True) + _dot(p_c, vc_ref[:, kv_cols])) / den
        for gi in range(group):
            hq = kk * group + gi
            part = out4[gi * ROWS:(gi + 1) * ROWS]
            if hq % 2 != k_half:
                part = pltpu.roll(part, HDC, axis=1)
            prev = out_tiles[hq // 2]
            out_tiles[hq // 2] = part if prev is None else jnp.where(low if hq % 2 == 0 else ~low, part, prev)
    for t in range(HC // 2):
        o_ref[:, t * LANES:(t + 1) * LANES] = out_tiles[t].astype(o_ref.dtype)


def _swa(proj, sinks, units, q_len, n_chunks, blocks, prev_k=None, prev_v=None):
    n = proj.shape[0]
    n_chains = n // (ROWS * n_chunks)
    assert n_chains % blocks == 0
    kv_w = HKV * HDC
    k_blk, v_blk = (HC * HDC) // kv_w, (HC * HDC) // kv_w + 1
    prev_from_grid = prev_k is None
    proj3 = proj.reshape(n_chains, n_chunks * ROWS, proj.shape[-1])
    cur = lambda blk: pl.BlockSpec((blocks, ROWS, kv_w), lambda i, c: (i, c, blk))
    if prev_from_grid:
        prev = lambda blk: pl.BlockSpec((blocks, ROWS, kv_w), lambda i, c: (i, jnp.maximum(c - 1, 0), blk))
        prev_specs = [prev(k_blk), prev(v_blk)]
        prev_args = [proj3, proj3]
    else:
        prev_specs = [pl.BlockSpec((blocks, units * WINDOW, kv_w), lambda i, c: (i, 0, 0))] * 2
        prev_args = [prev_k.reshape(n_chains, units * WINDOW, kv_w), prev_v.reshape(n_chains, units * WINDOW, kv_w)]
    sink_row = jnp.pad(sinks.astype(F32), (0, LANES - HC)).reshape(1, LANES)
    y = pl.pallas_call(
        functools.partial(_swa_kernel, blocks=blocks, units=units, q_len=q_len, prev_from_grid=prev_from_grid),
        grid=(n_chains // blocks, n_chunks),
        in_specs=[pl.BlockSpec((blocks, ROWS, HC * HDC), lambda i, c: (i, c, 0)),
                  cur(k_blk), cur(v_blk)] + prev_specs +
                 [pl.BlockSpec((1, LANES), lambda i, c: (0, 0))],
        out_specs=pl.BlockSpec((blocks, ROWS, HC * HDC), lambda i, c: (i, c, 0)),
        out_shape=jax.ShapeDtypeStruct((n_chains, n_chunks * ROWS, HC * HDC), BF16),
        compiler_params=_params(("parallel", "arbitrary")),
        name="swa_mixer",
    )(proj3, proj3, proj3, *prev_args, sink_row)
    return y.reshape(n, HC * HDC)


def _proj_spec(w, i):
    kind, j = i % N_MIXERS, i // N_MIXERS
    spec = {'g': (w['norm_mix'], i), 'bias': None, 'w_extra': None, 'out_dtype': BF16}
    if kind == 0:
        spec.update(w=(w['a_w_in'], j), e_main=2 * HA * (DKA + DVA), w_extra=(w['a_w_gates'], j))
    elif kind == 1:
        spec.update(w=(w['b_w_in'], j), e_main=2 * HB * (DKB + DVB), w_extra=(w['b_w_z'], j))
    else:
        spec.update(w=(w['c_w_in'], j), e_main=(HC + 2 * HKV) * HDC, bias=(w['c_b_in'], j), out_dtype=F32)
    return spec


def _trunk(x3, st, w, is_prompt):
    bsz, t_in, _ = x3.shape
    if is_prompt:
        t, t_valid, gb, seg_len = t_in, ROWS, 1, ROWS
        units = math.gcd(MIXER_UNITS, bsz)
        x = x3.reshape(bsz * t, D_MODEL)
    else:
        t, t_valid, gb, seg_len = SAMPLE_T_PAD, t_in, ROWS // SAMPLE_T_PAD, SAMPLE_T_PAD
        units = 1
        x = jnp.pad(x3, ((0, 0), (0, t - t_in), (0, 0))).reshape(bsz * t, D_MODEL)
    n_chunks = (gb * t) // ROWS
    new = {'a_n': [], 'a_m': [], 'b_s': [], 'c_k': [], 'c_v': [], 'f': []}
    c_layers = []

    projected = _norm_matmul(x, _proj_spec(w, 0))
    for i in range(DEPTH):
        kind, j = i % N_MIXERS, i // N_MIXERS
        if kind == 0:
            proj, gates = projected
            last_a = j == st['a_c'].shape[0] - 1
            y, c1, n1, m1 = _mlstm(proj, gates, w['a_b_i'][j], w['a_b_f'][j], w['a_norm'][j],
                                   st['a_c'], j, c_layers if last_a else [], st['a_n'][j], st['a_m'][j],
                                   gb, seg_len, t_valid, n_chunks, units=units)
            if last_a:
                c_final = c1
            else:
                c_layers.append(c1[0])
            new['a_n'].append(n1)
            new['a_m'].append(m1)
            w_out = (w['a_w_out'], j)
        elif kind == 1:
            proj, z = projected
            y, s1 = _gla(proj, z, w['b_w_gate_up'][j], w['b_b_gate'][j], w['b_norm'][j], st['b_s'][j],
                         gb, seg_len, t_valid, n_chunks, heads=HB,
                         units=units)
            new['b_s'].append(s1)
            w_out = (w['b_w_out'], j)
        else:
            proj = projected[0]
            kv_w = HKV * HDC
            end = t if is_prompt else t_valid
            keep = min(WINDOW, end)
            newest = proj.reshape(bsz, t, -1)[:, end - keep:end, HC * HDC:]
            k_new = newest[:, :, :kv_w].reshape(bsz, keep, HKV, HDC)
            v_new = newest[:, :, kv_w:].reshape(bsz, keep, HKV, HDC)
            if is_prompt:
                y = _swa(proj, w['c_sinks'][j], 1, ROWS, n_chunks, units)
                new['c_k'].append(k_new)
                new['c_v'].append(v_new)
            else:
                k_buf, v_buf = st['c_k'][j], st['c_v'][j]
                y = _swa(proj, w['c_sinks'][j], gb, seg_len, n_chunks, 1,
                         prev_k=k_buf.reshape(bsz * WINDOW, kv_w), prev_v=v_buf.reshape(bsz * WINDOW, kv_w))
                new['c_k'].append(jnp.concatenate([k_buf[:, t_valid:], k_new], axis=1))
                new['c_v'].append(jnp.concatenate([v_buf[:, t_valid:], v_new], axis=1))
            w_out = (w['c_w_out'], j)

        last = i == DEPTH - 1
        tail_args = (x, y, w_out, w['ffn'], i)
        tail_kw = dict(seq_len=t, final_g=w['norm_final'] if last else None,
                       next_proj=None if last else _proj_spec(w, i + 1),
                       tm=TAIL_TM_PROMPT if is_prompt else TAIL_TM_SAMPLE)
        if is_prompt:
            x, gate_tail, *projected = _layer_tail(*tail_args, **tail_kw)
            seq_tails = gate_tail.reshape(bsz, t // TAIL_TM_PROMPT, SUBLANES, D_FF)[:, -1]
            new['f'].append(seq_tails[:, SUBLANES - (CONV_W - 1):])
        else:
            x, new_state, *projected = _layer_tail(*tail_args, conv_state=st['f'], t_valid=t_valid, **tail_kw)
            new['f'].append(new_state)

    out = {name: jnp.stack(vals) for name, vals in new.items()}
    out['a_c'] = c_final
    y = x.reshape(bsz, t, D_MODEL)[:, :t_in]
    return y, out


def _prepare_weights(norm_mix_g, norm_ffn_g, norm_final_g, a_w_in, a_b_i, a_b_f, a_norm_g, a_w_out, b_w_in,
                     b_w_gate_up, b_b_gate, b_norm_g, b_w_out, c_w_in, c_b_in, c_sinks, c_w_out, f_w_up,
                     f_conv_w, f_conv_b, f_w_down):
    n_a, n_b = a_w_in.shape[0], b_w_in.shape[0]
    e_a = 2 * HA * (DKA + DVA)
    e_b = 2 * HB * (DKB + DVB)
    a_w_gates = jnp.zeros((n_a, D_MODEL, 2 * LANES), BF16)
    a_w_gates = a_w_gates.at[:, :, :HA].set(a_w_in[:, :, e_a:e_a + HA].astype(BF16))
    a_w_gates = a_w_gates.at[:, :, LANES:LANES + HA].set(a_w_in[:, :, e_a + HA:].astype(BF16))
    b_w_z = jnp.zeros((n_b, D_MODEL, LANES), BF16).at[:, :, :GLA_RANK].set(b_w_in[:, :, e_b:].astype(BF16))
    ffn = {'g': norm_ffn_g[:, None, :], 'w_up': f_w_up.astype(BF16), 'conv_w': f_conv_w,
           'conv_b': f_conv_b[:, None, :], 'w_down': f_w_down.astype(BF16)}
    w = {'norm_mix': norm_mix_g[:, None, :], 'norm_final': norm_final_g[None, :], 'ffn': ffn,
         'a_w_in': a_w_in.astype(BF16), 'a_w_gates': a_w_gates, 'a_b_i': a_b_i, 'a_b_f': a_b_f,
         'a_norm': a_norm_g, 'a_w_out': a_w_out.astype(BF16),
         'b_w_in': b_w_in.astype(BF16), 'b_w_z': b_w_z, 'b_w_gate_up': b_w_gate_up, 'b_b_gate': b_b_gate,
         'b_norm': b_norm_g, 'b_w_out': b_w_out.astype(BF16),
         'c_w_in': c_w_in.astype(BF16), 'c_b_in': c_b_in[:, None, :], 'c_sinks': c_sinks,
         'c_w_out': c_w_out.astype(BF16)}
    return w


def kernel(x_prompt, x_sample, state_mlstm_c, state_mlstm_n, state_mlstm_m, state_gla, cache_swa_k, cache_swa_v, state_ffn_conv, norm_mix_g, norm_ffn_g, norm_final_g, a_w_in, a_b_i, a_b_f, a_norm_g, a_w_out, b_w_in, b_w_gate_up, b_b_gate, b_norm_g, b_w_out, c_w_in, c_b_in, c_sinks, c_w_out, f_w_up, f_conv_w, f_conv_b, f_w_down):
    w = _prepare_weights(norm_mix_g, norm_ffn_g, norm_final_g, a_w_in, a_b_i, a_b_f, a_norm_g, a_w_out, b_w_in,
                         b_w_gate_up, b_b_gate, b_norm_g, b_w_out, c_w_in, c_b_in, c_sinks, c_w_out, f_w_up,
                         f_conv_w, f_conv_b, f_w_down)
    n_a, n_b, n_c = a_w_in.shape[0], b_w_in.shape[0], c_w_in.shape[0]
    bp = x_prompt.shape[0]
    st_p = {'a_c': jnp.zeros((n_a, bp, HA, DKA, DVA), F32),
            'a_n': jnp.zeros((n_a, bp, HA, DKA), F32),
            'a_m': jnp.zeros((n_a, bp, HA), F32),
            'b_s': jnp.zeros((n_b, bp, HB, DKB, DVB), F32),
            'c_k': [None] * n_c, 'c_v': [None] * n_c, 'f': None}
    st_s = {'a_c': state_mlstm_c, 'a_n': state_mlstm_n, 'a_m': state_mlstm_m, 'b_s': state_gla,
            'c_k': cache_swa_k, 'c_v': cache_swa_v, 'f': state_ffn_conv}
    y_prompt, np_ = _trunk(x_prompt, st_p, w, True)
    y_sample, ns_ = _trunk(x_sample, st_s, w, False)
    return (y_prompt, y_sample,
            np_['a_c'], np_['a_n'], np_['a_m'], np_['b_s'], np_['c_k'], np_['c_v'], np_['f'],
            ns_['a_c'], ns_['a_n'], ns_['a_m'], ns_['b_s'], ns_['c_k'], ns_['c_v'], ns_['f'])
```

```python
import functools
import math

import jax
import jax.numpy as jnp
from jax import lax
from jax.experimental import pallas as pl
from jax.experimental.pallas import tpu as pltpu

F32 = jnp.float32
BF16 = jnp.bfloat16

D_MODEL = 1024
DEPTH = 4
N_MIXERS = 3
HA, DKA, DVA = 8, 64, 128
GATE_SOFTCAP = 15.0
HB, DKB, DVB = 4, 128, 256
GLA_RANK = 16
GLA_TAU = 16.0
HC, HKV, HDC = 16, 4, 64
WINDOW = 128
D_FF = 2816
CONV_W = 3
EPS = 1e-6
NEG = -1e30
LOG2_E = 1.4426950408889634

ROWS = 128
LANES = 128
SUBLANES = 8
SAMPLE_T_PAD = 8
GLA_SUB = 8
FFN_CHUNK = 256
PROJ_CHUNK = 512
MIXER_UNITS = 4
TAIL_TM_PROMPT = 512
TAIL_TM_SAMPLE = 256
VMEM_LIMIT = 56 * 1024 * 1024


def _params(sem):
    return pltpu.CompilerParams(dimension_semantics=sem, vmem_limit_bytes=VMEM_LIMIT)


def _dot(a, b):
    return jnp.dot(a.astype(BF16), b.astype(BF16), preferred_element_type=F32)


def _dot_nt(a, b):
    return lax.dot_general(a.astype(BF16), b.astype(BF16), (((1,), (1,)), ((), ())),
                           preferred_element_type=F32)


def _dot_tn(a, b):
    return lax.dot_general(a.astype(BF16), b.astype(BF16), (((0,), (0,)), ((), ())),
                           preferred_element_type=F32)


def _sigmoid(x):
    return 1.0 / (1.0 + jnp.exp(-x))


def _log_sigmoid(x):
    return jnp.minimum(x, 0.0) - jnp.log(1.0 + jnp.exp(-jnp.abs(x)))


def _softcap(z):
    return GATE_SOFTCAP * jnp.tanh(z / GATE_SOFTCAP)


def _col(x, h):
    lane = lax.broadcasted_iota(jnp.int32, x.shape, 1)
    return jnp.sum(jnp.where(lane == h, x, 0.0), axis=-1, keepdims=True)


def _expand_heads(x, n_heads, width, terms):
    lanes = x.shape[1]
    src = lax.broadcasted_iota(jnp.int32, (terms * lanes, n_heads * width), 0) % lanes
    dst = lax.broadcasted_iota(jnp.int32, (terms * lanes, n_heads * width), 1) // width
    pieces, rest = [], x
    for _ in range(terms):
        piece = rest.astype(BF16)
        pieces.append(piece)
        rest = rest - piece.astype(F32)
    return jnp.dot(jnp.concatenate(pieces, axis=1), (src == dst).astype(BF16), preferred_element_type=F32)


def _rep_rows(x, reps):
    g, c = x.shape
    if g == 1:
        return jnp.broadcast_to(x, (reps, c))
    return jnp.concatenate([jnp.broadcast_to(x[b:b + 1], (reps, c)) for b in range(g)], axis=0)


def _seg_scan(x, seg_len, op, tpos):
    s = 1
    while s < seg_len:
        shifted = pltpu.roll(x, s, axis=0)
        x = jnp.where(tpos >= s, op(x, shifted), x)
        s *= 2
    return x


def _seg_cumsum(x, seg_len):
    rows = x.shape[0]
    dst = lax.broadcasted_iota(jnp.int32, (rows, 3 * rows), 0)
    src = lax.broadcasted_iota(jnp.int32, (rows, 3 * rows), 1) % rows
    tri = ((src <= dst) & ((src // seg_len) == (dst // seg_len))).astype(BF16)
    pieces, rest = [], x
    for _ in range(3):
        piece = rest.astype(BF16)
        pieces.append(piece)
        rest = rest - piece.astype(F32)
    return jnp.dot(tri, jnp.concatenate(pieces, axis=0), preferred_element_type=F32)


def _expand_blockdiag(x, gb, seg_len):
    if gb == 1:
        return x
    seq = lax.broadcasted_iota(jnp.int32, x.shape, 0) // seg_len
    return jnp.concatenate([jnp.where(seq == b, x, 0.0) for b in range(gb)], axis=1)


def _resident(arr, layer=None):
    shape = arr.shape if layer is None else arr.shape[1:]
    block = shape if layer is None else (None,) + shape
    lead = () if layer is None else (layer,)
    return pl.BlockSpec(block, lambda *_: lead + (0,) * len(shape), pipeline_mode=pl.Buffered(1))


def _project(x, rows, gn_ref, wn_ref, bn_ref, wne_ref, p_ref, pe_ref, e_main):
    ms = jnp.mean(x * x, axis=-1, keepdims=True)
    xn = (x * lax.rsqrt(ms + EPS) * gn_ref[...]).astype(BF16)
    for c in range(e_main // PROJ_CHUNK):
        cols = slice(c * PROJ_CHUNK, (c + 1) * PROJ_CHUNK)
        acc = jnp.dot(xn, wn_ref[:, cols], preferred_element_type=F32)
        if bn_ref is not None:
            acc = acc + bn_ref[:, cols]
        p_ref[rows, cols] = acc.astype(p_ref.dtype)
    if wne_ref is not None:
        pe_ref[rows, :] = jnp.dot(xn, wne_ref[...], preferred_element_type=F32)


def _proj_operands(spec, n, tm):
    g, g_layer = spec['g']
    w, w_layer = spec['w']
    in_specs = [_resident(g, g_layer), _resident(w, w_layer)]
    args = [g, w]
    if spec['bias'] is not None:
        in_specs.append(_resident(*spec['bias']))
        args.append(spec['bias'][0])
    out_specs = [pl.BlockSpec((tm, spec['e_main']), lambda i: (i, 0))]
    out_shape = [jax.ShapeDtypeStruct((n, spec['e_main']), spec['out_dtype'])]
    if spec['w_extra'] is not None:
        in_specs.append(_resident(*spec['w_extra']))
        args.append(spec['w_extra'][0])
        ex = spec['w_extra'][0].shape[-1]
        out_specs.append(pl.BlockSpec((tm, ex), lambda i: (i, 0)))
        out_shape.append(jax.ShapeDtypeStruct((n, ex), F32))
    return in_specs, args, out_specs, out_shape


def _proj_kernel(*refs, e_main, has_bias, has_extra, tm):
    refs = list(refs)
    take = lambda cond=True: refs.pop(0) if cond else None
    x_ref, gn_ref, wn_ref = take(), take(), take()
    bn_ref, wne_ref = take(has_bias), take(has_extra)
    p_ref, pe_ref = take(), take(has_extra)
    for r in range(2):
        rows = slice(r * tm // 2, (r + 1) * tm // 2)
        _project(x_ref[rows, :], rows, gn_ref, wn_ref, bn_ref, wne_ref, p_ref, pe_ref, e_main)


def _norm_matmul(x, spec, tm=512):
    n, d = x.shape
    assert n % tm == 0 and spec['e_main'] % PROJ_CHUNK == 0
    in_specs, args, out_specs, out_shape = _proj_operands(spec, n, tm)
    return pl.pallas_call(
        functools.partial(_proj_kernel, e_main=spec['e_main'], has_bias=spec['bias'] is not None,
                          has_extra=spec['w_extra'] is not None, tm=tm),
        grid=(n // tm,),
        in_specs=[pl.BlockSpec((tm, d), lambda i: (i, 0))] + in_specs,
        out_specs=out_specs, out_shape=out_shape,
        compiler_params=_params(("parallel",)),
        name="norm_matmul",
    )(x, *args)


def _tail_kernel(*refs, tm, seq_len, t_valid, carry_mode, apply_final, proj_main, proj_bias, proj_extra,
                 has_wo):
    refs = list(refs)
    take = lambda cond=True: refs.pop(0) if cond else None
    x_ref, y_ref, wo_ref = take(), take(), take(has_wo)
    g_ref, wu_ref, cw_ref, cb_ref, wd_ref = [take() for _ in range(5)]
    st_ref = take(not carry_mode)
    fg_ref = take(apply_final)
    has_proj = proj_main > 0
    gn_ref, wn_ref = take(has_proj), take(has_proj)
    bn_ref, wne_ref = take(proj_bias), take(proj_extra)
    o_ref, go_ref = take(), take()
    p_ref, pe_ref = take(has_proj), take(proj_extra)
    h_ref = take()
    carry_ref = take(carry_mode)

    d_ff = wd_ref.shape[0]
    if carry_mode:
        @pl.when((pl.program_id(0) * tm) % seq_len == 0)
        def _():
            carry_ref[...] = jnp.zeros_like(carry_ref)
        row = lax.broadcasted_iota(jnp.int32, (tm, FFN_CHUNK), 0)
    else:
        n_seq = tm // seq_len
        tpos = lax.broadcasted_iota(jnp.int32, (n_seq, seq_len, FFN_CHUNK), 1)

    x2_halves, xn_halves = [], []
    for r in range(2):
        rows = slice(r * tm // 2, (r + 1) * tm // 2)
        if has_wo:
            x2_r = x_ref[rows, :] + jnp.dot(y_ref[rows, :].astype(BF16), wo_ref[...], preferred_element_type=F32)
        else:
            x2_r = x_ref[rows, :] + y_ref[rows, :]
        ms = jnp.mean(x2_r * x2_r, axis=-1, keepdims=True)
        x2_halves.append(x2_r)
        xn_halves.append((x2_r * lax.rsqrt(ms + EPS) * g_ref[...]).astype(BF16))
    x2 = jnp.concatenate(x2_halves, axis=0)
    xn = jnp.concatenate(xn_halves, axis=0)

    def rows_dot(w, split):
        if split:
            return jnp.concatenate([jnp.dot(h, w, preferred_element_type=F32) for h in xn_halves], axis=0)
        return jnp.dot(xn, w, preferred_element_type=F32)

    for c in range(d_ff // FFN_CHUNK):
        cols = slice(c * FFN_CHUNK, (c + 1) * FFN_CHUNK)
        up_cols = slice(d_ff + c * FFN_CHUNK, d_ff + (c + 1) * FFN_CHUNK)
        gate = rows_dot(wu_ref[:, cols], c == 0)
        up = rows_dot(wu_ref[:, up_cols], c == 0)
        if carry_mode:
            prev = carry_ref[:, cols]
            p1, p2 = prev[SUBLANES - 1:SUBLANES], prev[SUBLANES - 2:SUBLANES - 1]
            g1 = jnp.where(row == 0, p1, pltpu.roll(gate, 1, axis=0))
            g2 = jnp.where(row == 0, p2, jnp.where(row == 1, p1, pltpu.roll(gate, 2, axis=0)))
            tail = gate[tm - SUBLANES:tm]
            carry_ref[:, cols] = tail
            go_ref[0, :, cols] = tail
        else:
            gate3 = gate.reshape(n_seq, seq_len, FFN_CHUNK)
            st_2, st_1 = st_ref[:, 0:1, cols], st_ref[:, 1:2, cols]
            g1 = jnp.where(tpos >= 1, pltpu.roll(gate3, 1, axis=1), st_1)
            g2 = jnp.where(tpos >= 2, pltpu.roll(gate3, 2, axis=1), jnp.where(tpos == 0, st_2, st_1))
            g1 = g1.reshape(tm, FFN_CHUNK)
            g2 = g2.reshape(tm, FFN_CHUNK)
            go_ref[:, :, cols] = gate3[:, t_valid - (CONV_W - 1):t_valid, :]
        cw = cw_ref[:, cols]
        conv = cb_ref[:, cols] + cw[0:1] * g2 + cw[1:2] * g1 + cw[2:3] * gate
        h_ref[:, cols] = (conv * _sigmoid(conv) * up).astype(BF16)

    out = x2 + jnp.dot(h_ref[...], wd_ref[...], preferred_element_type=F32)
    if apply_final:
        ms = jnp.mean(out * out, axis=-1, keepdims=True)
        out = out * lax.rsqrt(ms + EPS) * fg_ref[...]
    o_ref[...] = out

    if has_proj:
        _project(out, slice(None), gn_ref, wn_ref, bn_ref, wne_ref, p_ref, pe_ref, proj_main)


def _layer_tail(x, y, w_out, ffn, layer, seq_len, conv_state=None, t_valid=None, final_g=None, next_proj=None,
                tm=256):
    n, d = x.shape
    f = ffn['w_down'].shape[1]
    assert n % tm == 0 and f % FFN_CHUNK == 0
    carry_mode = conv_state is None
    apply_final = final_g is not None
    in_specs = [pl.BlockSpec((tm, d), lambda i: (i, 0)),
                pl.BlockSpec((tm, d), lambda i: (i, 0))]
    args = [x, y]
    if w_out is not None:
        in_specs.append(_resident(*w_out))
        args.append(w_out[0])
    for name in ('g', 'w_up', 'conv_w', 'conv_b', 'w_down'):
        in_specs.append(_resident(ffn[name], layer))
        args.append(ffn[name])
    scratch = []
    if carry_mode:
        assert seq_len % tm == 0
        go_shape = jax.ShapeDtypeStruct((n // tm, SUBLANES, f), F32)
        go_spec = pl.BlockSpec((1, SUBLANES, f), lambda i: (i, 0, 0))
        scratch.append(pltpu.VMEM((SUBLANES, f), F32))
    else:
        assert seq_len == SUBLANES and tm % seq_len == 0 and t_valid >= CONV_W - 1
        state_blk = (tm // seq_len, CONV_W - 1, f)
        in_specs.append(pl.BlockSpec((None,) + state_blk, lambda i: (layer, i, 0, 0)))
        args.append(conv_state)
        go_shape = jax.ShapeDtypeStruct((n // seq_len, CONV_W - 1, f), F32)
        go_spec = pl.BlockSpec(state_blk, lambda i: (i, 0, 0))
    if apply_final:
        in_specs.append(_resident(final_g))
        args.append(final_g)
    out_specs = [pl.BlockSpec((tm, d), lambda i: (i, 0)), go_spec]
    out_shape = [jax.ShapeDtypeStruct((n, d), F32), go_shape]
    proj_main, proj_bias, proj_extra = 0, False, False
    if next_proj is not None:
        proj_main = next_proj['e_main']
        proj_bias = next_proj['bias'] is not None
        proj_extra = next_proj['w_extra'] is not None
        assert proj_main % PROJ_CHUNK == 0
        p_in_specs, p_args, p_out_specs, p_out_shape = _proj_operands(next_proj, n, tm)
        in_specs += p_in_specs
        args += p_args
        out_specs += p_out_specs
        out_shape += p_out_shape
    scratch = [pltpu.VMEM((tm, f), BF16)] + scratch
    return pl.pallas_call(
        functools.partial(_tail_kernel, tm=tm, seq_len=seq_len, t_valid=t_valid, carry_mode=carry_mode,
                          apply_final=apply_final,
                          proj_main=proj_main, proj_bias=proj_bias, proj_extra=proj_extra,
                          has_wo=w_out is not None),
        grid=(n // tm,),
        in_specs=in_specs,
        out_specs=out_specs,
        out_shape=out_shape,
        scratch_shapes=scratch,
        compiler_params=_params(("arbitrary",)),
        name="layer_tail",
    )(*args)


def _mlstm_kernel(*refs, units, gb, seg_len, t_valid, n_prev):
    q_ref, k_ref, v_ref, o_ref, gi_ref, gf_ref, bi_ref, bf_ref, ng_ref, c0_ref, n0_ref, m0_ref = refs[:12]
    prev_refs = refs[12:12 + n_prev]
    y_ref, c_out_ref, n1_ref, m1_ref, fs_ref, ms_ref = refs[12 + n_prev:]
    c1_ref = c_out_ref.at[n_prev]

    @pl.when(pl.program_id(1) == 0)
    def _():
        for k in range(n_prev):
            c_out_ref[k] = prev_refs[k][...]
        c1_ref[...] = c0_ref[...]
        n1_ref[...] = n0_ref[...]
        m1_ref[...] = m0_ref[...]

    for u in range(units):
        seqs = pl.ds(u * gb, gb)
        _mlstm_unit(q_ref.at[u], k_ref.at[u], v_ref.at[u], o_ref.at[u], gi_ref.at[u], gf_ref.at[u],
                    bi_ref, bf_ref, ng_ref, y_ref.at[u], c1_ref.at[seqs], n1_ref.at[u], m1_ref.at[u],
                    fs_ref.at[u], ms_ref.at[u], gb=gb, seg_len=seg_len, t_valid=t_valid)


def _mlstm_unit(q_ref, k_ref, v_ref, o_ref, gi_ref, gf_ref, bi_ref, bf_ref, ng_ref,
                y_ref, c1_ref, n1_ref, m1_ref, fs_ref, ms_ref, *, gb, seg_len, t_valid):
    shape = (ROWS, LANES)
    row = lax.broadcasted_iota(jnp.int32, shape, 0)
    lane = lax.broadcasted_iota(jnp.int32, shape, 1)
    tpos = row % seg_len
    low_half = lane < DKA

    ig = _softcap(gi_ref[...] + bi_ref[...])
    lf = _log_sigmoid(_softcap(gf_ref[...] + bf_ref[...]))
    if t_valid < seg_len:
        valid = tpos < t_valid
        ig = jnp.where(valid, ig, NEG)
        lf = jnp.where(valid, lf, 0.0)
    f_cum = _seg_scan(lf, seg_len, jnp.add, tpos)
    a = ig - f_cum
    cmax = _seg_scan(a, seg_len, jnp.maximum, tpos)
    m_prev_g = m1_ref[...]
    m_prev = _rep_rows(m_prev_g, seg_len)
    m_t = f_cum + jnp.maximum(m_prev, cmax)
    g = jnp.exp(jnp.minimum(f_cum + m_prev - m_t, 0.0))
    u = f_cum - m_t
    em = jnp.exp(-m_t)

    fs_ref[...] = f_cum
    ms_ref[...] = m_t
    f_last_g = fs_ref[pl.ds(seg_len - 1, gb, stride=seg_len), :]
    m_last_g = ms_ref[pl.ds(seg_len - 1, gb, stride=seg_len), :]
    f_last = _rep_rows(f_last_g, seg_len)
    m_last = _rep_rows(m_last_g, seg_len)
    w_last = jnp.exp(jnp.minimum(a + f_last - m_last, 0.0))
    g_last_g = jnp.exp(jnp.minimum(f_last_g + m_prev_g - m_last_g, 0.0))

    a_t = a.T
    u_w = _expand_heads(u, HA, DVA, 3)
    g_w = _expand_heads(g, HA, DVA, 2)
    em_w = _expand_heads(em, HA, DVA, 2)
    wl_w = _expand_heads(w_last, HA, DKA, 2)
    gl_rows = g_last_g if gb >= SUBLANES else jnp.broadcast_to(g_last_g, (SUBLANES, LANES))
    gl_w = _expand_heads(gl_rows, HA, DKA, 2)[:gb]

    col_i = lax.broadcasted_iota(jnp.int32, (ROWS, 2 * ROWS), 1) % ROWS
    row_i = lax.broadcasted_iota(jnp.int32, (ROWS, 2 * ROWS), 0)
    causal2 = (col_i <= row_i) & ((col_i // seg_len) == (row_i // seg_len))
    state_low = lax.broadcasted_iota(jnp.int32, (gb * LANES, LANES), 0) % LANES < DKA
    flat_low = lax.broadcasted_iota(jnp.int32, (LANES, gb * LANES), 1) % LANES < DKA
    sel = (lax.broadcasted_iota(jnp.int32, (max(gb, SUBLANES), ROWS), 1) // seg_len
           == lax.broadcasted_iota(jnp.int32, (max(gb, SUBLANES), ROWS), 0)).astype(BF16)
    ones_blk = (lax.broadcasted_iota(jnp.int32, (2 * ROWS, 2 * DVA), 0) // ROWS
                == lax.broadcasted_iota(jnp.int32, (2 * ROWS, 2 * DVA), 1) // DVA).astype(BF16)
    zeros_v = jnp.zeros((ROWS, DVA), v_ref.dtype)

    q_all = q_ref[...]
    k_all = k_ref[...] * (DKA ** -0.5)
    k2_all = k_all * wl_w

    for p in range(HA // 2):
        cols = slice(p * LANES, (p + 1) * LANES)
        wide = slice(2 * p * DVA, (2 * p + 2) * DVA)
        qp, kp, k2 = q_all[:, cols], k_all[:, cols], k2_all[:, cols]
        c_p = c1_ref[:, p].reshape(gb * LANES, DVA)
        n_p = n1_ref[p]
        v2 = v_ref[:, wide]
        v_a, v_b = v2[:, :DVA], v2[:, DVA:]

        k_sep = jnp.concatenate([jnp.where(low_half, kp, 0.0), jnp.where(low_half, 0.0, kp)], axis=0)
        s2 = _dot_nt(qp, k_sep)
        logw2 = jnp.concatenate([a_t[2 * p:2 * p + 1], a_t[2 * p + 1:2 * p + 2]], axis=1) + u_w[:, wide]
        qk2 = (s2 * jnp.where(causal2, jnp.exp(jnp.minimum(logw2, 0.0)), 0.0)).astype(BF16)
        v_bd = jnp.concatenate([jnp.concatenate([v_a, zeros_v], axis=1),
                                jnp.concatenate([zeros_v, v_b], axis=1)], axis=0)
        num2 = jnp.dot(qk2, v_bd, preferred_element_type=F32)
        dsum2 = jnp.dot(qk2, ones_blk, preferred_element_type=F32)

        qx = _expand_blockdiag(qp, gb, seg_len)
        c_sep = jnp.concatenate([jnp.where(state_low, c_p, 0.0), jnp.where(state_low, 0.0, c_p)], axis=1)
        inter2 = _dot(qx, c_sep)
        n_flat = n_p if gb == 1 else jnp.concatenate([n_p[b:b + 1] for b in range(gb)], axis=1)
        n_b = jnp.broadcast_to(n_flat, (LANES, gb * LANES))
        n_sep = jnp.concatenate([jnp.where(flat_low, n_b, 0.0), jnp.where(flat_low, 0.0, n_b)], axis=0)
        dint2 = _dot_nt(qx, n_sep)

        g2 = g_w[:, wide]
        hh2 = (num2 + g2 * inter2) / jnp.maximum(jnp.abs(dsum2 + g2 * dint2), em_w[:, wide])
        for half in range(2):
            hcols = slice((2 * p + half) * DVA, (2 * p + half + 1) * DVA)
            hh = hh2[:, half * DVA:(half + 1) * DVA]
            msq = jnp.mean(hh * hh, axis=-1, keepdims=True)
            hn = hh * lax.rsqrt(msq + EPS) * ng_ref[:, hcols]
            y_ref[:, hcols] = (hn * _sigmoid(o_ref[:, hcols].astype(F32))).astype(y_ref.dtype)

        upd2 = _dot_tn(_expand_blockdiag(k2, gb, seg_len), v2)
        upd = jnp.where(state_low, upd2[:, :DVA], upd2[:, DVA:])
        n_scale = gl_w[:, cols]
        scale = jnp.concatenate([jnp.broadcast_to(n_scale[b:b + 1], (LANES, LANES)).T for b in range(gb)],
                                axis=0)
        c1_ref[:, p] = (scale * c_p + upd).reshape(gb, LANES, DVA)
        n_upd = jnp.dot(sel, k2.astype(BF16), preferred_element_type=F32)[:gb]
        n1_ref[p] = n_scale * n_p + n_upd

    m1_ref[...] = m_last_g


def _mlstm(proj, gates, b_i, b_f, norm_g, c_all, layer, c_prev, n0, m0, gb, seg_len, t_valid, n_chunks, units):
    n = proj.shape[0]
    n_blocks = n // (ROWS * n_chunks)
    assert n_blocks % units == 0
    bsz = n_blocks * gb
    half_pairs = HA // 2
    rows3 = lambda a: a.reshape(n_blocks, n_chunks * ROWS, a.shape[-1])
    proj3, gates3 = rows3(proj), rows3(gates)
    c0r = c_all.reshape(c_all.shape[0], bsz, half_pairs, LANES, DVA)
    n0r = n0.reshape(n_blocks, gb, half_pairs, LANES).transpose(0, 2, 1, 3)
    m0r = jnp.pad(m0, ((0, 0), (0, LANES - HA))).reshape(n_blocks, gb, LANES)
    pad = lambda b: jnp.pad(b, (0, LANES - HA)).reshape(1, LANES)
    blk = lambda width, cb: pl.BlockSpec((units, ROWS, width), lambda i, c: (i, c, cb))
    c_blk = (units * gb, half_pairs, LANES, DVA)
    n_prev = len(c_prev)
    c_prev = [c.reshape(bsz, half_pairs, LANES, DVA) for c in c_prev]
    y, c1, n1, m1 = pl.pallas_call(
        functools.partial(_mlstm_kernel, units=units, gb=gb, seg_len=seg_len, t_valid=t_valid, n_prev=n_prev),
        grid=(n_blocks // units, n_chunks),
        in_specs=[blk(HA * DKA, 0), blk(HA * DKA, 1), blk(HA * DVA, 1), blk(HA * DVA, 2),
                  blk(LANES, 0), blk(LANES, 1),
                  pl.BlockSpec((1, LANES), lambda i, c: (0, 0)),
                  pl.BlockSpec((1, LANES), lambda i, c: (0, 0)),
                  pl.BlockSpec((1, HA * DVA), lambda i, c: (0, 0)),
                  pl.BlockSpec((None,) + c_blk, lambda i, c: (layer, i, 0, 0, 0)),
                  pl.BlockSpec((units, half_pairs, gb, LANES), lambda i, c: (i, 0, 0, 0)),
                  pl.BlockSpec((units, gb, LANES), lambda i, c: (i, 0, 0))]
                 + [pl.BlockSpec(c_blk, lambda i, c: (i, 0, 0, 0))] * n_prev,
        out_specs=[blk(HA * DVA, 0),
                   pl.BlockSpec((n_prev + 1,) + c_blk, lambda i, c: (0, i, 0, 0, 0)),
                   pl.BlockSpec((units, half_pairs, gb, LANES), lambda i, c: (i, 0, 0, 0)),
                   pl.BlockSpec((units, gb, LANES), lambda i, c: (i, 0, 0))],
        out_shape=[jax.ShapeDtypeStruct((n_blocks, n_chunks * ROWS, HA * DVA), BF16),
                   jax.ShapeDtypeStruct((n_prev + 1,) + c0r.shape[1:], F32),
                   jax.ShapeDtypeStruct(n0r.shape, F32),
                   jax.ShapeDtypeStruct(m0r.shape, F32)],
        scratch_shapes=[pltpu.VMEM((units, ROWS, LANES), F32), pltpu.VMEM((units, ROWS, LANES), F32)],
        compiler_params=_params(("parallel", "arbitrary")),
        name="mlstm_mixer",
    )(proj3, proj3, proj3, proj3, gates3, gates3, pad(b_i), pad(b_f), norm_g.reshape(1, -1), c0r, n0r, m0r,
      *c_prev)
    y = y.reshape(n, HA * DVA)
    c1 = c1.reshape(n_prev + 1, bsz, HA, DKA, DVA)
    n1 = n1.transpose(0, 2, 1, 3).reshape(bsz, HA, DKA)
    m1 = m1.reshape(bsz, LANES)[:, :HA]
    return y, c1, n1, m1


def _gla_kernel(q_ref, k_ref, v_ref, r_ref, z_ref, wgu_ref, bg_ref, ng_ref, s0_ref, wo_ref,
                yo_ref, s1_ref, bc_ref, b2_ref, kf_ref, y_ref, *, units, gb, seg_len, t_valid, heads):
    @pl.when(pl.program_id(2) == 0)
    def _():
        s1_ref[...] = s0_ref[...]

    for u in range(units):
        _gla_unit(q_ref.at[u], k_ref.at[u], v_ref.at[u], r_ref.at[u], z_ref.at[u], wgu_ref, bg_ref, ng_ref,
                  y_ref.at[u], s1_ref.at[pl.ds(u * gb, gb)], bc_ref.at[u], b2_ref.at[u], kf_ref.at[u],
                  gb=gb, seg_len=seg_len, t_valid=t_valid, heads=heads)
        yo_ref[u] = jnp.dot(y_ref[u], wo_ref[...], preferred_element_type=F32)


def _gla_unit(q_ref, k_ref, v_ref, r_ref, z_ref, wgu_ref, bg_ref, ng_ref, y_ref, s1_ref, bc_ref, b2_ref, kf_ref,
              *, gb, seg_len, t_valid, heads):
    width = heads * DKB
    row = lax.broadcasted_iota(jnp.int32, (ROWS, width), 0)
    tpos = row % seg_len
    lg = _log_sigmoid(jnp.dot(z_ref[...].astype(BF16), wgu_ref[...], preferred_element_type=F32)
                      + bg_ref[...]) / GLA_TAU
    k_all = k_ref[...].astype(F32)
    if t_valid < seg_len:
        valid = tpos < t_valid
        lg = jnp.where(valid, lg, 0.0)
        k_all = jnp.where(valid, k_all, 0.0)
    bc = _seg_cumsum(lg, seg_len)

    n_sub = ROWS // GLA_SUB
    sub_row = lax.broadcasted_iota(jnp.int32, (GLA_SUB, ROWS), 0)
    sub_col = lax.broadcasted_iota(jnp.int32, (GLA_SUB, ROWS), 1)

    for h in range(heads):
        kcols = slice(h * DKB, (h + 1) * DKB)
        vcols = slice(h * DVB, (h + 1) * DVB)
        qh = q_ref[:, kcols].astype(F32) * (DKB ** -0.5)
        kh = k_all[:, kcols]
        vh = v_ref[:, vcols]
        bch = bc[:, kcols]
        bc_ref[...] = bch
        b2_ref[...] = bch * LOG2_E
        kf_ref[...] = kh
        b_last_g = bc_ref[pl.ds(seg_len - 1, gb, stride=seg_len), :]
        b_last = _rep_rows(b_last_g, seg_len)
        s_h = s1_ref[:, h].reshape(gb * DKB, DVB)

        inter = _dot(_expand_blockdiag(qh * jnp.exp(bch), gb, seg_len), s_h)

        att_rows = []
        k_decayed = []
        for blk in range(n_sub):
            r0 = blk * GLA_SUB
            q_b = qh[r0:r0 + GLA_SUB]
            bc_b = bch[r0:r0 + GLA_SUB]
            b2_b = b2_ref[r0:r0 + GLA_SUB, :]
            att = jnp.zeros((GLA_SUB, ROWS), F32)
            for s in range(GLA_SUB):
                k_row = jnp.broadcast_to(kf_ref[r0 + s:r0 + s + 1, :], (GLA_SUB, DKB))
                b2_row = jnp.broadcast_to(b2_ref[r0 + s:r0 + s + 1, :], (GLA_SUB, DKB))
                cs = jnp.sum(q_b * k_row * jnp.exp2(b2_b - b2_row), axis=-1, keepdims=True)
                att = jnp.where(sub_col == r0 + s, cs, att)
            att = jnp.where(sub_col <= sub_row + r0, att, 0.0)
            if seg_len > GLA_SUB and blk > 0:
                b_edge = bch[r0 - 1:r0]
                if blk > 1:
                    step = jnp.exp(b_edge - bch[r0 - GLA_SUB - 1:r0 - GLA_SUB])
                    k_decayed = [kb * step for kb in k_decayed]
                k_decayed.append(kh[r0 - GLA_SUB:r0] * jnp.exp(b_edge - bch[r0 - GLA_SUB:r0]))
                k_t = jnp.concatenate(k_decayed + [jnp.zeros((ROWS - r0, DKB), F32)], axis=0)
                att = att + _dot_nt(q_b * jnp.exp(bc_b - b_edge), k_t)
            att_rows.append(att)
        att_full = jnp.concatenate(att_rows, axis=0)

        o = inter + _dot(att_full, vh)
        msq = jnp.mean(o * o, axis=-1, keepdims=True)
        on = o * lax.rsqrt(msq + EPS) * ng_ref[:, vcols]
        r = r_ref[:, vcols].astype(F32)
        y_ref[:, vcols] = (on * (r * _sigmoid(r))).astype(y_ref.dtype)

        k_d = kh * jnp.exp(b_last - bch)
        upd = _dot_tn(_expand_blockdiag(k_d, gb, seg_len), vh)
        e_last = jnp.exp(b_last_g)
        e_cols = jnp.concatenate(
            [jnp.broadcast_to(e_last[b:b + 1], (DKB, DKB)).T for b in range(gb)], axis=0)
        e_cols = jnp.concatenate([e_cols] * (DVB // DKB), axis=1)
        s1_ref[:, h] = (e_cols * s_h + upd).reshape(gb, DKB, DVB)


def _gla(proj, z, w_gate_up, b_gate, norm_g, s0, w_out, gb, seg_len, t_valid, n_chunks, heads, units):
    n = proj.shape[0]
    n_blocks = n // (ROWS * n_chunks)
    assert n_blocks % units == 0 and seg_len % GLA_SUB == 0 and heads == HB
    n_hb = HB // heads
    wq, wv = heads * DKB, heads * DVB
    q_off, k_off = 0, (HB * DKB) // wq
    v_off, r_off = (2 * HB * DKB) // wv, (2 * HB * DKB + HB * DVB) // wv
    rows3 = lambda a: a.reshape(n_blocks, n_chunks * ROWS, a.shape[-1])
    proj3, z3 = rows3(proj), rows3(z)
    blk = lambda width, off: pl.BlockSpec((units, ROWS, width), lambda i, hb, c: (i, c, off + hb))
    state_spec = pl.BlockSpec((units * gb, heads, DKB, DVB), lambda i, hb, c: (i, hb, 0, 0))
    wgu = jnp.zeros((LANES, HB * DKB), BF16).at[:GLA_RANK].set(w_gate_up.astype(BF16))
    yo, s1 = pl.pallas_call(
        functools.partial(_gla_kernel, units=units, gb=gb, seg_len=seg_len, t_valid=t_valid, heads=heads),
        grid=(n_blocks // units, n_hb, n_chunks),
        in_specs=[blk(wq, q_off), blk(wq, k_off), blk(wv, v_off), blk(wv, r_off),
                  pl.BlockSpec((units, ROWS, LANES), lambda i, hb, c: (i, c, 0)),
                  pl.BlockSpec((LANES, wq), lambda i, hb, c: (0, hb)),
                  pl.BlockSpec((1, wq), lambda i, hb, c: (0, hb)),
                  pl.BlockSpec((1, wv), lambda i, hb, c: (0, hb)),
                  state_spec, _resident(*w_out)],
        out_specs=[pl.BlockSpec((units, ROWS, D_MODEL), lambda i, hb, c: (i, c, 0)), state_spec],
        out_shape=[jax.ShapeDtypeStruct((n_blocks, n_chunks * ROWS, D_MODEL), F32),
                   jax.ShapeDtypeStruct(s0.shape, F32)],
        scratch_shapes=[pltpu.VMEM((units, ROWS, DKB), F32)] * 3 + [pltpu.VMEM((units, ROWS, wv), BF16)],
        compiler_params=_params(("parallel", "parallel", "arbitrary")),
        name="gla_mixer",
    )(proj3, proj3, proj3, proj3, z3, wgu, b_gate.reshape(1, -1), norm_g.reshape(1, -1), s0, w_out[0])
    return yo.reshape(n, D_MODEL), s1


def _swa_kernel(q_ref, kc_ref, vc_ref, kp_ref, vp_ref, sink_ref, o_ref, *, blocks, units, q_len, prev_from_grid):
    for b in range(blocks):
        _swa_block(q_ref.at[b], kc_ref.at[b], vc_ref.at[b], kp_ref.at[b], vp_ref.at[b], sink_ref, o_ref.at[b],
                   units=units, q_len=q_len, prev_from_grid=prev_from_grid)


def _swa_block(q_ref, kc_ref, vc_ref, kp_ref, vp_ref, sink_ref, o_ref, *, units, q_len, prev_from_grid):
    group = HC // HKV
    rows4 = group * ROWS
    row = lax.broadcasted_iota(jnp.int32, (rows4, ROWS), 0) % ROWS
    key = lax.broadcasted_iota(jnp.int32, (rows4, ROWS), 1)
    q_pos = row % q_len
    mask_p = key >= q_pos
    if prev_from_grid:
        mask_p = mask_p & (pl.program_id(1) > 0)
    mask_c = ((key // q_len) == (row // q_len)) & ((key % q_len) <= q_pos)
    bias_p = jnp.where(mask_p, 0.0, NEG)
    bias_c = jnp.where(mask_c, 0.0, NEG)
    low = lax.broadcasted_iota(jnp.int32, (ROWS, LANES), 1) < HDC
    ones = jnp.ones((WINDOW, LANES), BF16)
    sinks = sink_ref[...]
    scale = HDC ** -0.5

    def per_unit(lhs, rhs, contract_rhs_rows):
        mm = _dot if contract_rhs_rows else _dot_nt
        if units == 1:
            return mm(lhs, rhs)
        pieces = [[None] * units for _ in range(group)]
        for u in range(units):
            rows_u = jnp.concatenate([lhs[gi * ROWS + u * q_len:gi * ROWS + (u + 1) * q_len]
                                      for gi in range(group)], axis=0)
            res = mm(rows_u, rhs[u * WINDOW:(u + 1) * WINDOW])
            for gi in range(group):
                pieces[gi][u] = res[gi * q_len:(gi + 1) * q_len]
        return jnp.concatenate([pieces[gi][u] for gi in range(group) for u in range(units)], axis=0)

    out_tiles = [None] * (HC // 2)
    for kk in range(HKV):
        kv_cols = slice((kk // 2) * LANES, (kk // 2 + 1) * LANES)
        k_half = kk % 2
        k_sel = low if k_half == 0 else ~low
        q_parts, sink_parts = [], []
        for gi in range(group):
            hq = kk * group + gi
            tile = q_ref[:, (hq // 2) * LANES:(hq // 2 + 1) * LANES] * scale
            if hq % 2 != k_half:
                tile = pltpu.roll(tile, HDC, axis=1)
            q_parts.append(jnp.where(k_sel, tile, 0.0))
            sink_parts.append(jnp.broadcast_to(_col(sinks, hq), (ROWS, LANES)))
        q4 = jnp.concatenate(q_parts, axis=0)
        sink = jnp.concatenate(sink_parts, axis=0)
        s_p = per_unit(q4, kp_ref[:, kv_cols], False) + bias_p
        s_c = _dot_nt(q4, kc_ref[:, kv_cols]) + bias_c
        row_max = jnp.max(jnp.maximum(s_p, s_c), axis=-1, keepdims=True)
        mx = jnp.maximum(jnp.broadcast_to(row_max, (rows4, LANES)), sink)
        p_p = jnp.exp(s_p - mx).astype(BF16)
        p_c = jnp.exp(s_c - mx).astype(BF16)
        den = (jnp.dot(p_p, ones, preferred_element_type=F32) + jnp.dot(p_c, ones, preferred_element_type=F32)
               + jnp.exp(sink - mx))
        out4 = (per_unit(p_p, vp_ref[:, kv_cols], True) + _dot(p_c, vc_ref[:, kv_cols])) / den
        for gi in range(group):
            hq = kk * group + gi
            part = out4[gi * ROWS:(gi + 1) * ROWS]
            if hq % 2 != k_half:
                part = pltpu.roll(part, HDC, axis=1)
            prev = out_tiles[hq // 2]
            out_tiles[hq // 2] = part if prev is None else jnp.where(low if hq % 2 == 0 else ~low, part, prev)
    for t in range(HC // 2):
        o_ref[:, t * LANES:(t + 1) * LANES] = out_tiles[t].astype(o_ref.dtype)


def _swa(proj, sinks, units, q_len, n_chunks, blocks, prev_k=None, prev_v=None):
    n = proj.shape[0]
    n_chains = n // (ROWS * n_chunks)
    assert n_chains % blocks == 0
    kv_w = HKV * HDC
    k_blk, v_blk = (HC * HDC) // kv_w, (HC * HDC) // kv_w + 1
    prev_from_grid = prev_k is None
    proj3 = proj.reshape(n_chains, n_chunks * ROWS, proj.shape[-1])
    cur = lambda blk: pl.BlockSpec((blocks, ROWS, kv_w), lambda i, c: (i, c, blk))
    if prev_from_grid:
        prev = lambda blk: pl.BlockSpec((blocks, ROWS, kv_w), lambda i, c: (i, jnp.maximum(c - 1, 0), blk))
        prev_specs = [prev(k_blk), prev(v_blk)]
        prev_args = [proj3, proj3]
    else:
        prev_specs = [pl.BlockSpec((blocks, units * WINDOW, kv_w), lambda i, c: (i, 0, 0))] * 2
        prev_args = [prev_k.reshape(n_chains, units * WINDOW, kv_w), prev_v.reshape(n_chains, units * WINDOW, kv_w)]
    sink_row = jnp.pad(sinks.astype(F32), (0, LANES - HC)).reshape(1, LANES)
    y = pl.pallas_call(
        functools.partial(_swa_kernel, blocks=blocks, units=units, q_len=q_len, prev_from_grid=prev_from_grid),
        grid=(n_chains // blocks, n_chunks),
        in_specs=[pl.BlockSpec((blocks, ROWS, HC * HDC), lambda i, c: (i, c, 0)),
                  cur(k_blk), cur(v_blk)] + prev_specs +
                 [pl.BlockSpec((1, LANES), lambda i, c: (0, 0))],
        out_specs=pl.BlockSpec((blocks, ROWS, HC * HDC), lambda i, c: (i, c, 0)),
        out_shape=jax.ShapeDtypeStruct((n_chains, n_chunks * ROWS, HC * HDC), BF16),
        compiler_params=_params(("parallel", "arbitrary")),
        name="swa_mixer",
    )(proj3, proj3, proj3, *prev_args, sink_row)
    return y.reshape(n, HC * HDC)


def _proj_spec(w, i):
    kind, j = i % N_MIXERS, i // N_MIXERS
    spec = {'g': (w['norm_mix'], i), 'bias': None, 'w_extra': None, 'out_dtype': BF16}
    if kind == 0:
        spec.update(w=(w['a_w_in'], j), e_main=2 * HA * (DKA + DVA), w_extra=(w['a_w_gates'], j))
    elif kind == 1:
        spec.update(w=(w['b_w_in'], j), e_main=2 * HB * (DKB + DVB), w_extra=(w['b_w_z'], j))
    else:
        spec.update(w=(w['c_w_in'], j), e_main=(HC + 2 * HKV) * HDC, bias=(w['c_b_in'], j), out_dtype=F32)
    return spec


def _trunk(x3, st, w, is_prompt):
    bsz, t_in, _ = x3.shape
    if is_prompt:
        t, t_valid, gb, seg_len = t_in, ROWS, 1, ROWS
        units = math.gcd(MIXER_UNITS, bsz)
        x = x3.reshape(bsz * t, D_MODEL)
    else:
        t, t_valid, gb, seg_len = SAMPLE_T_PAD, t_in, ROWS // SAMPLE_T_PAD, SAMPLE_T_PAD
        units = 1
        x = jnp.pad(x3, ((0, 0), (0, t - t_in), (0, 0))).reshape(bsz * t, D_MODEL)
    n_chunks = (gb * t) // ROWS
    new = {'a_n': [], 'a_m': [], 'b_s': [], 'c_k': [], 'c_v': [], 'f': []}
    c_layers = []

    projected = _norm_matmul(x, _proj_spec(w, 0))
    for i in range(DEPTH):
        kind, j = i % N_MIXERS, i // N_MIXERS
        if kind == 0:
            proj, gates = projected
            last_a = j == st['a_c'].shape[0] - 1
            y, c1, n1, m1 = _mlstm(proj, gates, w['a_b_i'][j], w['a_b_f'][j], w['a_norm'][j],
                                   st['a_c'], j, c_layers if last_a else [], st['a_n'][j], st['a_m'][j],
                                   gb, seg_len, t_valid, n_chunks, units=units)
            if last_a:
                c_final = c1
            else:
                c_layers.append(c1[0])
            new['a_n'].append(n1)
            new['a_m'].append(m1)
            w_out = (w['a_w_out'], j)
        elif kind == 1:
            proj, z = projected
            y, s1 = _gla(proj, z, w['b_w_gate_up'][j], w['b_b_gate'][j], w['b_norm'][j], st['b_s'][j],
                         (w['b_w_out'], j), gb, seg_len, t_valid, n_chunks, heads=HB, units=units)
            new['b_s'].append(s1)
            w_out = None
        else:
            proj = projected[0]
            kv_w = HKV * HDC
            end = t if is_prompt else t_valid
            keep = min(WINDOW, end)
            newest = proj.reshape(bsz, t, -1)[:, end - keep:end, HC * HDC:]
            k_new = newest[:, :, :kv_w].reshape(bsz, keep, HKV, HDC)
            v_new = newest[:, :, kv_w:].reshape(bsz, keep, HKV, HDC)
            if is_prompt:
                y = _swa(proj, w['c_sinks'][j], 1, ROWS, n_chunks, units)
                new['c_k'].append(k_new)
                new['c_v'].append(v_new)
            else:
                k_buf, v_buf = st['c_k'][j], st['c_v'][j]
                y = _swa(proj, w['c_sinks'][j], gb, seg_len, n_chunks, 1,
                         prev_k=k_buf.reshape(bsz * WINDOW, kv_w), prev_v=v_buf.reshape(bsz * WINDOW, kv_w))
                new['c_k'].append(jnp.concatenate([k_buf[:, t_valid:], k_new], axis=1))
                new['c_v'].append(jnp.concatenate([v_buf[:, t_valid:], v_new], axis=1))
            w_out = (w['c_w_out'], j)

        last = i == DEPTH - 1
        tail_args = (x, y, w_out, w['ffn'], i)
        tail_kw = dict(seq_len=t, final_g=w['norm_final'] if last else None,
                       next_proj=None if last else _proj_spec(w, i + 1),
                       tm=TAIL_TM_PROMPT if is_prompt else TAIL_TM_SAMPLE)
        if is_prompt:
            x, gate_tail, *projected = _layer_tail(*tail_args, **tail_kw)
            seq_tails = gate_tail.reshape(bsz, t // TAIL_TM_PROMPT, SUBLANES, D_FF)[:, -1]
            new['f'].append(seq_tails[:, SUBLANES - (CONV_W - 1):])
        else:
            x, new_state, *projected = _layer_tail(*tail_args, conv_state=st['f'], t_valid=t_valid, **tail_kw)
            new['f'].append(new_state)

    out = {name: jnp.stack(vals) for name, vals in new.items()}
    out['a_c'] = c_final
    y = x.reshape(bsz, t, D_MODEL)[:, :t_in]
    return y, out


def _prepare_weights(norm_mix_g, norm_ffn_g, norm_final_g, a_w_in, a_b_i, a_b_f, a_norm_g, a_w_out, b_w_in,
                     b_w_gate_up, b_b_gate, b_norm_g, b_w_out, c_w_in, c_b_in, c_sinks, c_w_out, f_w_up,
                     f_conv_w, f_conv_b, f_w_down):
    n_a, n_b = a_w_in.shape[0], b_w_in.shape[0]
    e_a = 2 * HA * (DKA + DVA)
    e_b = 2 * HB * (DKB + DVB)
    a_w_gates = jnp.zeros((n_a, D_MODEL, 2 * LANES), BF16)
    a_w_gates = a_w_gates.at[:, :, :HA].set(a_w_in[:, :, e_a:e_a + HA].astype(BF16))
    a_w_gates = a_w_gates.at[:, :, LANES:LANES + HA].set(a_w_in[:, :, e_a + HA:].astype(BF16))
    b_w_z = jnp.zeros((n_b, D_MODEL, LANES), BF16).at[:, :, :GLA_RANK].set(b_w_in[:, :, e_b:].astype(BF16))
    ffn = {'g': norm_ffn_g[:, None, :], 'w_up': f_w_up.astype(BF16), 'conv_w': f_conv_w,
           'conv_b': f_conv_b[:, None, :], 'w_down': f_w_down.astype(BF16)}
    w = {'norm_mix': norm_mix_g[:, None, :], 'norm_final': norm_final_g[None, :], 'ffn': ffn,
         'a_w_in': a_w_in.astype(BF16), 'a_w_gates': a_w_gates, 'a_b_i': a_b_i, 'a_b_f': a_b_f,
         'a_norm': a_norm_g, 'a_w_out': a_w_out.astype(BF16),
         'b_w_in': b_w_in.astype(BF16), 'b_w_z': b_w_z, 'b_w_gate_up': b_w_gate_up, 'b_b_gate': b_b_gate,
         'b_norm': b_norm_g, 'b_w_out': b_w_out.astype(BF16),
         'c_w_in': c_w_in.astype(BF16), 'c_b_in': c_b_in[:, None, :], 'c_sinks': c_sinks,
         'c_w_out': c_w_out.astype(BF16)}
    return w


def kernel(x_prompt, x_sample, state_mlstm_c, state_mlstm_n, state_mlstm_m, state_gla, cache_swa_k, cache_swa_v, state_ffn_conv, norm_mix_g, norm_ffn_g, norm_final_g, a_w_in, a_b_i, a_b_f, a_norm_g, a_w_out, b_w_in, b_w_gate_up, b_b_gate, b_norm_g, b_w_out, c_w_in, c_b_in, c_sinks, c_w_out, f_w_up, f_conv_w, f_conv_b, f_w_down):
    w = _prepare_weights(norm_mix_g, norm_ffn_g, norm_final_g, a_w_in, a_b_i, a_b_f, a_norm_g, a_w_out, b_w_in,
                         b_w_gate_up, b_b_gate, b_norm_g, b_w_out, c_w_in, c_b_in, c_sinks, c_w_out, f_w_up,
                         f_conv_w, f_conv_b, f_w_down)
    n_a, n_b, n_c = a_w_in.shape[0], b_w_in.shape[0], c_w_in.shape[0]
    bp = x_prompt.shape[0]
    st_p = {'a_c': jnp.zeros((n_a, bp, HA, DKA, DVA), F32),
            'a_n': jnp.zeros((n_a, bp, HA, DKA), F32),
            'a_m': jnp.zeros((n_a, bp, HA), F32),
            'b_s': jnp.zeros((n_b, bp, HB, DKB, DVB), F32),
            'c_k': [None] * n_c, 'c_v': [None] * n_c, 'f': None}
    st_s = {'a_c': state_mlstm_c, 'a_n': state_mlstm_n, 'a_m': state_mlstm_m, 'b_s': state_gla,
            'c_k': cache_swa_k, 'c_v': cache_swa_v, 'f': state_ffn_conv}
    y_prompt, np_ = _trunk(x_prompt, st_p, w, True)
    y_sample, ns_ = _trunk(x_sample, st_s, w, False)
    return (y_prompt, y_sample,
            np_['a_c'], np_['a_n'], np_['a_m'], np_['b_s'], np_['c_k'], np_['c_v'], np_['f'],
            ns_['a_c'], ns_['a_n'], ns_['a_m'], ns_['b_s'], ns_['c_k'], ns_['c_v'], ns_['f'])
```
